```python
import math
import jax
import jax.numpy as jnp
from jax import lax
import numpy as np

D_MODEL = 2048
BATCH = 2
SEQ = 4096
DEPTH = 1

N_META = 16
D_MIX = D_MODEL
A_WIDTH = D_MIX // 2
A_V_DIM = 128
A_QK_DIM = A_V_DIM // 2
A_HEADS = A_WIDTH // A_V_DIM
A_QK_COLS = A_HEADS * 2 * A_QK_DIM
R_WIDTH = D_MIX - A_WIDTH
R_HEAD = 64
R_HEADS = R_WIDTH // R_HEAD
DECAY_LORA = 96
ICLR_LORA = 96
R_SHIFT_COLS = 3 * R_WIDTH + DECAY_LORA + ICLR_LORA
D_IN = 2 * A_QK_COLS + 2 * A_WIDTH + R_SHIFT_COLS + R_WIDTH
N_BUCKETS = 32
MAX_DISTANCE = 128
Q_BLOCK = 128
PAD_FRONT = Q_BLOCK - N_META
LN_EPS = 1e-5
SUBLN_EPS = 1e-5
GN_EPS = 64e-5
DEEPNORM_ALPHA = (2 * DEPTH) ** 0.25
DEEPNORM_BETA = (8 * DEPTH) ** -0.25

A_SPLITS = (A_QK_COLS, 2 * A_QK_COLS, 2 * A_QK_COLS + A_WIDTH, 2 * A_QK_COLS + 2 * A_WIDTH,
            2 * A_QK_COLS + 2 * A_WIDTH + R_SHIFT_COLS)
R_SPLITS = (R_WIDTH, 2 * R_WIDTH, 3 * R_WIDTH, 3 * R_WIDTH + DECAY_LORA)

kernel_name = 'hymba_diffattn_rwkv7_deepnorm'


def layer_norm(x, g, b):
    xf = x.astype(jnp.float32)
    mu = jnp.mean(xf, axis=-1, keepdims=True)
    var = jnp.mean(jnp.square(xf - mu), axis=-1, keepdims=True)
    y = (xf - mu) * lax.rsqrt(var + LN_EPS) * g.astype(jnp.float32) + b.astype(jnp.float32)
    return y.astype(x.dtype)


def t5_bucket(q_idx, k_idx):
    n = jnp.maximum(q_idx[:, None] - k_idx[None, :], 0)
    max_exact = N_BUCKETS // 2
    nf = jnp.maximum(n, 1).astype(jnp.float32)
    large = max_exact + (jnp.log(nf / max_exact) / math.log(MAX_DISTANCE / max_exact)
                         * (N_BUCKETS - max_exact)).astype(jnp.int32)
    large = jnp.minimum(large, N_BUCKETS - 1)
    return jnp.where(n < max_exact, n, large)


def diff_attention(q, k, v, rel_bias, lam):
    B, P = q.shape[0], q.shape[1]
    nb = P // Q_BLOCK
    k_idx = jnp.arange(P)
    q_blocks = jnp.moveaxis(q.reshape(B, nb, Q_BLOCK, A_HEADS, 2, A_QK_DIM), 1, 0)
    scale = A_QK_DIM ** -0.5
    neg = jnp.finfo(jnp.float32).min

    def one_block(args):
        blk, q_blk = args
        q_idx = blk * Q_BLOCK + jnp.arange(Q_BLOCK)
        bias = jnp.moveaxis(rel_bias[t5_bucket(q_idx, k_idx)].astype(jnp.float32), -1, 0)
        kk_, qq_ = k_idx[None, :], q_idx[:, None]
        visible = (kk_ <= qq_) & ((kk_ >= PAD_FRONT) | (kk_ == qq_))
        s = jnp.einsum('bqhmd,bkhmd->bhmqk', q_blk, k).astype(jnp.float32) * scale + bias[None, :, None]
        s = jnp.where(visible, s, neg)
        p = jax.nn.softmax(s, axis=-1)
        p = p[:, :, 0] - lam * p[:, :, 1]
        return jnp.einsum('bhqk,bkhd->bqhd', p.astype(v.dtype), v)

    out = lax.map(one_block, (jnp.arange(nb), q_blocks))
    return jnp.moveaxis(out, 0, 1).reshape(B, P, A_HEADS, A_V_DIM)


def rwkv7_step(S, inp):
    r_t, w_t, k_t, v_t, a_t, b_t = inp
    sa = jnp.einsum('bhvk,bhk->bhv', S, a_t)
    S = S * w_t[:, :, None, :] + sa[..., None] * b_t[:, :, None, :] + v_t[..., None] * k_t[:, :, None, :]
    y = jnp.einsum('bhvk,bhk->bhv', S, r_t)
    return S, y


def rwkv7_time_mix(zr, mu, w0, w_up, a0, a_up, k_k, k_a, r_k, gn_g, gn_b):
    B, L, _ = zr.shape
    f32 = jnp.float32
    z_prev = jnp.pad(zr, ((0, 0), (1, 0), (0, 0)))[:, :L]
    zs = zr + (z_prev - zr) * mu
    r, k, v, wd, ad = jnp.split(zs, R_SPLITS, axis=-1)
    w_log = -jax.nn.softplus(-(w0 + jnp.tanh(wd) @ w_up)) - 0.5
    decay = jnp.exp(-jnp.exp(w_log.astype(f32)))
    a = jax.nn.sigmoid(a0 + ad @ a_up)

    def heads(t):
        return t.astype(f32).reshape(B, L, R_HEADS, R_HEAD)

    def head_param(p):
        return p.astype(f32).reshape(R_HEADS, R_HEAD)

    r, k, v, a, decay = heads(r), heads(k), heads(v), heads(a), heads(decay)
    kk = k * head_param(k_k)
    kk = kk / jnp.maximum(jnp.sqrt(jnp.sum(kk * kk, axis=-1, keepdims=True)), 1e-12)
    k = k * (1.0 + (a - 1.0) * head_param(k_a))
    seq_first = lambda t: jnp.moveaxis(t, 1, 0)
    S0 = jnp.zeros((B, R_HEADS, R_HEAD, R_HEAD), f32)
    xs = (seq_first(r), seq_first(decay), seq_first(k), seq_first(v), seq_first(-kk), seq_first(kk * a))
    _, y = lax.scan(rwkv7_step, S0, xs)
    y = jnp.moveaxis(y, 0, 1)
    m = jnp.mean(y, axis=-1, keepdims=True)
    var = jnp.mean(jnp.square(y - m), axis=-1, keepdims=True)
    y = (y - m) * lax.rsqrt(var + GN_EPS) * head_param(gn_g) + head_param(gn_b)
    y = y + jnp.sum(r * k * r_k.astype(f32), axis=-1, keepdims=True) * v
    return y.reshape(B, L, R_WIDTH).astype(zr.dtype)


def setup_inputs(seed: int = 0) -> dict:
    key = jax.random.key(seed)
    ks = jax.random.split(key, 24)
    nrm = jax.random.normal
    f = jnp.float32
    return {
        'x': nrm(ks[0], (BATCH, SEQ, D_MODEL), f),
        'meta_tokens': nrm(ks[1], (N_META, D_MODEL), f),
        'ln_emb_g': 1.0 + 0.02 * nrm(ks[2], (D_MODEL,), f),
        'ln_emb_b': 0.02 * nrm(ks[3], (D_MODEL,), f),
        'rel_bias': 0.5 * nrm(ks[4], (N_BUCKETS, A_HEADS), f),
        'w_in': nrm(ks[5], (DEPTH, D_MODEL, D_IN), f) * D_MODEL ** -0.5,
        'w_out': nrm(ks[6], (DEPTH, D_MIX, D_MODEL), f) * (D_MIX ** -0.5 * DEEPNORM_BETA),
        'lambda_q1': 0.1 * nrm(ks[7], (DEPTH, A_QK_DIM), f),
        'lambda_k1': 0.1 * nrm(ks[8], (DEPTH, A_QK_DIM), f),
        'lambda_q2': 0.1 * nrm(ks[9], (DEPTH, A_QK_DIM), f),
        'lambda_k2': 0.1 * nrm(ks[10], (DEPTH, A_QK_DIM), f),
        'subln_g': 1.0 + 0.02 * nrm(ks[11], (DEPTH, A_V_DIM), f),
        'rw_mu': jax.random.uniform(ks[12], (DEPTH, R_SHIFT_COLS), f),
        'rw_w0': jax.random.uniform(ks[13], (DEPTH, R_WIDTH), f, minval=-6.0, maxval=-1.0),
        'rw_w_up': nrm(ks[14], (DEPTH, DECAY_LORA, R_WIDTH), f) * (0.5 * DECAY_LORA ** -0.5),
        'rw_a0': 0.5 * nrm(ks[15], (DEPTH, R_WIDTH), f),
        'rw_a_up': nrm(ks[16], (DEPTH, ICLR_LORA, R_WIDTH), f) * (0.5 * ICLR_LORA ** -0.5),
        'rw_k_k': 0.85 + 0.05 * nrm(ks[17], (DEPTH, R_WIDTH), f),
        'rw_k_a': 1.0 + 0.05 * nrm(ks[18], (DEPTH, R_WIDTH), f),
        'rw_r_k': 0.1 * nrm(ks[19], (DEPTH, R_HEADS, R_HEAD), f),
        'rw_gn_g': 1.0 + 0.02 * nrm(ks[20], (DEPTH, R_WIDTH), f),
        'rw_gn_b': 0.02 * nrm(ks[21], (DEPTH, R_WIDTH), f),
        'ln_post_g': 1.0 + 0.02 * nrm(ks[22], (DEPTH, D_MODEL), f),
        'ln_post_b': 0.02 * nrm(ks[23], (DEPTH, D_MODEL), f),
    }


def reference(x, meta_tokens, ln_emb_g, ln_emb_b, rel_bias, w_in, w_out, lambda_q1, lambda_k1,
              lambda_q2, lambda_k2, subln_g, rw_mu, rw_w0, rw_w_up, rw_a0, rw_a_up, rw_k_k, rw_k_a,
              rw_r_k, rw_gn_g, rw_gn_b, ln_post_g, ln_post_b):
    f32 = jnp.float32
    B = x.shape[0]
    meta = jnp.broadcast_to(meta_tokens.astype(x.dtype)[None], (B, N_META, D_MODEL))
    h = layer_norm(jnp.concatenate([meta, x], axis=1), ln_emb_g, ln_emb_b)
    L = h.shape[1]
    pad = ((0, 0), (PAD_FRONT, 0), (0, 0))
    for l in range(DEPTH):
        z = jnp.einsum('bld,de->ble', h, w_in[l])
        q, k, v, g_a, zr, g_r = jnp.split(z, A_SPLITS, axis=-1)
        q = jnp.pad(q, pad).reshape(B, PAD_FRONT + L, A_HEADS, 2, A_QK_DIM)
        k = jnp.pad(k, pad).reshape(B, PAD_FRONT + L, A_HEADS, 2, A_QK_DIM)
        v = jnp.pad(v, pad).reshape(B, PAD_FRONT + L, A_HEADS, A_V_DIM)
        lam_init = 0.8 - 0.6 * math.exp(-0.3 * l)
        lam = (jnp.exp(jnp.sum(lambda_q1[l].astype(f32) * lambda_k1[l].astype(f32)))
               - jnp.exp(jnp.sum(lambda_q2[l].astype(f32) * lambda_k2[l].astype(f32))) + lam_init)
        o = diff_attention(q, k, v, rel_bias, lam)[:, PAD_FRONT:].astype(f32)
        o = o * lax.rsqrt(jnp.mean(o * o, axis=-1, keepdims=True) + SUBLN_EPS) * subln_g[l].astype(f32)
        o_attn = (o * (1.0 - lam_init)).reshape(B, L, A_WIDTH).astype(x.dtype) * jax.nn.silu(g_a)
        o_rwkv = rwkv7_time_mix(zr, rw_mu[l], rw_w0[l], rw_w_up[l], rw_a0[l], rw_a_up[l], rw_k_k[l],
                                rw_k_a[l], rw_r_k[l], rw_gn_g[l], rw_gn_b[l]) * jax.nn.silu(g_r)
        y = jnp.einsum('ble,ed->bld', jnp.concatenate([o_attn, o_rwkv], axis=-1), w_out[l])
        h = layer_norm(DEEPNORM_ALPHA * h + y, ln_post_g[l], ln_post_b[l])
    return h[:, N_META:]
```

```python
import functools
import math

import numpy as np
import jax
import jax.numpy as jnp
from jax import lax
from jax.experimental import pallas as pl
from jax.experimental.pallas import tpu as pltpu

D_MODEL = 2048
N_META = 16
A_HEADS = 8
A_V_DIM = 128
A_QK_DIM = 64
A_WIDTH = A_HEADS * A_V_DIM
R_HEAD = 64
R_WIDTH = 1024
R_PAIRS = R_WIDTH // (2 * R_HEAD)
DECAY_LORA = 96
ICLR_LORA = 96
LORA_PAD = 256
ZR_PAD = 3 * R_WIDTH + LORA_PAD
N_BUCKETS = 32
MAX_DISTANCE = 128
LN_EPS = 1e-5
SUBLN_EPS = 1e-5
GN_EPS = 64e-5
DEPTH = 1
DEEPNORM_ALPHA = (2 * DEPTH) ** 0.25
LAM_INIT = 0.8 - 0.6 * math.exp(-0.3 * 0)
NEG = -1e30

ATT_T = 256
RW_TB = 128
RW_C = 64
VMEM_LIMIT = 48 * 1024 * 1024

F32 = jnp.float32
BF16 = jnp.bfloat16
HI = lax.Precision.HIGHEST


def _ln_rows(x, g, b):
  mu = jnp.mean(x, axis=-1, keepdims=True)
  xc = x - mu
  var = jnp.mean(xc * xc, axis=-1, keepdims=True)
  return xc * lax.rsqrt(var + LN_EPS) * g + b


def _ln_mm_kernel(x_ref, g_ref, b_ref, w_ref, o_ref, hn_ref):
  @pl.when(pl.program_id(1) == 0)
  def _():
    hn_ref[...] = _ln_rows(x_ref[...], g_ref[...], b_ref[...]).astype(BF16)

  o_ref[...] = jnp.dot(hn_ref[...], w_ref[...], preferred_element_type=F32).astype(o_ref.dtype)


def _ln_matmul(x2d, g, b, w, out_dtype, tm, tn):
  m, d = x2d.shape
  n = w.shape[1]
  return pl.pallas_call(
      _ln_mm_kernel,
      grid=(m // tm, n // tn),
      in_specs=[
          pl.BlockSpec((tm, d), lambda i, j: (i, 0)),
          pl.BlockSpec((1, d), lambda i, j: (0, 0)),
          pl.BlockSpec((1, d), lambda i, j: (0, 0)),
          pl.BlockSpec((d, tn), lambda i, j: (0, j)),
      ],
      out_specs=pl.BlockSpec((tm, tn), lambda i, j: (i, j)),
      out_shape=jax.ShapeDtypeStruct((m, n), out_dtype),
      scratch_shapes=[pltpu.VMEM((tm, d), BF16)],
      compiler_params=pltpu.CompilerParams(
          dimension_semantics=("parallel", "arbitrary"),
          vmem_limit_bytes=VMEM_LIMIT),
  )(x2d, g, b, w)


def _bucket_thresholds():
  n = np.arange(0, 4 * MAX_DISTANCE, dtype=np.int64)
  max_exact = N_BUCKETS // 2
  nf = np.maximum(n, 1).astype(np.float32)
  large = max_exact + (np.log(nf / np.float32(max_exact)) / np.float32(math.log(MAX_DISTANCE / max_exact))
                       * np.float32(N_BUCKETS - max_exact)).astype(np.int32)
  large = np.minimum(large, N_BUCKETS - 1)
  bucket = np.where(n < max_exact, n, large)
  assert np.all(np.diff(bucket) >= 0) and bucket[-1] == N_BUCKETS - 1
  return [int(np.argmax(bucket >= b)) for b in range(N_BUCKETS)]


_THR = _bucket_thresholds()


def _bias_kernel(rb_ref, diag_ref, sub_ref, meta_ref):
  h = pl.program_id(0)
  far = rb_ref[N_BUCKETS - 1, h]

  def bias_of(n):
    out = jnp.full(n.shape, rb_ref[0, h] - far, F32)
    for b in range(1, N_BUCKETS):
      out = jnp.where(n >= _THR[b], rb_ref[b, h] - far, out)
    return out

  t = ATT_T
  qi = lax.broadcasted_iota(jnp.int32, (t, t), 0)
  kj = lax.broadcasted_iota(jnp.int32, (t, t), 1)
  d = qi - kj
  diag_ref[...] = jnp.where(d >= 0, bias_of(d), NEG)
  sub_ref[...] = bias_of(d + t)
  qm = lax.broadcasted_iota(jnp.int32, (t, N_META), 0)
  km = lax.broadcasted_iota(jnp.int32, (t, N_META), 1)
  meta_ref[...] = bias_of(qm - km + N_META)


def _bias_tiles(rel_bias):
  t = ATT_T
  return pl.pallas_call(
      _bias_kernel,
      grid=(A_HEADS,),
      in_specs=[pl.BlockSpec(memory_space=pltpu.SMEM)],
      out_specs=[
          pl.BlockSpec((None, t, t), lambda h: (h, 0, 0)),
          pl.BlockSpec((None, t, t), lambda h: (h, 0, 0)),
          pl.BlockSpec((None, t, N_META), lambda h: (h, 0, 0)),
      ],
      out_shape=[
          jax.ShapeDtypeStruct((A_HEADS, t, t), F32),
          jax.ShapeDtypeStruct((A_HEADS, t, t), F32),
          jax.ShapeDtypeStruct((A_HEADS, t, N_META), F32),
      ],
  )(rel_bias)


def _dot_nt(a, b):
  return lax.dot_general(a, b, (((1,), (1,)), ((), ())), preferred_element_type=F32)


def _attn_kernel(q_ref, kx_ref, vx_ref, km_ref, vm_ref, ga_ref, bd_ref, bs_ref, bm_ref,
                 lam_ref, sg_ref, o_ref, m_ref, l_ref, acc_ref):
  t = ATT_T
  qi = pl.program_id(2)
  q = q_ref[...]
  lane = lax.broadcasted_iota(jnp.int32, q.shape, 1)
  zero = jnp.zeros_like(q)
  qs = (jnp.where(lane < A_QK_DIM, q, zero), jnp.where(lane >= A_QK_DIM, q, zero))

  def init(mp, s, v):
    m = jnp.max(s, axis=1, keepdims=True)
    p = jnp.exp(s - m)
    m_ref[mp] = m
    l_ref[mp] = jnp.sum(p, axis=1, keepdims=True)
    acc_ref[mp] = jnp.dot(p.astype(BF16), v, preferred_element_type=F32)

  def update(mp, s, v):
    m_prev = m_ref[mp]
    m_new = jnp.maximum(m_prev, jnp.max(s, axis=1, keepdims=True))
    alpha = jnp.exp(m_prev - m_new)
    p = jnp.exp(s - m_new)
    m_ref[mp] = m_new
    l_ref[mp] = alpha * l_ref[mp] + jnp.sum(p, axis=1, keepdims=True)
    acc_ref[mp] = alpha * acc_ref[mp] + jnp.dot(p.astype(BF16), v, preferred_element_type=F32)

  km = km_ref[...]
  vm = vm_ref[...]
  bm = jnp.where(qi == 0, bm_ref[...], 0.0)
  for mp in range(2):
    init(mp, _dot_nt(qs[mp], km) + bm, vm)

  def far_body(j, carry):
    off = pl.multiple_of(j * t, t)
    k = kx_ref[pl.ds(off, t), :]
    v = vx_ref[pl.ds(off, t), :]
    for mp in range(2):
      update(mp, _dot_nt(qs[mp], k), v)
    return carry

  lax.fori_loop(0, jnp.maximum(qi - 1, 0), far_body, 0)

  @pl.when(qi >= 1)
  def _():
    off = pl.multiple_of((qi - 1) * t, t)
    k = kx_ref[pl.ds(off, t), :]
    v = vx_ref[pl.ds(off, t), :]
    bs = bs_ref[...]
    for mp in range(2):
      update(mp, _dot_nt(qs[mp], k) + bs, v)

  off = pl.multiple_of(qi * t, t)
  k = kx_ref[pl.ds(off, t), :]
  v = vx_ref[pl.ds(off, t), :]
  bd = bd_ref[...]
  for mp in range(2):
    update(mp, _dot_nt(qs[mp], k) + bd, v)

  lp = lam_ref[...]
  lam = (jnp.exp(jnp.sum(lp[0:1] * lp[1:2], axis=1, keepdims=True))
         - jnp.exp(jnp.sum(lp[2:3] * lp[3:4], axis=1, keepdims=True)) + LAM_INIT)
  o = acc_ref[0] / l_ref[0] - lam * (acc_ref[1] / l_ref[1])
  o = o * lax.rsqrt(jnp.mean(o * o, axis=1, keepdims=True) + SUBLN_EPS) * sg_ref[...]
  o = o * (1.0 - LAM_INIT)
  g = ga_ref[...]
  o_ref[...] = (o * (g / (1.0 + jnp.exp(-g)))).astype(o_ref.dtype)


def _attention(qkv_x, qkv_m, gz_x, bias_d, bias_s, bias_m, lam_p, subln_g):
  b, s, _ = qkv_x.shape
  t = ATT_T
  h = A_HEADS
  return pl.pallas_call(
      _attn_kernel,
      grid=(b, h, s // t),
      in_specs=[
          pl.BlockSpec((None, t, 128), lambda bi, hi, qi: (bi, qi, hi)),
          pl.BlockSpec((None, s, 128), lambda bi, hi, qi: (bi, 0, h + hi)),
          pl.BlockSpec((None, s, 128), lambda bi, hi, qi: (bi, 0, 2 * h + hi)),
          pl.BlockSpec((N_META, 128), lambda bi, hi, qi: (0, h + hi)),
          pl.BlockSpec((N_META, 128), lambda bi, hi, qi: (0, 2 * h + hi)),
          pl.BlockSpec((None, t, 128), lambda bi, hi, qi: (bi, qi, hi)),
          pl.BlockSpec((None, t, t), lambda bi, hi, qi: (hi, 0, 0)),
          pl.BlockSpec((None, t, t), lambda bi, hi, qi: (hi, 0, 0)),
          pl.BlockSpec((None, t, N_META), lambda bi, hi, qi: (hi, 0, 0)),
          pl.BlockSpec((4, A_QK_DIM), lambda bi, hi, qi: (0, 0)),
          pl.BlockSpec((1, A_V_DIM), lambda bi, hi, qi: (0, 0)),
      ],
      out_specs=pl.BlockSpec((None, t, 128), lambda bi, hi, qi: (bi, qi, hi)),
      out_shape=jax.ShapeDtypeStruct((b, s, A_WIDTH), BF16),
      scratch_shapes=[
          pltpu.VMEM((2, t, 1), F32),
          pltpu.VMEM((2, t, 1), F32),
          pltpu.VMEM((2, t, A_V_DIM), F32),
      ],
      compiler_params=pltpu.CompilerParams(
          dimension_semantics=("parallel", "parallel", "arbitrary"),
          vmem_limit_bytes=VMEM_LIMIT),
  )(qkv_x, qkv_x, qkv_x, qkv_m, qkv_m, gz_x, bias_d, bias_s, bias_m, lam_p, subln_g)


def _seg_sum(x):
  lane = lax.broadcasted_iota(jnp.int32, x.shape, 1)
  first = lane < R_HEAD
  lo = jnp.sum(jnp.where(first, x, 0.0), axis=1, keepdims=True)
  hi = jnp.sum(jnp.where(first, 0.0, x), axis=1, keepdims=True)
  return jnp.where(first, lo, hi)


def _dot_hi(a, b):
  return jnp.dot(a, b, preferred_element_type=F32, precision=HI)


def _dot_hi_nt(a, b):
  return lax.dot_general(a, b, (((1,), (1,)), ((), ())), preferred_element_type=F32, precision=HI)


def _dot_hi_tn(a, b):
  return lax.dot_general(a, b, (((0,), (0,)), ((), ())), preferred_element_type=F32, precision=HI)


def _rwkv_kernel(rx_ref, kx_ref, vx_ref, lx_ref, gr_ref, rm_ref, kmt_ref, vmt_ref, lm_ref,
                 pv_ref, mul_ref, wup_ref, aup_ref, o_ref, s_ref, prev_ref, prevl_ref):
  tb, c = RW_TB, RW_C
  nh = R_HEAD
  ti = pl.program_id(2)
  is_meta = ti == 0

  @pl.when(is_meta)
  def _():
    s_ref[...] = jnp.zeros_like(s_ref)
    prev_ref[...] = jnp.zeros_like(prev_ref)
    prevl_ref[...] = jnp.zeros_like(prevl_ref)

  pv = pv_ref[...]
  mu_r, mu_k, mu_v = pv[0:1], pv[1:2], pv[2:3]
  w0, a0, k_k, k_a, r_k, gn_g, gn_b = pv[3:4], pv[4:5], pv[5:6], pv[6:7], pv[7:8], pv[8:9], pv[9:10]

  row = lax.broadcasted_iota(jnp.int32, (tb, 128), 0)
  rowl = lax.broadcasted_iota(jnp.int32, (tb, LORA_PAD), 0)

  def shifted(z, prev, mu, rows):
    z_prev = jnp.where(rows == 0, prev, pltpu.roll(z, 1, 0))
    return z + (z_prev - z) * mu

  z_r = jnp.where(is_meta, rm_ref[...], rx_ref[...])
  z_k = jnp.where(is_meta, kmt_ref[...], kx_ref[...])
  z_v = jnp.where(is_meta, vmt_ref[...], vx_ref[...])
  z_l = jnp.where(is_meta, lm_ref[...], lx_ref[...])
  r = shifted(z_r, prev_ref[0:1], mu_r, row)
  k = shifted(z_k, prev_ref[1:2], mu_k, row)
  v = shifted(z_v, prev_ref[2:3], mu_v, row)
  lo = shifted(z_l, prevl_ref[...], mul_ref[...], rowl)
  prev_ref[0:1] = z_r[tb - 1:tb]
  prev_ref[1:2] = z_k[tb - 1:tb]
  prev_ref[2:3] = z_v[tb - 1:tb]
  prevl_ref[...] = z_l[tb - 1:tb]

  u = -(w0 + _dot_hi(jnp.tanh(lo), wup_ref[...]))
  softplus = jnp.maximum(u, 0.0) + jnp.log(1.0 + jnp.exp(-jnp.abs(u)))
  logw = -jnp.exp(-softplus - 0.5)
  a = 1.0 / (1.0 + jnp.exp(-(a0 + _dot_hi(lo, aup_ref[...]))))
  kk = k * k_k
  kk = kk / jnp.maximum(jnp.sqrt(_seg_sum(kk * kk)), 1e-12)
  k_mod = k * (1.0 + (a - 1.0) * k_a)
  bonus = _seg_sum(r * k_mod * r_k) * v

  ii = lax.broadcasted_iota(jnp.int32, (tb, tb), 0)
  jj = lax.broadcasted_iota(jnp.int32, (tb, tb), 1)
  shift = int(math.log2(c))
  same = lax.shift_right_logical(ii, shift) == lax.shift_right_logical(jj, shift)
  cum = _dot_hi(jnp.where(same & (jj <= ii), 1.0, 0.0), logw)
  tot = _dot_hi(jnp.where(same, 1.0, 0.0), logw)
  p_in = jnp.exp(cum)
  p_inv = jnp.exp(-cum)
  a_t = -kk * jnp.exp(cum - logw)
  b_t = kk * a * p_inv
  k_t = k_mod * p_inv
  r_t = r * p_in
  p_end = jnp.exp(tot - cum)
  b_h = kk * a * p_end
  k_h = k_mod * p_end
  g_diag = jnp.exp(tot)

  ci = lax.broadcasted_iota(jnp.int32, (c, c), 0)
  cj = lax.broadcasted_iota(jnp.int32, (c, c), 1)
  strict = cj < ci
  incl = cj <= ci
  eye = jnp.where(ci == cj, 1.0, 0.0)

  y_rows = []
  for cc in range(tb // c):
    rs = slice(cc * c, (cc + 1) * c)
    y_heads = []
    for e in range(2):
      ls = slice(e * nh, (e + 1) * nh)
      at, bt, kt, rt = a_t[rs, ls], b_t[rs, ls], k_t[rs, ls], r_t[rs, ls]
      bh, kh, vv = b_h[rs, ls], k_h[rs, ls], v[rs, ls]
      nm = jnp.where(strict, _dot_hi_nt(at, bt), 0.0)
      mak = jnp.where(strict, _dot_hi_nt(at, kt), 0.0)
      lb = jnp.where(incl, _dot_hi_nt(rt, bt), 0.0)
      lk = jnp.where(incl, _dot_hi_nt(rt, kt), 0.0)
      tinv = eye + nm
      npow = nm
      for _ in range(5):
        npow = _dot_hi(npow, npow)
        tinv = tinv + _dot_hi(tinv, npow)
      w_m = _dot_hi(tinv, at)
      u0 = _dot_hi(tinv, _dot_hi(mak, vv))
      q_h = rt + _dot_hi(lb, w_m)
      y0 = _dot_hi(lb, u0) + _dot_hi(lk, vv)
      gd = g_diag[cc * c:cc * c + 1, ls]
      g_m = jnp.where(ci == cj, gd, 0.0) + _dot_hi_tn(w_m, bh)
      h_m = _dot_hi_tn(u0, bh) + _dot_hi_tn(vv, kh)
      s_old = s_ref[e]
      y_heads.append(_dot_hi_nt(q_h, s_old) + y0)
      s_ref[e] = _dot_hi(s_old, g_m) + h_m
    y_rows.append(jnp.concatenate(y_heads, axis=1))
  y = jnp.concatenate(y_rows, axis=0)

  mean = _seg_sum(y) * (1.0 / nh)
  yc = y - mean
  var = _seg_sum(yc * yc) * (1.0 / nh)
  yn = yc * lax.rsqrt(var + GN_EPS) * gn_g + gn_b
  g = gr_ref[...]
  o_ref[...] = ((yn + bonus) * (g / (1.0 + jnp.exp(-g)))).astype(o_ref.dtype)


def _rwkv(gz_x, gz_mp, pvec, mu_l, wup, aup):
  b, s, _ = gz_x.shape
  tb = RW_TB
  nt = s // tb + 1
  zr0 = A_WIDTH // 128
  lo_blk = (A_WIDTH + 3 * R_WIDTH) // LORA_PAD
  gr0 = (A_WIDTH + ZR_PAD) // 128

  def xmap(off):
    return lambda bi, hp, ti: (bi, jnp.maximum(ti - 1, 0), off + hp)

  def mmap(off):
    return lambda bi, hp, ti: (0, off + hp)

  return pl.pallas_call(
      _rwkv_kernel,
      grid=(b, R_PAIRS, nt),
      in_specs=[
          pl.BlockSpec((None, tb, 128), xmap(zr0)),
          pl.BlockSpec((None, tb, 128), xmap(zr0 + 8)),
          pl.BlockSpec((None, tb, 128), xmap(zr0 + 16)),
          pl.BlockSpec((None, tb, LORA_PAD), lambda bi, hp, ti: (bi, jnp.maximum(ti - 1, 0), lo_blk)),
          pl.BlockSpec((None, tb, 128), xmap(gr0)),
          pl.BlockSpec((tb, 128), mmap(zr0)),
          pl.BlockSpec((tb, 128), mmap(zr0 + 8)),
          pl.BlockSpec((tb, 128), mmap(zr0 + 16)),
          pl.BlockSpec((tb, LORA_PAD), lambda bi, hp, ti: (0, lo_blk)),
          pl.BlockSpec((16, 128), lambda bi, hp, ti: (0, hp)),
          pl.BlockSpec((1, LORA_PAD), lambda bi, hp, ti: (0, 0)),
          pl.BlockSpec((LORA_PAD, 128), lambda bi, hp, ti: (0, hp)),
          pl.BlockSpec((LORA_PAD, 128), lambda bi, hp, ti: (0, hp)),
      ],
      out_specs=pl.BlockSpec((None, tb, 128), lambda bi, hp, ti: (bi, jnp.maximum(ti - 1, 0), hp)),
      out_shape=jax.ShapeDtypeStruct((b, s, R_WIDTH), BF16),
      scratch_shapes=[
          pltpu.VMEM((2, R_HEAD, R_HEAD), F32),
          pltpu.VMEM((8, 128), F32),
          pltpu.VMEM((1, LORA_PAD), F32),
      ],
      compiler_params=pltpu.CompilerParams(
          dimension_semantics=("parallel", "parallel", "arbitrary"),
          vmem_limit_bytes=VMEM_LIMIT),
  )(gz_x, gz_x, gz_x, gz_x, gz_x, gz_mp, gz_mp, gz_mp, gz_mp, pvec, mu_l, wup, aup)


def _out_kernel(x_ref, oa_ref, or_ref, wa_ref, wr_ref, ge_ref, be_ref, gp_ref, bp_ref, o_ref):
  h = _ln_rows(x_ref[...], ge_ref[...], be_ref[...])
  y = (jnp.dot(oa_ref[...], wa_ref[...], preferred_element_type=F32)
       + jnp.dot(or_ref[...], wr_ref[...], preferred_element_type=F32))
  o_ref[...] = _ln_rows(DEEPNORM_ALPHA * h + y, gp_ref[...], bp_ref[...])


def _out_proj(x2d, oa, orw, wa, wr, ge, be, gp, bp, tm):
  m, d = x2d.shape
  vec = pl.BlockSpec((1, d), lambda i: (0, 0))
  return pl.pallas_call(
      _out_kernel,
      grid=(m // tm,),
      in_specs=[
          pl.BlockSpec((tm, d), lambda i: (i, 0)),
          pl.BlockSpec((tm, A_WIDTH), lambda i: (i, 0)),
          pl.BlockSpec((tm, R_WIDTH), lambda i: (i, 0)),
          pl.BlockSpec((A_WIDTH, d), lambda i: (0, 0)),
          pl.BlockSpec((R_WIDTH, d), lambda i: (0, 0)),
          vec, vec, vec, vec,
      ],
      out_specs=pl.BlockSpec((tm, d), lambda i: (i, 0)),
      out_shape=jax.ShapeDtypeStruct((m, d), F32),
      compiler_params=pltpu.CompilerParams(
          dimension_semantics=("parallel",),
          vmem_limit_bytes=VMEM_LIMIT),
  )(x2d, oa, orw, wa, wr, ge, be, gp, bp)


def kernel(x, meta_tokens, ln_emb_g, ln_emb_b, rel_bias, w_in, w_out, lambda_q1, lambda_k1, lambda_q2,
           lambda_k2, subln_g, rw_mu, rw_w0, rw_w_up, rw_a0, rw_a_up, rw_k_k, rw_k_a, rw_r_k, rw_gn_g,
           rw_gn_b, ln_post_g, ln_post_b):
  b, s, d = x.shape
  l = 0
  wi = w_in[l]
  c_q, c_k, c_v, c_ga = A_WIDTH, 2 * A_WIDTH, 3 * A_WIDTH, 4 * A_WIDTH
  c_zr = c_ga + 3 * R_WIDTH + DECAY_LORA + ICLR_LORA
  w_qkv = jnp.concatenate([wi[:, :c_q] * (A_QK_DIM ** -0.5), wi[:, c_q:c_v]], axis=1).astype(BF16)
  lora_pad = LORA_PAD - DECAY_LORA - ICLR_LORA
  w_gz = jnp.concatenate(
      [wi[:, c_v:c_zr], jnp.zeros((d, lora_pad), wi.dtype), wi[:, c_zr:]], axis=1).astype(BF16)

  ge, be = ln_emb_g.reshape(1, d), ln_emb_b.reshape(1, d)
  x2d = x.reshape(b * s, d)
  qkv_x = _ln_matmul(x2d, ge, be, w_qkv, BF16, 1024, 768).reshape(b, s, 3 * A_WIDTH)
  gz_x = _ln_matmul(x2d, ge, be, w_gz, F32, 1024, 768).reshape(b, s, w_gz.shape[1])
  qkv_m = _ln_matmul(meta_tokens, ge, be, w_qkv, BF16, N_META, 768)
  gz_m = _ln_matmul(meta_tokens, ge, be, w_gz, F32, N_META, 768)

  bias_d, bias_s, bias_m = _bias_tiles(rel_bias)
  lam_p = jnp.stack([lambda_q1[l], lambda_k1[l], lambda_q2[l], lambda_k2[l]], axis=0)
  o_attn = _attention(qkv_x, qkv_m, gz_x, bias_d, bias_s, bias_m, lam_p, subln_g[l].reshape(1, A_V_DIM))

  mu = rw_mu[l]
  zeros = jnp.zeros((R_WIDTH,), F32)
  pvec = jnp.stack([mu[:R_WIDTH], mu[R_WIDTH:2 * R_WIDTH], mu[2 * R_WIDTH:3 * R_WIDTH], rw_w0[l], rw_a0[l],
                    rw_k_k[l], rw_k_a[l], rw_r_k[l].reshape(R_WIDTH), rw_gn_g[l], rw_gn_b[l]]
                   + [zeros] * 6, axis=0)
  mu_l = jnp.pad(mu[3 * R_WIDTH:], (0, lora_pad)).reshape(1, LORA_PAD)
  wup = jnp.pad(rw_w_up[l], ((0, LORA_PAD - DECAY_LORA), (0, 0)))
  aup = jnp.pad(rw_a_up[l], ((DECAY_LORA, lora_pad), (0, 0)))
  gz_mp = jnp.pad(gz_m, ((RW_TB - N_META, 0), (0, 0)))
  o_rwkv = _rwkv(gz_x, gz_mp, pvec, mu_l, wup, aup)

  wo = w_out[l].astype(BF16)
  out = _out_proj(x2d, o_attn.reshape(b * s, A_WIDTH), o_rwkv.reshape(b * s, R_WIDTH),
                  wo[:A_WIDTH], wo[A_WIDTH:], ge, be,
                  ln_post_g[l].reshape(1, d), ln_post_b[l].reshape(1, d), 512)
  return out.reshape(b, s, d)
```

```python
import functools
import math

import numpy as np
import jax
import jax.numpy as jnp
from jax import lax
from jax.experimental import pallas as pl
from jax.experimental.pallas import tpu as pltpu

D_MODEL = 2048
N_META = 16
A_HEADS = 8
A_V_DIM = 128
A_QK_DIM = 64
A_WIDTH = A_HEADS * A_V_DIM
R_HEAD = 64
R_WIDTH = 1024
R_PAIRS = R_WIDTH // (2 * R_HEAD)
DECAY_LORA = 96
ICLR_LORA = 96
LORA_PAD = 256
N_BUCKETS = 32
MAX_DISTANCE = 128
LN_EPS = 1e-5
SUBLN_EPS = 1e-5
GN_EPS = 64e-5
DEPTH = 1
DEEPNORM_ALPHA = (2 * DEPTH) ** 0.25
LAM_INIT = 0.8 - 0.6 * math.exp(-0.3 * 0)
NEG = -1e30

ATT_T = 256
RW_TB = 128
RW_C = 64
RW_P = 4
VMEM_LIMIT = 48 * 1024 * 1024

F32 = jnp.float32
BF16 = jnp.bfloat16


def _ln_rows(x, g, b):
  mu = jnp.mean(x, axis=-1, keepdims=True)
  xc = x - mu
  var = jnp.mean(xc * xc, axis=-1, keepdims=True)
  return xc * lax.rsqrt(var + LN_EPS) * g + b


def _ln_mm_kernel(x_ref, g_ref, b_ref, w_ref, o_ref, hn_ref):
  @pl.when(pl.program_id(1) == 0)
  def _():
    hn_ref[...] = _ln_rows(x_ref[...], g_ref[...], b_ref[...]).astype(BF16)

  o_ref[...] = jnp.dot(hn_ref[...], w_ref[...], preferred_element_type=F32).astype(o_ref.dtype)


def _ln_matmul(x2d, g, b, w, out_dtype, tm, tn):
  m, d = x2d.shape
  n = w.shape[1]
  return pl.pallas_call(
      _ln_mm_kernel,
      name="ln_inproj",
      grid=(m // tm, n // tn),
      in_specs=[
          pl.BlockSpec((tm, d), lambda i, j: (i, 0)),
          pl.BlockSpec((1, d), lambda i, j: (0, 0)),
          pl.BlockSpec((1, d), lambda i, j: (0, 0)),
          pl.BlockSpec((d, tn), lambda i, j: (0, j)),
      ],
      out_specs=pl.BlockSpec((tm, tn), lambda i, j: (i, j)),
      out_shape=jax.ShapeDtypeStruct((m, n), out_dtype),
      scratch_shapes=[pltpu.VMEM((tm, d), BF16)],
      compiler_params=pltpu.CompilerParams(
          dimension_semantics=("parallel", "arbitrary"),
          vmem_limit_bytes=VMEM_LIMIT),
  )(x2d, g, b, w)


def _bucket_thresholds():
  n = np.arange(0, 4 * MAX_DISTANCE, dtype=np.int64)
  max_exact = N_BUCKETS // 2
  nf = np.maximum(n, 1).astype(np.float32)
  large = max_exact + (np.log(nf / np.float32(max_exact)) / np.float32(math.log(MAX_DISTANCE / max_exact))
                       * np.float32(N_BUCKETS - max_exact)).astype(np.int32)
  large = np.minimum(large, N_BUCKETS - 1)
  bucket = np.where(n < max_exact, n, large)
  assert np.all(np.diff(bucket) >= 0) and bucket[-1] == N_BUCKETS - 1
  return [int(np.argmax(bucket >= b)) for b in range(N_BUCKETS)]


_THR = _bucket_thresholds()


def _bias_kernel(rb_ref, diag_ref, sub_ref, meta_ref):
  h = pl.program_id(0)
  far = rb_ref[N_BUCKETS - 1, h]

  def bias_of(n):
    out = jnp.full(n.shape, rb_ref[0, h] - far, F32)
    for b in range(1, N_BUCKETS):
      out = jnp.where(n >= _THR[b], rb_ref[b, h] - far, out)
    return out

  t = ATT_T
  qi = lax.broadcasted_iota(jnp.int32, (t, t), 0)
  kj = lax.broadcasted_iota(jnp.int32, (t, t), 1)
  d = qi - kj
  diag_ref[...] = jnp.where(d >= 0, bias_of(d), NEG)
  sub_ref[...] = bias_of(d + t)
  qm = lax.broadcasted_iota(jnp.int32, (t, N_META), 0)
  km = lax.broadcasted_iota(jnp.int32, (t, N_META), 1)
  meta_ref[...] = bias_of(qm - km + N_META)


def _bias_tiles(rel_bias):
  t = ATT_T
  return pl.pallas_call(
      _bias_kernel,
      name="bias_tiles",
      grid=(A_HEADS,),
      in_specs=[pl.BlockSpec(memory_space=pltpu.SMEM)],
      out_specs=[
          pl.BlockSpec((None, t, t), lambda h: (h, 0, 0)),
          pl.BlockSpec((None, t, t), lambda h: (h, 0, 0)),
          pl.BlockSpec((None, t, N_META), lambda h: (h, 0, 0)),
      ],
      out_shape=[
          jax.ShapeDtypeStruct((A_HEADS, t, t), F32),
          jax.ShapeDtypeStruct((A_HEADS, t, t), F32),
          jax.ShapeDtypeStruct((A_HEADS, t, N_META), F32),
      ],
  )(rel_bias)


def _dot_nt(a, b):
  return lax.dot_general(a, b, (((1,), (1,)), ((), ())), preferred_element_type=F32)


def _attn_kernel(q_ref, kx_ref, vx_ref, km_ref, vm_ref, ga_ref, bd_ref, bs_ref, bm_ref,
                 lam_ref, sg_ref, o_ref, m_ref, l_ref, acc_ref):
  t = ATT_T
  qi = pl.program_id(2)
  q = q_ref[...]
  lane = lax.broadcasted_iota(jnp.int32, q.shape, 1)
  zero = jnp.zeros_like(q)
  qs = (jnp.where(lane < A_QK_DIM, q, zero), jnp.where(lane >= A_QK_DIM, q, zero))

  def init(mp, s, v):
    m = jnp.max(s, axis=1, keepdims=True)
    p = jnp.exp(s - m)
    m_ref[mp] = m
    l_ref[mp] = jnp.sum(p, axis=1, keepdims=True)
    acc_ref[mp] = jnp.dot(p.astype(BF16), v, preferred_element_type=F32)

  def update(mp, s, v):
    m_prev = m_ref[mp]
    m_new = jnp.maximum(m_prev, jnp.max(s, axis=1, keepdims=True))
    alpha = jnp.exp(m_prev - m_new)
    p = jnp.exp(s - m_new)
    m_ref[mp] = m_new
    l_ref[mp] = alpha * l_ref[mp] + jnp.sum(p, axis=1, keepdims=True)
    acc_ref[mp] = alpha * acc_ref[mp] + jnp.dot(p.astype(BF16), v, preferred_element_type=F32)

  km = km_ref[...]
  vm = vm_ref[...]
  bm = jnp.where(qi == 0, bm_ref[...], 0.0)
  for mp in range(2):
    init(mp, _dot_nt(qs[mp], km) + bm, vm)

  def far_body(j, carry):
    off = pl.multiple_of(j * t, t)
    k = kx_ref[pl.ds(off, t), :]
    v = vx_ref[pl.ds(off, t), :]
    for mp in range(2):
      update(mp, _dot_nt(qs[mp], k), v)
    return carry

  lax.fori_loop(0, jnp.maximum(qi - 1, 0), far_body, 0)

  @pl.when(qi >= 1)
  def _():
    off = pl.multiple_of((qi - 1) * t, t)
    k = kx_ref[pl.ds(off, t), :]
    v = vx_ref[pl.ds(off, t), :]
    bs = bs_ref[...]
    for mp in range(2):
      update(mp, _dot_nt(qs[mp], k) + bs, v)

  off = pl.multiple_of(qi * t, t)
  k = kx_ref[pl.ds(off, t), :]
  v = vx_ref[pl.ds(off, t), :]
  bd = bd_ref[...]
  for mp in range(2):
    update(mp, _dot_nt(qs[mp], k) + bd, v)

  lp = lam_ref[...]
  lam = (jnp.exp(jnp.sum(lp[0:1] * lp[1:2], axis=1, keepdims=True))
         - jnp.exp(jnp.sum(lp[2:3] * lp[3:4], axis=1, keepdims=True)) + LAM_INIT)
  o = acc_ref[0] / l_ref[0] - lam * (acc_ref[1] / l_ref[1])
  o = o * lax.rsqrt(jnp.mean(o * o, axis=1, keepdims=True) + SUBLN_EPS) * sg_ref[...]
  o = o * (1.0 - LAM_INIT)
  g = ga_ref[...]
  o_ref[...] = (o * (g / (1.0 + jnp.exp(-g)))).astype(o_ref.dtype)


def _attention(qkv_x, qkv_m, gz_x, bias_d, bias_s, bias_m, lam_p, subln_g):
  b, s, _ = qkv_x.shape
  t = ATT_T
  h = A_HEADS
  return pl.pallas_call(
      _attn_kernel,
      name="diff_attn",
      grid=(b, h, s // t),
      in_specs=[
          pl.BlockSpec((None, t, 128), lambda bi, hi, qi: (bi, qi, hi)),
          pl.BlockSpec((None, s, 128), lambda bi, hi, qi: (bi, 0, h + hi)),
          pl.BlockSpec((None, s, 128), lambda bi, hi, qi: (bi, 0, 2 * h + hi)),
          pl.BlockSpec((N_META, 128), lambda bi, hi, qi: (0, h + hi)),
          pl.BlockSpec((N_META, 128), lambda bi, hi, qi: (0, 2 * h + hi)),
          pl.BlockSpec((None, t, 128), lambda bi, hi, qi: (bi, qi, hi)),
          pl.BlockSpec((None, t, t), lambda bi, hi, qi: (hi, 0, 0)),
          pl.BlockSpec((None, t, t), lambda bi, hi, qi: (hi, 0, 0)),
          pl.BlockSpec((None, t, N_META), lambda bi, hi, qi: (hi, 0, 0)),
          pl.BlockSpec((4, A_QK_DIM), lambda bi, hi, qi: (0, 0)),
          pl.BlockSpec((1, A_V_DIM), lambda bi, hi, qi: (0, 0)),
      ],
      out_specs=pl.BlockSpec((None, t, 128), lambda bi, hi, qi: (bi, qi, hi)),
      out_shape=jax.ShapeDtypeStruct((b, s, A_WIDTH), BF16),
      scratch_shapes=[
          pltpu.VMEM((2, t, 1), F32),
          pltpu.VMEM((2, t, 1), F32),
          pltpu.VMEM((2, t, A_V_DIM), F32),
      ],
      compiler_params=pltpu.CompilerParams(
          dimension_semantics=("parallel", "parallel", "arbitrary"),
          vmem_limit_bytes=VMEM_LIMIT),
  )(qkv_x, qkv_x, qkv_x, qkv_m, qkv_m, gz_x, bias_d, bias_s, bias_m, lam_p, subln_g)


def _seg_sum(x):
  lane = lax.broadcasted_iota(jnp.int32, x.shape, 1)
  first = lane < R_HEAD
  lo = jnp.sum(jnp.where(first, x, 0.0), axis=1, keepdims=True)
  hi = jnp.sum(jnp.where(first, 0.0, x), axis=1, keepdims=True)
  return jnp.where(first, lo, hi)


def _split_bf16(x):
  hi = x.astype(BF16)
  return hi, x - hi.astype(F32)


def _dot_tn(a, b):
  return lax.dot_general(a, b, (((0,), (0,)), ((), ())), preferred_element_type=F32)


def _bdot(a, b):
  return lax.dot_general(a, b, (((2,), (1,)), ((0,), (0,))), preferred_element_type=F32)


def _bdot_nt(a, b):
  return lax.dot_general(a, b, (((2,), (2,)), ((0,), (0,))), preferred_element_type=F32)


def _rwkv_kernel(rx_ref, kx_ref, vx_ref, lx_ref, gr_ref, rm_ref, kmt_ref, vmt_ref, lm_ref,
                 pv_ref, mul_ref, wuph_ref, wupl_ref, aup_ref, o_ref, s_ref, prev_ref, prevl_ref):
  tb, c = RW_TB, RW_C
  nh = R_HEAD
  c2 = 2 * c
  ti = pl.program_id(2)
  is_meta = ti == 0

  @pl.when(is_meta)
  def _():
    s_ref[...] = jnp.zeros_like(s_ref)
    prev_ref[...] = jnp.zeros_like(prev_ref)
    prevl_ref[...] = jnp.zeros_like(prevl_ref)

  row = lax.broadcasted_iota(jnp.int32, (tb, 128), 0)
  rowl = lax.broadcasted_iota(jnp.int32, (tb, LORA_PAD), 0)

  def shifted(z, prev, mu, rows):
    z_prev = jnp.where(rows == 0, prev, pltpu.roll(z, 1, 0))
    return z + (z_prev - z) * mu

  z_l = jnp.where(is_meta, lm_ref[...], lx_ref[...])
  lo = shifted(z_l, prevl_ref[...], mul_ref[...], rowl)
  prevl_ref[...] = z_l[tb - 1:tb]
  th_h, th_l = _split_bf16(jnp.tanh(lo))
  w_lora = (jnp.dot(th_h, wuph_ref[...], preferred_element_type=F32)
            + jnp.dot(th_h, wupl_ref[...], preferred_element_type=F32)
            + jnp.dot(th_l.astype(BF16), wuph_ref[...], preferred_element_type=F32))
  a_lora = jnp.dot(lo.astype(BF16), aup_ref[...], preferred_element_type=F32)

  ii = lax.broadcasted_iota(jnp.int32, (tb, tb), 0)
  jj = lax.broadcasted_iota(jnp.int32, (tb, tb), 1)
  shift = int(math.log2(c))
  same = lax.shift_right_logical(ii, shift) == lax.shift_right_logical(jj, shift)
  cum_op = jnp.where(same, jnp.where(jj <= ii, 1.0, 0.0), 0.0).astype(BF16)

  ci = lax.broadcasted_iota(jnp.int32, (c2, c2), 0)
  cj = lax.broadcasted_iota(jnp.int32, (c2, c2), 1)
  diag = ci == cj
  strict2 = jnp.concatenate([cj < ci, cj < ci], axis=1)
  incl2 = jnp.concatenate([cj <= ci, cj <= ci], axis=1)
  first = lax.broadcasted_iota(jnp.int32, (c, 128), 1) < nh

  def stack(x):
    return jnp.concatenate([jnp.where(first, x, 0.0), jnp.where(first, 0.0, x)], axis=0)

  ncc = tb // c
  chains = {name: [] for name in ("at", "rt", "bt", "kt", "bh", "kh", "vv", "gd")}
  post = []
  for p in range(RW_P):
    ls = slice(p * 128, (p + 1) * 128)
    pv = pv_ref[:, ls]
    mu_r, mu_k, mu_v = pv[0:1], pv[1:2], pv[2:3]
    w0, a0, k_k, k_a, r_k, gn_g, gn_b = pv[3:4], pv[4:5], pv[5:6], pv[6:7], pv[7:8], pv[8:9], pv[9:10]

    z_r = jnp.where(is_meta, rm_ref[:, ls], rx_ref[:, ls])
    z_k = jnp.where(is_meta, kmt_ref[:, ls], kx_ref[:, ls])
    z_v = jnp.where(is_meta, vmt_ref[:, ls], vx_ref[:, ls])
    r = shifted(z_r, prev_ref[0:1, ls], mu_r, row)
    k = shifted(z_k, prev_ref[1:2, ls], mu_k, row)
    v = shifted(z_v, prev_ref[2:3, ls], mu_v, row)
    prev_ref[0:1, ls] = z_r[tb - 1:tb]
    prev_ref[1:2, ls] = z_k[tb - 1:tb]
    prev_ref[2:3, ls] = z_v[tb - 1:tb]

    u = -(w0 + w_lora[:, ls])
    softplus = jnp.maximum(u, 0.0) + jnp.log(1.0 + jnp.exp(-jnp.abs(u)))
    logw = -jnp.exp(-softplus - 0.5)
    a = 1.0 / (1.0 + jnp.exp(-(a0 + a_lora[:, ls])))
    kk = k * k_k
    kk = kk / jnp.maximum(jnp.sqrt(_seg_sum(kk * kk)), 1e-12)
    k_mod = k * (1.0 + (a - 1.0) * k_a)
    bonus = _seg_sum(r * k_mod * r_k) * v

    lw_h, lw_r = _split_bf16(logw)
    lw_m, lw_l = _split_bf16(lw_r)
    cum3 = jnp.dot(cum_op, jnp.concatenate([lw_h, lw_m, lw_l.astype(BF16)], axis=1),
                   preferred_element_type=F32)
    cum = cum3[:, :128] + cum3[:, 128:256] + cum3[:, 256:]
    tot = jnp.concatenate([jnp.broadcast_to(cum[cc * c + c - 1:cc * c + c], (c, 128)) for cc in range(ncc)],
                          axis=0)
    p_inv = jnp.exp(-cum)
    a_t = -kk * jnp.exp(cum - logw)
    b_t = kk * a * p_inv
    k_t = k_mod * p_inv
    r_t = r * jnp.exp(cum)
    p_end = jnp.exp(tot - cum)
    b_h = kk * a * p_end
    k_h = k_mod * p_end
    g_diag = jnp.exp(tot)

    for cc in range(ncc):
      rs = slice(cc * c, (cc + 1) * c)
      for name, val in (("at", a_t), ("rt", r_t), ("bt", b_t), ("kt", k_t), ("bh", b_h), ("kh", k_h),
                        ("vv", v)):
        chains[name].append(stack(val[rs]))
      chains["gd"].append(g_diag[cc * c:cc * c + 1])
    post.append((bonus, gn_g, gn_b))

  nb = RW_P * ncc
  at, rt, bt, kt, bh, kh, vv = (jnp.stack(chains[name]) for name in ("at", "rt", "bt", "kt", "bh", "kh", "vv"))
  at_b, vv_b, bh_b = at.astype(BF16), vv.astype(BF16), bh.astype(BF16)
  bk = jnp.concatenate([bt, kt], axis=1).astype(BF16)
  top = jnp.where(strict2, _bdot_nt(at_b, bk), 0.0)
  lblk = jnp.where(incl2, _bdot_nt(rt.astype(BF16), bk), 0.0)
  nm, mak = top[:, :, :c2], top[:, :, c2:]
  tinv = jnp.where(diag, 1.0, nm)
  npow = nm.astype(BF16)
  for _ in range(5):
    npow = _bdot(npow, npow).astype(BF16)
    tinv = tinv + _bdot(tinv.astype(BF16), npow)
  x1 = _bdot(mak.astype(BF16), vv_b)
  wu_b = _bdot(tinv.astype(BF16), jnp.concatenate([at_b, x1.astype(BF16)], axis=2)).astype(BF16)
  rhs = jnp.concatenate([wu_b, jnp.concatenate([jnp.zeros_like(vv_b), vv_b], axis=2)], axis=1)
  qy = _bdot(lblk.astype(BF16), rhs)
  q_h = (rt + qy[:, :, :c2]).astype(BF16)
  y0 = qy[:, :, c2:]
  uv = jnp.concatenate([wu_b[:, :, c2:], vv_b], axis=1)
  bkh = jnp.concatenate([bh_b, kh.astype(BF16)], axis=1)
  g_m = [(jnp.where(diag, chains["gd"][n], 0.0) + _dot_tn(wu_b[n, :, :c2], bh_b[n])).astype(BF16)
         for n in range(nb)]
  h_m = [_dot_tn(uv[n], bkh[n]) for n in range(nb)]

  states = [s_ref[p] for p in range(RW_P)]
  y_rows = [[] for _ in range(RW_P)]
  for cc in range(ncc):
    for p in range(RW_P):
      n = p * ncc + cc
      s_old_b = states[p].astype(BF16)
      y2 = _dot_nt(q_h[n], s_old_b) + y0[n]
      states[p] = jnp.dot(s_old_b, g_m[n], preferred_element_type=F32) + h_m[n]
      y_rows[p].append(y2[:c] + y2[c:])

  for p in range(RW_P):
    ls = slice(p * 128, (p + 1) * 128)
    s_ref[p] = states[p]
    bonus, gn_g, gn_b = post[p]
    y = jnp.concatenate(y_rows[p], axis=0)
    mean = _seg_sum(y) * (1.0 / nh)
    yc = y - mean
    var = _seg_sum(yc * yc) * (1.0 / nh)
    yn = yc * lax.rsqrt(var + GN_EPS) * gn_g + gn_b
    g = gr_ref[:, ls]
    o_ref[:, ls] = ((yn + bonus) * (g / (1.0 + jnp.exp(-g)))).astype(o_ref.dtype)


def _rwkv(gz_x, gz_mp, pvec, mu_l, wup_h, wup_l, aup):
  b, s, _ = gz_x.shape
  tb = RW_TB
  nt = s // tb + 1
  pw = 128 * RW_P
  zr0 = A_WIDTH // pw
  rw = R_WIDTH // pw
  gr0 = zr0 + 3 * rw
  lo_blk = (A_WIDTH + 4 * R_WIDTH) // LORA_PAD

  def xmap(off):
    return lambda bi, hp, ti: (bi, jnp.maximum(ti - 1, 0), off + hp)

  def mmap(off):
    return lambda bi, hp, ti: (0, off + hp)

  return pl.pallas_call(
      _rwkv_kernel,
      name="rwkv7",
      grid=(b, R_PAIRS // RW_P, nt),
      in_specs=[
          pl.BlockSpec((None, tb, pw), xmap(zr0)),
          pl.BlockSpec((None, tb, pw), xmap(zr0 + rw)),
          pl.BlockSpec((None, tb, pw), xmap(zr0 + 2 * rw)),
          pl.BlockSpec((None, tb, LORA_PAD), lambda bi, hp, ti: (bi, jnp.maximum(ti - 1, 0), lo_blk)),
          pl.BlockSpec((None, tb, pw), xmap(gr0)),
          pl.BlockSpec((tb, pw), mmap(zr0)),
          pl.BlockSpec((tb, pw), mmap(zr0 + rw)),
          pl.BlockSpec((tb, pw), mmap(zr0 + 2 * rw)),
          pl.BlockSpec((tb, LORA_PAD), lambda bi, hp, ti: (0, lo_blk)),
          pl.BlockSpec((16, pw), lambda bi, hp, ti: (0, hp)),
          pl.BlockSpec((1, LORA_PAD), lambda bi, hp, ti: (0, 0)),
          pl.BlockSpec((LORA_PAD, pw), lambda bi, hp, ti: (0, hp)),
          pl.BlockSpec((LORA_PAD, pw), lambda bi, hp, ti: (0, hp)),
          pl.BlockSpec((LORA_PAD, pw), lambda bi, hp, ti: (0, hp)),
      ],
      out_specs=pl.BlockSpec((None, tb, pw), lambda bi, hp, ti: (bi, jnp.maximum(ti - 1, 0), hp)),
      out_shape=jax.ShapeDtypeStruct((b, s, R_WIDTH), BF16),
      scratch_shapes=[
          pltpu.VMEM((RW_P, 2 * R_HEAD, 2 * R_HEAD), F32),
          pltpu.VMEM((8, pw), F32),
          pltpu.VMEM((1, LORA_PAD), F32),
      ],
      compiler_params=pltpu.CompilerParams(
          dimension_semantics=("parallel", "parallel", "arbitrary"),
          vmem_limit_bytes=VMEM_LIMIT),
  )(gz_x, gz_x, gz_x, gz_x, gz_x, gz_mp, gz_mp, gz_mp, gz_mp, pvec, mu_l, wup_h, wup_l, aup)


def _out_kernel(x_ref, oa_ref, or_ref, wa_ref, wr_ref, ge_ref, be_ref, gp_ref, bp_ref, o_ref):
  h = _ln_rows(x_ref[...], ge_ref[...], be_ref[...])
  y = (jnp.dot(oa_ref[...], wa_ref[...], preferred_element_type=F32)
       + jnp.dot(or_ref[...], wr_ref[...], preferred_element_type=F32))
  o_ref[...] = _ln_rows(DEEPNORM_ALPHA * h + y, gp_ref[...], bp_ref[...])


def _out_proj(x2d, oa, orw, wa, wr, ge, be, gp, bp, tm):
  m, d = x2d.shape
  vec = pl.BlockSpec((1, d), lambda i: (0, 0))
  return pl.pallas_call(
      _out_kernel,
      name="out_proj",
      grid=(m // tm,),
      in_specs=[
          pl.BlockSpec((tm, d), lambda i: (i, 0)),
          pl.BlockSpec((tm, A_WIDTH), lambda i: (i, 0)),
          pl.BlockSpec((tm, R_WIDTH), lambda i: (i, 0)),
          pl.BlockSpec((A_WIDTH, d), lambda i: (0, 0)),
          pl.BlockSpec((R_WIDTH, d), lambda i: (0, 0)),
          vec, vec, vec, vec,
      ],
      out_specs=pl.BlockSpec((tm, d), lambda i: (i, 0)),
      out_shape=jax.ShapeDtypeStruct((m, d), F32),
      compiler_params=pltpu.CompilerParams(
          dimension_semantics=("parallel",),
          vmem_limit_bytes=VMEM_LIMIT),
  )(x2d, oa, orw, wa, wr, ge, be, gp, bp)


def kernel(x, meta_tokens, ln_emb_g, ln_emb_b, rel_bias, w_in, w_out, lambda_q1, lambda_k1, lambda_q2,
           lambda_k2, subln_g, rw_mu, rw_w0, rw_w_up, rw_a0, rw_a_up, rw_k_k, rw_k_a, rw_r_k, rw_gn_g,
           rw_gn_b, ln_post_g, ln_post_b):
  b, s, d = x.shape
  l = 0
  wi = w_in[l]
  c_q, c_k, c_v, c_ga = A_WIDTH, 2 * A_WIDTH, 3 * A_WIDTH, 4 * A_WIDTH
  c_zr = c_ga + 3 * R_WIDTH + DECAY_LORA + ICLR_LORA
  w_qkv = jnp.concatenate([wi[:, :c_q] * (A_QK_DIM ** -0.5), wi[:, c_q:c_v]], axis=1).astype(BF16)
  lora_pad = LORA_PAD - DECAY_LORA - ICLR_LORA
  c_lo = c_ga + 3 * R_WIDTH
  w_gz = jnp.concatenate(
      [wi[:, c_v:c_lo], wi[:, c_zr:], wi[:, c_lo:c_zr], jnp.zeros((d, lora_pad), wi.dtype)],
      axis=1).astype(BF16)

  ge, be = ln_emb_g.reshape(1, d), ln_emb_b.reshape(1, d)
  x2d = x.reshape(b * s, d)
  qkv_x = _ln_matmul(x2d, ge, be, w_qkv, BF16, 1024, 768).reshape(b, s, 3 * A_WIDTH)
  gz_x = _ln_matmul(x2d, ge, be, w_gz, F32, 1024, 768).reshape(b, s, w_gz.shape[1])
  qkv_m = _ln_matmul(meta_tokens, ge, be, w_qkv, BF16, N_META, 768)
  gz_m = _ln_matmul(meta_tokens, ge, be, w_gz, F32, N_META, 768)

  bias_d, bias_s, bias_m = _bias_tiles(rel_bias)
  lam_p = jnp.stack([lambda_q1[l], lambda_k1[l], lambda_q2[l], lambda_k2[l]], axis=0)
  o_attn = _attention(qkv_x, qkv_m, gz_x, bias_d, bias_s, bias_m, lam_p, subln_g[l].reshape(1, A_V_DIM))

  mu = rw_mu[l]
  zeros = jnp.zeros((R_WIDTH,), F32)
  pvec = jnp.stack([mu[:R_WIDTH], mu[R_WIDTH:2 * R_WIDTH], mu[2 * R_WIDTH:3 * R_WIDTH], rw_w0[l], rw_a0[l],
                    rw_k_k[l], rw_k_a[l], rw_r_k[l].reshape(R_WIDTH), rw_gn_g[l], rw_gn_b[l]]
                   + [zeros] * 6, axis=0)
  mu_l = jnp.pad(mu[3 * R_WIDTH:], (0, lora_pad)).reshape(1, LORA_PAD)
  wup = jnp.pad(rw_w_up[l], ((0, LORA_PAD - DECAY_LORA), (0, 0)))
  wup_h = wup.astype(BF16)
  wup_l = (wup - wup_h.astype(F32)).astype(BF16)
  aup = jnp.pad(rw_a_up[l], ((DECAY_LORA, lora_pad), (0, 0))).astype(BF16)
  gz_mp = jnp.pad(gz_m, ((RW_TB - N_META, 0), (0, 0)))
  o_rwkv = _rwkv(gz_x, gz_mp, pvec, mu_l, wup_h, wup_l, aup)

  wo = w_out[l].astype(BF16)
  out = _out_proj(x2d, o_attn.reshape(b * s, A_WIDTH), o_rwkv.reshape(b * s, R_WIDTH),
                  wo[:A_WIDTH], wo[A_WIDTH:], ge, be,
                  ln_post_g[l].reshape(1, d), ln_post_b[l].reshape(1, d), 512)
  return out.reshape(b, s, d)
```

```python
import functools
import math

import numpy as np
import jax
import jax.numpy as jnp
from jax import lax
from jax.experimental import pallas as pl
from jax.experimental.pallas import tpu as pltpu

D_MODEL = 2048
N_META = 16
A_HEADS = 8
A_V_DIM = 128
A_QK_DIM = 64
A_WIDTH = A_HEADS * A_V_DIM
R_HEAD = 64
R_WIDTH = 1024
R_PAIRS = R_WIDTH // (2 * R_HEAD)
DECAY_LORA = 96
ICLR_LORA = 96
LORA_PAD = 256
N_BUCKETS = 32
MAX_DISTANCE = 128
LN_EPS = 1e-5
SUBLN_EPS = 1e-5
GN_EPS = 64e-5
DEPTH = 1
DEEPNORM_ALPHA = (2 * DEPTH) ** 0.25
LAM_INIT = 0.8 - 0.6 * math.exp(-0.3 * 0)
NEG = -1e30

ATT_T = 256
ATT_G = 4
RW_TB = 128
RW_C = 64
RW_P = 4
VMEM_LIMIT = 48 * 1024 * 1024

F32 = jnp.float32
BF16 = jnp.bfloat16


def _ln_rows(x, g, b):
  mu = jnp.mean(x, axis=-1, keepdims=True)
  xc = x - mu
  var = jnp.mean(xc * xc, axis=-1, keepdims=True)
  return xc * lax.rsqrt(var + LN_EPS) * g + b


def _ln_mm_kernel(x_ref, g_ref, b_ref, w_ref, o_ref, hn_ref):
  @pl.when(pl.program_id(1) == 0)
  def _():
    hn_ref[...] = _ln_rows(x_ref[...], g_ref[...], b_ref[...]).astype(BF16)

  o_ref[...] = jnp.dot(hn_ref[...], w_ref[...], preferred_element_type=F32).astype(o_ref.dtype)


def _ln_matmul(x2d, g, b, w, out_dtype, tm, tn):
  m, d = x2d.shape
  n = w.shape[1]
  return pl.pallas_call(
      _ln_mm_kernel,
      name="ln_inproj",
      grid=(m // tm, n // tn),
      in_specs=[
          pl.BlockSpec((tm, d), lambda i, j: (i, 0)),
          pl.BlockSpec((1, d), lambda i, j: (0, 0)),
          pl.BlockSpec((1, d), lambda i, j: (0, 0)),
          pl.BlockSpec((d, tn), lambda i, j: (0, j)),
      ],
      out_specs=pl.BlockSpec((tm, tn), lambda i, j: (i, j)),
      out_shape=jax.ShapeDtypeStruct((m, n), out_dtype),
      scratch_shapes=[pltpu.VMEM((tm, d), BF16)],
      compiler_params=pltpu.CompilerParams(
          dimension_semantics=("parallel", "arbitrary"),
          vmem_limit_bytes=VMEM_LIMIT),
  )(x2d, g, b, w)


def _bucket_thresholds():
  n = np.arange(0, 4 * MAX_DISTANCE, dtype=np.int64)
  max_exact = N_BUCKETS // 2
  nf = np.maximum(n, 1).astype(np.float32)
  large = max_exact + (np.log(nf / np.float32(max_exact)) / np.float32(math.log(MAX_DISTANCE / max_exact))
                       * np.float32(N_BUCKETS - max_exact)).astype(np.int32)
  large = np.minimum(large, N_BUCKETS - 1)
  bucket = np.where(n < max_exact, n, large)
  assert np.all(np.diff(bucket) >= 0) and bucket[-1] == N_BUCKETS - 1
  return [int(np.argmax(bucket >= b)) for b in range(N_BUCKETS)]


_THR = _bucket_thresholds()


def _bias_kernel(rb_ref, diag_ref, sub_ref, meta_ref):
  h = pl.program_id(0)
  far = rb_ref[N_BUCKETS - 1, h]

  def bias_of(n):
    out = jnp.full(n.shape, rb_ref[0, h] - far, F32)
    for b in range(1, N_BUCKETS):
      out = jnp.where(n >= _THR[b], rb_ref[b, h] - far, out)
    return out

  t = ATT_T
  qi = lax.broadcasted_iota(jnp.int32, (t, t), 0)
  kj = lax.broadcasted_iota(jnp.int32, (t, t), 1)
  d = qi - kj
  diag_ref[...] = jnp.where(d >= 0, bias_of(d), NEG)
  sub_ref[...] = bias_of(d + t)
  qm = lax.broadcasted_iota(jnp.int32, (t, N_META), 0)
  km = lax.broadcasted_iota(jnp.int32, (t, N_META), 1)
  meta_ref[...] = bias_of(qm - km + N_META)


def _bias_tiles(rel_bias):
  t = ATT_T
  return pl.pallas_call(
      _bias_kernel,
      name="bias_tiles",
      grid=(A_HEADS,),
      in_specs=[pl.BlockSpec(memory_space=pltpu.SMEM)],
      out_specs=[
          pl.BlockSpec((None, t, t), lambda h: (h, 0, 0)),
          pl.BlockSpec((None, t, t), lambda h: (h, 0, 0)),
          pl.BlockSpec((None, t, N_META), lambda h: (h, 0, 0)),
      ],
      out_shape=[
          jax.ShapeDtypeStruct((A_HEADS, t, t), F32),
          jax.ShapeDtypeStruct((A_HEADS, t, t), F32),
          jax.ShapeDtypeStruct((A_HEADS, t, N_META), F32),
      ],
  )(rel_bias)


def _dot_nt(a, b):
  return lax.dot_general(a, b, (((1,), (1,)), ((), ())), preferred_element_type=F32)


def _attn_kernel(q_ref, kx_ref, vx_ref, km_ref, vm_ref, ga_ref, bd_ref, bs_ref, bm_ref,
                 lam_ref, sg_ref, o_ref, m_ref, acc_ref):
  t = ATT_T
  g = ATT_G
  nc = 2 * g
  dv = A_V_DIM
  qi = pl.program_id(2)
  lane = lax.broadcasted_iota(jnp.int32, (t, 128), 1)
  qs = []
  for hh in range(g):
    q = q_ref[:, hh * 128:(hh + 1) * 128]
    zero = jnp.zeros_like(q)
    qs += [jnp.where(lane < A_QK_DIM, q, zero), jnp.where(lane >= A_QK_DIM, q, zero)]

  def widen(x, n):
    return x[:, :n] if n <= 128 else jnp.concatenate([x] * (n // 128), axis=1)

  def block(ks, vs, biases, first):
    n = ks[0].shape[0]
    s = [_dot_nt(qs[c], ks[c // 2]) for c in range(nc)]
    if biases is not None:
      s = [s[c] + biases[c // 2] for c in range(nc)]
    rowmax = [jnp.broadcast_to(jnp.max(s[c], axis=1, keepdims=True), (t, 128)) for c in range(nc)]
    if first:
      m_new = rowmax
    else:
      m_prev = [m_ref[c] for c in range(nc)]
      m_new = [jnp.maximum(m_prev[c], rowmax[c]) for c in range(nc)]
    p = [jnp.exp(s[c] - widen(m_new[c], n)).astype(BF16) for c in range(nc)]
    pv = [jnp.dot(p[c], vs[c // 2], preferred_element_type=F32) for c in range(nc)]
    for c in range(nc):
      m_ref[c] = m_new[c]
      if first:
        acc_ref[c] = pv[c]
      else:
        acc_ref[c] = widen(jnp.exp(m_prev[c] - m_new[c]), 2 * dv) * acc_ref[c] + pv[c]

  def kv_tile(j):
    off = pl.multiple_of(j * t, t)
    ones = jnp.ones((t, dv), BF16)
    ks = [kx_ref[pl.ds(off, t), hh * 128:(hh + 1) * 128] for hh in range(g)]
    vs = [jnp.concatenate([vx_ref[pl.ds(off, t), hh * dv:(hh + 1) * dv], ones], axis=1) for hh in range(g)]
    return ks, vs

  ones_m = jnp.ones((N_META, dv), BF16)
  block([km_ref[:, hh * 128:(hh + 1) * 128] for hh in range(g)],
        [jnp.concatenate([vm_ref[:, hh * dv:(hh + 1) * dv], ones_m], axis=1) for hh in range(g)],
        [jnp.where(qi == 0, bm_ref[hh], 0.0) for hh in range(g)], True)

  def far_body(j, carry):
    ks, vs = kv_tile(j)
    block(ks, vs, None, False)
    return carry

  lax.fori_loop(0, jnp.maximum(qi - 1, 0), far_body, 0)

  @pl.when(qi >= 1)
  def _():
    ks, vs = kv_tile(qi - 1)
    block(ks, vs, [bs_ref[hh] for hh in range(g)], False)

  ks, vs = kv_tile(qi)
  block(ks, vs, [bd_ref[hh] for hh in range(g)], False)

  lp = lam_ref[...]
  lam = (jnp.exp(jnp.sum(lp[0:1] * lp[1:2], axis=1, keepdims=True))
         - jnp.exp(jnp.sum(lp[2:3] * lp[3:4], axis=1, keepdims=True)) + LAM_INIT)
  for hh in range(g):
    a0, a1 = acc_ref[2 * hh], acc_ref[2 * hh + 1]
    o = a0[:, :dv] / a0[:, dv:] - lam * (a1[:, :dv] / a1[:, dv:])
    o = o * lax.rsqrt(jnp.mean(o * o, axis=1, keepdims=True) + SUBLN_EPS) * sg_ref[...]
    o = o * (1.0 - LAM_INIT)
    gate = ga_ref[:, hh * dv:(hh + 1) * dv]
    o_ref[:, hh * dv:(hh + 1) * dv] = (o * (gate / (1.0 + jnp.exp(-gate)))).astype(o_ref.dtype)


def _attention(qkv_x, qkv_m, gz_x, bias_d, bias_s, bias_m, lam_p, subln_g):
  b, s, _ = qkv_x.shape
  t = ATT_T
  g = ATT_G
  w = 128 * g
  hb = A_HEADS // g
  return pl.pallas_call(
      _attn_kernel,
      name="diff_attn",
      grid=(b, hb, s // t),
      in_specs=[
          pl.BlockSpec((None, t, w), lambda bi, hi, qi: (bi, qi, hi)),
          pl.BlockSpec((None, s, w), lambda bi, hi, qi: (bi, 0, hb + hi)),
          pl.BlockSpec((None, s, w), lambda bi, hi, qi: (bi, 0, 2 * hb + hi)),
          pl.BlockSpec((N_META, w), lambda bi, hi, qi: (0, hb + hi)),
          pl.BlockSpec((N_META, w), lambda bi, hi, qi: (0, 2 * hb + hi)),
          pl.BlockSpec((None, t, w), lambda bi, hi, qi: (bi, qi, hi)),
          pl.BlockSpec((g, t, t), lambda bi, hi, qi: (hi, 0, 0)),
          pl.BlockSpec((g, t, t), lambda bi, hi, qi: (hi, 0, 0)),
          pl.BlockSpec((g, t, N_META), lambda bi, hi, qi: (hi, 0, 0)),
          pl.BlockSpec((4, A_QK_DIM), lambda bi, hi, qi: (0, 0)),
          pl.BlockSpec((1, A_V_DIM), lambda bi, hi, qi: (0, 0)),
      ],
      out_specs=pl.BlockSpec((None, t, w), lambda bi, hi, qi: (bi, qi, hi)),
      out_shape=jax.ShapeDtypeStruct((b, s, A_WIDTH), BF16),
      scratch_shapes=[
          pltpu.VMEM((2 * g, t, 128), F32),
          pltpu.VMEM((2 * g, t, 2 * A_V_DIM), F32),
      ],
      compiler_params=pltpu.CompilerParams(
          dimension_semantics=("parallel", "parallel", "arbitrary"),
          vmem_limit_bytes=VMEM_LIMIT),
  )(qkv_x, qkv_x, qkv_x, qkv_m, qkv_m, gz_x, bias_d, bias_s, bias_m, lam_p, subln_g)


def _seg_sum(x):
  lane = lax.broadcasted_iota(jnp.int32, x.shape, 1)
  first = lane < R_HEAD
  lo = jnp.sum(jnp.where(first, x, 0.0), axis=1, keepdims=True)
  hi = jnp.sum(jnp.where(first, 0.0, x), axis=1, keepdims=True)
  return jnp.where(first, lo, hi)


def _split_bf16(x):
  hi = x.astype(BF16)
  return hi, x - hi.astype(F32)


def _dot_tn(a, b):
  return lax.dot_general(a, b, (((0,), (0,)), ((), ())), preferred_element_type=F32)


def _bdot(a, b):
  return lax.dot_general(a, b, (((2,), (1,)), ((0,), (0,))), preferred_element_type=F32)


def _bdot_nt(a, b):
  return lax.dot_general(a, b, (((2,), (2,)), ((0,), (0,))), preferred_element_type=F32)


def _rwkv_kernel(rx_ref, kx_ref, vx_ref, lx_ref, gr_ref, rm_ref, kmt_ref, vmt_ref, lm_ref,
                 pv_ref, mul_ref, wuph_ref, wupl_ref, aup_ref, o_ref, s_ref, prev_ref, prevl_ref):
  tb, c = RW_TB, RW_C
  nh = R_HEAD
  c2 = 2 * c
  ti = pl.program_id(2)
  is_meta = ti == 0

  @pl.when(is_meta)
  def _():
    s_ref[...] = jnp.zeros_like(s_ref)
    prev_ref[...] = jnp.zeros_like(prev_ref)
    prevl_ref[...] = jnp.zeros_like(prevl_ref)

  row = lax.broadcasted_iota(jnp.int32, (tb, 128), 0)
  rowl = lax.broadcasted_iota(jnp.int32, (tb, LORA_PAD), 0)

  def shifted(z, prev, mu, rows):
    z_prev = jnp.where(rows == 0, prev, pltpu.roll(z, 1, 0))
    return z + (z_prev - z) * mu

  z_l = jnp.where(is_meta, lm_ref[...], lx_ref[...])
  lo = shifted(z_l, prevl_ref[...], mul_ref[...], rowl)
  prevl_ref[...] = z_l[tb - 1:tb]
  th_h, th_l = _split_bf16(jnp.tanh(lo))
  w_lora = (jnp.dot(th_h, wuph_ref[...], preferred_element_type=F32)
            + jnp.dot(th_h, wupl_ref[...], preferred_element_type=F32)
            + jnp.dot(th_l.astype(BF16), wuph_ref[...], preferred_element_type=F32))
  a_lora = jnp.dot(lo.astype(BF16), aup_ref[...], preferred_element_type=F32)

  ii = lax.broadcasted_iota(jnp.int32, (tb, tb), 0)
  jj = lax.broadcasted_iota(jnp.int32, (tb, tb), 1)
  shift = int(math.log2(c))
  same = lax.shift_right_logical(ii, shift) == lax.shift_right_logical(jj, shift)
  cum_op = jnp.where(same, jnp.where(jj <= ii, 1.0, 0.0), 0.0).astype(BF16)

  ci = lax.broadcasted_iota(jnp.int32, (c2, c2), 0)
  cj = lax.broadcasted_iota(jnp.int32, (c2, c2), 1)
  diag = ci == cj
  strict2 = jnp.concatenate([cj < ci, cj < ci], axis=1)
  incl2 = jnp.concatenate([cj <= ci, cj <= ci], axis=1)
  first = lax.broadcasted_iota(jnp.int32, (c, 128), 1) < nh

  def stack(x):
    return jnp.concatenate([jnp.where(first, x, 0.0), jnp.where(first, 0.0, x)], axis=0)

  ncc = tb // c
  chains = {name: [] for name in ("at", "rt", "bt", "kt", "bh", "kh", "vv", "gd")}
  post = []
  for p in range(RW_P):
    ls = slice(p * 128, (p + 1) * 128)
    pv = pv_ref[:, ls]
    mu_r, mu_k, mu_v = pv[0:1], pv[1:2], pv[2:3]
    w0, a0, k_k, k_a, r_k, gn_g, gn_b = pv[3:4], pv[4:5], pv[5:6], pv[6:7], pv[7:8], pv[8:9], pv[9:10]

    z_r = jnp.where(is_meta, rm_ref[:, ls], rx_ref[:, ls])
    z_k = jnp.where(is_meta, kmt_ref[:, ls], kx_ref[:, ls])
    z_v = jnp.where(is_meta, vmt_ref[:, ls], vx_ref[:, ls])
    r = shifted(z_r, prev_ref[0:1, ls], mu_r, row)
    k = shifted(z_k, prev_ref[1:2, ls], mu_k, row)
    v = shifted(z_v, prev_ref[2:3, ls], mu_v, row)
    prev_ref[0:1, ls] = z_r[tb - 1:tb]
    prev_ref[1:2, ls] = z_k[tb - 1:tb]
    prev_ref[2:3, ls] = z_v[tb - 1:tb]

    u = -(w0 + w_lora[:, ls])
    softplus = jnp.maximum(u, 0.0) + jnp.log(1.0 + jnp.exp(-jnp.abs(u)))
    logw = -jnp.exp(-softplus - 0.5)
    a = 1.0 / (1.0 + jnp.exp(-(a0 + a_lora[:, ls])))
    kk = k * k_k
    kk = kk / jnp.maximum(jnp.sqrt(_seg_sum(kk * kk)), 1e-12)
    k_mod = k * (1.0 + (a - 1.0) * k_a)
    bonus = _seg_sum(r * k_mod * r_k) * v

    lw_h, lw_r = _split_bf16(logw)
    lw_m, lw_l = _split_bf16(lw_r)
    cum3 = jnp.dot(cum_op, jnp.concatenate([lw_h, lw_m, lw_l.astype(BF16)], axis=1),
                   preferred_element_type=F32)
    cum = cum3[:, :128] + cum3[:, 128:256] + cum3[:, 256:]
    tot = jnp.concatenate([jnp.broadcast_to(cum[cc * c + c - 1:cc * c + c], (c, 128)) for cc in range(ncc)],
                          axis=0)
    p_inv = jnp.exp(-cum)
    a_t = -kk * jnp.exp(cum - logw)
    b_t = kk * a * p_inv
    k_t = k_mod * p_inv
    r_t = r * jnp.exp(cum)
    p_end = jnp.exp(tot - cum)
    b_h = kk * a * p_end
    k_h = k_mod * p_end
    g_diag = jnp.exp(tot)

    for cc in range(ncc):
      rs = slice(cc * c, (cc + 1) * c)
      for name, val in (("at", a_t), ("rt", r_t), ("bt", b_t), ("kt", k_t), ("bh", b_h), ("kh", k_h),
                        ("vv", v)):
        chains[name].append(stack(val[rs]))
      chains["gd"].append(g_diag[cc * c:cc * c + 1])
    post.append((bonus, gn_g, gn_b))

  nb = RW_P * ncc
  at, rt, bt, kt, bh, kh, vv = (jnp.stack(chains[name]) for name in ("at", "rt", "bt", "kt", "bh", "kh", "vv"))
  at_b, vv_b, bh_b = at.astype(BF16), vv.astype(BF16), bh.astype(BF16)
  bk = jnp.concatenate([bt, kt], axis=1).astype(BF16)
  top = jnp.where(strict2, _bdot_nt(at_b, bk), 0.0)
  lblk = jnp.where(incl2, _bdot_nt(rt.astype(BF16), bk), 0.0)
  nm, mak = top[:, :, :c2], top[:, :, c2:]
  tinv = jnp.where(diag, 1.0, nm)
  npow = nm.astype(BF16)
  for _ in range(5):
    npow = _bdot(npow, npow).astype(BF16)
    tinv = tinv + _bdot(tinv.astype(BF16), npow)
  x1 = _bdot(mak.astype(BF16), vv_b)
  wu_b = _bdot(tinv.astype(BF16), jnp.concatenate([at_b, x1.astype(BF16)], axis=2)).astype(BF16)
  rhs = jnp.concatenate([wu_b, jnp.concatenate([jnp.zeros_like(vv_b), vv_b], axis=2)], axis=1)
  qy = _bdot(lblk.astype(BF16), rhs)
  q_h = (rt + qy[:, :, :c2]).astype(BF16)
  y0 = qy[:, :, c2:]
  uv = jnp.concatenate([wu_b[:, :, c2:], vv_b], axis=1)
  bkh = jnp.concatenate([bh_b, kh.astype(BF16)], axis=1)
  g_m = [(jnp.where(diag, chains["gd"][n], 0.0) + _dot_tn(wu_b[n, :, :c2], bh_b[n])).astype(BF16)
         for n in range(nb)]
  h_m = [_dot_tn(uv[n], bkh[n]) for n in range(nb)]

  states = [s_ref[p] for p in range(RW_P)]
  y_rows = [[] for _ in range(RW_P)]
  for cc in range(ncc):
    for p in range(RW_P):
      n = p * ncc + cc
      s_old_b = states[p].astype(BF16)
      y2 = _dot_nt(q_h[n], s_old_b) + y0[n]
      states[p] = jnp.dot(s_old_b, g_m[n], preferred_element_type=F32) + h_m[n]
      y_rows[p].append(y2[:c] + y2[c:])

  for p in range(RW_P):
    ls = slice(p * 128, (p + 1) * 128)
    s_ref[p] = states[p]
    bonus, gn_g, gn_b = post[p]
    y = jnp.concatenate(y_rows[p], axis=0)
    mean = _seg_sum(y) * (1.0 / nh)
    yc = y - mean
    var = _seg_sum(yc * yc) * (1.0 / nh)
    yn = yc * lax.rsqrt(var + GN_EPS) * gn_g + gn_b
    g = gr_ref[:, ls]
    o_ref[:, ls] = ((yn + bonus) * (g / (1.0 + jnp.exp(-g)))).astype(o_ref.dtype)


def _rwkv(gz_x, gz_mp, pvec, mu_l, wup_h, wup_l, aup):
  b, s, _ = gz_x.shape
  tb = RW_TB
  nt = s // tb + 1
  pw = 128 * RW_P
  zr0 = A_WIDTH // pw
  rw = R_WIDTH // pw
  gr0 = zr0 + 3 * rw
  lo_blk = (A_WIDTH + 4 * R_WIDTH) // LORA_PAD

  def xmap(off):
    return lambda bi, hp, ti: (bi, jnp.maximum(ti - 1, 0), off + hp)

  def mmap(off):
    return lambda bi, hp, ti: (0, off + hp)

  return pl.pallas_call(
      _rwkv_kernel,
      name="rwkv7",
      grid=(b, R_PAIRS // RW_P, nt),
      in_specs=[
          pl.BlockSpec((None, tb, pw), xmap(zr0)),
          pl.BlockSpec((None, tb, pw), xmap(zr0 + rw)),
          pl.BlockSpec((None, tb, pw), xmap(zr0 + 2 * rw)),
          pl.BlockSpec((None, tb, LORA_PAD), lambda bi, hp, ti: (bi, jnp.maximum(ti - 1, 0), lo_blk)),
          pl.BlockSpec((None, tb, pw), xmap(gr0)),
          pl.BlockSpec((tb, pw), mmap(zr0)),
          pl.BlockSpec((tb, pw), mmap(zr0 + rw)),
          pl.BlockSpec((tb, pw), mmap(zr0 + 2 * rw)),
          pl.BlockSpec((tb, LORA_PAD), lambda bi, hp, ti: (0, lo_blk)),
          pl.BlockSpec((16, pw), lambda bi, hp, ti: (0, hp)),
          pl.BlockSpec((1, LORA_PAD), lambda bi, hp, ti: (0, 0)),
          pl.BlockSpec((LORA_PAD, pw), lambda bi, hp, ti: (0, hp)),
          pl.BlockSpec((LORA_PAD, pw), lambda bi, hp, ti: (0, hp)),
          pl.BlockSpec((LORA_PAD, pw), lambda bi, hp, ti: (0, hp)),
      ],
      out_specs=pl.BlockSpec((None, tb, pw), lambda bi, hp, ti: (bi, jnp.maximum(ti - 1, 0), hp)),
      out_shape=jax.ShapeDtypeStruct((b, s, R_WIDTH), BF16),
      scratch_shapes=[
          pltpu.VMEM((RW_P, 2 * R_HEAD, 2 * R_HEAD), F32),
          pltpu.VMEM((8, pw), F32),
          pltpu.VMEM((1, LORA_PAD), F32),
      ],
      compiler_params=pltpu.CompilerParams(
          dimension_semantics=("parallel", "parallel", "arbitrary"),
          vmem_limit_bytes=VMEM_LIMIT),
  )(gz_x, gz_x, gz_x, gz_x, gz_x, gz_mp, gz_mp, gz_mp, gz_mp, pvec, mu_l, wup_h, wup_l, aup)


def _out_kernel(x_ref, oa_ref, or_ref, wa_ref, wr_ref, ge_ref, be_ref, gp_ref, bp_ref, o_ref):
  h = _ln_rows(x_ref[...], ge_ref[...], be_ref[...])
  y = (jnp.dot(oa_ref[...], wa_ref[...], preferred_element_type=F32)
       + jnp.dot(or_ref[...], wr_ref[...], preferred_element_type=F32))
  o_ref[...] = _ln_rows(DEEPNORM_ALPHA * h + y, gp_ref[...], bp_ref[...])


def _out_proj(x2d, oa, orw, wa, wr, ge, be, gp, bp, tm):
  m, d = x2d.shape
  vec = pl.BlockSpec((1, d), lambda i: (0, 0))
  return pl.pallas_call(
      _out_kernel,
      name="out_proj",
      grid=(m // tm,),
      in_specs=[
          pl.BlockSpec((tm, d), lambda i: (i, 0)),
          pl.BlockSpec((tm, A_WIDTH), lambda i: (i, 0)),
          pl.BlockSpec((tm, R_WIDTH), lambda i: (i, 0)),
          pl.BlockSpec((A_WIDTH, d), lambda i: (0, 0)),
          pl.BlockSpec((R_WIDTH, d), lambda i: (0, 0)),
          vec, vec, vec, vec,
      ],
      out_specs=pl.BlockSpec((tm, d), lambda i: (i, 0)),
      out_shape=jax.ShapeDtypeStruct((m, d), F32),
      compiler_params=pltpu.CompilerParams(
          dimension_semantics=("parallel",),
          vmem_limit_bytes=VMEM_LIMIT),
  )(x2d, oa, orw, wa, wr, ge, be, gp, bp)


def kernel(x, meta_tokens, ln_emb_g, ln_emb_b, rel_bias, w_in, w_out, lambda_q1, lambda_k1, lambda_q2,
           lambda_k2, subln_g, rw_mu, rw_w0, rw_w_up, rw_a0, rw_a_up, rw_k_k, rw_k_a, rw_r_k, rw_gn_g,
           rw_gn_b, ln_post_g, ln_post_b):
  b, s, d = x.shape
  l = 0
  wi = w_in[l]
  c_q, c_k, c_v, c_ga = A_WIDTH, 2 * A_WIDTH, 3 * A_WIDTH, 4 * A_WIDTH
  c_zr = c_ga + 3 * R_WIDTH + DECAY_LORA + ICLR_LORA
  w_qkv = jnp.concatenate([wi[:, :c_q] * (A_QK_DIM ** -0.5), wi[:, c_q:c_v]], axis=1).astype(BF16)
  lora_pad = LORA_PAD - DECAY_LORA - ICLR_LORA
  c_lo = c_ga + 3 * R_WIDTH
  w_gz = jnp.concatenate(
      [wi[:, c_v:c_lo], wi[:, c_zr:], wi[:, c_lo:c_zr], jnp.zeros((d, lora_pad), wi.dtype)],
      axis=1).astype(BF16)

  ge, be = ln_emb_g.reshape(1, d), ln_emb_b.reshape(1, d)
  x2d = x.reshape(b * s, d)
  qkv_x = _ln_matmul(x2d, ge, be, w_qkv, BF16, 1024, 768).reshape(b, s, 3 * A_WIDTH)
  gz_x = _ln_matmul(x2d, ge, be, w_gz, F32, 1024, 768).reshape(b, s, w_gz.shape[1])
  qkv_m = _ln_matmul(meta_tokens, ge, be, w_qkv, BF16, N_META, 768)
  gz_m = _ln_matmul(meta_tokens, ge, be, w_gz, F32, N_META, 768)

  bias_d, bias_s, bias_m = _bias_tiles(rel_bias)
  lam_p = jnp.stack([lambda_q1[l], lambda_k1[l], lambda_q2[l], lambda_k2[l]], axis=0)
  o_attn = _attention(qkv_x, qkv_m, gz_x, bias_d, bias_s, bias_m, lam_p, subln_g[l].reshape(1, A_V_DIM))

  mu = rw_mu[l]
  zeros = jnp.zeros((R_WIDTH,), F32)
  pvec = jnp.stack([mu[:R_WIDTH], mu[R_WIDTH:2 * R_WIDTH], mu[2 * R_WIDTH:3 * R_WIDTH], rw_w0[l], rw_a0[l],
                    rw_k_k[l], rw_k_a[l], rw_r_k[l].reshape(R_WIDTH), rw_gn_g[l], rw_gn_b[l]]
                   + [zeros] * 6, axis=0)
  mu_l = jnp.pad(mu[3 * R_WIDTH:], (0, lora_pad)).reshape(1, LORA_PAD)
  wup = jnp.pad(rw_w_up[l], ((0, LORA_PAD - DECAY_LORA), (0, 0)))
  wup_h = wup.astype(BF16)
  wup_l = (wup - wup_h.astype(F32)).astype(BF16)
  aup = jnp.pad(rw_a_up[l], ((DECAY_LORA, lora_pad), (0, 0))).astype(BF16)
  gz_mp = jnp.pad(gz_m, ((RW_TB - N_META, 0), (0, 0)))
  o_rwkv = _rwkv(gz_x, gz_mp, pvec, mu_l, wup_h, wup_l, aup)

  wo = w_out[l].astype(BF16)
  out = _out_proj(x2d, o_attn.reshape(b * s, A_WIDTH), o_rwkv.reshape(b * s, R_WIDTH),
                  wo[:A_WIDTH], wo[A_WIDTH:], ge, be,
                  ln_post_g[l].reshape(1, d), ln_post_b[l].reshape(1, d), 512)
  return out.reshape(b, s, d)
```

```python
import functools
import math

import numpy as np
import jax
import jax.numpy as jnp
from jax import lax
from jax.experimental import pallas as pl
from jax.experimental.pallas import tpu as pltpu

D_MODEL = 2048
N_META = 16
A_HEADS = 8
A_V_DIM = 128
A_QK_DIM = 64
A_WIDTH = A_HEADS * A_V_DIM
R_HEAD = 64
R_WIDTH = 1024
R_PAIRS = R_WIDTH // (2 * R_HEAD)
DECAY_LORA = 96
ICLR_LORA = 96
LORA_PAD = 256
N_BUCKETS = 32
MAX_DISTANCE = 128
LN_EPS = 1e-5
SUBLN_EPS = 1e-5
GN_EPS = 64e-5
DEPTH = 1
DEEPNORM_ALPHA = (2 * DEPTH) ** 0.25
LAM_INIT = 0.8 - 0.6 * math.exp(-0.3 * 0)
NEG = -1e30

ATT_T = 256
ATT_G = 4
IN_TN = 1024
Z_Q, Z_K, Z_V, Z_GA, Z_RR, Z_RK, Z_RV, Z_GR = (i * 1024 for i in range(8))
RW_TB = 128
RW_C = 64
RW_P = 4
VMEM_LIMIT = 48 * 1024 * 1024

F32 = jnp.float32
BF16 = jnp.bfloat16


def _ln_rows(x, g, b):
  mu = jnp.mean(x, axis=-1, keepdims=True)
  xc = x - mu
  var = jnp.mean(xc * xc, axis=-1, keepdims=True)
  return xc * lax.rsqrt(var + LN_EPS) * g + b


def _ln_mm_kernel(x_ref, g_ref, b_ref, wm_ref, wl_ref, om_ref, ol_ref, hn_ref):
  j = pl.program_id(1)
  n_main = pl.num_programs(1) - 1

  @pl.when(j == 0)
  def _():
    hn_ref[...] = _ln_rows(x_ref[...], g_ref[...], b_ref[...]).astype(BF16)

  @pl.when(j < n_main)
  def _():
    om_ref[...] = jnp.dot(hn_ref[...], wm_ref[...], preferred_element_type=F32).astype(om_ref.dtype)

  @pl.when(j == n_main)
  def _():
    ol_ref[...] = jnp.dot(hn_ref[...], wl_ref[...], preferred_element_type=F32)


def _ln_matmul(x2d, g, b, w_main, w_lora, tm):
  m, d = x2d.shape
  n = w_main.shape[1]
  tn = IN_TN
  nj = n // tn
  last = nj - 1
  return pl.pallas_call(
      _ln_mm_kernel,
      name="ln_inproj",
      grid=(m // tm, nj + 1),
      in_specs=[
          pl.BlockSpec((tm, d), lambda i, j: (i, 0)),
          pl.BlockSpec((1, d), lambda i, j: (0, 0)),
          pl.BlockSpec((1, d), lambda i, j: (0, 0)),
          pl.BlockSpec((d, tn), lambda i, j: (0, jnp.minimum(j, last))),
          pl.BlockSpec((d, LORA_PAD), lambda i, j: (0, 0)),
      ],
      out_specs=[
          pl.BlockSpec((tm, tn), lambda i, j: (i, jnp.minimum(j, last))),
          pl.BlockSpec((tm, LORA_PAD), lambda i, j: (i, 0)),
      ],
      out_shape=[
          jax.ShapeDtypeStruct((m, n), BF16),
          jax.ShapeDtypeStruct((m, LORA_PAD), F32),
      ],
      scratch_shapes=[pltpu.VMEM((tm, d), BF16)],
      compiler_params=pltpu.CompilerParams(
          dimension_semantics=("parallel", "arbitrary"),
          vmem_limit_bytes=VMEM_LIMIT),
  )(x2d, g, b, w_main, w_lora)


def _bucket_thresholds():
  n = np.arange(0, 4 * MAX_DISTANCE, dtype=np.int64)
  max_exact = N_BUCKETS // 2
  nf = np.maximum(n, 1).astype(np.float32)
  large = max_exact + (np.log(nf / np.float32(max_exact)) / np.float32(math.log(MAX_DISTANCE / max_exact))
                       * np.float32(N_BUCKETS - max_exact)).astype(np.int32)
  large = np.minimum(large, N_BUCKETS - 1)
  bucket = np.where(n < max_exact, n, large)
  assert np.all(np.diff(bucket) >= 0) and bucket[-1] == N_BUCKETS - 1
  return [int(np.argmax(bucket >= b)) for b in range(N_BUCKETS)]


_THR = _bucket_thresholds()


def _bias_kernel(rb_ref, diag_ref, sub_ref, meta_ref):
  h = pl.program_id(0)
  far = rb_ref[N_BUCKETS - 1, h]

  def bias_of(n):
    out = jnp.full(n.shape, rb_ref[0, h] - far, F32)
    for b in range(1, N_BUCKETS):
      out = jnp.where(n >= _THR[b], rb_ref[b, h] - far, out)
    return out

  t = ATT_T
  qi = lax.broadcasted_iota(jnp.int32, (t, t), 0)
  kj = lax.broadcasted_iota(jnp.int32, (t, t), 1)
  d = qi - kj
  diag_ref[...] = jnp.where(d >= 0, bias_of(d), NEG)
  sub_ref[...] = bias_of(d + t)
  qm = lax.broadcasted_iota(jnp.int32, (t, N_META), 0)
  km = lax.broadcasted_iota(jnp.int32, (t, N_META), 1)
  meta_ref[...] = bias_of(qm - km + N_META)


def _bias_tiles(rel_bias):
  t = ATT_T
  return pl.pallas_call(
      _bias_kernel,
      name="bias_tiles",
      grid=(A_HEADS,),
      in_specs=[pl.BlockSpec(memory_space=pltpu.SMEM)],
      out_specs=[
          pl.BlockSpec((None, t, t), lambda h: (h, 0, 0)),
          pl.BlockSpec((None, t, t), lambda h: (h, 0, 0)),
          pl.BlockSpec((None, t, N_META), lambda h: (h, 0, 0)),
      ],
      out_shape=[
          jax.ShapeDtypeStruct((A_HEADS, t, t), F32),
          jax.ShapeDtypeStruct((A_HEADS, t, t), F32),
          jax.ShapeDtypeStruct((A_HEADS, t, N_META), F32),
      ],
  )(rel_bias)


def _dot_nt(a, b):
  return lax.dot_general(a, b, (((1,), (1,)), ((), ())), preferred_element_type=F32)


def _attn_kernel(q_ref, kx_ref, vx_ref, km_ref, vm_ref, ga_ref, bd_ref, bs_ref, bm_ref,
                 lam_ref, sg_ref, o_ref, m_ref, acc_ref):
  t = ATT_T
  g = ATT_G
  nc = 2 * g
  dv = A_V_DIM
  qi = pl.program_id(2)
  lane = lax.broadcasted_iota(jnp.int32, (t, 128), 1)
  qs = []
  for hh in range(g):
    q = q_ref[:, hh * 128:(hh + 1) * 128] * (A_QK_DIM ** -0.5)
    zero = jnp.zeros_like(q)
    qs += [jnp.where(lane < A_QK_DIM, q, zero), jnp.where(lane >= A_QK_DIM, q, zero)]

  def widen(x, n):
    return x[:, :n] if n <= 128 else jnp.concatenate([x] * (n // 128), axis=1)

  def block(ks, vs, biases, first):
    n = ks[0].shape[0]
    s = [_dot_nt(qs[c], ks[c // 2]) for c in range(nc)]
    if biases is not None:
      s = [s[c] + biases[c // 2] for c in range(nc)]
    rowmax = [jnp.broadcast_to(jnp.max(s[c], axis=1, keepdims=True), (t, 128)) for c in range(nc)]
    if first:
      m_new = rowmax
    else:
      m_prev = [m_ref[c] for c in range(nc)]
      m_new = [jnp.maximum(m_prev[c], rowmax[c]) for c in range(nc)]
    p = [jnp.exp(s[c] - widen(m_new[c], n)).astype(BF16) for c in range(nc)]
    pv = [jnp.dot(p[c], vs[c // 2], preferred_element_type=F32) for c in range(nc)]
    for c in range(nc):
      m_ref[c] = m_new[c]
      if first:
        acc_ref[c] = pv[c]
      else:
        acc_ref[c] = widen(jnp.exp(m_prev[c] - m_new[c]), 2 * dv) * acc_ref[c] + pv[c]

  def kv_tile(j):
    off = pl.multiple_of(j * t, t)
    ones = jnp.ones((t, dv), BF16)
    ks = [kx_ref[pl.ds(off, t), hh * 128:(hh + 1) * 128] for hh in range(g)]
    vs = [jnp.concatenate([vx_ref[pl.ds(off, t), hh * dv:(hh + 1) * dv], ones], axis=1) for hh in range(g)]
    return ks, vs

  ones_m = jnp.ones((N_META, dv), BF16)
  block([km_ref[:, hh * 128:(hh + 1) * 128] for hh in range(g)],
        [jnp.concatenate([vm_ref[:, hh * dv:(hh + 1) * dv], ones_m], axis=1) for hh in range(g)],
        [jnp.where(qi == 0, bm_ref[hh], 0.0) for hh in range(g)], True)

  def far_body(j, carry):
    ks, vs = kv_tile(j)
    block(ks, vs, None, False)
    return carry

  lax.fori_loop(0, jnp.maximum(qi - 1, 0), far_body, 0)

  @pl.when(qi >= 1)
  def _():
    ks, vs = kv_tile(qi - 1)
    block(ks, vs, [bs_ref[hh] for hh in range(g)], False)

  ks, vs = kv_tile(qi)
  block(ks, vs, [bd_ref[hh] for hh in range(g)], False)

  lp = lam_ref[...]
  lam = (jnp.exp(jnp.sum(lp[0:1] * lp[1:2], axis=1, keepdims=True))
         - jnp.exp(jnp.sum(lp[2:3] * lp[3:4], axis=1, keepdims=True)) + LAM_INIT)
  for hh in range(g):
    a0, a1 = acc_ref[2 * hh], acc_ref[2 * hh + 1]
    o = a0[:, :dv] / a0[:, dv:] - lam * (a1[:, :dv] / a1[:, dv:])
    o = o * lax.rsqrt(jnp.mean(o * o, axis=1, keepdims=True) + SUBLN_EPS) * sg_ref[...]
    o = o * (1.0 - LAM_INIT)
    gate = ga_ref[:, hh * dv:(hh + 1) * dv].astype(F32)
    o_ref[:, hh * dv:(hh + 1) * dv] = (o * (gate / (1.0 + jnp.exp(-gate)))).astype(o_ref.dtype)


def _attention(z_x, z_m, bias_d, bias_s, bias_m, lam_p, subln_g):
  b, s, _ = z_x.shape
  t = ATT_T
  g = ATT_G
  w = 128 * g
  hb = A_HEADS // g
  kb, vb, gb = Z_K // w, Z_V // w, Z_GA // w
  return pl.pallas_call(
      _attn_kernel,
      name="diff_attn",
      grid=(b, hb, s // t),
      in_specs=[
          pl.BlockSpec((None, t, w), lambda bi, hi, qi: (bi, qi, hi)),
          pl.BlockSpec((None, s, w), lambda bi, hi, qi: (bi, 0, kb + hi)),
          pl.BlockSpec((None, s, w), lambda bi, hi, qi: (bi, 0, vb + hi)),
          pl.BlockSpec((N_META, w), lambda bi, hi, qi: (0, kb + hi)),
          pl.BlockSpec((N_META, w), lambda bi, hi, qi: (0, vb + hi)),
          pl.BlockSpec((None, t, w), lambda bi, hi, qi: (bi, qi, gb + hi)),
          pl.BlockSpec((g, t, t), lambda bi, hi, qi: (hi, 0, 0)),
          pl.BlockSpec((g, t, t), lambda bi, hi, qi: (hi, 0, 0)),
          pl.BlockSpec((g, t, N_META), lambda bi, hi, qi: (hi, 0, 0)),
          pl.BlockSpec((4, A_QK_DIM), lambda bi, hi, qi: (0, 0)),
          pl.BlockSpec((1, A_V_DIM), lambda bi, hi, qi: (0, 0)),
      ],
      out_specs=pl.BlockSpec((None, t, w), lambda bi, hi, qi: (bi, qi, hi)),
      out_shape=jax.ShapeDtypeStruct((b, s, A_WIDTH), BF16),
      scratch_shapes=[
          pltpu.VMEM((2 * g, t, 128), F32),
          pltpu.VMEM((2 * g, t, 2 * A_V_DIM), F32),
      ],
      compiler_params=pltpu.CompilerParams(
          dimension_semantics=("parallel", "parallel", "arbitrary"),
          vmem_limit_bytes=VMEM_LIMIT),
  )(z_x, z_x, z_x, z_m, z_m, z_x, bias_d, bias_s, bias_m, lam_p, subln_g)


def _seg_sum(x):
  lane = lax.broadcasted_iota(jnp.int32, x.shape, 1)
  first = lane < R_HEAD
  lo = jnp.sum(jnp.where(first, x, 0.0), axis=1, keepdims=True)
  hi = jnp.sum(jnp.where(first, 0.0, x), axis=1, keepdims=True)
  return jnp.where(first, lo, hi)


def _split_bf16(x):
  hi = x.astype(BF16)
  return hi, x - hi.astype(F32)


def _dot_tn(a, b):
  return lax.dot_general(a, b, (((0,), (0,)), ((), ())), preferred_element_type=F32)


def _bdot(a, b):
  return lax.dot_general(a, b, (((2,), (1,)), ((0,), (0,))), preferred_element_type=F32)


def _bdot_nt(a, b):
  return lax.dot_general(a, b, (((2,), (2,)), ((0,), (0,))), preferred_element_type=F32)


def _rwkv_kernel(rx_ref, kx_ref, vx_ref, lx_ref, gr_ref, rm_ref, kmt_ref, vmt_ref, lm_ref,
                 pv_ref, mul_ref, wuph_ref, wupl_ref, aup_ref, o_ref, s_ref, prev_ref, prevl_ref):
  tb, c = RW_TB, RW_C
  nh = R_HEAD
  c2 = 2 * c
  ti = pl.program_id(2)
  is_meta = ti == 0

  @pl.when(is_meta)
  def _():
    s_ref[...] = jnp.zeros_like(s_ref)
    prev_ref[...] = jnp.zeros_like(prev_ref)
    prevl_ref[...] = jnp.zeros_like(prevl_ref)

  row = lax.broadcasted_iota(jnp.int32, (tb, 128), 0)
  rowl = lax.broadcasted_iota(jnp.int32, (tb, LORA_PAD), 0)

  def shifted(z, prev, mu, rows):
    z_prev = jnp.where(rows == 0, prev, pltpu.roll(z, 1, 0))
    return z + (z_prev - z) * mu

  z_l = jnp.where(is_meta, lm_ref[...], lx_ref[...])
  lo = shifted(z_l, prevl_ref[...], mul_ref[...], rowl)
  prevl_ref[...] = z_l[tb - 1:tb]
  th_h, th_l = _split_bf16(jnp.tanh(lo))
  w_lora = (jnp.dot(th_h, wuph_ref[...], preferred_element_type=F32)
            + jnp.dot(th_h, wupl_ref[...], preferred_element_type=F32)
            + jnp.dot(th_l.astype(BF16), wuph_ref[...], preferred_element_type=F32))
  a_lora = jnp.dot(lo.astype(BF16), aup_ref[...], preferred_element_type=F32)

  ii = lax.broadcasted_iota(jnp.int32, (tb, tb), 0)
  jj = lax.broadcasted_iota(jnp.int32, (tb, tb), 1)
  shift = int(math.log2(c))
  same = lax.shift_right_logical(ii, shift) == lax.shift_right_logical(jj, shift)
  cum_op = jnp.where(same, jnp.where(jj <= ii, 1.0, 0.0), 0.0).astype(BF16)

  ci = lax.broadcasted_iota(jnp.int32, (c2, c2), 0)
  cj = lax.broadcasted_iota(jnp.int32, (c2, c2), 1)
  diag = ci == cj
  strict2 = jnp.concatenate([cj < ci, cj < ci], axis=1)
  incl2 = jnp.concatenate([cj <= ci, cj <= ci], axis=1)
  first = lax.broadcasted_iota(jnp.int32, (c, 128), 1) < nh

  def stack(x):
    return jnp.concatenate([jnp.where(first, x, 0.0), jnp.where(first, 0.0, x)], axis=0)

  ncc = tb // c
  chains = {name: [] for name in ("at", "rt", "bt", "kt", "bh", "kh", "vv", "gd")}
  post = []
  for p in range(RW_P):
    ls = slice(p * 128, (p + 1) * 128)
    pv = pv_ref[:, ls]
    mu_r, mu_k, mu_v = pv[0:1], pv[1:2], pv[2:3]
    w0, a0, k_k, k_a, r_k, gn_g, gn_b = pv[3:4], pv[4:5], pv[5:6], pv[6:7], pv[7:8], pv[8:9], pv[9:10]

    z_r = jnp.where(is_meta, rm_ref[:, ls], rx_ref[:, ls]).astype(F32)
    z_k = jnp.where(is_meta, kmt_ref[:, ls], kx_ref[:, ls]).astype(F32)
    z_v = jnp.where(is_meta, vmt_ref[:, ls], vx_ref[:, ls]).astype(F32)
    r = shifted(z_r, prev_ref[0:1, ls], mu_r, row)
    k = shifted(z_k, prev_ref[1:2, ls], mu_k, row)
    v = shifted(z_v, prev_ref[2:3, ls], mu_v, row)
    prev_ref[0:1, ls] = z_r[tb - 1:tb]
    prev_ref[1:2, ls] = z_k[tb - 1:tb]
    prev_ref[2:3, ls] = z_v[tb - 1:tb]

    u = -(w0 + w_lora[:, ls])
    softplus = jnp.maximum(u, 0.0) + jnp.log(1.0 + jnp.exp(-jnp.abs(u)))
    logw = -jnp.exp(-softplus - 0.5)
    a = 1.0 / (1.0 + jnp.exp(-(a0 + a_lora[:, ls])))
    kk = k * k_k
    kk = kk / jnp.maximum(jnp.sqrt(_seg_sum(kk * kk)), 1e-12)
    k_mod = k * (1.0 + (a - 1.0) * k_a)
    bonus = _seg_sum(r * k_mod * r_k) * v

    lw_h, lw_r = _split_bf16(logw)
    lw_m, lw_l = _split_bf16(lw_r)
    cum3 = jnp.dot(cum_op, jnp.concatenate([lw_h, lw_m, lw_l.astype(BF16)], axis=1),
                   preferred_element_type=F32)
    cum = cum3[:, :128] + cum3[:, 128:256] + cum3[:, 256:]
    tot = jnp.concatenate([jnp.broadcast_to(cum[cc * c + c - 1:cc * c + c], (c, 128)) for cc in range(ncc)],
                          axis=0)
    p_inv = jnp.exp(-cum)
    a_t = -kk * jnp.exp(cum - logw)
    b_t = kk * a * p_inv
    k_t = k_mod * p_inv
    r_t = r * jnp.exp(cum)
    p_end = jnp.exp(tot - cum)
    b_h = kk * a * p_end
    k_h = k_mod * p_end
    g_diag = jnp.exp(tot)

    for cc in range(ncc):
      rs = slice(cc * c, (cc + 1) * c)
      for name, val in (("at", a_t), ("rt", r_t), ("bt", b_t), ("kt", k_t), ("bh", b_h), ("kh", k_h),
                        ("vv", v)):
        chains[name].append(stack(val[rs]))
      chains["gd"].append(g_diag[cc * c:cc * c + 1])
    post.append((bonus, gn_g, gn_b))

  nb = RW_P * ncc
  at, rt, bt, kt, bh, kh, vv = (jnp.stack(chains[name]) for name in ("at", "rt", "bt", "kt", "bh", "kh", "vv"))
  at_b, vv_b, bh_b = at.astype(BF16), vv.astype(BF16), bh.astype(BF16)
  bk = jnp.concatenate([bt, kt], axis=1).astype(BF16)
  top = jnp.where(strict2, _bdot_nt(at_b, bk), 0.0)
  lblk = jnp.where(incl2, _bdot_nt(rt.astype(BF16), bk), 0.0)
  nm, mak = top[:, :, :c2], top[:, :, c2:]
  tinv = jnp.where(diag, 1.0, nm)
  npow = nm.astype(BF16)
  for _ in range(5):
    npow = _bdot(npow, npow).astype(BF16)
    tinv = tinv + _bdot(tinv.astype(BF16), npow)
  x1 = _bdot(mak.astype(BF16), vv_b)
  wu_b = _bdot(tinv.astype(BF16), jnp.concatenate([at_b, x1.astype(BF16)], axis=2)).astype(BF16)
  rhs = jnp.concatenate([wu_b, jnp.concatenate([jnp.zeros_like(vv_b), vv_b], axis=2)], axis=1)
  qy = _bdot(lblk.astype(BF16), rhs)
  q_h = (rt + qy[:, :, :c2]).astype(BF16)
  y0 = qy[:, :, c2:]
  uv = jnp.concatenate([wu_b[:, :, c2:], vv_b], axis=1)
  bkh = jnp.concatenate([bh_b, kh.astype(BF16)], axis=1)
  g_m = [(jnp.where(diag, chains["gd"][n], 0.0) + _dot_tn(wu_b[n, :, :c2], bh_b[n])).astype(BF16)
         for n in range(nb)]
  h_m = [_dot_tn(uv[n], bkh[n]) for n in range(nb)]

  states = [s_ref[p] for p in range(RW_P)]
  y_rows = [[] for _ in range(RW_P)]
  for cc in range(ncc):
    for p in range(RW_P):
      n = p * ncc + cc
      s_old_b = states[p].astype(BF16)
      y2 = _dot_nt(q_h[n], s_old_b) + y0[n]
      states[p] = jnp.dot(s_old_b, g_m[n], preferred_element_type=F32) + h_m[n]
      y_rows[p].append(y2[:c] + y2[c:])

  for p in range(RW_P):
    ls = slice(p * 128, (p + 1) * 128)
    s_ref[p] = states[p]
    bonus, gn_g, gn_b = post[p]
    y = jnp.concatenate(y_rows[p], axis=0)
    mean = _seg_sum(y) * (1.0 / nh)
    yc = y - mean
    var = _seg_sum(yc * yc) * (1.0 / nh)
    yn = yc * lax.rsqrt(var + GN_EPS) * gn_g + gn_b
    g = gr_ref[:, ls].astype(F32)
    o_ref[:, ls] = ((yn + bonus) * (g / (1.0 + jnp.exp(-g)))).astype(o_ref.dtype)


def _rwkv(z_x, lo_x, z_mp, lo_mp, pvec, mu_l, wup_h, wup_l, aup):
  b, s, _ = z_x.shape
  tb = RW_TB
  nt = s // tb + 1
  pw = 128 * RW_P

  def xmap(col):
    return lambda bi, hp, ti: (bi, jnp.maximum(ti - 1, 0), col // pw + hp)

  def mmap(col):
    return lambda bi, hp, ti: (0, col // pw + hp)

  return pl.pallas_call(
      _rwkv_kernel,
      name="rwkv7",
      grid=(b, R_PAIRS // RW_P, nt),
      in_specs=[
          pl.BlockSpec((None, tb, pw), xmap(Z_RR)),
          pl.BlockSpec((None, tb, pw), xmap(Z_RK)),
          pl.BlockSpec((None, tb, pw), xmap(Z_RV)),
          pl.BlockSpec((None, tb, LORA_PAD), lambda bi, hp, ti: (bi, jnp.maximum(ti - 1, 0), 0)),
          pl.BlockSpec((None, tb, pw), xmap(Z_GR)),
          pl.BlockSpec((tb, pw), mmap(Z_RR)),
          pl.BlockSpec((tb, pw), mmap(Z_RK)),
          pl.BlockSpec((tb, pw), mmap(Z_RV)),
          pl.BlockSpec((tb, LORA_PAD), lambda bi, hp, ti: (0, 0)),
          pl.BlockSpec((16, pw), lambda bi, hp, ti: (0, hp)),
          pl.BlockSpec((1, LORA_PAD), lambda bi, hp, ti: (0, 0)),
          pl.BlockSpec((LORA_PAD, pw), lambda bi, hp, ti: (0, hp)),
          pl.BlockSpec((LORA_PAD, pw), lambda bi, hp, ti: (0, hp)),
          pl.BlockSpec((LORA_PAD, pw), lambda bi, hp, ti: (0, hp)),
      ],
      out_specs=pl.BlockSpec((None, tb, pw), lambda bi, hp, ti: (bi, jnp.maximum(ti - 1, 0), hp)),
      out_shape=jax.ShapeDtypeStruct((b, s, R_WIDTH), BF16),
      scratch_shapes=[
          pltpu.VMEM((RW_P, 2 * R_HEAD, 2 * R_HEAD), F32),
          pltpu.VMEM((8, pw), F32),
          pltpu.VMEM((1, LORA_PAD), F32),
      ],
      compiler_params=pltpu.CompilerParams(
          dimension_semantics=("parallel", "parallel", "arbitrary"),
          vmem_limit_bytes=VMEM_LIMIT),
  )(z_x, z_x, z_x, lo_x, z_x, z_mp, z_mp, z_mp, lo_mp, pvec, mu_l, wup_h, wup_l, aup)


def _out_kernel(x_ref, oa_ref, or_ref, wa_ref, wr_ref, ge_ref, be_ref, gp_ref, bp_ref, o_ref):
  h = _ln_rows(x_ref[...], ge_ref[...], be_ref[...])
  y = (jnp.dot(oa_ref[...], wa_ref[...], preferred_element_type=F32)
       + jnp.dot(or_ref[...], wr_ref[...], preferred_element_type=F32))
  o_ref[...] = _ln_rows(DEEPNORM_ALPHA * h + y, gp_ref[...], bp_ref[...])


def _out_proj(x2d, oa, orw, wa, wr, ge, be, gp, bp, tm):
  m, d = x2d.shape
  vec = pl.BlockSpec((1, d), lambda i: (0, 0))
  return pl.pallas_call(
      _out_kernel,
      name="out_proj",
      grid=(m // tm,),
      in_specs=[
          pl.BlockSpec((tm, d), lambda i: (i, 0)),
          pl.BlockSpec((tm, A_WIDTH), lambda i: (i, 0)),
          pl.BlockSpec((tm, R_WIDTH), lambda i: (i, 0)),
          pl.BlockSpec((A_WIDTH, d), lambda i: (0, 0)),
          pl.BlockSpec((R_WIDTH, d), lambda i: (0, 0)),
          vec, vec, vec, vec,
      ],
      out_specs=pl.BlockSpec((tm, d), lambda i: (i, 0)),
      out_shape=jax.ShapeDtypeStruct((m, d), F32),
      compiler_params=pltpu.CompilerParams(
          dimension_semantics=("parallel",),
          vmem_limit_bytes=VMEM_LIMIT),
  )(x2d, oa, orw, wa, wr, ge, be, gp, bp)


def kernel(x, meta_tokens, ln_emb_g, ln_emb_b, rel_bias, w_in, w_out, lambda_q1, lambda_k1, lambda_q2,
           lambda_k2, subln_g, rw_mu, rw_w0, rw_w_up, rw_a0, rw_a_up, rw_k_k, rw_k_a, rw_r_k, rw_gn_g,
           rw_gn_b, ln_post_g, ln_post_b):
  b, s, d = x.shape
  l = 0
  wi = w_in[l]
  c_lo = 4 * A_WIDTH + 3 * R_WIDTH
  c_gr = c_lo + DECAY_LORA + ICLR_LORA
  lora_pad = LORA_PAD - DECAY_LORA - ICLR_LORA
  w_main = jnp.concatenate([wi[:, :c_lo], wi[:, c_gr:]], axis=1).astype(BF16)
  w_lora = jnp.pad(wi[:, c_lo:c_gr], ((0, 0), (0, lora_pad))).astype(BF16)

  ge, be = ln_emb_g.reshape(1, d), ln_emb_b.reshape(1, d)
  x2d = x.reshape(b * s, d)
  z_x, lo_x = _ln_matmul(x2d, ge, be, w_main, w_lora, 1024)
  z_x, lo_x = z_x.reshape(b, s, -1), lo_x.reshape(b, s, LORA_PAD)
  z_m, lo_m = _ln_matmul(meta_tokens, ge, be, w_main, w_lora, N_META)

  bias_d, bias_s, bias_m = _bias_tiles(rel_bias)
  lam_p = jnp.stack([lambda_q1[l], lambda_k1[l], lambda_q2[l], lambda_k2[l]], axis=0)
  o_attn = _attention(z_x, z_m, bias_d, bias_s, bias_m, lam_p, subln_g[l].reshape(1, A_V_DIM))

  mu = rw_mu[l]
  zeros = jnp.zeros((R_WIDTH,), F32)
  pvec = jnp.stack([mu[:R_WIDTH], mu[R_WIDTH:2 * R_WIDTH], mu[2 * R_WIDTH:3 * R_WIDTH], rw_w0[l], rw_a0[l],
                    rw_k_k[l], rw_k_a[l], rw_r_k[l].reshape(R_WIDTH), rw_gn_g[l], rw_gn_b[l]]
                   + [zeros] * 6, axis=0)
  mu_l = jnp.pad(mu[3 * R_WIDTH:], (0, lora_pad)).reshape(1, LORA_PAD)
  wup = jnp.pad(rw_w_up[l], ((0, LORA_PAD - DECAY_LORA), (0, 0)))
  wup_h = wup.astype(BF16)
  wup_l = (wup - wup_h.astype(F32)).astype(BF16)
  aup = jnp.pad(rw_a_up[l], ((DECAY_LORA, lora_pad), (0, 0))).astype(BF16)
  front = ((RW_TB - N_META, 0), (0, 0))
  o_rwkv = _rwkv(z_x, lo_x, jnp.pad(z_m, front), jnp.pad(lo_m, front), pvec, mu_l, wup_h, wup_l, aup)

  wo = w_out[l].astype(BF16)
  out = _out_proj(x2d, o_attn.reshape(b * s, A_WIDTH), o_rwkv.reshape(b * s, R_WIDTH),
                  wo[:A_WIDTH], wo[A_WIDTH:], ge, be,
                  ln_post_g[l].reshape(1, d), ln_post_b[l].reshape(1, d), 512)
  return out.reshape(b, s, d)
```

```python
import functools
import math

import numpy as np
import jax
import jax.numpy as jnp
from jax import lax
from jax.experimental import pallas as pl
from jax.experimental.pallas import tpu as pltpu

D_MODEL = 2048
N_META = 16
A_HEADS = 8
A_V_DIM = 128
A_QK_DIM = 64
A_WIDTH = A_HEADS * A_V_DIM
R_HEAD = 64
R_WIDTH = 1024
R_PAIRS = R_WIDTH // (2 * R_HEAD)
DECAY_LORA = 96
ICLR_LORA = 96
LORA_PAD = 256
N_BUCKETS = 32
MAX_DISTANCE = 128
LN_EPS = 1e-5
SUBLN_EPS = 1e-5
GN_EPS = 64e-5
DEPTH = 1
DEEPNORM_ALPHA = (2 * DEPTH) ** 0.25
LAM_INIT = 0.8 - 0.6 * math.exp(-0.3 * 0)
NEG = -1e30

ATT_T = 256
ATT_G = 4
ONES_ROWS = 16
IN_TN = 1024
Z_Q, Z_K, Z_GA, Z_RR, Z_RK, Z_RV, Z_GR = (i * 1024 for i in range(7))
LOG2E = math.log2(math.e)
Q_SCALE = A_QK_DIM ** -0.5 * LOG2E
RW_TB = 128
RW_C = 64
RW_P = 4
VMEM_LIMIT = 56 * 1024 * 1024

F32 = jnp.float32
BF16 = jnp.bfloat16


def _ln_rows(x, g, b):
  mu = jnp.mean(x, axis=-1, keepdims=True)
  xc = x - mu
  var = jnp.mean(xc * xc, axis=-1, keepdims=True)
  return xc * lax.rsqrt(var + LN_EPS) * g + b


def _ln_mm_kernel(x_ref, g_ref, b_ref, wm_ref, wvt_ref, wl_ref, om_ref, ovt_ref, ol_ref, hn_ref):
  j = pl.program_id(1)
  n_main = pl.num_programs(1) - 2

  @pl.when(j == 0)
  def _():
    hn_ref[...] = _ln_rows(x_ref[...], g_ref[...], b_ref[...]).astype(BF16)

  @pl.when(j < n_main)
  def _():
    scale = jnp.where(j == Z_Q // IN_TN, Q_SCALE, 1.0)
    z = jnp.dot(hn_ref[...], wm_ref[...], preferred_element_type=F32)
    om_ref[...] = (z * scale).astype(om_ref.dtype)

  @pl.when(j == n_main)
  def _():
    zt = _dot_nt(wvt_ref[...], hn_ref[...])
    tv = ovt_ref.shape[2]
    for c in range(ovt_ref.shape[0]):
      ovt_ref[c] = zt[:, c * tv:(c + 1) * tv].astype(ovt_ref.dtype)

  @pl.when(j == n_main + 1)
  def _():
    ol_ref[...] = jnp.dot(hn_ref[...], wl_ref[...], preferred_element_type=F32)


def _ln_matmul(x2d, g, b, w_main, w_vt, w_lora, tm):
  m, d = x2d.shape
  n = w_main.shape[1]
  tn = IN_TN
  nj = n // tn
  last = nj - 1
  tv = min(ATT_T, tm)
  return pl.pallas_call(
      _ln_mm_kernel,
      name="ln_inproj",
      grid=(m // tm, nj + 2),
      in_specs=[
          pl.BlockSpec((tm, d), lambda i, j: (i, 0)),
          pl.BlockSpec((1, d), lambda i, j: (0, 0)),
          pl.BlockSpec((1, d), lambda i, j: (0, 0)),
          pl.BlockSpec((d, tn), lambda i, j: (0, jnp.minimum(j, last))),
          pl.BlockSpec((A_WIDTH, d), lambda i, j: (0, 0)),
          pl.BlockSpec((d, LORA_PAD), lambda i, j: (0, 0)),
      ],
      out_specs=[
          pl.BlockSpec((tm, tn), lambda i, j: (i, jnp.minimum(j, last))),
          pl.BlockSpec((tm // tv, A_WIDTH, tv), lambda i, j: (i, 0, 0)),
          pl.BlockSpec((tm, LORA_PAD), lambda i, j: (i, 0)),
      ],
      out_shape=[
          jax.ShapeDtypeStruct((m, n), BF16),
          jax.ShapeDtypeStruct((m // tv, A_WIDTH, tv), BF16),
          jax.ShapeDtypeStruct((m, LORA_PAD), F32),
      ],
      scratch_shapes=[pltpu.VMEM((tm, d), BF16)],
      compiler_params=pltpu.CompilerParams(
          dimension_semantics=("parallel", "arbitrary"),
          vmem_limit_bytes=VMEM_LIMIT),
  )(x2d, g, b, w_main, w_vt, w_lora)


def _bucket_thresholds():
  n = np.arange(0, 4 * MAX_DISTANCE, dtype=np.int64)
  max_exact = N_BUCKETS // 2
  nf = np.maximum(n, 1).astype(np.float32)
  large = max_exact + (np.log(nf / np.float32(max_exact)) / np.float32(math.log(MAX_DISTANCE / max_exact))
                       * np.float32(N_BUCKETS - max_exact)).astype(np.int32)
  large = np.minimum(large, N_BUCKETS - 1)
  bucket = np.where(n < max_exact, n, large)
  assert np.all(np.diff(bucket) >= 0) and bucket[-1] == N_BUCKETS - 1
  return [int(np.argmax(bucket >= b)) for b in range(N_BUCKETS)]


_THR = _bucket_thresholds()


def _bias_kernel(rb_ref, diag_ref, sub_ref, meta_ref):
  h = pl.program_id(0)
  far = rb_ref[N_BUCKETS - 1, h]

  def bias_of(n):
    out = jnp.full(n.shape, (rb_ref[0, h] - far) * LOG2E, F32)
    for b in range(1, N_BUCKETS):
      out = jnp.where(n >= _THR[b], (rb_ref[b, h] - far) * LOG2E, out)
    return out

  t = ATT_T
  kj = lax.broadcasted_iota(jnp.int32, (t, t), 0)
  qi = lax.broadcasted_iota(jnp.int32, (t, t), 1)
  d = qi - kj
  diag_ref[...] = jnp.where(d >= 0, bias_of(d), NEG)
  sub_ref[...] = bias_of(d + t)
  km = lax.broadcasted_iota(jnp.int32, (N_META, t), 0)
  qm = lax.broadcasted_iota(jnp.int32, (N_META, t), 1)
  meta_ref[...] = bias_of(qm - km + N_META)


def _bias_tiles(rel_bias):
  t = ATT_T
  return pl.pallas_call(
      _bias_kernel,
      name="bias_tiles",
      grid=(A_HEADS,),
      in_specs=[pl.BlockSpec(memory_space=pltpu.SMEM)],
      out_specs=[
          pl.BlockSpec((None, t, t), lambda h: (h, 0, 0)),
          pl.BlockSpec((None, t, t), lambda h: (h, 0, 0)),
          pl.BlockSpec((None, N_META, t), lambda h: (h, 0, 0)),
      ],
      out_shape=[
          jax.ShapeDtypeStruct((A_HEADS, t, t), F32),
          jax.ShapeDtypeStruct((A_HEADS, t, t), F32),
          jax.ShapeDtypeStruct((A_HEADS, N_META, t), F32),
      ],
  )(rel_bias)


def _dot_nt(a, b):
  return lax.dot_general(a, b, (((1,), (1,)), ((), ())), preferred_element_type=F32)


def _attn_kernel(q_ref, kx_ref, vt_ref, km_ref, vmt_ref, ga_ref, bd_ref, bs_ref, bm_ref,
                 lam_ref, sg_ref, o_ref, m_ref, acc_ref):
  t = ATT_T
  g = ATT_G
  nc = 2 * g
  dv = A_V_DIM
  qi = pl.program_id(2)
  lane = lax.broadcasted_iota(jnp.int32, (t, 128), 1)
  qs = []
  for hh in range(g):
    q = q_ref[:, hh * 128:(hh + 1) * 128]
    zero = jnp.zeros_like(q)
    qs += [jnp.where(lane < A_QK_DIM, q, zero), jnp.where(lane >= A_QK_DIM, q, zero)]

  m_ref[...] = jnp.full(m_ref.shape, NEG, F32)
  acc_ref[...] = jnp.zeros(acc_ref.shape, F32)

  def block(segments):
    st = []
    for ks, _, biases in segments:
      s = [_dot_nt(ks[c // 2], qs[c]) for c in range(nc)]
      if biases is not None:
        s = [s[c] + biases[c // 2] for c in range(nc)]
      st.append(s)
    m_prev = [m_ref[c] for c in range(nc)]
    m_new = m_prev
    for s in st:
      m_new = [jnp.maximum(m_new[c], jnp.max(s[c], axis=0, keepdims=True)) for c in range(nc)]
    pv = None
    for (_, vts, _), s in zip(segments, st):
      pt = [jnp.exp2(s[c] - m_new[c]).astype(BF16) for c in range(nc)]
      d = [jnp.dot(vts[c // 2], pt[c], preferred_element_type=F32) for c in range(nc)]
      pv = d if pv is None else [pv[c] + d[c] for c in range(nc)]
    for c in range(nc):
      m_ref[c] = m_new[c]
      acc_ref[c] = jnp.exp2(m_prev[c] - m_new[c]) * acc_ref[c] + pv[c]

  def kv_tile(j, biases=None):
    off = pl.multiple_of(j * t, t)
    ones = jnp.ones((ONES_ROWS, t), BF16)
    ks = [kx_ref[pl.ds(off, t), hh * 128:(hh + 1) * 128] for hh in range(g)]
    vts = [jnp.concatenate([vt_ref[j, hh * dv:(hh + 1) * dv, :], ones], axis=0) for hh in range(g)]
    return ks, vts, biases

  n_far = jnp.maximum(qi - 1, 0)

  def far_body(j, carry):
    block([kv_tile(2 * j), kv_tile(2 * j + 1)])
    return carry

  lax.fori_loop(0, n_far // 2, far_body, 0)

  @pl.when(n_far % 2 == 1)
  def _():
    block([kv_tile(n_far - 1)])

  @pl.when(qi >= 1)
  def _():
    block([kv_tile(qi - 1, [bs_ref[hh] for hh in range(g)])])

  ones_m = jnp.ones((ONES_ROWS, N_META), BF16)
  meta = ([km_ref[:, hh * 128:(hh + 1) * 128] for hh in range(g)],
          [jnp.concatenate([vmt_ref[0, hh * dv:(hh + 1) * dv, :], ones_m], axis=0) for hh in range(g)],
          [jnp.where(qi == 0, bm_ref[hh], 0.0) for hh in range(g)])
  block([kv_tile(qi, [bd_ref[hh] for hh in range(g)]), meta])

  lp = lam_ref[...]
  lam = (jnp.exp(jnp.sum(lp[0:1] * lp[1:2], axis=1, keepdims=True))
         - jnp.exp(jnp.sum(lp[2:3] * lp[3:4], axis=1, keepdims=True)) + LAM_INIT)
  for hh in range(g):
    a0, a1 = acc_ref[2 * hh], acc_ref[2 * hh + 1]
    ot = a0[:dv] / a0[dv:dv + 1] - lam * (a1[:dv] / a1[dv:dv + 1])
    ot = ot * lax.rsqrt(jnp.mean(ot * ot, axis=0, keepdims=True) + SUBLN_EPS)
    o = ot.T * (sg_ref[...] * (1.0 - LAM_INIT))
    gate = ga_ref[:, hh * dv:(hh + 1) * dv].astype(F32)
    o_ref[:, hh * dv:(hh + 1) * dv] = (o * (gate / (1.0 + jnp.exp(-gate)))).astype(o_ref.dtype)


def _attention(z_x, vt_x, z_m, vt_m, bias_d, bias_s, bias_m, lam_p, subln_g):
  b, s, _ = z_x.shape
  t = ATT_T
  g = ATT_G
  w = 128 * g
  hb = A_HEADS // g
  kb, gb = Z_K // w, Z_GA // w
  return pl.pallas_call(
      _attn_kernel,
      name="diff_attn",
      grid=(b, hb, s // t),
      in_specs=[
          pl.BlockSpec((None, t, w), lambda bi, hi, qi: (bi, qi, hi)),
          pl.BlockSpec((None, s, w), lambda bi, hi, qi: (bi, 0, kb + hi)),
          pl.BlockSpec((None, s // t, w, t), lambda bi, hi, qi: (bi, 0, hi, 0)),
          pl.BlockSpec((N_META, w), lambda bi, hi, qi: (0, kb + hi)),
          pl.BlockSpec((1, w, N_META), lambda bi, hi, qi: (0, hi, 0)),
          pl.BlockSpec((None, t, w), lambda bi, hi, qi: (bi, qi, gb + hi)),
          pl.BlockSpec((g, t, t), lambda bi, hi, qi: (hi, 0, 0)),
          pl.BlockSpec((g, t, t), lambda bi, hi, qi: (hi, 0, 0)),
          pl.BlockSpec((g, N_META, t), lambda bi, hi, qi: (hi, 0, 0)),
          pl.BlockSpec((4, A_QK_DIM), lambda bi, hi, qi: (0, 0)),
          pl.BlockSpec((1, A_V_DIM), lambda bi, hi, qi: (0, 0)),
      ],
      out_specs=pl.BlockSpec((None, t, w), lambda bi, hi, qi: (bi, qi, hi)),
      out_shape=jax.ShapeDtypeStruct((b, s, A_WIDTH), BF16),
      scratch_shapes=[
          pltpu.VMEM((2 * g, 1, t), F32),
          pltpu.VMEM((2 * g, A_V_DIM + ONES_ROWS, t), F32),
      ],
      compiler_params=pltpu.CompilerParams(
          dimension_semantics=("parallel", "parallel", "arbitrary"),
          vmem_limit_bytes=VMEM_LIMIT),
  )(z_x, z_x, vt_x.reshape(b, s // t, A_WIDTH, t), z_m, vt_m, z_x, bias_d, bias_s, bias_m, lam_p, subln_g)


def _seg_sum(x):
  lane = lax.broadcasted_iota(jnp.int32, x.shape, 1)
  first = lane < R_HEAD
  lo = jnp.sum(jnp.where(first, x, 0.0), axis=1, keepdims=True)
  hi = jnp.sum(jnp.where(first, 0.0, x), axis=1, keepdims=True)
  return jnp.where(first, lo, hi)


def _split_bf16(x):
  hi = x.astype(BF16)
  return hi, x - hi.astype(F32)


def _dot_tn(a, b):
  return lax.dot_general(a, b, (((0,), (0,)), ((), ())), preferred_element_type=F32)


def _bdot(a, b):
  return lax.dot_general(a, b, (((2,), (1,)), ((0,), (0,))), preferred_element_type=F32)


def _bdot_nt(a, b):
  return lax.dot_general(a, b, (((2,), (2,)), ((0,), (0,))), preferred_element_type=F32)


def _rwkv_kernel(rx_ref, kx_ref, vx_ref, lx_ref, gr_ref, rm_ref, kmt_ref, vmt_ref, lm_ref,
                 pv_ref, mul_ref, wuph_ref, wupl_ref, aup_ref, o_ref, s_ref, prev_ref, prevl_ref):
  tb, c = RW_TB, RW_C
  nh = R_HEAD
  c2 = 2 * c
  ti = pl.program_id(2)
  is_meta = ti == 0

  @pl.when(is_meta)
  def _():
    s_ref[...] = jnp.zeros_like(s_ref)
    prev_ref[...] = jnp.zeros_like(prev_ref)
    prevl_ref[...] = jnp.zeros_like(prevl_ref)

  row = lax.broadcasted_iota(jnp.int32, (tb, 128), 0)
  rowl = lax.broadcasted_iota(jnp.int32, (tb, LORA_PAD), 0)

  def shifted(z, prev, mu, rows):
    z_prev = jnp.where(rows == 0, prev, pltpu.roll(z, 1, 0))
    return z + (z_prev - z) * mu

  z_l = jnp.where(is_meta, lm_ref[...], lx_ref[...])
  lo = shifted(z_l, prevl_ref[...], mul_ref[...], rowl)
  prevl_ref[...] = z_l[tb - 1:tb]
  th_h, th_l = _split_bf16(jnp.tanh(lo))
  w_lora = (jnp.dot(th_h, wuph_ref[...], preferred_element_type=F32)
            + jnp.dot(th_h, wupl_ref[...], preferred_element_type=F32)
            + jnp.dot(th_l.astype(BF16), wuph_ref[...], preferred_element_type=F32))
  a_lora = jnp.dot(lo.astype(BF16), aup_ref[...], preferred_element_type=F32)

  ii = lax.broadcasted_iota(jnp.int32, (tb, tb), 0)
  jj = lax.broadcasted_iota(jnp.int32, (tb, tb), 1)
  shift = int(math.log2(c))
  same = lax.shift_right_logical(ii, shift) == lax.shift_right_logical(jj, shift)
  cum_op = jnp.where(same, jnp.where(jj <= ii, 1.0, 0.0), 0.0).astype(BF16)

  ci = lax.broadcasted_iota(jnp.int32, (c2, c2), 0)
  cj = lax.broadcasted_iota(jnp.int32, (c2, c2), 1)
  diag = ci == cj
  strict2 = jnp.concatenate([cj < ci, cj < ci], axis=1)
  incl2 = jnp.concatenate([cj <= ci, cj <= ci], axis=1)
  first = lax.broadcasted_iota(jnp.int32, (c, 128), 1) < nh

  def stack(x):
    return jnp.concatenate([jnp.where(first, x, 0.0), jnp.where(first, 0.0, x)], axis=0)

  ncc = tb // c
  chains = {name: [] for name in ("at", "rt", "bt", "kt", "bh", "kh", "vv", "gd")}
  post = []
  for p in range(RW_P):
    ls = slice(p * 128, (p + 1) * 128)
    pv = pv_ref[:, ls]
    mu_r, mu_k, mu_v = pv[0:1], pv[1:2], pv[2:3]
    w0, a0, k_k, k_a, r_k, gn_g, gn_b = pv[3:4], pv[4:5], pv[5:6], pv[6:7], pv[7:8], pv[8:9], pv[9:10]

    z_r = jnp.where(is_meta, rm_ref[:, ls], rx_ref[:, ls]).astype(F32)
    z_k = jnp.where(is_meta, kmt_ref[:, ls], kx_ref[:, ls]).astype(F32)
    z_v = jnp.where(is_meta, vmt_ref[:, ls], vx_ref[:, ls]).astype(F32)
    r = shifted(z_r, prev_ref[0:1, ls], mu_r, row)
    k = shifted(z_k, prev_ref[1:2, ls], mu_k, row)
    v = shifted(z_v, prev_ref[2:3, ls], mu_v, row)
    prev_ref[0:1, ls] = z_r[tb - 1:tb]
    prev_ref[1:2, ls] = z_k[tb - 1:tb]
    prev_ref[2:3, ls] = z_v[tb - 1:tb]

    u = -(w0 + w_lora[:, ls])
    softplus = jnp.maximum(u, 0.0) + jnp.log(1.0 + jnp.exp(-jnp.abs(u)))
    logw = -jnp.exp(-softplus - 0.5)
    a = 1.0 / (1.0 + jnp.exp(-(a0 + a_lora[:, ls])))
    kk = k * k_k
    kk = kk / jnp.maximum(jnp.sqrt(_seg_sum(kk * kk)), 1e-12)
    k_mod = k * (1.0 + (a - 1.0) * k_a)
    bonus = _seg_sum(r * k_mod * r_k) * v

    lw_h, lw_r = _split_bf16(logw)
    lw_m, lw_l = _split_bf16(lw_r)
    cum3 = jnp.dot(cum_op, jnp.concatenate([lw_h, lw_m, lw_l.astype(BF16)], axis=1),
                   preferred_element_type=F32)
    cum = cum3[:, :128] + cum3[:, 128:256] + cum3[:, 256:]
    tot = jnp.concatenate([jnp.broadcast_to(cum[cc * c + c - 1:cc * c + c], (c, 128)) for cc in range(ncc)],
                          axis=0)
    p_inv = jnp.exp(-cum)
    a_t = -kk * jnp.exp(cum - logw)
    b_t = kk * a * p_inv
    k_t = k_mod * p_inv
    r_t = r * jnp.exp(cum)
    p_end = jnp.exp(tot - cum)
    b_h = kk * a * p_end
    k_h = k_mod * p_end
    g_diag = jnp.exp(tot)

    for cc in range(ncc):
      rs = slice(cc * c, (cc + 1) * c)
      for name, val in (("at", a_t), ("rt", r_t), ("bt", b_t), ("kt", k_t), ("bh", b_h), ("kh", k_h),
                        ("vv", v)):
        chains[name].append(stack(val[rs]))
      chains["gd"].append(g_diag[cc * c:cc * c + 1])
    post.append((bonus, gn_g, gn_b))

  nb = RW_P * ncc
  at, rt, bt, kt, bh, kh, vv = (jnp.stack(chains[name]) for name in ("at", "rt", "bt", "kt", "bh", "kh", "vv"))
  at_b, vv_b, bh_b = at.astype(BF16), vv.astype(BF16), bh.astype(BF16)
  bk = jnp.concatenate([bt, kt], axis=1).astype(BF16)
  top = jnp.where(strict2, _bdot_nt(at_b, bk), 0.0)
  lblk = jnp.where(incl2, _bdot_nt(rt.astype(BF16), bk), 0.0)
  nm, mak = top[:, :, :c2], top[:, :, c2:]
  tinv = jnp.where(diag, 1.0, nm)
  npow = nm.astype(BF16)
  for _ in range(5):
    npow = _bdot(npow, npow).astype(BF16)
    tinv = tinv + _bdot(tinv.astype(BF16), npow)
  x1 = _bdot(mak.astype(BF16), vv_b)
  wu_b = _bdot(tinv.astype(BF16), jnp.concatenate([at_b, x1.astype(BF16)], axis=2)).astype(BF16)
  rhs = jnp.concatenate([wu_b, jnp.concatenate([jnp.zeros_like(vv_b), vv_b], axis=2)], axis=1)
  qy = _bdot(lblk.astype(BF16), rhs)
  q_h = (rt + qy[:, :, :c2]).astype(BF16)
  y0 = qy[:, :, c2:]
  uv = jnp.concatenate([wu_b[:, :, c2:], vv_b], axis=1)
  bkh = jnp.concatenate([bh_b, kh.astype(BF16)], axis=1)
  g_m = [(jnp.where(diag, chains["gd"][n], 0.0) + _dot_tn(wu_b[n, :, :c2], bh_b[n])).astype(BF16)
         for n in range(nb)]
  h_m = [_dot_tn(uv[n], bkh[n]) for n in range(nb)]

  states = [s_ref[p] for p in range(RW_P)]
  y_rows = [[] for _ in range(RW_P)]
  for cc in range(ncc):
    for p in range(RW_P):
      n = p * ncc + cc
      s_old_b = states[p].astype(BF16)
      y2 = _dot_nt(q_h[n], s_old_b) + y0[n]
      states[p] = jnp.dot(s_old_b, g_m[n], preferred_element_type=F32) + h_m[n]
      y_rows[p].append(y2[:c] + y2[c:])

  for p in range(RW_P):
    ls = slice(p * 128, (p + 1) * 128)
    s_ref[p] = states[p]
    bonus, gn_g, gn_b = post[p]
    y = jnp.concatenate(y_rows[p], axis=0)
    mean = _seg_sum(y) * (1.0 / nh)
    yc = y - mean
    var = _seg_sum(yc * yc) * (1.0 / nh)
    yn = yc * lax.rsqrt(var + GN_EPS) * gn_g + gn_b
    g = gr_ref[:, ls].astype(F32)
    o_ref[:, ls] = ((yn + bonus) * (g / (1.0 + jnp.exp(-g)))).astype(o_ref.dtype)


def _rwkv(z_x, lo_x, z_mp, lo_mp, pvec, mu_l, wup_h, wup_l, aup):
  b, s, _ = z_x.shape
  tb = RW_TB
  nt = s // tb + 1
  pw = 128 * RW_P

  def xmap(col):
    return lambda bi, hp, ti: (bi, jnp.maximum(ti - 1, 0), col // pw + hp)

  def mmap(col):
    return lambda bi, hp, ti: (0, col // pw + hp)

  return pl.pallas_call(
      _rwkv_kernel,
      name="rwkv7",
      grid=(b, R_PAIRS // RW_P, nt),
      in_specs=[
          pl.BlockSpec((None, tb, pw), xmap(Z_RR)),
          pl.BlockSpec((None, tb, pw), xmap(Z_RK)),
          pl.BlockSpec((None, tb, pw), xmap(Z_RV)),
          pl.BlockSpec((None, tb, LORA_PAD), lambda bi, hp, ti: (bi, jnp.maximum(ti - 1, 0), 0)),
          pl.BlockSpec((None, tb, pw), xmap(Z_GR)),
          pl.BlockSpec((tb, pw), mmap(Z_RR)),
          pl.BlockSpec((tb, pw), mmap(Z_RK)),
          pl.BlockSpec((tb, pw), mmap(Z_RV)),
          pl.BlockSpec((tb, LORA_PAD), lambda bi, hp, ti: (0, 0)),
          pl.BlockSpec((16, pw), lambda bi, hp, ti: (0, hp)),
          pl.BlockSpec((1, LORA_PAD), lambda bi, hp, ti: (0, 0)),
          pl.BlockSpec((LORA_PAD, pw), lambda bi, hp, ti: (0, hp)),
          pl.BlockSpec((LORA_PAD, pw), lambda bi, hp, ti: (0, hp)),
          pl.BlockSpec((LORA_PAD, pw), lambda bi, hp, ti: (0, hp)),
      ],
      out_specs=pl.BlockSpec((None, tb, pw), lambda bi, hp, ti: (bi, jnp.maximum(ti - 1, 0), hp)),
      out_shape=jax.ShapeDtypeStruct((b, s, R_WIDTH), BF16),
      scratch_shapes=[
          pltpu.VMEM((RW_P, 2 * R_HEAD, 2 * R_HEAD), F32),
          pltpu.VMEM((8, pw), F32),
          pltpu.VMEM((1, LORA_PAD), F32),
      ],
      compiler_params=pltpu.CompilerParams(
          dimension_semantics=("parallel", "parallel", "arbitrary"),
          vmem_limit_bytes=VMEM_LIMIT),
  )(z_x, z_x, z_x, lo_x, z_x, z_mp, z_mp, z_mp, lo_mp, pvec, mu_l, wup_h, wup_l, aup)


def _out_kernel(x_ref, oa_ref, or_ref, wa_ref, wr_ref, ge_ref, be_ref, gp_ref, bp_ref, o_ref):
  h = _ln_rows(x_ref[...], ge_ref[...], be_ref[...])
  y = (jnp.dot(oa_ref[...], wa_ref[...], preferred_element_type=F32)
       + jnp.dot(or_ref[...], wr_ref[...], preferred_element_type=F32))
  o_ref[...] = _ln_rows(DEEPNORM_ALPHA * h + y, gp_ref[...], bp_ref[...])


def _out_proj(x2d, oa, orw, wa, wr, ge, be, gp, bp, tm):
  m, d = x2d.shape
  vec = pl.BlockSpec((1, d), lambda i: (0, 0))
  return pl.pallas_call(
      _out_kernel,
      name="out_proj",
      grid=(m // tm,),
      in_specs=[
          pl.BlockSpec((tm, d), lambda i: (i, 0)),
          pl.BlockSpec((tm, A_WIDTH), lambda i: (i, 0)),
          pl.BlockSpec((tm, R_WIDTH), lambda i: (i, 0)),
          pl.BlockSpec((A_WIDTH, d), lambda i: (0, 0)),
          pl.BlockSpec((R_WIDTH, d), lambda i: (0, 0)),
          vec, vec, vec, vec,
      ],
      out_specs=pl.BlockSpec((tm, d), lambda i: (i, 0)),
      out_shape=jax.ShapeDtypeStruct((m, d), F32),
      compiler_params=pltpu.CompilerParams(
          dimension_semantics=("parallel",),
          vmem_limit_bytes=VMEM_LIMIT),
  )(x2d, oa, orw, wa, wr, ge, be, gp, bp)


def kernel(x, meta_tokens, ln_emb_g, ln_emb_b, rel_bias, w_in, w_out, lambda_q1, lambda_k1, lambda_q2,
           lambda_k2, subln_g, rw_mu, rw_w0, rw_w_up, rw_a0, rw_a_up, rw_k_k, rw_k_a, rw_r_k, rw_gn_g,
           rw_gn_b, ln_post_g, ln_post_b):
  b, s, d = x.shape
  l = 0
  wi = w_in[l]
  c_lo = 4 * A_WIDTH + 3 * R_WIDTH
  c_gr = c_lo + DECAY_LORA + ICLR_LORA
  lora_pad = LORA_PAD - DECAY_LORA - ICLR_LORA
  c_v = 2 * A_WIDTH
  w_main = jnp.concatenate([wi[:, :c_v], wi[:, c_v + A_WIDTH:c_lo], wi[:, c_gr:]], axis=1).astype(BF16)
  w_vt = wi[:, c_v:c_v + A_WIDTH].T.astype(BF16)
  w_lora = jnp.pad(wi[:, c_lo:c_gr], ((0, 0), (0, lora_pad))).astype(BF16)

  ge, be = ln_emb_g.reshape(1, d), ln_emb_b.reshape(1, d)
  x2d = x.reshape(b * s, d)
  z_x, vt_x, lo_x = _ln_matmul(x2d, ge, be, w_main, w_vt, w_lora, 1024)
  z_x, lo_x = z_x.reshape(b, s, -1), lo_x.reshape(b, s, LORA_PAD)
  z_m, vt_m, lo_m = _ln_matmul(meta_tokens, ge, be, w_main, w_vt, w_lora, N_META)

  bias_d, bias_s, bias_m = _bias_tiles(rel_bias)
  lam_p = jnp.stack([lambda_q1[l], lambda_k1[l], lambda_q2[l], lambda_k2[l]], axis=0)
  o_attn = _attention(z_x, vt_x, z_m, vt_m, bias_d, bias_s, bias_m, lam_p, subln_g[l].reshape(1, A_V_DIM))

  mu = rw_mu[l]
  zeros = jnp.zeros((R_WIDTH,), F32)
  pvec = jnp.stack([mu[:R_WIDTH], mu[R_WIDTH:2 * R_WIDTH], mu[2 * R_WIDTH:3 * R_WIDTH], rw_w0[l], rw_a0[l],
                    rw_k_k[l], rw_k_a[l], rw_r_k[l].reshape(R_WIDTH), rw_gn_g[l], rw_gn_b[l]]
                   + [zeros] * 6, axis=0)
  mu_l = jnp.pad(mu[3 * R_WIDTH:], (0, lora_pad)).reshape(1, LORA_PAD)
  wup = jnp.pad(rw_w_up[l], ((0, LORA_PAD - DECAY_LORA), (0, 0)))
  wup_h = wup.astype(BF16)
  wup_l = (wup - wup_h.astype(F32)).astype(BF16)
  aup = jnp.pad(rw_a_up[l], ((DECAY_LORA, lora_pad), (0, 0))).astype(BF16)
  front = ((RW_TB - N_META, 0), (0, 0))
  o_rwkv = _rwkv(z_x, lo_x, jnp.pad(z_m, front), jnp.pad(lo_m, front), pvec, mu_l, wup_h, wup_l, aup)

  wo = w_out[l].astype(BF16)
  out = _out_proj(x2d, o_attn.reshape(b * s, A_WIDTH), o_rwkv.reshape(b * s, R_WIDTH),
                  wo[:A_WIDTH], wo[A_WIDTH:], ge, be,
                  ln_post_g[l].reshape(1, d), ln_post_b[l].reshape(1, d), 512)
  return out.reshape(b, s, d)
```

```python
import functools
import math

import numpy as np
import jax
import jax.numpy as jnp
from jax import lax
from jax.experimental import pallas as pl
from jax.experimental.pallas import tpu as pltpu

D_MODEL = 2048
N_META = 16
A_HEADS = 8
A_V_DIM = 128
A_QK_DIM = 64
A_WIDTH = A_HEADS * A_V_DIM
R_HEAD = 64
R_WIDTH = 1024
R_PAIRS = R_WIDTH // (2 * R_HEAD)
DECAY_LORA = 96
ICLR_LORA = 96
LORA_PAD = 256
N_BUCKETS = 32
MAX_DISTANCE = 128
LN_EPS = 1e-5
SUBLN_EPS = 1e-5
GN_EPS = 64e-5
DEPTH = 1
DEEPNORM_ALPHA = (2 * DEPTH) ** 0.25
LAM_INIT = 0.8 - 0.6 * math.exp(-0.3 * 0)
NEG = -1e30

ATT_T = 256
ATT_G = 4
ONES_ROWS = 16
IN_TN = 1024
Z_Q, Z_K, Z_GA, Z_RR, Z_RK, Z_RV, Z_GR = (i * 1024 for i in range(7))
LOG2E = math.log2(math.e)
Q_SCALE = A_QK_DIM ** -0.5 * LOG2E
RW_TB = 128
RW_C = 64
RW_P = 4
VMEM_LIMIT = 56 * 1024 * 1024

F32 = jnp.float32
BF16 = jnp.bfloat16


def _ln_rows(x, g, b):
  mu = jnp.mean(x, axis=-1, keepdims=True)
  xc = x - mu
  var = jnp.mean(xc * xc, axis=-1, keepdims=True)
  return xc * lax.rsqrt(var + LN_EPS) * g + b


def _ln_mm_kernel(x_ref, g_ref, b_ref, wm_ref, wgr_ref, wvt_ref, wl_ref, om_ref, ovt_ref, ol_ref, hn_ref):
  j = pl.program_id(1)
  n_main = pl.num_programs(1) - 2

  @pl.when(j == 0)
  def _():
    hn_ref[...] = _ln_rows(x_ref[...], g_ref[...], b_ref[...]).astype(BF16)

  @pl.when(j < n_main - 1)
  def _():
    scale = jnp.where(j == Z_Q // IN_TN, Q_SCALE, 1.0)
    z = jnp.dot(hn_ref[...], wm_ref[...], preferred_element_type=F32)
    om_ref[...] = (z * scale).astype(om_ref.dtype)

  @pl.when(j == n_main - 1)
  def _():
    om_ref[...] = jnp.dot(hn_ref[...], wgr_ref[...], preferred_element_type=F32).astype(om_ref.dtype)

  @pl.when(j == n_main)
  def _():
    zt = _dot_nt(wvt_ref[...], hn_ref[...])
    tv = ovt_ref.shape[2]
    for c in range(ovt_ref.shape[0]):
      ovt_ref[c] = zt[:, c * tv:(c + 1) * tv].astype(ovt_ref.dtype)

  @pl.when(j == n_main + 1)
  def _():
    ol_ref[...] = jnp.dot(hn_ref[...], wl_ref[...], preferred_element_type=F32)


def _ln_matmul(x2d, g, b, w_all, w_gr, w_vt, w_lora, tm):
  m, d = x2d.shape
  tn = IN_TN
  nj = Z_GR // tn + 1
  n = nj * tn
  last = nj - 1
  tv = min(ATT_T, tm)
  v_tile = 2 * A_WIDTH // tn
  once = dict(pipeline_mode=pl.Buffered(1))

  def w_map(i, j):
    jj = jnp.minimum(j, last - 1)
    return (0, jnp.where(jj >= v_tile, jj + 1, jj))

  return pl.pallas_call(
      _ln_mm_kernel,
      name="ln_inproj",
      grid=(m // tm, nj + 2),
      in_specs=[
          pl.BlockSpec((tm, d), lambda i, j: (i, 0)),
          pl.BlockSpec((1, d), lambda i, j: (0, 0)),
          pl.BlockSpec((1, d), lambda i, j: (0, 0)),
          pl.BlockSpec((d, tn), w_map),
          pl.BlockSpec((d, tn), lambda i, j: (0, 0), **once),
          pl.BlockSpec((A_WIDTH, d), lambda i, j: (0, 0), **once),
          pl.BlockSpec((d, LORA_PAD), lambda i, j: (0, 0), **once),
      ],
      out_specs=[
          pl.BlockSpec((tm, tn), lambda i, j: (i, jnp.minimum(j, last))),
          pl.BlockSpec((tm // tv, A_WIDTH, tv), lambda i, j: (i, 0, 0)),
          pl.BlockSpec((tm, LORA_PAD), lambda i, j: (i, 0)),
      ],
      out_shape=[
          jax.ShapeDtypeStruct((m, n), BF16),
          jax.ShapeDtypeStruct((m // tv, A_WIDTH, tv), BF16),
          jax.ShapeDtypeStruct((m, LORA_PAD), F32),
      ],
      scratch_shapes=[pltpu.VMEM((tm, d), BF16)],
      compiler_params=pltpu.CompilerParams(
          dimension_semantics=("parallel", "arbitrary"),
          vmem_limit_bytes=VMEM_LIMIT),
  )(x2d, g, b, w_all, w_gr, w_vt, w_lora)


def _bucket_thresholds():
  n = np.arange(0, 4 * MAX_DISTANCE, dtype=np.int64)
  max_exact = N_BUCKETS // 2
  nf = np.maximum(n, 1).astype(np.float32)
  large = max_exact + (np.log(nf / np.float32(max_exact)) / np.float32(math.log(MAX_DISTANCE / max_exact))
                       * np.float32(N_BUCKETS - max_exact)).astype(np.int32)
  large = np.minimum(large, N_BUCKETS - 1)
  bucket = np.where(n < max_exact, n, large)
  assert np.all(np.diff(bucket) >= 0) and bucket[-1] == N_BUCKETS - 1
  return [int(np.argmax(bucket >= b)) for b in range(N_BUCKETS)]


_THR = _bucket_thresholds()


def _bias_kernel(rb_ref, diag_ref, sub_ref, meta_ref):
  h = pl.program_id(0)
  far = rb_ref[N_BUCKETS - 1, h]

  def bias_of(n):
    out = jnp.full(n.shape, (rb_ref[0, h] - far) * LOG2E, F32)
    for b in range(1, N_BUCKETS):
      out = jnp.where(n >= _THR[b], (rb_ref[b, h] - far) * LOG2E, out)
    return out

  t = ATT_T
  kj = lax.broadcasted_iota(jnp.int32, (t, t), 0)
  qi = lax.broadcasted_iota(jnp.int32, (t, t), 1)
  d = qi - kj
  diag_ref[...] = jnp.where(d >= 0, bias_of(d), NEG)
  sub_ref[...] = bias_of(d + t)
  km = lax.broadcasted_iota(jnp.int32, (N_META, t), 0)
  qm = lax.broadcasted_iota(jnp.int32, (N_META, t), 1)
  meta_ref[...] = bias_of(qm - km + N_META)


def _bias_tiles(rel_bias):
  t = ATT_T
  return pl.pallas_call(
      _bias_kernel,
      name="bias_tiles",
      grid=(A_HEADS,),
      in_specs=[pl.BlockSpec(memory_space=pltpu.SMEM)],
      out_specs=[
          pl.BlockSpec((None, t, t), lambda h: (h, 0, 0)),
          pl.BlockSpec((None, t, t), lambda h: (h, 0, 0)),
          pl.BlockSpec((None, N_META, t), lambda h: (h, 0, 0)),
      ],
      out_shape=[
          jax.ShapeDtypeStruct((A_HEADS, t, t), F32),
          jax.ShapeDtypeStruct((A_HEADS, t, t), F32),
          jax.ShapeDtypeStruct((A_HEADS, N_META, t), F32),
      ],
  )(rel_bias)


def _dot_nt(a, b):
  return lax.dot_general(a, b, (((1,), (1,)), ((), ())), preferred_element_type=F32)


def _attn_kernel(q_ref, kx_ref, vt_ref, km_ref, vmt_ref, ga_ref, bd_ref, bs_ref, bm_ref,
                 lam_ref, sg_ref, o_ref, m_ref, alpha_ref, acc_ref, pt_ref):
  t = ATT_T
  g = ATT_G
  nc = 2 * g
  dv = A_V_DIM
  qi = pl.program_id(2)
  lane = lax.broadcasted_iota(jnp.int32, (t, 128), 1)
  qs = []
  for hh in range(g):
    q = q_ref[:, hh * 128:(hh + 1) * 128]
    zero = jnp.zeros_like(q)
    qs += [jnp.where(lane < A_QK_DIM, q, zero), jnp.where(lane >= A_QK_DIM, q, zero)]

  m_ref[...] = jnp.full(m_ref.shape, NEG, F32)
  acc_ref[...] = jnp.zeros(acc_ref.shape, F32)
  alpha_ref[...] = jnp.ones(alpha_ref.shape, F32)
  pt_ref[...] = jnp.zeros(pt_ref.shape, BF16)

  def v_tile(j):
    ones = jnp.ones((ONES_ROWS, t), BF16)
    return [jnp.concatenate([vt_ref[j, hh * dv:(hh + 1) * dv, :], ones], axis=0) for hh in range(g)]

  def k_tile(j):
    off = pl.multiple_of(j * t, t)
    return [kx_ref[pl.ds(off, t), hh * 128:(hh + 1) * 128] for hh in range(g)]

  def softmax_stage(c, m_prev, s_list):
    m_new = m_prev
    for s in s_list:
      m_new = jnp.maximum(m_new, jnp.max(s, axis=0, keepdims=True))
    m_ref[c] = m_new
    alpha_ref[c] = jnp.exp2(m_prev - m_new)
    return [jnp.exp2(s - m_new).astype(BF16) for s in s_list]

  def step(j_cur, j_prev, biases):
    ks, vts = k_tile(j_cur), v_tile(j_prev)
    pv, st = [], []
    for c in range(nc):
      pv.append(jnp.dot(vts[c // 2], pt_ref[c], preferred_element_type=F32))
      s = _dot_nt(ks[c // 2], qs[c])
      st.append(s if biases is None else s + biases[c // 2])
    for c in range(nc):
      acc_ref[c] = alpha_ref[c] * acc_ref[c] + pv[c]
    for c in range(nc):
      pt_ref[c], = softmax_stage(c, m_ref[c], [st[c]])

  n_far = jnp.maximum(qi - 1, 0)

  def far_body(j, carry):
    step(2 * j, jnp.maximum(2 * j - 1, 0), None)
    step(2 * j + 1, 2 * j, None)
    return carry

  lax.fori_loop(0, n_far // 2, far_body, 0)

  @pl.when(n_far % 2 == 1)
  def _():
    step(n_far - 1, jnp.maximum(n_far - 2, 0), None)

  @pl.when(qi >= 1)
  def _():
    step(qi - 1, jnp.maximum(qi - 2, 0), [bs_ref[hh] for hh in range(g)])

  vts = v_tile(jnp.maximum(qi - 1, 0))
  ks = k_tile(qi)
  pv, st, sm = [], [], []
  for c in range(nc):
    hh = c // 2
    pv.append(jnp.dot(vts[hh], pt_ref[c], preferred_element_type=F32))
    st.append(_dot_nt(ks[hh], qs[c]) + bd_ref[hh])
    sm.append(_dot_nt(km_ref[:, hh * 128:(hh + 1) * 128], qs[c]) + jnp.where(qi == 0, bm_ref[hh], 0.0))
  for c in range(nc):
    acc_ref[c] = alpha_ref[c] * acc_ref[c] + pv[c]
  pts = [softmax_stage(c, m_ref[c], [st[c], sm[c]]) for c in range(nc)]
  vts = v_tile(qi)
  ones_m = jnp.ones((ONES_ROWS, N_META), BF16)
  vtm = [jnp.concatenate([vmt_ref[0, hh * dv:(hh + 1) * dv, :], ones_m], axis=0) for hh in range(g)]
  for c in range(nc):
    pv = (jnp.dot(vts[c // 2], pts[c][0], preferred_element_type=F32)
          + jnp.dot(vtm[c // 2], pts[c][1], preferred_element_type=F32))
    acc_ref[c] = alpha_ref[c] * acc_ref[c] + pv

  lp = lam_ref[...]
  lam = (jnp.exp(jnp.sum(lp[0:1] * lp[1:2], axis=1, keepdims=True))
         - jnp.exp(jnp.sum(lp[2:3] * lp[3:4], axis=1, keepdims=True)) + LAM_INIT)
  for hh in range(g):
    a0, a1 = acc_ref[2 * hh], acc_ref[2 * hh + 1]
    ot = a0[:dv] / a0[dv:dv + 1] - lam * (a1[:dv] / a1[dv:dv + 1])
    ot = ot * lax.rsqrt(jnp.mean(ot * ot, axis=0, keepdims=True) + SUBLN_EPS)
    o = ot.T * (sg_ref[...] * (1.0 - LAM_INIT))
    gate = ga_ref[:, hh * dv:(hh + 1) * dv].astype(F32)
    o_ref[:, hh * dv:(hh + 1) * dv] = (o * (gate / (1.0 + jnp.exp(-gate)))).astype(o_ref.dtype)


def _attention(z_x, vt_x, z_m, vt_m, bias_d, bias_s, bias_m, lam_p, subln_g):
  b, s, _ = z_x.shape
  t = ATT_T
  g = ATT_G
  w = 128 * g
  hb = A_HEADS // g
  kb, gb = Z_K // w, Z_GA // w
  return pl.pallas_call(
      _attn_kernel,
      name="diff_attn",
      grid=(b, hb, s // t),
      in_specs=[
          pl.BlockSpec((None, t, w), lambda bi, hi, qi: (bi, qi, hi)),
          pl.BlockSpec((None, s, w), lambda bi, hi, qi: (bi, 0, kb + hi)),
          pl.BlockSpec((None, s // t, w, t), lambda bi, hi, qi: (bi, 0, hi, 0)),
          pl.BlockSpec((N_META, w), lambda bi, hi, qi: (0, kb + hi)),
          pl.BlockSpec((1, w, N_META), lambda bi, hi, qi: (0, hi, 0)),
          pl.BlockSpec((None, t, w), lambda bi, hi, qi: (bi, qi, gb + hi)),
          pl.BlockSpec((g, t, t), lambda bi, hi, qi: (hi, 0, 0)),
          pl.BlockSpec((g, t, t), lambda bi, hi, qi: (hi, 0, 0)),
          pl.BlockSpec((g, N_META, t), lambda bi, hi, qi: (hi, 0, 0)),
          pl.BlockSpec((4, A_QK_DIM), lambda bi, hi, qi: (0, 0)),
          pl.BlockSpec((1, A_V_DIM), lambda bi, hi, qi: (0, 0)),
      ],
      out_specs=pl.BlockSpec((None, t, w), lambda bi, hi, qi: (bi, qi, hi)),
      out_shape=jax.ShapeDtypeStruct((b, s, A_WIDTH), BF16),
      scratch_shapes=[
          pltpu.VMEM((2 * g, 1, t), F32),
          pltpu.VMEM((2 * g, 1, t), F32),
          pltpu.VMEM((2 * g, A_V_DIM + ONES_ROWS, t), F32),
          pltpu.VMEM((2 * g, t, t), BF16),
      ],
      compiler_params=pltpu.CompilerParams(
          dimension_semantics=("parallel", "parallel", "arbitrary"),
          vmem_limit_bytes=VMEM_LIMIT),
  )(z_x, z_x, vt_x.reshape(b, s // t, A_WIDTH, t), z_m, vt_m, z_x, bias_d, bias_s, bias_m, lam_p, subln_g)


def _seg_sum(x):
  lane = lax.broadcasted_iota(jnp.int32, x.shape, 1)
  first = lane < R_HEAD
  lo = jnp.sum(jnp.where(first, x, 0.0), axis=1, keepdims=True)
  hi = jnp.sum(jnp.where(first, 0.0, x), axis=1, keepdims=True)
  return jnp.where(first, lo, hi)


def _split_bf16(x):
  hi = x.astype(BF16)
  return hi, x - hi.astype(F32)


def _dot_tn(a, b):
  return lax.dot_general(a, b, (((0,), (0,)), ((), ())), preferred_element_type=F32)


def _bdot(a, b):
  return lax.dot_general(a, b, (((2,), (1,)), ((0,), (0,))), preferred_element_type=F32)


def _bdot_nt(a, b):
  return lax.dot_general(a, b, (((2,), (2,)), ((0,), (0,))), preferred_element_type=F32)


def _rwkv_kernel(rx_ref, kx_ref, vx_ref, lx_ref, gr_ref, rm_ref, kmt_ref, vmt_ref, lm_ref,
                 pv_ref, mul_ref, wuph_ref, wupl_ref, aup_ref, o_ref, s_ref, prev_ref, prevl_ref):
  tb, c = RW_TB, RW_C
  nh = R_HEAD
  c2 = 2 * c
  ti = pl.program_id(2)
  is_meta = ti == 0

  @pl.when(is_meta)
  def _():
    s_ref[...] = jnp.zeros_like(s_ref)
    prev_ref[...] = jnp.zeros_like(prev_ref)
    prevl_ref[...] = jnp.zeros_like(prevl_ref)

  row = lax.broadcasted_iota(jnp.int32, (tb, 128), 0)
  rowl = lax.broadcasted_iota(jnp.int32, (tb, LORA_PAD), 0)

  def shifted(z, prev, mu, rows):
    z_prev = jnp.where(rows == 0, prev, pltpu.roll(z, 1, 0))
    return z + (z_prev - z) * mu

  z_l = jnp.where(is_meta, lm_ref[...], lx_ref[...])
  lo = shifted(z_l, prevl_ref[...], mul_ref[...], rowl)
  prevl_ref[...] = z_l[tb - 1:tb]
  th_h, th_l = _split_bf16(jnp.tanh(lo))
  w_lora = (jnp.dot(th_h, wuph_ref[...], preferred_element_type=F32)
            + jnp.dot(th_h, wupl_ref[...], preferred_element_type=F32)
            + jnp.dot(th_l.astype(BF16), wuph_ref[...], preferred_element_type=F32))
  a_lora = jnp.dot(lo.astype(BF16), aup_ref[...], preferred_element_type=F32)

  ii = lax.broadcasted_iota(jnp.int32, (tb, tb), 0)
  jj = lax.broadcasted_iota(jnp.int32, (tb, tb), 1)
  shift = int(math.log2(c))
  same = lax.shift_right_logical(ii, shift) == lax.shift_right_logical(jj, shift)
  cum_op = jnp.where(same, jnp.where(jj <= ii, 1.0, 0.0), 0.0).astype(BF16)

  ci = lax.broadcasted_iota(jnp.int32, (c2, c2), 0)
  cj = lax.broadcasted_iota(jnp.int32, (c2, c2), 1)
  diag = ci == cj
  strict2 = jnp.concatenate([cj < ci, cj < ci], axis=1)
  incl2 = jnp.concatenate([cj <= ci, cj <= ci], axis=1)
  first = lax.broadcasted_iota(jnp.int32, (c, 128), 1) < nh

  def stack(x):
    return jnp.concatenate([jnp.where(first, x, 0.0), jnp.where(first, 0.0, x)], axis=0)

  ncc = tb // c
  chains = {name: [] for name in ("at", "rt", "bt", "kt", "bh", "kh", "vv", "gd")}
  post = []
  for p in range(RW_P):
    ls = slice(p * 128, (p + 1) * 128)
    pv = pv_ref[:, ls]
    mu_r, mu_k, mu_v = pv[0:1], pv[1:2], pv[2:3]
    w0, a0, k_k, k_a, r_k, gn_g, gn_b = pv[3:4], pv[4:5], pv[5:6], pv[6:7], pv[7:8], pv[8:9], pv[9:10]

    z_r = jnp.where(is_meta, rm_ref[:, ls], rx_ref[:, ls]).astype(F32)
    z_k = jnp.where(is_meta, kmt_ref[:, ls], kx_ref[:, ls]).astype(F32)
    z_v = jnp.where(is_meta, vmt_ref[:, ls], vx_ref[:, ls]).astype(F32)
    r = shifted(z_r, prev_ref[0:1, ls], mu_r, row)
    k = shifted(z_k, prev_ref[1:2, ls], mu_k, row)
    v = shifted(z_v, prev_ref[2:3, ls], mu_v, row)
    prev_ref[0:1, ls] = z_r[tb - 1:tb]
    prev_ref[1:2, ls] = z_k[tb - 1:tb]
    prev_ref[2:3, ls] = z_v[tb - 1:tb]

    u = -(w0 + w_lora[:, ls])
    softplus = jnp.maximum(u, 0.0) + jnp.log(1.0 + jnp.exp(-jnp.abs(u)))
    logw = -jnp.exp(-softplus - 0.5)
    a = 1.0 / (1.0 + jnp.exp(-(a0 + a_lora[:, ls])))
    kk = k * k_k
    kk = kk / jnp.maximum(jnp.sqrt(_seg_sum(kk * kk)), 1e-12)
    k_mod = k * (1.0 + (a - 1.0) * k_a)
    bonus = _seg_sum(r * k_mod * r_k) * v

    lw_h, lw_r = _split_bf16(logw)
    lw_m, lw_l = _split_bf16(lw_r)
    cum3 = jnp.dot(cum_op, jnp.concatenate([lw_h, lw_m, lw_l.astype(BF16)], axis=1),
                   preferred_element_type=F32)
    cum = cum3[:, :128] + cum3[:, 128:256] + cum3[:, 256:]
    tot = jnp.concatenate([jnp.broadcast_to(cum[cc * c + c - 1:cc * c + c], (c, 128)) for cc in range(ncc)],
                          axis=0)
    p_inv = jnp.exp(-cum)
    a_t = -kk * jnp.exp(cum - logw)
    b_t = kk * a * p_inv
    k_t = k_mod * p_inv
    r_t = r * jnp.exp(cum)
    p_end = jnp.exp(tot - cum)
    b_h = kk * a * p_end
    k_h = k_mod * p_end
    g_diag = jnp.exp(tot)

    for cc in range(ncc):
      rs = slice(cc * c, (cc + 1) * c)
      for name, val in (("at", a_t), ("rt", r_t), ("bt", b_t), ("kt", k_t), ("bh", b_h), ("kh", k_h),
                        ("vv", v)):
        chains[name].append(stack(val[rs]))
      chains["gd"].append(g_diag[cc * c:cc * c + 1])
    post.append((bonus, gn_g, gn_b))

  nb = RW_P * ncc
  at, rt, bt, kt, bh, kh, vv = (jnp.stack(chains[name]) for name in ("at", "rt", "bt", "kt", "bh", "kh", "vv"))
  at_b, vv_b, bh_b = at.astype(BF16), vv.astype(BF16), bh.astype(BF16)
  bk = jnp.concatenate([bt, kt], axis=1).astype(BF16)
  top = jnp.where(strict2, _bdot_nt(at_b, bk), 0.0)
  lblk = jnp.where(incl2, _bdot_nt(rt.astype(BF16), bk), 0.0)
  nm, mak = top[:, :, :c2], top[:, :, c2:]
  tinv = jnp.where(diag, 1.0, nm)
  npow = nm.astype(BF16)
  for _ in range(5):
    npow = _bdot(npow, npow).astype(BF16)
    tinv = tinv + _bdot(tinv.astype(BF16), npow)
  x1 = _bdot(mak.astype(BF16), vv_b)
  wu_b = _bdot(tinv.astype(BF16), jnp.concatenate([at_b, x1.astype(BF16)], axis=2)).astype(BF16)
  rhs = jnp.concatenate([wu_b, jnp.concatenate([jnp.zeros_like(vv_b), vv_b], axis=2)], axis=1)
  qy = _bdot(lblk.astype(BF16), rhs)
  q_h = (rt + qy[:, :, :c2]).astype(BF16)
  y0 = qy[:, :, c2:]
  uv = jnp.concatenate([wu_b[:, :, c2:], vv_b], axis=1)
  bkh = jnp.concatenate([bh_b, kh.astype(BF16)], axis=1)
  g_m = [(jnp.where(diag, chains["gd"][n], 0.0) + _dot_tn(wu_b[n, :, :c2], bh_b[n])).astype(BF16)
         for n in range(nb)]
  h_m = [_dot_tn(uv[n], bkh[n]) for n in range(nb)]

  states = [s_ref[p] for p in range(RW_P)]
  y_rows = [[] for _ in range(RW_P)]
  for cc in range(ncc):
    for p in range(RW_P):
      n = p * ncc + cc
      s_old_b = states[p].astype(BF16)
      y2 = _dot_nt(q_h[n], s_old_b) + y0[n]
      states[p] = jnp.dot(s_old_b, g_m[n], preferred_element_type=F32) + h_m[n]
      y_rows[p].append(y2[:c] + y2[c:])

  for p in range(RW_P):
    ls = slice(p * 128, (p + 1) * 128)
    s_ref[p] = states[p]
    bonus, gn_g, gn_b = post[p]
    y = jnp.concatenate(y_rows[p], axis=0)
    mean = _seg_sum(y) * (1.0 / nh)
    yc = y - mean
    var = _seg_sum(yc * yc) * (1.0 / nh)
    yn = yc * lax.rsqrt(var + GN_EPS) * gn_g + gn_b
    g = gr_ref[:, ls].astype(F32)
    o_ref[:, ls] = ((yn + bonus) * (g / (1.0 + jnp.exp(-g)))).astype(o_ref.dtype)


def _rwkv(z_x, lo_x, z_mp, lo_mp, pvec, mu_l, wup_h, wup_l, aup):
  b, s, _ = z_x.shape
  tb = RW_TB
  nt = s // tb + 1
  pw = 128 * RW_P

  def xmap(col):
    return lambda bi, hp, ti: (bi, jnp.maximum(ti - 1, 0), col // pw + hp)

  def mmap(col):
    return lambda bi, hp, ti: (0, col // pw + hp)

  return pl.pallas_call(
      _rwkv_kernel,
      name="rwkv7",
      grid=(b, R_PAIRS // RW_P, nt),
      in_specs=[
          pl.BlockSpec((None, tb, pw), xmap(Z_RR)),
          pl.BlockSpec((None, tb, pw), xmap(Z_RK)),
          pl.BlockSpec((None, tb, pw), xmap(Z_RV)),
          pl.BlockSpec((None, tb, LORA_PAD), lambda bi, hp, ti: (bi, jnp.maximum(ti - 1, 0), 0)),
          pl.BlockSpec((None, tb, pw), xmap(Z_GR)),
          pl.BlockSpec((tb, pw), mmap(Z_RR)),
          pl.BlockSpec((tb, pw), mmap(Z_RK)),
          pl.BlockSpec((tb, pw), mmap(Z_RV)),
          pl.BlockSpec((tb, LORA_PAD), lambda bi, hp, ti: (0, 0)),
          pl.BlockSpec((16, pw), lambda bi, hp, ti: (0, hp)),
          pl.BlockSpec((1, LORA_PAD), lambda bi, hp, ti: (0, 0)),
          pl.BlockSpec((LORA_PAD, pw), lambda bi, hp, ti: (0, hp)),
          pl.BlockSpec((LORA_PAD, pw), lambda bi, hp, ti: (0, hp)),
          pl.BlockSpec((LORA_PAD, pw), lambda bi, hp, ti: (0, hp)),
      ],
      out_specs=pl.BlockSpec((None, tb, pw), lambda bi, hp, ti: (bi, jnp.maximum(ti - 1, 0), hp)),
      out_shape=jax.ShapeDtypeStruct((b, s, R_WIDTH), BF16),
      scratch_shapes=[
          pltpu.VMEM((RW_P, 2 * R_HEAD, 2 * R_HEAD), F32),
          pltpu.VMEM((8, pw), F32),
          pltpu.VMEM((1, LORA_PAD), F32),
      ],
      compiler_params=pltpu.CompilerParams(
          dimension_semantics=("parallel", "parallel", "arbitrary"),
          vmem_limit_bytes=VMEM_LIMIT),
  )(z_x, z_x, z_x, lo_x, z_x, z_mp, z_mp, z_mp, lo_mp, pvec, mu_l, wup_h, wup_l, aup)


def _out_kernel(x_ref, oa_ref, or_ref, wa_ref, wr_ref, ge_ref, be_ref, gp_ref, bp_ref, o_ref):
  h = _ln_rows(x_ref[...], ge_ref[...], be_ref[...])
  y = (jnp.dot(oa_ref[...], wa_ref[...], preferred_element_type=F32)
       + jnp.dot(or_ref[...], wr_ref[...], preferred_element_type=F32))
  o_ref[...] = _ln_rows(DEEPNORM_ALPHA * h + y, gp_ref[...], bp_ref[...])


def _out_proj(x2d, oa, orw, wa, wr, ge, be, gp, bp, tm):
  m, d = x2d.shape
  vec = pl.BlockSpec((1, d), lambda i: (0, 0))
  return pl.pallas_call(
      _out_kernel,
      name="out_proj",
      grid=(m // tm,),
      in_specs=[
          pl.BlockSpec((tm, d), lambda i: (i, 0)),
          pl.BlockSpec((tm, A_WIDTH), lambda i: (i, 0)),
          pl.BlockSpec((tm, R_WIDTH), lambda i: (i, 0)),
          pl.BlockSpec((A_WIDTH, d), lambda i: (0, 0)),
          pl.BlockSpec((R_WIDTH, d), lambda i: (0, 0)),
          vec, vec, vec, vec,
      ],
      out_specs=pl.BlockSpec((tm, d), lambda i: (i, 0)),
      out_shape=jax.ShapeDtypeStruct((m, d), F32),
      compiler_params=pltpu.CompilerParams(
          dimension_semantics=("parallel",),
          vmem_limit_bytes=VMEM_LIMIT),
  )(x2d, oa, orw, wa, wr, ge, be, gp, bp)


def kernel(x, meta_tokens, ln_emb_g, ln_emb_b, rel_bias, w_in, w_out, lambda_q1, lambda_k1, lambda_q2,
           lambda_k2, subln_g, rw_mu, rw_w0, rw_w_up, rw_a0, rw_a_up, rw_k_k, rw_k_a, rw_r_k, rw_gn_g,
           rw_gn_b, ln_post_g, ln_post_b):
  b, s, d = x.shape
  l = 0
  wi = w_in[l]
  c_lo = 4 * A_WIDTH + 3 * R_WIDTH
  c_gr = c_lo + DECAY_LORA + ICLR_LORA
  lora_pad = LORA_PAD - DECAY_LORA - ICLR_LORA
  c_v = 2 * A_WIDTH
  w_all = wi.astype(BF16)
  w_gr = wi[:, c_gr:].astype(BF16)
  w_vt = wi[:, c_v:c_v + A_WIDTH].T.astype(BF16)
  w_lora = jnp.pad(wi[:, c_lo:c_gr], ((0, 0), (0, lora_pad))).astype(BF16)

  ge, be = ln_emb_g.reshape(1, d), ln_emb_b.reshape(1, d)
  x2d = x.reshape(b * s, d)
  z_x, vt_x, lo_x = _ln_matmul(x2d, ge, be, w_all, w_gr, w_vt, w_lora, 1024)
  z_x, lo_x = z_x.reshape(b, s, -1), lo_x.reshape(b, s, LORA_PAD)
  z_m, vt_m, lo_m = _ln_matmul(meta_tokens, ge, be, w_all, w_gr, w_vt, w_lora, N_META)

  bias_d, bias_s, bias_m = _bias_tiles(rel_bias)
  lam_p = jnp.stack([lambda_q1[l], lambda_k1[l], lambda_q2[l], lambda_k2[l]], axis=0)
  o_attn = _attention(z_x, vt_x, z_m, vt_m, bias_d, bias_s, bias_m, lam_p, subln_g[l].reshape(1, A_V_DIM))

  mu = rw_mu[l]
  zeros = jnp.zeros((R_WIDTH,), F32)
  pvec = jnp.stack([mu[:R_WIDTH], mu[R_WIDTH:2 * R_WIDTH], mu[2 * R_WIDTH:3 * R_WIDTH], rw_w0[l], rw_a0[l],
                    rw_k_k[l], rw_k_a[l], rw_r_k[l].reshape(R_WIDTH), rw_gn_g[l], rw_gn_b[l]]
                   + [zeros] * 6, axis=0)
  mu_l = jnp.pad(mu[3 * R_WIDTH:], (0, lora_pad)).reshape(1, LORA_PAD)
  wup = jnp.pad(rw_w_up[l], ((0, LORA_PAD - DECAY_LORA), (0, 0)))
  wup_h = wup.astype(BF16)
  wup_l = (wup - wup_h.astype(F32)).astype(BF16)
  aup = jnp.pad(rw_a_up[l], ((DECAY_LORA, lora_pad), (0, 0))).astype(BF16)
  front = ((RW_TB - N_META, 0), (0, 0))
  o_rwkv = _rwkv(z_x, lo_x, jnp.pad(z_m, front), jnp.pad(lo_m, front), pvec, mu_l, wup_h, wup_l, aup)

  wo = w_out[l].astype(BF16)
  out = _out_proj(x2d, o_attn.reshape(b * s, A_WIDTH), o_rwkv.reshape(b * s, R_WIDTH),
                  wo[:A_WIDTH], wo[A_WIDTH:], ge, be,
                  ln_post_g[l].reshape(1, d), ln_post_b[l].reshape(1, d), 512)
  return out.reshape(b, s, d)
```

```python
import functools
import math

import numpy as np
import jax
import jax.numpy as jnp
from jax import lax
from jax.experimental import pallas as pl
from jax.experimental.pallas import tpu as pltpu

D_MODEL = 2048
N_META = 16
A_HEADS = 8
A_V_DIM = 128
A_QK_DIM = 64
A_WIDTH = A_HEADS * A_V_DIM
R_HEAD = 64
R_WIDTH = 1024
R_PAIRS = R_WIDTH // (2 * R_HEAD)
DECAY_LORA = 96
ICLR_LORA = 96
LORA_PAD = 256
N_BUCKETS = 32
MAX_DISTANCE = 128
LN_EPS = 1e-5
SUBLN_EPS = 1e-5
GN_EPS = 64e-5
DEPTH = 1
DEEPNORM_ALPHA = (2 * DEPTH) ** 0.25
LAM_INIT = 0.8 - 0.6 * math.exp(-0.3 * 0)
NEG = -1e30

ATT_T = 256
ATT_G = 4
ONES_ROWS = 16
IN_TN = 1024
Z_Q, Z_K, Z_GA, Z_RR, Z_RK, Z_RV, Z_GR = (i * 1024 for i in range(7))
LOG2E = math.log2(math.e)
Q_SCALE = A_QK_DIM ** -0.5 * LOG2E
RW_TB = 128
RW_C = 64
RW_P = 8
VMEM_LIMIT = 56 * 1024 * 1024

F32 = jnp.float32
BF16 = jnp.bfloat16


def _ln_rows(x, g, b):
  mu = jnp.mean(x, axis=-1, keepdims=True)
  xc = x - mu
  var = jnp.mean(xc * xc, axis=-1, keepdims=True)
  return xc * lax.rsqrt(var + LN_EPS) * g + b


def _ln_mm_kernel(x_ref, g_ref, b_ref, wm_ref, wgr_ref, wvt_ref, wl_ref, om_ref, ovt_ref, ol_ref, hn_ref):
  j = pl.program_id(1)
  n_main = pl.num_programs(1) - 2

  @pl.when(j == 0)
  def _():
    hn_ref[...] = _ln_rows(x_ref[...], g_ref[...], b_ref[...]).astype(BF16)

  @pl.when(j < n_main - 1)
  def _():
    scale = jnp.where(j == Z_Q // IN_TN, Q_SCALE, 1.0)
    z = jnp.dot(hn_ref[...], wm_ref[...], preferred_element_type=F32)
    om_ref[...] = (z * scale).astype(om_ref.dtype)

  @pl.when(j == n_main - 1)
  def _():
    om_ref[...] = jnp.dot(hn_ref[...], wgr_ref[...], preferred_element_type=F32).astype(om_ref.dtype)

  @pl.when(j == n_main)
  def _():
    zt = _dot_nt(wvt_ref[...], hn_ref[...])
    tv = ovt_ref.shape[2]
    for c in range(ovt_ref.shape[0]):
      ovt_ref[c] = zt[:, c * tv:(c + 1) * tv].astype(ovt_ref.dtype)

  @pl.when(j == n_main + 1)
  def _():
    ol_ref[...] = jnp.dot(hn_ref[...], wl_ref[...], preferred_element_type=F32)


def _ln_matmul(x2d, g, b, w_all, w_gr, w_vt, w_lora, tm):
  m, d = x2d.shape
  tn = IN_TN
  nj = Z_GR // tn + 1
  n = nj * tn
  last = nj - 1
  tv = min(ATT_T, tm)
  v_tile = 2 * A_WIDTH // tn
  once = dict(pipeline_mode=pl.Buffered(1))

  def w_map(i, j):
    jj = jnp.minimum(j, last - 1)
    return (0, jnp.where(jj >= v_tile, jj + 1, jj))

  return pl.pallas_call(
      _ln_mm_kernel,
      name="ln_inproj",
      grid=(m // tm, nj + 2),
      in_specs=[
          pl.BlockSpec((tm, d), lambda i, j: (i, 0)),
          pl.BlockSpec((1, d), lambda i, j: (0, 0)),
          pl.BlockSpec((1, d), lambda i, j: (0, 0)),
          pl.BlockSpec((d, tn), w_map),
          pl.BlockSpec((d, tn), lambda i, j: (0, 0), **once),
          pl.BlockSpec((A_WIDTH, d), lambda i, j: (0, 0), **once),
          pl.BlockSpec((d, LORA_PAD), lambda i, j: (0, 0), **once),
      ],
      out_specs=[
          pl.BlockSpec((tm, tn), lambda i, j: (i, jnp.minimum(j, last))),
          pl.BlockSpec((tm // tv, A_WIDTH, tv), lambda i, j: (i, 0, 0)),
          pl.BlockSpec((tm, LORA_PAD), lambda i, j: (i, 0)),
      ],
      out_shape=[
          jax.ShapeDtypeStruct((m, n), BF16),
          jax.ShapeDtypeStruct((m // tv, A_WIDTH, tv), BF16),
          jax.ShapeDtypeStruct((m, LORA_PAD), F32),
      ],
      scratch_shapes=[pltpu.VMEM((tm, d), BF16)],
      compiler_params=pltpu.CompilerParams(
          dimension_semantics=("parallel", "arbitrary"),
          vmem_limit_bytes=VMEM_LIMIT),
  )(x2d, g, b, w_all, w_gr, w_vt, w_lora)


def _bucket_thresholds():
  n = np.arange(0, 4 * MAX_DISTANCE, dtype=np.int64)
  max_exact = N_BUCKETS // 2
  nf = np.maximum(n, 1).astype(np.float32)
  large = max_exact + (np.log(nf / np.float32(max_exact)) / np.float32(math.log(MAX_DISTANCE / max_exact))
                       * np.float32(N_BUCKETS - max_exact)).astype(np.int32)
  large = np.minimum(large, N_BUCKETS - 1)
  bucket = np.where(n < max_exact, n, large)
  assert np.all(np.diff(bucket) >= 0) and bucket[-1] == N_BUCKETS - 1
  return [int(np.argmax(bucket >= b)) for b in range(N_BUCKETS)]


_THR = _bucket_thresholds()


def _bias_kernel(rb_ref, diag_ref, sub_ref, meta_ref):
  h = pl.program_id(0)
  far = rb_ref[N_BUCKETS - 1, h]

  def bias_of(n):
    out = jnp.full(n.shape, (rb_ref[0, h] - far) * LOG2E, F32)
    for b in range(1, N_BUCKETS):
      out = jnp.where(n >= _THR[b], (rb_ref[b, h] - far) * LOG2E, out)
    return out

  t = ATT_T
  kj = lax.broadcasted_iota(jnp.int32, (t, t), 0)
  qi = lax.broadcasted_iota(jnp.int32, (t, t), 1)
  d = qi - kj
  diag_ref[...] = jnp.where(d >= 0, bias_of(d), NEG)
  sub_ref[...] = bias_of(d + t)
  km = lax.broadcasted_iota(jnp.int32, (N_META, t), 0)
  qm = lax.broadcasted_iota(jnp.int32, (N_META, t), 1)
  meta_ref[...] = bias_of(qm - km + N_META)


def _bias_tiles(rel_bias):
  t = ATT_T
  return pl.pallas_call(
      _bias_kernel,
      name="bias_tiles",
      grid=(A_HEADS,),
      in_specs=[pl.BlockSpec(memory_space=pltpu.SMEM)],
      out_specs=[
          pl.BlockSpec((None, t, t), lambda h: (h, 0, 0)),
          pl.BlockSpec((None, t, t), lambda h: (h, 0, 0)),
          pl.BlockSpec((None, N_META, t), lambda h: (h, 0, 0)),
      ],
      out_shape=[
          jax.ShapeDtypeStruct((A_HEADS, t, t), F32),
          jax.ShapeDtypeStruct((A_HEADS, t, t), F32),
          jax.ShapeDtypeStruct((A_HEADS, N_META, t), F32),
      ],
  )(rel_bias)


def _dot_nt(a, b):
  return lax.dot_general(a, b, (((1,), (1,)), ((), ())), preferred_element_type=F32)


def _attn_kernel(q_ref, kx_ref, vt_ref, km_ref, vmt_ref, ga_ref, bd_ref, bs_ref, bm_ref,
                 lam_ref, sg_ref, o_ref, m_ref, alpha_ref, acc_ref, pt_ref):
  t = ATT_T
  g = ATT_G
  nc = 2 * g
  dv = A_V_DIM
  qi = pl.program_id(2)
  lane = lax.broadcasted_iota(jnp.int32, (t, 128), 1)
  qs = []
  for hh in range(g):
    q = q_ref[:, hh * 128:(hh + 1) * 128]
    zero = jnp.zeros_like(q)
    qs += [jnp.where(lane < A_QK_DIM, q, zero), jnp.where(lane >= A_QK_DIM, q, zero)]

  m_ref[...] = jnp.full(m_ref.shape, NEG, F32)
  acc_ref[...] = jnp.zeros(acc_ref.shape, F32)
  alpha_ref[...] = jnp.ones(alpha_ref.shape, F32)
  pt_ref[...] = jnp.zeros(pt_ref.shape, BF16)

  def v_tile(j):
    ones = jnp.ones((ONES_ROWS, t), BF16)
    return [jnp.concatenate([vt_ref[j, hh * dv:(hh + 1) * dv, :], ones], axis=0) for hh in range(g)]

  def k_tile(j):
    off = pl.multiple_of(j * t, t)
    return [kx_ref[pl.ds(off, t), hh * 128:(hh + 1) * 128] for hh in range(g)]

  def softmax_stage(c, m_prev, s_list):
    m_new = m_prev
    for s in s_list:
      m_new = jnp.maximum(m_new, jnp.max(s, axis=0, keepdims=True))
    m_ref[c] = m_new
    alpha_ref[c] = jnp.exp2(m_prev - m_new)
    return [jnp.exp2(s - m_new).astype(BF16) for s in s_list]

  def step(j_cur, j_prev, biases):
    ks, vts = k_tile(j_cur), v_tile(j_prev)
    pv, st = [], []
    for c in range(nc):
      pv.append(jnp.dot(vts[c // 2], pt_ref[c], preferred_element_type=F32))
      s = _dot_nt(ks[c // 2], qs[c])
      st.append(s if biases is None else s + biases[c // 2])
    for c in range(nc):
      acc_ref[c] = alpha_ref[c] * acc_ref[c] + pv[c]
    for c in range(nc):
      pt_ref[c], = softmax_stage(c, m_ref[c], [st[c]])

  n_far = jnp.maximum(qi - 1, 0)

  def far_body(j, carry):
    step(2 * j, jnp.maximum(2 * j - 1, 0), None)
    step(2 * j + 1, 2 * j, None)
    return carry

  lax.fori_loop(0, n_far // 2, far_body, 0)

  @pl.when(n_far % 2 == 1)
  def _():
    step(n_far - 1, jnp.maximum(n_far - 2, 0), None)

  @pl.when(qi >= 1)
  def _():
    step(qi - 1, jnp.maximum(qi - 2, 0), [bs_ref[hh] for hh in range(g)])

  vts = v_tile(jnp.maximum(qi - 1, 0))
  ks = k_tile(qi)
  pv, st, sm = [], [], []
  for c in range(nc):
    hh = c // 2
    pv.append(jnp.dot(vts[hh], pt_ref[c], preferred_element_type=F32))
    st.append(_dot_nt(ks[hh], qs[c]) + bd_ref[hh])
    sm.append(_dot_nt(km_ref[:, hh * 128:(hh + 1) * 128], qs[c]) + jnp.where(qi == 0, bm_ref[hh], 0.0))
  for c in range(nc):
    acc_ref[c] = alpha_ref[c] * acc_ref[c] + pv[c]
  pts = [softmax_stage(c, m_ref[c], [st[c], sm[c]]) for c in range(nc)]
  vts = v_tile(qi)
  ones_m = jnp.ones((ONES_ROWS, N_META), BF16)
  vtm = [jnp.concatenate([vmt_ref[0, hh * dv:(hh + 1) * dv, :], ones_m], axis=0) for hh in range(g)]
  for c in range(nc):
    pv = (jnp.dot(vts[c // 2], pts[c][0], preferred_element_type=F32)
          + jnp.dot(vtm[c // 2], pts[c][1], preferred_element_type=F32))
    acc_ref[c] = alpha_ref[c] * acc_ref[c] + pv

  lp = lam_ref[...]
  lam = (jnp.exp(jnp.sum(lp[0:1] * lp[1:2], axis=1, keepdims=True))
         - jnp.exp(jnp.sum(lp[2:3] * lp[3:4], axis=1, keepdims=True)) + LAM_INIT)
  for hh in range(g):
    a0, a1 = acc_ref[2 * hh], acc_ref[2 * hh + 1]
    ot = a0[:dv] / a0[dv:dv + 1] - lam * (a1[:dv] / a1[dv:dv + 1])
    ot = ot * lax.rsqrt(jnp.mean(ot * ot, axis=0, keepdims=True) + SUBLN_EPS)
    o = ot.T * (sg_ref[...] * (1.0 - LAM_INIT))
    gate = ga_ref[:, hh * dv:(hh + 1) * dv].astype(F32)
    o_ref[:, hh * dv:(hh + 1) * dv] = (o * (gate / (1.0 + jnp.exp(-gate)))).astype(o_ref.dtype)


def _attention(z_x, vt_x, z_m, vt_m, bias_d, bias_s, bias_m, lam_p, subln_g):
  b, s, _ = z_x.shape
  t = ATT_T
  g = ATT_G
  w = 128 * g
  hb = A_HEADS // g
  kb, gb = Z_K // w, Z_GA // w
  return pl.pallas_call(
      _attn_kernel,
      name="diff_attn",
      grid=(b, hb, s // t),
      in_specs=[
          pl.BlockSpec((None, t, w), lambda bi, hi, qi: (bi, qi, hi)),
          pl.BlockSpec((None, s, w), lambda bi, hi, qi: (bi, 0, kb + hi)),
          pl.BlockSpec((None, s // t, w, t), lambda bi, hi, qi: (bi, 0, hi, 0)),
          pl.BlockSpec((N_META, w), lambda bi, hi, qi: (0, kb + hi)),
          pl.BlockSpec((1, w, N_META), lambda bi, hi, qi: (0, hi, 0)),
          pl.BlockSpec((None, t, w), lambda bi, hi, qi: (bi, qi, gb + hi)),
          pl.BlockSpec((g, t, t), lambda bi, hi, qi: (hi, 0, 0)),
          pl.BlockSpec((g, t, t), lambda bi, hi, qi: (hi, 0, 0)),
          pl.BlockSpec((g, N_META, t), lambda bi, hi, qi: (hi, 0, 0)),
          pl.BlockSpec((4, A_QK_DIM), lambda bi, hi, qi: (0, 0)),
          pl.BlockSpec((1, A_V_DIM), lambda bi, hi, qi: (0, 0)),
      ],
      out_specs=pl.BlockSpec((None, t, w), lambda bi, hi, qi: (bi, qi, hi)),
      out_shape=jax.ShapeDtypeStruct((b, s, A_WIDTH), BF16),
      scratch_shapes=[
          pltpu.VMEM((2 * g, 1, t), F32),
          pltpu.VMEM((2 * g, 1, t), F32),
          pltpu.VMEM((2 * g, A_V_DIM + ONES_ROWS, t), F32),
          pltpu.VMEM((2 * g, t, t), BF16),
      ],
      compiler_params=pltpu.CompilerParams(
          dimension_semantics=("parallel", "parallel", "arbitrary"),
          vmem_limit_bytes=VMEM_LIMIT),
  )(z_x, z_x, vt_x.reshape(b, s // t, A_WIDTH, t), z_m, vt_m, z_x, bias_d, bias_s, bias_m, lam_p, subln_g)


def _seg_sum(x):
  lane = lax.broadcasted_iota(jnp.int32, x.shape, 1)
  first = lane < R_HEAD
  lo = jnp.sum(jnp.where(first, x, 0.0), axis=1, keepdims=True)
  hi = jnp.sum(jnp.where(first, 0.0, x), axis=1, keepdims=True)
  return jnp.where(first, lo, hi)


def _split_bf16(x):
  hi = x.astype(BF16)
  return hi, x - hi.astype(F32)


def _dot_tn(a, b):
  return lax.dot_general(a, b, (((0,), (0,)), ((), ())), preferred_element_type=F32)


def _bdot(a, b):
  return lax.dot_general(a, b, (((2,), (1,)), ((0,), (0,))), preferred_element_type=F32)


def _bdot_nt(a, b):
  return lax.dot_general(a, b, (((2,), (2,)), ((0,), (0,))), preferred_element_type=F32)


def _rwkv_kernel(rx_ref, kx_ref, vx_ref, lx_ref, gr_ref, rm_ref, kmt_ref, vmt_ref, lm_ref,
                 pv_ref, mul_ref, wuph_ref, wupl_ref, aup_ref, o_ref, s_ref, prev_ref, prevl_ref):
  tb, c = RW_TB, RW_C
  nh = R_HEAD
  c2 = 2 * c
  ti = pl.program_id(2)
  is_meta = ti == 0

  @pl.when(is_meta)
  def _():
    s_ref[...] = jnp.zeros_like(s_ref)
    prev_ref[...] = jnp.zeros_like(prev_ref)
    prevl_ref[...] = jnp.zeros_like(prevl_ref)

  row = lax.broadcasted_iota(jnp.int32, (tb, 128), 0)
  rowl = lax.broadcasted_iota(jnp.int32, (tb, LORA_PAD), 0)

  def shifted(z, prev, mu, rows):
    z_prev = jnp.where(rows == 0, prev, pltpu.roll(z, 1, 0))
    return z + (z_prev - z) * mu

  z_l = jnp.where(is_meta, lm_ref[...], lx_ref[...])
  lo = shifted(z_l, prevl_ref[...], mul_ref[...], rowl)
  prevl_ref[...] = z_l[tb - 1:tb]
  th_h, th_l = _split_bf16(jnp.tanh(lo))
  w_lora = (jnp.dot(th_h, wuph_ref[...], preferred_element_type=F32)
            + jnp.dot(th_h, wupl_ref[...], preferred_element_type=F32)
            + jnp.dot(th_l.astype(BF16), wuph_ref[...], preferred_element_type=F32))
  a_lora = jnp.dot(lo.astype(BF16), aup_ref[...], preferred_element_type=F32)

  ii = lax.broadcasted_iota(jnp.int32, (tb, tb), 0)
  jj = lax.broadcasted_iota(jnp.int32, (tb, tb), 1)
  shift = int(math.log2(c))
  same = lax.shift_right_logical(ii, shift) == lax.shift_right_logical(jj, shift)
  cum_op = jnp.where(same, jnp.where(jj <= ii, 1.0, 0.0), 0.0).astype(BF16)

  ci = lax.broadcasted_iota(jnp.int32, (c2, c2), 0)
  cj = lax.broadcasted_iota(jnp.int32, (c2, c2), 1)
  diag = ci == cj
  strict2 = jnp.concatenate([cj < ci, cj < ci], axis=1)
  incl2 = jnp.concatenate([cj <= ci, cj <= ci], axis=1)
  first = lax.broadcasted_iota(jnp.int32, (c, 128), 1) < nh

  def stack(x):
    return jnp.concatenate([jnp.where(first, x, 0.0), jnp.where(first, 0.0, x)], axis=0)

  ncc = tb // c
  chains = {name: [] for name in ("at", "rt", "bt", "kt", "bh", "kh", "vv", "gd")}
  post = []
  for p in range(RW_P):
    ls = slice(p * 128, (p + 1) * 128)
    pv = pv_ref[:, ls]
    mu_r, mu_k, mu_v = pv[0:1], pv[1:2], pv[2:3]
    w0, a0, k_k, k_a, r_k, gn_g, gn_b = pv[3:4], pv[4:5], pv[5:6], pv[6:7], pv[7:8], pv[8:9], pv[9:10]

    z_r = jnp.where(is_meta, rm_ref[:, ls], rx_ref[:, ls]).astype(F32)
    z_k = jnp.where(is_meta, kmt_ref[:, ls], kx_ref[:, ls]).astype(F32)
    z_v = jnp.where(is_meta, vmt_ref[:, ls], vx_ref[:, ls]).astype(F32)
    r = shifted(z_r, prev_ref[0:1, ls], mu_r, row)
    k = shifted(z_k, prev_ref[1:2, ls], mu_k, row)
    v = shifted(z_v, prev_ref[2:3, ls], mu_v, row)
    prev_ref[0:1, ls] = z_r[tb - 1:tb]
    prev_ref[1:2, ls] = z_k[tb - 1:tb]
    prev_ref[2:3, ls] = z_v[tb - 1:tb]

    u = -(w0 + w_lora[:, ls])
    softplus = jnp.maximum(u, 0.0) + jnp.log(1.0 + jnp.exp(-jnp.abs(u)))
    logw = -jnp.exp(-softplus - 0.5)
    a = 1.0 / (1.0 + jnp.exp(-(a0 + a_lora[:, ls])))
    kk = k * k_k
    kk = kk / jnp.maximum(jnp.sqrt(_seg_sum(kk * kk)), 1e-12)
    k_mod = k * (1.0 + (a - 1.0) * k_a)
    bonus = _seg_sum(r * k_mod * r_k) * v

    lw_h, lw_r = _split_bf16(logw)
    lw_m, lw_l = _split_bf16(lw_r)
    cum3 = jnp.dot(cum_op, jnp.concatenate([lw_h, lw_m, lw_l.astype(BF16)], axis=1),
                   preferred_element_type=F32)
    cum = cum3[:, :128] + cum3[:, 128:256] + cum3[:, 256:]
    tot = jnp.concatenate([jnp.broadcast_to(cum[cc * c + c - 1:cc * c + c], (c, 128)) for cc in range(ncc)],
                          axis=0)
    p_inv = jnp.exp(-cum)
    a_t = -kk * jnp.exp(cum - logw)
    b_t = kk * a * p_inv
    k_t = k_mod * p_inv
    r_t = r * jnp.exp(cum)
    p_end = jnp.exp(tot - cum)
    b_h = kk * a * p_end
    k_h = k_mod * p_end
    g_diag = jnp.exp(tot)

    for cc in range(ncc):
      rs = slice(cc * c, (cc + 1) * c)
      for name, val in (("at", a_t), ("rt", r_t), ("bt", b_t), ("kt", k_t), ("bh", b_h), ("kh", k_h),
                        ("vv", v)):
        chains[name].append(stack(val[rs]))
      chains["gd"].append(g_diag[cc * c:cc * c + 1])
    post.append((bonus, gn_g, gn_b))

  nb = RW_P * ncc
  at, rt, bt, kt, bh, kh, vv = (jnp.stack(chains[name]) for name in ("at", "rt", "bt", "kt", "bh", "kh", "vv"))
  at_b, vv_b, bh_b = at.astype(BF16), vv.astype(BF16), bh.astype(BF16)
  bk = jnp.concatenate([bt, kt], axis=1).astype(BF16)
  top = jnp.where(strict2, _bdot_nt(at_b, bk), 0.0)
  lblk = jnp.where(incl2, _bdot_nt(rt.astype(BF16), bk), 0.0)
  nm, mak = top[:, :, :c2], top[:, :, c2:]
  tinv = jnp.where(diag, 1.0, nm)
  npow = nm.astype(BF16)
  for _ in range(5):
    npow = _bdot(npow, npow).astype(BF16)
    tinv = tinv + _bdot(tinv.astype(BF16), npow)
  x1 = _bdot(mak.astype(BF16), vv_b)
  wu_b = _bdot(tinv.astype(BF16), jnp.concatenate([at_b, x1.astype(BF16)], axis=2)).astype(BF16)
  rhs = jnp.concatenate([wu_b, jnp.concatenate([jnp.zeros_like(vv_b), vv_b], axis=2)], axis=1)
  qy = _bdot(lblk.astype(BF16), rhs)
  q_h = (rt + qy[:, :, :c2]).astype(BF16)
  y0 = qy[:, :, c2:]
  uv = jnp.concatenate([wu_b[:, :, c2:], vv_b], axis=1)
  bkh = jnp.concatenate([bh_b, kh.astype(BF16)], axis=1)
  g_m = [(jnp.where(diag, chains["gd"][n], 0.0) + _dot_tn(wu_b[n, :, :c2], bh_b[n])).astype(BF16)
         for n in range(nb)]
  h_m = [_dot_tn(uv[n], bkh[n]) for n in range(nb)]

  states = [s_ref[p] for p in range(RW_P)]
  y_rows = [[] for _ in range(RW_P)]
  for cc in range(ncc):
    for p in range(RW_P):
      n = p * ncc + cc
      s_old_b = states[p].astype(BF16)
      y2 = _dot_nt(q_h[n], s_old_b) + y0[n]
      states[p] = jnp.dot(s_old_b, g_m[n], preferred_element_type=F32) + h_m[n]
      y_rows[p].append(y2[:c] + y2[c:])

  for p in range(RW_P):
    ls = slice(p * 128, (p + 1) * 128)
    s_ref[p] = states[p]
    bonus, gn_g, gn_b = post[p]
    y = jnp.concatenate(y_rows[p], axis=0)
    mean = _seg_sum(y) * (1.0 / nh)
    yc = y - mean
    var = _seg_sum(yc * yc) * (1.0 / nh)
    yn = yc * lax.rsqrt(var + GN_EPS) * gn_g + gn_b
    g = gr_ref[:, ls].astype(F32)
    o_ref[:, ls] = ((yn + bonus) * (g / (1.0 + jnp.exp(-g)))).astype(o_ref.dtype)


def _rwkv(z_x, lo_x, z_mp, lo_mp, pvec, mu_l, wup_h, wup_l, aup):
  b, s, _ = z_x.shape
  tb = RW_TB
  nt = s // tb + 1
  pw = 128 * RW_P

  def xmap(col):
    return lambda bi, hp, ti: (bi, jnp.maximum(ti - 1, 0), col // pw + hp)

  def mmap(col):
    return lambda bi, hp, ti: (0, col // pw + hp)

  return pl.pallas_call(
      _rwkv_kernel,
      name="rwkv7",
      grid=(b, R_PAIRS // RW_P, nt),
      in_specs=[
          pl.BlockSpec((None, tb, pw), xmap(Z_RR)),
          pl.BlockSpec((None, tb, pw), xmap(Z_RK)),
          pl.BlockSpec((None, tb, pw), xmap(Z_RV)),
          pl.BlockSpec((None, tb, LORA_PAD), lambda bi, hp, ti: (bi, jnp.maximum(ti - 1, 0), 0)),
          pl.BlockSpec((None, tb, pw), xmap(Z_GR)),
          pl.BlockSpec((tb, pw), mmap(Z_RR)),
          pl.BlockSpec((tb, pw), mmap(Z_RK)),
          pl.BlockSpec((tb, pw), mmap(Z_RV)),
          pl.BlockSpec((tb, LORA_PAD), lambda bi, hp, ti: (0, 0)),
          pl.BlockSpec((16, pw), lambda bi, hp, ti: (0, hp)),
          pl.BlockSpec((1, LORA_PAD), lambda bi, hp, ti: (0, 0)),
          pl.BlockSpec((LORA_PAD, pw), lambda bi, hp, ti: (0, hp)),
          pl.BlockSpec((LORA_PAD, pw), lambda bi, hp, ti: (0, hp)),
          pl.BlockSpec((LORA_PAD, pw), lambda bi, hp, ti: (0, hp)),
      ],
      out_specs=pl.BlockSpec((None, tb, pw), lambda bi, hp, ti: (bi, jnp.maximum(ti - 1, 0), hp)),
      out_shape=jax.ShapeDtypeStruct((b, s, R_WIDTH), BF16),
      scratch_shapes=[
          pltpu.VMEM((RW_P, 2 * R_HEAD, 2 * R_HEAD), F32),
          pltpu.VMEM((8, pw), F32),
          pltpu.VMEM((1, LORA_PAD), F32),
      ],
      compiler_params=pltpu.CompilerParams(
          dimension_semantics=("parallel", "parallel", "arbitrary"),
          vmem_limit_bytes=VMEM_LIMIT),
  )(z_x, z_x, z_x, lo_x, z_x, z_mp, z_mp, z_mp, lo_mp, pvec, mu_l, wup_h, wup_l, aup)


def _out_kernel(x_ref, oa_ref, or_ref, wa_ref, wr_ref, ge_ref, be_ref, gp_ref, bp_ref, o_ref):
  h = _ln_rows(x_ref[...], ge_ref[...], be_ref[...])
  y = (jnp.dot(oa_ref[...], wa_ref[...], preferred_element_type=F32)
       + jnp.dot(or_ref[...], wr_ref[...], preferred_element_type=F32))
  o_ref[...] = _ln_rows(DEEPNORM_ALPHA * h + y, gp_ref[...], bp_ref[...])


def _out_proj(x2d, oa, orw, wa, wr, ge, be, gp, bp, tm):
  m, d = x2d.shape
  vec = pl.BlockSpec((1, d), lambda i: (0, 0))
  return pl.pallas_call(
      _out_kernel,
      name="out_proj",
      grid=(m // tm,),
      in_specs=[
          pl.BlockSpec((tm, d), lambda i: (i, 0)),
          pl.BlockSpec((tm, A_WIDTH), lambda i: (i, 0)),
          pl.BlockSpec((tm, R_WIDTH), lambda i: (i, 0)),
          pl.BlockSpec((A_WIDTH, d), lambda i: (0, 0), pipeline_mode=pl.Buffered(1)),
          pl.BlockSpec((R_WIDTH, d), lambda i: (0, 0), pipeline_mode=pl.Buffered(1)),
          vec, vec, vec, vec,
      ],
      out_specs=pl.BlockSpec((tm, d), lambda i: (i, 0)),
      out_shape=jax.ShapeDtypeStruct((m, d), F32),
      compiler_params=pltpu.CompilerParams(
          dimension_semantics=("parallel",),
          vmem_limit_bytes=VMEM_LIMIT),
  )(x2d, oa, orw, wa, wr, ge, be, gp, bp)


def kernel(x, meta_tokens, ln_emb_g, ln_emb_b, rel_bias, w_in, w_out, lambda_q1, lambda_k1, lambda_q2,
           lambda_k2, subln_g, rw_mu, rw_w0, rw_w_up, rw_a0, rw_a_up, rw_k_k, rw_k_a, rw_r_k, rw_gn_g,
           rw_gn_b, ln_post_g, ln_post_b):
  b, s, d = x.shape
  l = 0
  wi = w_in[l]
  c_lo = 4 * A_WIDTH + 3 * R_WIDTH
  c_gr = c_lo + DECAY_LORA + ICLR_LORA
  lora_pad = LORA_PAD - DECAY_LORA - ICLR_LORA
  c_v = 2 * A_WIDTH
  w_all = wi.astype(BF16)
  w_gr = wi[:, c_gr:].astype(BF16)
  w_vt = wi[:, c_v:c_v + A_WIDTH].T.astype(BF16)
  w_lora = jnp.pad(wi[:, c_lo:c_gr], ((0, 0), (0, lora_pad))).astype(BF16)

  ge, be = ln_emb_g.reshape(1, d), ln_emb_b.reshape(1, d)
  x2d = x.reshape(b * s, d)
  z_x, vt_x, lo_x = _ln_matmul(x2d, ge, be, w_all, w_gr, w_vt, w_lora, 1024)
  z_x, lo_x = z_x.reshape(b, s, -1), lo_x.reshape(b, s, LORA_PAD)
  z_m, vt_m, lo_m = _ln_matmul(meta_tokens, ge, be, w_all, w_gr, w_vt, w_lora, N_META)

  bias_d, bias_s, bias_m = _bias_tiles(rel_bias)
  lam_p = jnp.stack([lambda_q1[l], lambda_k1[l], lambda_q2[l], lambda_k2[l]], axis=0)
  o_attn = _attention(z_x, vt_x, z_m, vt_m, bias_d, bias_s, bias_m, lam_p, subln_g[l].reshape(1, A_V_DIM))

  mu = rw_mu[l]
  zeros = jnp.zeros((R_WIDTH,), F32)
  pvec = jnp.stack([mu[:R_WIDTH], mu[R_WIDTH:2 * R_WIDTH], mu[2 * R_WIDTH:3 * R_WIDTH], rw_w0[l], rw_a0[l],
                    rw_k_k[l], rw_k_a[l], rw_r_k[l].reshape(R_WIDTH), rw_gn_g[l], rw_gn_b[l]]
                   + [zeros] * 6, axis=0)
  mu_l = jnp.pad(mu[3 * R_WIDTH:], (0, lora_pad)).reshape(1, LORA_PAD)
  wup = jnp.pad(rw_w_up[l], ((0, LORA_PAD - DECAY_LORA), (0, 0)))
  wup_h = wup.astype(BF16)
  wup_l = (wup - wup_h.astype(F32)).astype(BF16)
  aup = jnp.pad(rw_a_up[l], ((DECAY_LORA, lora_pad), (0, 0))).astype(BF16)
  front = ((RW_TB - N_META, 0), (0, 0))
  o_rwkv = _rwkv(z_x, lo_x, jnp.pad(z_m, front), jnp.pad(lo_m, front), pvec, mu_l, wup_h, wup_l, aup)

  wo = w_out[l].astype(BF16)
  out = _out_proj(x2d, o_attn.reshape(b * s, A_WIDTH), o_rwkv.reshape(b * s, R_WIDTH),
                  wo[:A_WIDTH], wo[A_WIDTH:], ge, be,
                  ln_post_g[l].reshape(1, d), ln_post_b[l].reshape(1, d), 512)
  return out.reshape(b, s, d)
```

```python
import functools
import math

import numpy as np
import jax
import jax.numpy as jnp
from jax import lax
from jax.experimental import pallas as pl
from jax.experimental.pallas import tpu as pltpu

D_MODEL = 2048
N_META = 16
A_HEADS = 8
A_V_DIM = 128
A_QK_DIM = 64
A_WIDTH = A_HEADS * A_V_DIM
R_HEAD = 64
R_WIDTH = 1024
R_PAIRS = R_WIDTH // (2 * R_HEAD)
DECAY_LORA = 96
ICLR_LORA = 96
LORA_PAD = 256
N_BUCKETS = 32
MAX_DISTANCE = 128
LN_EPS = 1e-5
SUBLN_EPS = 1e-5
GN_EPS = 64e-5
DEPTH = 1
DEEPNORM_ALPHA = (2 * DEPTH) ** 0.25
LAM_INIT = 0.8 - 0.6 * math.exp(-0.3 * 0)
NEG = -1e30

ATT_T = 256
ATT_G = 8
ONES_ROWS = 16
IN_TN = 1024
Z_Q, Z_K, Z_GA, Z_RR, Z_RK, Z_RV, Z_GR = (i * 1024 for i in range(7))
LOG2E = math.log2(math.e)
Q_SCALE = A_QK_DIM ** -0.5 * LOG2E
RW_TB = 128
RW_C = 64
RW_P = 8
VMEM_LIMIT = 56 * 1024 * 1024

F32 = jnp.float32
BF16 = jnp.bfloat16


def _ln_rows(x, g, b):
  mu = jnp.mean(x, axis=-1, keepdims=True)
  xc = x - mu
  var = jnp.mean(xc * xc, axis=-1, keepdims=True)
  return xc * lax.rsqrt(var + LN_EPS) * g + b


def _ln_mm_kernel(x_ref, g_ref, b_ref, wm_ref, wgr_ref, wvt_ref, wl_ref, om_ref, ovt_ref, ol_ref, hn_ref):
  j = pl.program_id(1)
  n_main = pl.num_programs(1) - 2

  @pl.when(j == 0)
  def _():
    hn_ref[...] = _ln_rows(x_ref[...], g_ref[...], b_ref[...]).astype(BF16)

  @pl.when(j < n_main - 1)
  def _():
    scale = jnp.where(j == Z_Q // IN_TN, Q_SCALE, 1.0)
    z = jnp.dot(hn_ref[...], wm_ref[...], preferred_element_type=F32)
    om_ref[...] = (z * scale).astype(om_ref.dtype)

  @pl.when(j == n_main - 1)
  def _():
    om_ref[...] = jnp.dot(hn_ref[...], wgr_ref[...], preferred_element_type=F32).astype(om_ref.dtype)

  @pl.when(j == n_main)
  def _():
    zt = _dot_nt(wvt_ref[...], hn_ref[...])
    tv = ovt_ref.shape[2]
    for c in range(ovt_ref.shape[0]):
      ovt_ref[c] = zt[:, c * tv:(c + 1) * tv].astype(ovt_ref.dtype)

  @pl.when(j == n_main + 1)
  def _():
    ol_ref[...] = jnp.dot(hn_ref[...], wl_ref[...], preferred_element_type=F32)


def _ln_matmul(x2d, g, b, w_all, w_gr, w_vt, w_lora, tm):
  m, d = x2d.shape
  tn = IN_TN
  nj = Z_GR // tn + 1
  n = nj * tn
  last = nj - 1
  tv = min(ATT_T, tm)
  v_tile = 2 * A_WIDTH // tn
  once = dict(pipeline_mode=pl.Buffered(1))

  def w_map(i, j):
    jj = jnp.minimum(j, last - 1)
    return (0, jnp.where(jj >= v_tile, jj + 1, jj))

  return pl.pallas_call(
      _ln_mm_kernel,
      name="ln_inproj",
      grid=(m // tm, nj + 2),
      in_specs=[
          pl.BlockSpec((tm, d), lambda i, j: (i, 0)),
          pl.BlockSpec((1, d), lambda i, j: (0, 0)),
          pl.BlockSpec((1, d), lambda i, j: (0, 0)),
          pl.BlockSpec((d, tn), w_map),
          pl.BlockSpec((d, tn), lambda i, j: (0, 0), **once),
          pl.BlockSpec((A_WIDTH, d), lambda i, j: (0, 0), **once),
          pl.BlockSpec((d, LORA_PAD), lambda i, j: (0, 0), **once),
      ],
      out_specs=[
          pl.BlockSpec((tm, tn), lambda i, j: (i, jnp.minimum(j, last))),
          pl.BlockSpec((tm // tv, A_WIDTH, tv), lambda i, j: (i, 0, 0)),
          pl.BlockSpec((tm, LORA_PAD), lambda i, j: (i, 0)),
      ],
      out_shape=[
          jax.ShapeDtypeStruct((m, n), BF16),
          jax.ShapeDtypeStruct((m // tv, A_WIDTH, tv), BF16),
          jax.ShapeDtypeStruct((m, LORA_PAD), F32),
      ],
      scratch_shapes=[pltpu.VMEM((tm, d), BF16)],
      compiler_params=pltpu.CompilerParams(
          dimension_semantics=("parallel", "arbitrary"),
          vmem_limit_bytes=VMEM_LIMIT),
  )(x2d, g, b, w_all, w_gr, w_vt, w_lora)


def _bucket_thresholds():
  n = np.arange(0, 4 * MAX_DISTANCE, dtype=np.int64)
  max_exact = N_BUCKETS // 2
  nf = np.maximum(n, 1).astype(np.float32)
  large = max_exact + (np.log(nf / np.float32(max_exact)) / np.float32(math.log(MAX_DISTANCE / max_exact))
                       * np.float32(N_BUCKETS - max_exact)).astype(np.int32)
  large = np.minimum(large, N_BUCKETS - 1)
  bucket = np.where(n < max_exact, n, large)
  assert np.all(np.diff(bucket) >= 0) and bucket[-1] == N_BUCKETS - 1
  return [int(np.argmax(bucket >= b)) for b in range(N_BUCKETS)]


_THR = _bucket_thresholds()


def _bias_kernel(rb_ref, diag_ref, sub_ref, meta_ref):
  h = pl.program_id(0)
  far = rb_ref[N_BUCKETS - 1, h]

  def bias_of(n):
    out = jnp.full(n.shape, (rb_ref[0, h] - far) * LOG2E, F32)
    for b in range(1, N_BUCKETS):
      out = jnp.where(n >= _THR[b], (rb_ref[b, h] - far) * LOG2E, out)
    return out

  t = ATT_T
  kj = lax.broadcasted_iota(jnp.int32, (t, t), 0)
  qi = lax.broadcasted_iota(jnp.int32, (t, t), 1)
  d = qi - kj
  diag_ref[...] = jnp.where(d >= 0, bias_of(d), NEG)
  sub_ref[...] = bias_of(d + t)
  km = lax.broadcasted_iota(jnp.int32, (N_META, t), 0)
  qm = lax.broadcasted_iota(jnp.int32, (N_META, t), 1)
  meta_ref[...] = bias_of(qm - km + N_META)


def _bias_tiles(rel_bias):
  t = ATT_T
  return pl.pallas_call(
      _bias_kernel,
      name="bias_tiles",
      grid=(A_HEADS,),
      in_specs=[pl.BlockSpec(memory_space=pltpu.SMEM)],
      out_specs=[
          pl.BlockSpec((None, t, t), lambda h: (h, 0, 0)),
          pl.BlockSpec((None, t, t), lambda h: (h, 0, 0)),
          pl.BlockSpec((None, N_META, t), lambda h: (h, 0, 0)),
      ],
      out_shape=[
          jax.ShapeDtypeStruct((A_HEADS, t, t), F32),
          jax.ShapeDtypeStruct((A_HEADS, t, t), F32),
          jax.ShapeDtypeStruct((A_HEADS, N_META, t), F32),
      ],
  )(rel_bias)


def _dot_nt(a, b):
  return lax.dot_general(a, b, (((1,), (1,)), ((), ())), preferred_element_type=F32)


def _attn_kernel(q_ref, kx_ref, vt_ref, km_ref, vmt_ref, ga_ref, bd_ref, bs_ref, bm_ref,
                 lam_ref, sg_ref, o_ref, m_ref, alpha_ref, acc_ref, pt_ref):
  t = ATT_T
  g = ATT_G
  nc = 2 * g
  dv = A_V_DIM
  qi = pl.program_id(2)
  lane = lax.broadcasted_iota(jnp.int32, (t, 128), 1)
  qs = []
  for hh in range(g):
    q = q_ref[:, hh * 128:(hh + 1) * 128]
    zero = jnp.zeros_like(q)
    qs += [jnp.where(lane < A_QK_DIM, q, zero), jnp.where(lane >= A_QK_DIM, q, zero)]

  m_ref[...] = jnp.full(m_ref.shape, NEG, F32)
  acc_ref[...] = jnp.zeros(acc_ref.shape, F32)
  alpha_ref[...] = jnp.ones(alpha_ref.shape, F32)
  pt_ref[...] = jnp.zeros(pt_ref.shape, BF16)

  def v_tile(j):
    ones = jnp.ones((ONES_ROWS, t), BF16)
    return [jnp.concatenate([vt_ref[j, hh * dv:(hh + 1) * dv, :], ones], axis=0) for hh in range(g)]

  def k_tile(j):
    off = pl.multiple_of(j * t, t)
    return [kx_ref[pl.ds(off, t), hh * 128:(hh + 1) * 128] for hh in range(g)]

  def softmax_stage(c, m_prev, s_list):
    m_new = m_prev
    for s in s_list:
      m_new = jnp.maximum(m_new, jnp.max(s, axis=0, keepdims=True))
    m_ref[c] = m_new
    alpha_ref[c] = jnp.exp2(m_prev - m_new)
    return [jnp.exp2(s - m_new).astype(BF16) for s in s_list]

  def step(j_cur, j_prev, biases):
    ks, vts = k_tile(j_cur), v_tile(j_prev)
    pv, st = [], []
    for c in range(nc):
      pv.append(jnp.dot(vts[c // 2], pt_ref[c], preferred_element_type=F32))
      s = _dot_nt(ks[c // 2], qs[c])
      st.append(s if biases is None else s + biases[c // 2])
    for c in range(nc):
      acc_ref[c] = alpha_ref[c] * acc_ref[c] + pv[c]
    for c in range(nc):
      pt_ref[c], = softmax_stage(c, m_ref[c], [st[c]])

  n_far = jnp.maximum(qi - 1, 0)

  def far_body(j, carry):
    step(2 * j, jnp.maximum(2 * j - 1, 0), None)
    step(2 * j + 1, 2 * j, None)
    return carry

  lax.fori_loop(0, n_far // 2, far_body, 0)

  @pl.when(n_far % 2 == 1)
  def _():
    step(n_far - 1, jnp.maximum(n_far - 2, 0), None)

  @pl.when(qi >= 1)
  def _():
    step(qi - 1, jnp.maximum(qi - 2, 0), [bs_ref[hh] for hh in range(g)])

  vts = v_tile(jnp.maximum(qi - 1, 0))
  ks = k_tile(qi)
  pv, st, sm = [], [], []
  for c in range(nc):
    hh = c // 2
    pv.append(jnp.dot(vts[hh], pt_ref[c], preferred_element_type=F32))
    st.append(_dot_nt(ks[hh], qs[c]) + bd_ref[hh])
    sm.append(_dot_nt(km_ref[:, hh * 128:(hh + 1) * 128], qs[c]) + jnp.where(qi == 0, bm_ref[hh], 0.0))
  for c in range(nc):
    acc_ref[c] = alpha_ref[c] * acc_ref[c] + pv[c]
  pts = [softmax_stage(c, m_ref[c], [st[c], sm[c]]) for c in range(nc)]
  vts = v_tile(qi)
  ones_m = jnp.ones((ONES_ROWS, N_META), BF16)
  vtm = [jnp.concatenate([vmt_ref[0, hh * dv:(hh + 1) * dv, :], ones_m], axis=0) for hh in range(g)]
  for c in range(nc):
    pv = (jnp.dot(vts[c // 2], pts[c][0], preferred_element_type=F32)
          + jnp.dot(vtm[c // 2], pts[c][1], preferred_element_type=F32))
    acc_ref[c] = alpha_ref[c] * acc_ref[c] + pv

  lp = lam_ref[...]
  lam = (jnp.exp(jnp.sum(lp[0:1] * lp[1:2], axis=1, keepdims=True))
         - jnp.exp(jnp.sum(lp[2:3] * lp[3:4], axis=1, keepdims=True)) + LAM_INIT)
  for hh in range(g):
    a0, a1 = acc_ref[2 * hh], acc_ref[2 * hh + 1]
    ot = a0[:dv] / a0[dv:dv + 1] - lam * (a1[:dv] / a1[dv:dv + 1])
    ot = ot * lax.rsqrt(jnp.mean(ot * ot, axis=0, keepdims=True) + SUBLN_EPS)
    o = ot.T * (sg_ref[...] * (1.0 - LAM_INIT))
    gate = ga_ref[:, hh * dv:(hh + 1) * dv].astype(F32)
    o_ref[:, hh * dv:(hh + 1) * dv] = (o * (gate / (1.0 + jnp.exp(-gate)))).astype(o_ref.dtype)


def _attention(z_x, vt_x, z_m, vt_m, bias_d, bias_s, bias_m, lam_p, subln_g):
  b, s, _ = z_x.shape
  t = ATT_T
  g = ATT_G
  w = 128 * g
  hb = A_HEADS // g
  kb, gb = Z_K // w, Z_GA // w
  return pl.pallas_call(
      _attn_kernel,
      name="diff_attn",
      grid=(b, hb, s // t),
      in_specs=[
          pl.BlockSpec((None, t, w), lambda bi, hi, qi: (bi, qi, hi)),
          pl.BlockSpec((None, s, w), lambda bi, hi, qi: (bi, 0, kb + hi), pipeline_mode=pl.Buffered(1)),
          pl.BlockSpec((None, s // t, w, t), lambda bi, hi, qi: (bi, 0, hi, 0), pipeline_mode=pl.Buffered(1)),
          pl.BlockSpec((N_META, w), lambda bi, hi, qi: (0, kb + hi)),
          pl.BlockSpec((1, w, N_META), lambda bi, hi, qi: (0, hi, 0)),
          pl.BlockSpec((None, t, w), lambda bi, hi, qi: (bi, qi, gb + hi)),
          pl.BlockSpec((g, t, t), lambda bi, hi, qi: (hi, 0, 0)),
          pl.BlockSpec((g, t, t), lambda bi, hi, qi: (hi, 0, 0)),
          pl.BlockSpec((g, N_META, t), lambda bi, hi, qi: (hi, 0, 0)),
          pl.BlockSpec((4, A_QK_DIM), lambda bi, hi, qi: (0, 0)),
          pl.BlockSpec((1, A_V_DIM), lambda bi, hi, qi: (0, 0)),
      ],
      out_specs=pl.BlockSpec((None, t, w), lambda bi, hi, qi: (bi, qi, hi)),
      out_shape=jax.ShapeDtypeStruct((b, s, A_WIDTH), BF16),
      scratch_shapes=[
          pltpu.VMEM((2 * g, 1, t), F32),
          pltpu.VMEM((2 * g, 1, t), F32),
          pltpu.VMEM((2 * g, A_V_DIM + ONES_ROWS, t), F32),
          pltpu.VMEM((2 * g, t, t), BF16),
      ],
      compiler_params=pltpu.CompilerParams(
          dimension_semantics=("parallel", "parallel", "arbitrary"),
          vmem_limit_bytes=VMEM_LIMIT),
  )(z_x, z_x, vt_x.reshape(b, s // t, A_WIDTH, t), z_m, vt_m, z_x, bias_d, bias_s, bias_m, lam_p, subln_g)


def _seg_sum(x):
  lane = lax.broadcasted_iota(jnp.int32, x.shape, 1)
  first = lane < R_HEAD
  lo = jnp.sum(jnp.where(first, x, 0.0), axis=1, keepdims=True)
  hi = jnp.sum(jnp.where(first, 0.0, x), axis=1, keepdims=True)
  return jnp.where(first, lo, hi)


def _split_bf16(x):
  hi = x.astype(BF16)
  return hi, x - hi.astype(F32)


def _dot_tn(a, b):
  return lax.dot_general(a, b, (((0,), (0,)), ((), ())), preferred_element_type=F32)


def _bdot(a, b):
  return lax.dot_general(a, b, (((2,), (1,)), ((0,), (0,))), preferred_element_type=F32)


def _bdot_nt(a, b):
  return lax.dot_general(a, b, (((2,), (2,)), ((0,), (0,))), preferred_element_type=F32)


def _rwkv_kernel(rx_ref, kx_ref, vx_ref, lx_ref, gr_ref, rm_ref, kmt_ref, vmt_ref, lm_ref,
                 pv_ref, mul_ref, wuph_ref, wupl_ref, aup_ref, o_ref, s_ref, prev_ref, prevl_ref):
  tb, c = RW_TB, RW_C
  nh = R_HEAD
  c2 = 2 * c
  ti = pl.program_id(2)
  is_meta = ti == 0

  @pl.when(is_meta)
  def _():
    s_ref[...] = jnp.zeros_like(s_ref)
    prev_ref[...] = jnp.zeros_like(prev_ref)
    prevl_ref[...] = jnp.zeros_like(prevl_ref)

  row = lax.broadcasted_iota(jnp.int32, (tb, 128), 0)
  rowl = lax.broadcasted_iota(jnp.int32, (tb, LORA_PAD), 0)

  def shifted(z, prev, mu, rows):
    z_prev = jnp.where(rows == 0, prev, pltpu.roll(z, 1, 0))
    return z + (z_prev - z) * mu

  z_l = jnp.where(is_meta, lm_ref[...], lx_ref[...])
  lo = shifted(z_l, prevl_ref[...], mul_ref[...], rowl)
  prevl_ref[...] = z_l[tb - 1:tb]
  th_h, th_l = _split_bf16(jnp.tanh(lo))
  w_lora = (jnp.dot(th_h, wuph_ref[...], preferred_element_type=F32)
            + jnp.dot(th_h, wupl_ref[...], preferred_element_type=F32)
            + jnp.dot(th_l.astype(BF16), wuph_ref[...], preferred_element_type=F32))
  a_lora = jnp.dot(lo.astype(BF16), aup_ref[...], preferred_element_type=F32)

  ii = lax.broadcasted_iota(jnp.int32, (tb, tb), 0)
  jj = lax.broadcasted_iota(jnp.int32, (tb, tb), 1)
  shift = int(math.log2(c))
  same = lax.shift_right_logical(ii, shift) == lax.shift_right_logical(jj, shift)
  cum_op = jnp.where(same, jnp.where(jj <= ii, 1.0, 0.0), 0.0).astype(BF16)

  ci = lax.broadcasted_iota(jnp.int32, (c2, c2), 0)
  cj = lax.broadcasted_iota(jnp.int32, (c2, c2), 1)
  diag = ci == cj
  strict2 = jnp.concatenate([cj < ci, cj < ci], axis=1)
  incl2 = jnp.concatenate([cj <= ci, cj <= ci], axis=1)
  first = lax.broadcasted_iota(jnp.int32, (c, 128), 1) < nh

  def stack(x):
    return jnp.concatenate([jnp.where(first, x, 0.0), jnp.where(first, 0.0, x)], axis=0)

  ncc = tb // c
  chains = {name: [] for name in ("at", "rt", "bt", "kt", "bh", "kh", "vv", "gd")}
  post = []
  for p in range(RW_P):
    ls = slice(p * 128, (p + 1) * 128)
    pv = pv_ref[:, ls]
    mu_r, mu_k, mu_v = pv[0:1], pv[1:2], pv[2:3]
    w0, a0, k_k, k_a, r_k, gn_g, gn_b = pv[3:4], pv[4:5], pv[5:6], pv[6:7], pv[7:8], pv[8:9], pv[9:10]

    z_r = jnp.where(is_meta, rm_ref[:, ls], rx_ref[:, ls]).astype(F32)
    z_k = jnp.where(is_meta, kmt_ref[:, ls], kx_ref[:, ls]).astype(F32)
    z_v = jnp.where(is_meta, vmt_ref[:, ls], vx_ref[:, ls]).astype(F32)
    r = shifted(z_r, prev_ref[0:1, ls], mu_r, row)
    k = shifted(z_k, prev_ref[1:2, ls], mu_k, row)
    v = shifted(z_v, prev_ref[2:3, ls], mu_v, row)
    prev_ref[0:1, ls] = z_r[tb - 1:tb]
    prev_ref[1:2, ls] = z_k[tb - 1:tb]
    prev_ref[2:3, ls] = z_v[tb - 1:tb]

    u = -(w0 + w_lora[:, ls])
    softplus = jnp.maximum(u, 0.0) + jnp.log(1.0 + jnp.exp(-jnp.abs(u)))
    logw = -jnp.exp(-softplus - 0.5)
    a = 1.0 / (1.0 + jnp.exp(-(a0 + a_lora[:, ls])))
    kk = k * k_k
    kk = kk / jnp.maximum(jnp.sqrt(_seg_sum(kk * kk)), 1e-12)
    k_mod = k * (1.0 + (a - 1.0) * k_a)
    bonus = _seg_sum(r * k_mod * r_k) * v

    lw_h, lw_r = _split_bf16(logw)
    lw_m, lw_l = _split_bf16(lw_r)
    cum3 = jnp.dot(cum_op, jnp.concatenate([lw_h, lw_m, lw_l.astype(BF16)], axis=1),
                   preferred_element_type=F32)
    cum = cum3[:, :128] + cum3[:, 128:256] + cum3[:, 256:]
    tot = jnp.concatenate([jnp.broadcast_to(cum[cc * c + c - 1:cc * c + c], (c, 128)) for cc in range(ncc)],
                          axis=0)
    p_inv = jnp.exp(-cum)
    a_t = -kk * jnp.exp(cum - logw)
    b_t = kk * a * p_inv
    k_t = k_mod * p_inv
    r_t = r * jnp.exp(cum)
    p_end = jnp.exp(tot - cum)
    b_h = kk * a * p_end
    k_h = k_mod * p_end
    g_diag = jnp.exp(tot)

    for cc in range(ncc):
      rs = slice(cc * c, (cc + 1) * c)
      for name, val in (("at", a_t), ("rt", r_t), ("bt", b_t), ("kt", k_t), ("bh", b_h), ("kh", k_h),
                        ("vv", v)):
        chains[name].append(stack(val[rs]))
      chains["gd"].append(g_diag[cc * c:cc * c + 1])
    post.append((bonus, gn_g, gn_b))

  nb = RW_P * ncc
  at, rt, bt, kt, bh, kh, vv = (jnp.stack(chains[name]) for name in ("at", "rt", "bt", "kt", "bh", "kh", "vv"))
  at_b, vv_b, bh_b = at.astype(BF16), vv.astype(BF16), bh.astype(BF16)
  bk = jnp.concatenate([bt, kt], axis=1).astype(BF16)
  top = jnp.where(strict2, _bdot_nt(at_b, bk), 0.0)
  lblk = jnp.where(incl2, _bdot_nt(rt.astype(BF16), bk), 0.0)
  nm, mak = top[:, :, :c2], top[:, :, c2:]
  tinv = jnp.where(diag, 1.0, nm)
  npow = nm.astype(BF16)
  for _ in range(5):
    npow = _bdot(npow, npow).astype(BF16)
    tinv = tinv + _bdot(tinv.astype(BF16), npow)
  x1 = _bdot(mak.astype(BF16), vv_b)
  wu_b = _bdot(tinv.astype(BF16), jnp.concatenate([at_b, x1.astype(BF16)], axis=2)).astype(BF16)
  rhs = jnp.concatenate([wu_b, jnp.concatenate([jnp.zeros_like(vv_b), vv_b], axis=2)], axis=1)
  qy = _bdot(lblk.astype(BF16), rhs)
  q_h = (rt + qy[:, :, :c2]).astype(BF16)
  y0 = qy[:, :, c2:]
  uv = jnp.concatenate([wu_b[:, :, c2:], vv_b], axis=1)
  bkh = jnp.concatenate([bh_b, kh.astype(BF16)], axis=1)
  g_m = [(jnp.where(diag, chains["gd"][n], 0.0) + _dot_tn(wu_b[n, :, :c2], bh_b[n])).astype(BF16)
         for n in range(nb)]
  h_m = [_dot_tn(uv[n], bkh[n]) for n in range(nb)]

  states = [s_ref[p] for p in range(RW_P)]
  y_rows = [[] for _ in range(RW_P)]
  for cc in range(ncc):
    for p in range(RW_P):
      n = p * ncc + cc
      s_old_b = states[p].astype(BF16)
      y2 = _dot_nt(q_h[n], s_old_b) + y0[n]
      states[p] = jnp.dot(s_old_b, g_m[n], preferred_element_type=F32) + h_m[n]
      y_rows[p].append(y2[:c] + y2[c:])

  for p in range(RW_P):
    ls = slice(p * 128, (p + 1) * 128)
    s_ref[p] = states[p]
    bonus, gn_g, gn_b = post[p]
    y = jnp.concatenate(y_rows[p], axis=0)
    mean = _seg_sum(y) * (1.0 / nh)
    yc = y - mean
    var = _seg_sum(yc * yc) * (1.0 / nh)
    yn = yc * lax.rsqrt(var + GN_EPS) * gn_g + gn_b
    g = gr_ref[:, ls].astype(F32)
    o_ref[:, ls] = ((yn + bonus) * (g / (1.0 + jnp.exp(-g)))).astype(o_ref.dtype)


def _rwkv(z_x, lo_x, z_mp, lo_mp, pvec, mu_l, wup_h, wup_l, aup):
  b, s, _ = z_x.shape
  tb = RW_TB
  nt = s // tb + 1
  pw = 128 * RW_P

  def xmap(col):
    return lambda bi, hp, ti: (bi, jnp.maximum(ti - 1, 0), col // pw + hp)

  def mmap(col):
    return lambda bi, hp, ti: (0, col // pw + hp)

  return pl.pallas_call(
      _rwkv_kernel,
      name="rwkv7",
      grid=(b, R_PAIRS // RW_P, nt),
      in_specs=[
          pl.BlockSpec((None, tb, pw), xmap(Z_RR)),
          pl.BlockSpec((None, tb, pw), xmap(Z_RK)),
          pl.BlockSpec((None, tb, pw), xmap(Z_RV)),
          pl.BlockSpec((None, tb, LORA_PAD), lambda bi, hp, ti: (bi, jnp.maximum(ti - 1, 0), 0)),
          pl.BlockSpec((None, tb, pw), xmap(Z_GR)),
          pl.BlockSpec((tb, pw), mmap(Z_RR)),
          pl.BlockSpec((tb, pw), mmap(Z_RK)),
          pl.BlockSpec((tb, pw), mmap(Z_RV)),
          pl.BlockSpec((tb, LORA_PAD), lambda bi, hp, ti: (0, 0)),
          pl.BlockSpec((16, pw), lambda bi, hp, ti: (0, hp)),
          pl.BlockSpec((1, LORA_PAD), lambda bi, hp, ti: (0, 0)),
          pl.BlockSpec((LORA_PAD, pw), lambda bi, hp, ti: (0, hp)),
          pl.BlockSpec((LORA_PAD, pw), lambda bi, hp, ti: (0, hp)),
          pl.BlockSpec((LORA_PAD, pw), lambda bi, hp, ti: (0, hp)),
      ],
      out_specs=pl.BlockSpec((None, tb, pw), lambda bi, hp, ti: (bi, jnp.maximum(ti - 1, 0), hp)),
      out_shape=jax.ShapeDtypeStruct((b, s, R_WIDTH), BF16),
      scratch_shapes=[
          pltpu.VMEM((RW_P, 2 * R_HEAD, 2 * R_HEAD), F32),
          pltpu.VMEM((8, pw), F32),
          pltpu.VMEM((1, LORA_PAD), F32),
      ],
      compiler_params=pltpu.CompilerParams(
          dimension_semantics=("parallel", "parallel", "arbitrary"),
          vmem_limit_bytes=VMEM_LIMIT),
  )(z_x, z_x, z_x, lo_x, z_x, z_mp, z_mp, z_mp, lo_mp, pvec, mu_l, wup_h, wup_l, aup)


def _out_kernel(x_ref, oa_ref, or_ref, wa_ref, wr_ref, ge_ref, be_ref, gp_ref, bp_ref, o_ref):
  h = _ln_rows(x_ref[...], ge_ref[...], be_ref[...])
  y = (jnp.dot(oa_ref[...], wa_ref[...], preferred_element_type=F32)
       + jnp.dot(or_ref[...], wr_ref[...], preferred_element_type=F32))
  o_ref[...] = _ln_rows(DEEPNORM_ALPHA * h + y, gp_ref[...], bp_ref[...])


def _out_proj(x2d, oa, orw, wa, wr, ge, be, gp, bp, tm):
  m, d = x2d.shape
  vec = pl.BlockSpec((1, d), lambda i: (0, 0))
  return pl.pallas_call(
      _out_kernel,
      name="out_proj",
      grid=(m // tm,),
      in_specs=[
          pl.BlockSpec((tm, d), lambda i: (i, 0)),
          pl.BlockSpec((tm, A_WIDTH), lambda i: (i, 0)),
          pl.BlockSpec((tm, R_WIDTH), lambda i: (i, 0)),
          pl.BlockSpec((A_WIDTH, d), lambda i: (0, 0), pipeline_mode=pl.Buffered(1)),
          pl.BlockSpec((R_WIDTH, d), lambda i: (0, 0), pipeline_mode=pl.Buffered(1)),
          vec, vec, vec, vec,
      ],
      out_specs=pl.BlockSpec((tm, d), lambda i: (i, 0)),
      out_shape=jax.ShapeDtypeStruct((m, d), F32),
      compiler_params=pltpu.CompilerParams(
          dimension_semantics=("parallel",),
          vmem_limit_bytes=VMEM_LIMIT),
  )(x2d, oa, orw, wa, wr, ge, be, gp, bp)


def kernel(x, meta_tokens, ln_emb_g, ln_emb_b, rel_bias, w_in, w_out, lambda_q1, lambda_k1, lambda_q2,
           lambda_k2, subln_g, rw_mu, rw_w0, rw_w_up, rw_a0, rw_a_up, rw_k_k, rw_k_a, rw_r_k, rw_gn_g,
           rw_gn_b, ln_post_g, ln_post_b):
  b, s, d = x.shape
  l = 0
  wi = w_in[l]
  c_lo = 4 * A_WIDTH + 3 * R_WIDTH
  c_gr = c_lo + DECAY_LORA + ICLR_LORA
  lora_pad = LORA_PAD - DECAY_LORA - ICLR_LORA
  c_v = 2 * A_WIDTH
  w_all = wi.astype(BF16)
  w_gr = wi[:, c_gr:].astype(BF16)
  w_vt = wi[:, c_v:c_v + A_WIDTH].T.astype(BF16)
  w_lora = jnp.pad(wi[:, c_lo:c_gr], ((0, 0), (0, lora_pad))).astype(BF16)

  ge, be = ln_emb_g.reshape(1, d), ln_emb_b.reshape(1, d)
  x2d = x.reshape(b * s, d)
  z_x, vt_x, lo_x = _ln_matmul(x2d, ge, be, w_all, w_gr, w_vt, w_lora, 1024)
  z_x, lo_x = z_x.reshape(b, s, -1), lo_x.reshape(b, s, LORA_PAD)
  z_m, vt_m, lo_m = _ln_matmul(meta_tokens, ge, be, w_all, w_gr, w_vt, w_lora, N_META)

  bias_d, bias_s, bias_m = _bias_tiles(rel_bias)
  lam_p = jnp.stack([lambda_q1[l], lambda_k1[l], lambda_q2[l], lambda_k2[l]], axis=0)
  o_attn = _attention(z_x, vt_x, z_m, vt_m, bias_d, bias_s, bias_m, lam_p, subln_g[l].reshape(1, A_V_DIM))

  mu = rw_mu[l]
  zeros = jnp.zeros((R_WIDTH,), F32)
  pvec = jnp.stack([mu[:R_WIDTH], mu[R_WIDTH:2 * R_WIDTH], mu[2 * R_WIDTH:3 * R_WIDTH], rw_w0[l], rw_a0[l],
                    rw_k_k[l], rw_k_a[l], rw_r_k[l].reshape(R_WIDTH), rw_gn_g[l], rw_gn_b[l]]
                   + [zeros] * 6, axis=0)
  mu_l = jnp.pad(mu[3 * R_WIDTH:], (0, lora_pad)).reshape(1, LORA_PAD)
  wup = jnp.pad(rw_w_up[l], ((0, LORA_PAD - DECAY_LORA), (0, 0)))
  wup_h = wup.astype(BF16)
  wup_l = (wup - wup_h.astype(F32)).astype(BF16)
  aup = jnp.pad(rw_a_up[l], ((DECAY_LORA, lora_pad), (0, 0))).astype(BF16)
  front = ((RW_TB - N_META, 0), (0, 0))
  o_rwkv = _rwkv(z_x, lo_x, jnp.pad(z_m, front), jnp.pad(lo_m, front), pvec, mu_l, wup_h, wup_l, aup)

  wo = w_out[l].astype(BF16)
  out = _out_proj(x2d, o_attn.reshape(b * s, A_WIDTH), o_rwkv.reshape(b * s, R_WIDTH),
                  wo[:A_WIDTH], wo[A_WIDTH:], ge, be,
                  ln_post_g[l].reshape(1, d), ln_post_b[l].reshape(1, d), 512)
  return out.reshape(b, s, d)
```

```python
import functools
import math

import numpy as np
import jax
import jax.numpy as jnp
from jax import lax
from jax.experimental import pallas as pl
from jax.experimental.pallas import tpu as pltpu

D_MODEL = 2048
N_META = 16
A_HEADS = 8
A_V_DIM = 128
A_QK_DIM = 64
A_WIDTH = A_HEADS * A_V_DIM
R_HEAD = 64
R_WIDTH = 1024
R_PAIRS = R_WIDTH // (2 * R_HEAD)
DECAY_LORA = 96
ICLR_LORA = 96
LORA_PAD = 256
N_BUCKETS = 32
MAX_DISTANCE = 128
LN_EPS = 1e-5
SUBLN_EPS = 1e-5
GN_EPS = 64e-5
DEPTH = 1
DEEPNORM_ALPHA = (2 * DEPTH) ** 0.25
LAM_INIT = 0.8 - 0.6 * math.exp(-0.3 * 0)
NEG = -1e30

ATT_T = 256
ATT_G = 8
ONES_ROWS = 16
IN_TN = 1024
Z_Q, Z_K, Z_GA, Z_RR, Z_RK, Z_RV, Z_GR = (i * 1024 for i in range(7))
LOG2E = math.log2(math.e)
Q_SCALE = A_QK_DIM ** -0.5 * LOG2E
RW_TB = 128
RW_C = 64
RW_P = 8
VMEM_LIMIT = 56 * 1024 * 1024

F32 = jnp.float32
BF16 = jnp.bfloat16


def _ln_rows(x, g, b):
  mu = jnp.mean(x, axis=-1, keepdims=True)
  xc = x - mu
  var = jnp.mean(xc * xc, axis=-1, keepdims=True)
  return xc * lax.rsqrt(var + LN_EPS) * g + b


def _cast_kernel(x_ref, o_ref):
  o_ref[...] = x_ref[...].astype(o_ref.dtype)


def _cast_bf16(w, rows):
  k, n = w.shape
  return pl.pallas_call(
      _cast_kernel,
      name="cast_bf16",
      grid=(k // rows,),
      in_specs=[pl.BlockSpec((rows, n), lambda i: (i, 0))],
      out_specs=pl.BlockSpec((rows, n), lambda i: (i, 0)),
      out_shape=jax.ShapeDtypeStruct((k, n), BF16),
      compiler_params=pltpu.CompilerParams(
          dimension_semantics=("parallel",),
          vmem_limit_bytes=VMEM_LIMIT),
  )(w)


def _ln_mm_kernel(x_ref, g_ref, b_ref, wm_ref, wgr_ref, wvt_ref, wl_ref, om_ref, ovt_ref, ol_ref, hn_ref):
  j = pl.program_id(1)
  n_main = pl.num_programs(1) - 2

  @pl.when(j == 0)
  def _():
    hn_ref[...] = _ln_rows(x_ref[...], g_ref[...], b_ref[...]).astype(BF16)

  @pl.when(j < n_main - 1)
  def _():
    scale = jnp.where(j == Z_Q // IN_TN, Q_SCALE, 1.0)
    z = jnp.dot(hn_ref[...], wm_ref[...], preferred_element_type=F32)
    om_ref[...] = (z * scale).astype(om_ref.dtype)

  @pl.when(j == n_main - 1)
  def _():
    om_ref[...] = jnp.dot(hn_ref[...], wgr_ref[...], preferred_element_type=F32).astype(om_ref.dtype)

  @pl.when(j == n_main)
  def _():
    zt = _dot_nt(wvt_ref[...], hn_ref[...])
    tv = ovt_ref.shape[2]
    for c in range(ovt_ref.shape[0]):
      ovt_ref[c] = zt[:, c * tv:(c + 1) * tv].astype(ovt_ref.dtype)

  @pl.when(j == n_main + 1)
  def _():
    ol_ref[...] = jnp.dot(hn_ref[...], wl_ref[...], preferred_element_type=F32)


def _ln_matmul(x2d, g, b, w_all, w_gr, w_vt, w_lora, tm):
  m, d = x2d.shape
  tn = IN_TN
  nj = Z_GR // tn + 1
  n = nj * tn
  last = nj - 1
  tv = min(ATT_T, tm)
  v_tile = 2 * A_WIDTH // tn
  once = dict(pipeline_mode=pl.Buffered(1))

  def w_map(i, j):
    jj = jnp.minimum(j, last - 1)
    return (0, jnp.where(jj >= v_tile, jj + 1, jj))

  return pl.pallas_call(
      _ln_mm_kernel,
      name="ln_inproj",
      grid=(m // tm, nj + 2),
      in_specs=[
          pl.BlockSpec((tm, d), lambda i, j: (i, 0)),
          pl.BlockSpec((1, d), lambda i, j: (0, 0)),
          pl.BlockSpec((1, d), lambda i, j: (0, 0)),
          pl.BlockSpec((d, tn), w_map),
          pl.BlockSpec((d, tn), lambda i, j: (0, 0), **once),
          pl.BlockSpec((A_WIDTH, d), lambda i, j: (0, 0), **once),
          pl.BlockSpec((d, LORA_PAD), lambda i, j: (0, 0), **once),
      ],
      out_specs=[
          pl.BlockSpec((tm, tn), lambda i, j: (i, jnp.minimum(j, last))),
          pl.BlockSpec((tm // tv, A_WIDTH, tv), lambda i, j: (i, 0, 0)),
          pl.BlockSpec((tm, LORA_PAD), lambda i, j: (i, 0)),
      ],
      out_shape=[
          jax.ShapeDtypeStruct((m, n), BF16),
          jax.ShapeDtypeStruct((m // tv, A_WIDTH, tv), BF16),
          jax.ShapeDtypeStruct((m, LORA_PAD), F32),
      ],
      scratch_shapes=[pltpu.VMEM((tm, d), BF16)],
      compiler_params=pltpu.CompilerParams(
          dimension_semantics=("parallel", "arbitrary"),
          vmem_limit_bytes=VMEM_LIMIT),
  )(x2d, g, b, w_all, w_gr, w_vt, w_lora)


def _bucket_thresholds():
  n = np.arange(0, 4 * MAX_DISTANCE, dtype=np.int64)
  max_exact = N_BUCKETS // 2
  nf = np.maximum(n, 1).astype(np.float32)
  large = max_exact + (np.log(nf / np.float32(max_exact)) / np.float32(math.log(MAX_DISTANCE / max_exact))
                       * np.float32(N_BUCKETS - max_exact)).astype(np.int32)
  large = np.minimum(large, N_BUCKETS - 1)
  bucket = np.where(n < max_exact, n, large)
  assert np.all(np.diff(bucket) >= 0) and bucket[-1] == N_BUCKETS - 1
  return [int(np.argmax(bucket >= b)) for b in range(N_BUCKETS)]


_THR = _bucket_thresholds()


def _bias_kernel(rb_ref, diag_ref, sub_ref, meta_ref):
  h = pl.program_id(0)
  far = rb_ref[N_BUCKETS - 1, h]

  def bias_of(n):
    out = jnp.full(n.shape, (rb_ref[0, h] - far) * LOG2E, F32)
    for b in range(1, N_BUCKETS):
      out = jnp.where(n >= _THR[b], (rb_ref[b, h] - far) * LOG2E, out)
    return out

  t = ATT_T
  kj = lax.broadcasted_iota(jnp.int32, (t, t), 0)
  qi = lax.broadcasted_iota(jnp.int32, (t, t), 1)
  d = qi - kj
  diag_ref[...] = jnp.where(d >= 0, bias_of(d), NEG)
  sub_ref[...] = bias_of(d + t)
  km = lax.broadcasted_iota(jnp.int32, (N_META, t), 0)
  qm = lax.broadcasted_iota(jnp.int32, (N_META, t), 1)
  meta_ref[...] = bias_of(qm - km + N_META)


def _bias_tiles(rel_bias):
  t = ATT_T
  return pl.pallas_call(
      _bias_kernel,
      name="bias_tiles",
      grid=(A_HEADS,),
      in_specs=[pl.BlockSpec(memory_space=pltpu.SMEM)],
      out_specs=[
          pl.BlockSpec((None, t, t), lambda h: (h, 0, 0)),
          pl.BlockSpec((None, t, t), lambda h: (h, 0, 0)),
          pl.BlockSpec((None, N_META, t), lambda h: (h, 0, 0)),
      ],
      out_shape=[
          jax.ShapeDtypeStruct((A_HEADS, t, t), F32),
          jax.ShapeDtypeStruct((A_HEADS, t, t), F32),
          jax.ShapeDtypeStruct((A_HEADS, N_META, t), F32),
      ],
  )(rel_bias)


def _dot_nt(a, b):
  return lax.dot_general(a, b, (((1,), (1,)), ((), ())), preferred_element_type=F32)


def _attn_kernel(q_ref, kx_ref, vt_ref, km_ref, vmt_ref, ga_ref, bd_ref, bs_ref, bm_ref,
                 lam_ref, sg_ref, o_ref, m_ref, alpha_ref, acc_ref, pt_ref):
  t = ATT_T
  g = ATT_G
  nc = 2 * g
  dv = A_V_DIM
  qi = pl.program_id(2)
  lane = lax.broadcasted_iota(jnp.int32, (t, 128), 1)
  qs = []
  for hh in range(g):
    q = q_ref[:, hh * 128:(hh + 1) * 128]
    zero = jnp.zeros_like(q)
    qs += [jnp.where(lane < A_QK_DIM, q, zero), jnp.where(lane >= A_QK_DIM, q, zero)]

  m_ref[...] = jnp.full(m_ref.shape, NEG, F32)
  acc_ref[...] = jnp.zeros(acc_ref.shape, F32)
  alpha_ref[...] = jnp.ones(alpha_ref.shape, F32)
  pt_ref[...] = jnp.zeros(pt_ref.shape, BF16)

  def v_tile(j):
    ones = jnp.ones((ONES_ROWS, t), BF16)
    return [jnp.concatenate([vt_ref[j, hh * dv:(hh + 1) * dv, :], ones], axis=0) for hh in range(g)]

  def k_tile(j):
    off = pl.multiple_of(j * t, t)
    return [kx_ref[pl.ds(off, t), hh * 128:(hh + 1) * 128] for hh in range(g)]

  def softmax_stage(c, m_prev, s_list):
    m_new = m_prev
    for s in s_list:
      m_new = jnp.maximum(m_new, jnp.max(s, axis=0, keepdims=True))
    m_ref[c] = m_new
    alpha_ref[c] = jnp.exp2(m_prev - m_new)
    return [jnp.exp2(s - m_new).astype(BF16) for s in s_list]

  def step(j_cur, j_prev, biases):
    ks, vts = k_tile(j_cur), v_tile(j_prev)
    pv, st = [], []
    for c in range(nc):
      pv.append(jnp.dot(vts[c // 2], pt_ref[c], preferred_element_type=F32))
      s = _dot_nt(ks[c // 2], qs[c])
      st.append(s if biases is None else s + biases[c // 2])
    for c in range(nc):
      acc_ref[c] = alpha_ref[c] * acc_ref[c] + pv[c]
    for c in range(nc):
      pt_ref[c], = softmax_stage(c, m_ref[c], [st[c]])

  n_far = jnp.maximum(qi - 1, 0)

  def far_body(j, carry):
    step(2 * j, jnp.maximum(2 * j - 1, 0), None)
    step(2 * j + 1, 2 * j, None)
    return carry

  lax.fori_loop(0, n_far // 2, far_body, 0)

  @pl.when(n_far % 2 == 1)
  def _():
    step(n_far - 1, jnp.maximum(n_far - 2, 0), None)

  @pl.when(qi >= 1)
  def _():
    step(qi - 1, jnp.maximum(qi - 2, 0), [bs_ref[hh] for hh in range(g)])

  vts = v_tile(jnp.maximum(qi - 1, 0))
  ks = k_tile(qi)
  pv, st, sm = [], [], []
  for c in range(nc):
    hh = c // 2
    pv.append(jnp.dot(vts[hh], pt_ref[c], preferred_element_type=F32))
    st.append(_dot_nt(ks[hh], qs[c]) + bd_ref[hh])
    sm.append(_dot_nt(km_ref[:, hh * 128:(hh + 1) * 128], qs[c]) + jnp.where(qi == 0, bm_ref[hh], 0.0))
  for c in range(nc):
    acc_ref[c] = alpha_ref[c] * acc_ref[c] + pv[c]
  pts = [softmax_stage(c, m_ref[c], [st[c], sm[c]]) for c in range(nc)]
  vts = v_tile(qi)
  ones_m = jnp.ones((ONES_ROWS, N_META), BF16)
  vtm = [jnp.concatenate([vmt_ref[0, hh * dv:(hh + 1) * dv, :], ones_m], axis=0) for hh in range(g)]
  for c in range(nc):
    pv = (jnp.dot(vts[c // 2], pts[c][0], preferred_element_type=F32)
          + jnp.dot(vtm[c // 2], pts[c][1], preferred_element_type=F32))
    acc_ref[c] = alpha_ref[c] * acc_ref[c] + pv

  lp = lam_ref[...]
  lam = (jnp.exp(jnp.sum(lp[0:1] * lp[1:2], axis=1, keepdims=True))
         - jnp.exp(jnp.sum(lp[2:3] * lp[3:4], axis=1, keepdims=True)) + LAM_INIT)
  for hh in range(g):
    a0, a1 = acc_ref[2 * hh], acc_ref[2 * hh + 1]
    ot = a0[:dv] / a0[dv:dv + 1] - lam * (a1[:dv] / a1[dv:dv + 1])
    ot = ot * lax.rsqrt(jnp.mean(ot * ot, axis=0, keepdims=True) + SUBLN_EPS)
    o = ot.T * (sg_ref[...] * (1.0 - LAM_INIT))
    gate = ga_ref[:, hh * dv:(hh + 1) * dv].astype(F32)
    o_ref[:, hh * dv:(hh + 1) * dv] = (o * (gate / (1.0 + jnp.exp(-gate)))).astype(o_ref.dtype)


def _attention(z_x, vt_x, z_m, vt_m, bias_d, bias_s, bias_m, lam_p, subln_g):
  b, s, _ = z_x.shape
  t = ATT_T
  g = ATT_G
  w = 128 * g
  hb = A_HEADS // g
  kb, gb = Z_K // w, Z_GA // w
  return pl.pallas_call(
      _attn_kernel,
      name="diff_attn",
      grid=(b, hb, s // t),
      in_specs=[
          pl.BlockSpec((None, t, w), lambda bi, hi, qi: (bi, qi, hi)),
          pl.BlockSpec((None, s, w), lambda bi, hi, qi: (bi, 0, kb + hi), pipeline_mode=pl.Buffered(1)),
          pl.BlockSpec((None, s // t, w, t), lambda bi, hi, qi: (bi, 0, hi, 0), pipeline_mode=pl.Buffered(1)),
          pl.BlockSpec((N_META, w), lambda bi, hi, qi: (0, kb + hi)),
          pl.BlockSpec((1, w, N_META), lambda bi, hi, qi: (0, hi, 0)),
          pl.BlockSpec((None, t, w), lambda bi, hi, qi: (bi, qi, gb + hi)),
          pl.BlockSpec((g, t, t), lambda bi, hi, qi: (hi, 0, 0)),
          pl.BlockSpec((g, t, t), lambda bi, hi, qi: (hi, 0, 0)),
          pl.BlockSpec((g, N_META, t), lambda bi, hi, qi: (hi, 0, 0)),
          pl.BlockSpec((4, A_QK_DIM), lambda bi, hi, qi: (0, 0)),
          pl.BlockSpec((1, A_V_DIM), lambda bi, hi, qi: (0, 0)),
      ],
      out_specs=pl.BlockSpec((None, t, w), lambda bi, hi, qi: (bi, qi, hi)),
      out_shape=jax.ShapeDtypeStruct((b, s, A_WIDTH), BF16),
      scratch_shapes=[
          pltpu.VMEM((2 * g, 1, t), F32),
          pltpu.VMEM((2 * g, 1, t), F32),
          pltpu.VMEM((2 * g, A_V_DIM + ONES_ROWS, t), F32),
          pltpu.VMEM((2 * g, t, t), BF16),
      ],
      compiler_params=pltpu.CompilerParams(
          dimension_semantics=("parallel", "parallel", "arbitrary"),
          vmem_limit_bytes=VMEM_LIMIT),
  )(z_x, z_x, vt_x.reshape(b, s // t, A_WIDTH, t), z_m, vt_m, z_x, bias_d, bias_s, bias_m, lam_p, subln_g)


def _seg_sum(x):
  lane = lax.broadcasted_iota(jnp.int32, x.shape, 1)
  first = lane < R_HEAD
  lo = jnp.sum(jnp.where(first, x, 0.0), axis=1, keepdims=True)
  hi = jnp.sum(jnp.where(first, 0.0, x), axis=1, keepdims=True)
  return jnp.where(first, lo, hi)


def _split_bf16(x):
  hi = x.astype(BF16)
  return hi, x - hi.astype(F32)


def _dot_tn(a, b):
  return lax.dot_general(a, b, (((0,), (0,)), ((), ())), preferred_element_type=F32)


def _bdot(a, b):
  return lax.dot_general(a, b, (((2,), (1,)), ((0,), (0,))), preferred_element_type=F32)


def _bdot_nt(a, b):
  return lax.dot_general(a, b, (((2,), (2,)), ((0,), (0,))), preferred_element_type=F32)


def _rwkv_kernel(rx_ref, kx_ref, vx_ref, lx_ref, gr_ref, rm_ref, kmt_ref, vmt_ref, lm_ref,
                 pv_ref, mul_ref, wuph_ref, wupl_ref, aup_ref, o_ref, s_ref, prev_ref, prevl_ref):
  tb, c = RW_TB, RW_C
  nh = R_HEAD
  c2 = 2 * c
  ti = pl.program_id(2)
  is_meta = ti == 0

  @pl.when(is_meta)
  def _():
    s_ref[...] = jnp.zeros_like(s_ref)
    prev_ref[...] = jnp.zeros_like(prev_ref)
    prevl_ref[...] = jnp.zeros_like(prevl_ref)

  row = lax.broadcasted_iota(jnp.int32, (tb, 128), 0)
  rowl = lax.broadcasted_iota(jnp.int32, (tb, LORA_PAD), 0)

  def shifted(z, prev, mu, rows):
    z_prev = jnp.where(rows == 0, prev, pltpu.roll(z, 1, 0))
    return z + (z_prev - z) * mu

  z_l = jnp.where(is_meta, lm_ref[...], lx_ref[...])
  lo = shifted(z_l, prevl_ref[...], mul_ref[...], rowl)
  prevl_ref[...] = z_l[tb - 1:tb]
  th_h, th_l = _split_bf16(jnp.tanh(lo))
  w_lora = (jnp.dot(th_h, wuph_ref[...], preferred_element_type=F32)
            + jnp.dot(th_h, wupl_ref[...], preferred_element_type=F32)
            + jnp.dot(th_l.astype(BF16), wuph_ref[...], preferred_element_type=F32))
  a_lora = jnp.dot(lo.astype(BF16), aup_ref[...], preferred_element_type=F32)

  ii = lax.broadcasted_iota(jnp.int32, (tb, tb), 0)
  jj = lax.broadcasted_iota(jnp.int32, (tb, tb), 1)
  shift = int(math.log2(c))
  same = lax.shift_right_logical(ii, shift) == lax.shift_right_logical(jj, shift)
  cum_op = jnp.where(same, jnp.where(jj <= ii, 1.0, 0.0), 0.0).astype(BF16)

  ci = lax.broadcasted_iota(jnp.int32, (c2, c2), 0)
  cj = lax.broadcasted_iota(jnp.int32, (c2, c2), 1)
  diag = ci == cj
  strict2 = jnp.concatenate([cj < ci, cj < ci], axis=1)
  incl2 = jnp.concatenate([cj <= ci, cj <= ci], axis=1)
  first = lax.broadcasted_iota(jnp.int32, (c, 128), 1) < nh

  def stack(x):
    return jnp.concatenate([jnp.where(first, x, 0.0), jnp.where(first, 0.0, x)], axis=0)

  ncc = tb // c
  chains = {name: [] for name in ("at", "rt", "bt", "kt", "bh", "kh", "vv", "gd")}
  post = []
  for p in range(RW_P):
    ls = slice(p * 128, (p + 1) * 128)
    pv = pv_ref[:, ls]
    mu_r, mu_k, mu_v = pv[0:1], pv[1:2], pv[2:3]
    w0, a0, k_k, k_a, r_k, gn_g, gn_b = pv[3:4], pv[4:5], pv[5:6], pv[6:7], pv[7:8], pv[8:9], pv[9:10]

    z_r = jnp.where(is_meta, rm_ref[:, ls], rx_ref[:, ls]).astype(F32)
    z_k = jnp.where(is_meta, kmt_ref[:, ls], kx_ref[:, ls]).astype(F32)
    z_v = jnp.where(is_meta, vmt_ref[:, ls], vx_ref[:, ls]).astype(F32)
    r = shifted(z_r, prev_ref[0:1, ls], mu_r, row)
    k = shifted(z_k, prev_ref[1:2, ls], mu_k, row)
    v = shifted(z_v, prev_ref[2:3, ls], mu_v, row)
    prev_ref[0:1, ls] = z_r[tb - 1:tb]
    prev_ref[1:2, ls] = z_k[tb - 1:tb]
    prev_ref[2:3, ls] = z_v[tb - 1:tb]

    u = -(w0 + w_lora[:, ls])
    softplus = jnp.maximum(u, 0.0) + jnp.log(1.0 + jnp.exp(-jnp.abs(u)))
    logw = -jnp.exp(-softplus - 0.5)
    a = 1.0 / (1.0 + jnp.exp(-(a0 + a_lora[:, ls])))
    kk = k * k_k
    kk = kk / jnp.maximum(jnp.sqrt(_seg_sum(kk * kk)), 1e-12)
    k_mod = k * (1.0 + (a - 1.0) * k_a)
    bonus = _seg_sum(r * k_mod * r_k) * v

    lw_h, lw_r = _split_bf16(logw)
    lw_m, lw_l = _split_bf16(lw_r)
    cum3 = jnp.dot(cum_op, jnp.concatenate([lw_h, lw_m, lw_l.astype(BF16)], axis=1),
                   preferred_element_type=F32)
    cum = cum3[:, :128] + cum3[:, 128:256] + cum3[:, 256:]
    tot = jnp.concatenate([jnp.broadcast_to(cum[cc * c + c - 1:cc * c + c], (c, 128)) for cc in range(ncc)],
                          axis=0)
    p_inv = jnp.exp(-cum)
    a_t = -kk * jnp.exp(cum - logw)
    b_t = kk * a * p_inv
    k_t = k_mod * p_inv
    r_t = r * jnp.exp(cum)
    p_end = jnp.exp(tot - cum)
    b_h = kk * a * p_end
    k_h = k_mod * p_end
    g_diag = jnp.exp(tot)

    for cc in range(ncc):
      rs = slice(cc * c, (cc + 1) * c)
      for name, val in (("at", a_t), ("rt", r_t), ("bt", b_t), ("kt", k_t), ("bh", b_h), ("kh", k_h),
                        ("vv", v)):
        chains[name].append(stack(val[rs]))
      chains["gd"].append(g_diag[cc * c:cc * c + 1])
    post.append((bonus, gn_g, gn_b))

  nb = RW_P * ncc
  at, rt, bt, kt, bh, kh, vv = (jnp.stack(chains[name]) for name in ("at", "rt", "bt", "kt", "bh", "kh", "vv"))
  at_b, vv_b, bh_b = at.astype(BF16), vv.astype(BF16), bh.astype(BF16)
  bk = jnp.concatenate([bt, kt], axis=1).astype(BF16)
  top = jnp.where(strict2, _bdot_nt(at_b, bk), 0.0)
  lblk = jnp.where(incl2, _bdot_nt(rt.astype(BF16), bk), 0.0)
  nm, mak = top[:, :, :c2], top[:, :, c2:]
  tinv = jnp.where(diag, 1.0, nm)
  npow = nm.astype(BF16)
  for _ in range(5):
    npow = _bdot(npow, npow).astype(BF16)
    tinv = tinv + _bdot(tinv.astype(BF16), npow)
  x1 = _bdot(mak.astype(BF16), vv_b)
  wu_b = _bdot(tinv.astype(BF16), jnp.concatenate([at_b, x1.astype(BF16)], axis=2)).astype(BF16)
  rhs = jnp.concatenate([wu_b, jnp.concatenate([jnp.zeros_like(vv_b), vv_b], axis=2)], axis=1)
  qy = _bdot(lblk.astype(BF16), rhs)
  q_h = (rt + qy[:, :, :c2]).astype(BF16)
  y0 = qy[:, :, c2:]
  uv = jnp.concatenate([wu_b[:, :, c2:], vv_b], axis=1)
  bkh = jnp.concatenate([bh_b, kh.astype(BF16)], axis=1)
  g_m = [(jnp.where(diag, chains["gd"][n], 0.0) + _dot_tn(wu_b[n, :, :c2], bh_b[n])).astype(BF16)
         for n in range(nb)]
  h_m = [_dot_tn(uv[n], bkh[n]) for n in range(nb)]

  states = [s_ref[p] for p in range(RW_P)]
  y_rows = [[] for _ in range(RW_P)]
  for cc in range(ncc):
    for p in range(RW_P):
      n = p * ncc + cc
      s_old_b = states[p].astype(BF16)
      y2 = _dot_nt(q_h[n], s_old_b) + y0[n]
      states[p] = jnp.dot(s_old_b, g_m[n], preferred_element_type=F32) + h_m[n]
      y_rows[p].append(y2[:c] + y2[c:])

  for p in range(RW_P):
    ls = slice(p * 128, (p + 1) * 128)
    s_ref[p] = states[p]
    bonus, gn_g, gn_b = post[p]
    y = jnp.concatenate(y_rows[p], axis=0)
    mean = _seg_sum(y) * (1.0 / nh)
    yc = y - mean
    var = _seg_sum(yc * yc) * (1.0 / nh)
    yn = yc * lax.rsqrt(var + GN_EPS) * gn_g + gn_b
    g = gr_ref[:, ls].astype(F32)
    o_ref[:, ls] = ((yn + bonus) * (g / (1.0 + jnp.exp(-g)))).astype(o_ref.dtype)


def _rwkv(z_x, lo_x, z_mp, lo_mp, pvec, mu_l, wup_h, wup_l, aup):
  b, s, _ = z_x.shape
  tb = RW_TB
  nt = s // tb + 1
  pw = 128 * RW_P

  def xmap(col):
    return lambda bi, hp, ti: (bi, jnp.maximum(ti - 1, 0), col // pw + hp)

  def mmap(col):
    return lambda bi, hp, ti: (0, col // pw + hp)

  return pl.pallas_call(
      _rwkv_kernel,
      name="rwkv7",
      grid=(b, R_PAIRS // RW_P, nt),
      in_specs=[
          pl.BlockSpec((None, tb, pw), xmap(Z_RR)),
          pl.BlockSpec((None, tb, pw), xmap(Z_RK)),
          pl.BlockSpec((None, tb, pw), xmap(Z_RV)),
          pl.BlockSpec((None, tb, LORA_PAD), lambda bi, hp, ti: (bi, jnp.maximum(ti - 1, 0), 0)),
          pl.BlockSpec((None, tb, pw), xmap(Z_GR)),
          pl.BlockSpec((tb, pw), mmap(Z_RR)),
          pl.BlockSpec((tb, pw), mmap(Z_RK)),
          pl.BlockSpec((tb, pw), mmap(Z_RV)),
          pl.BlockSpec((tb, LORA_PAD), lambda bi, hp, ti: (0, 0)),
          pl.BlockSpec((16, pw), lambda bi, hp, ti: (0, hp)),
          pl.BlockSpec((1, LORA_PAD), lambda bi, hp, ti: (0, 0)),
          pl.BlockSpec((LORA_PAD, pw), lambda bi, hp, ti: (0, hp)),
          pl.BlockSpec((LORA_PAD, pw), lambda bi, hp, ti: (0, hp)),
          pl.BlockSpec((LORA_PAD, pw), lambda bi, hp, ti: (0, hp)),
      ],
      out_specs=pl.BlockSpec((None, tb, pw), lambda bi, hp, ti: (bi, jnp.maximum(ti - 1, 0), hp)),
      out_shape=jax.ShapeDtypeStruct((b, s, R_WIDTH), BF16),
      scratch_shapes=[
          pltpu.VMEM((RW_P, 2 * R_HEAD, 2 * R_HEAD), F32),
          pltpu.VMEM((8, pw), F32),
          pltpu.VMEM((1, LORA_PAD), F32),
      ],
      compiler_params=pltpu.CompilerParams(
          dimension_semantics=("parallel", "parallel", "arbitrary"),
          vmem_limit_bytes=VMEM_LIMIT),
  )(z_x, z_x, z_x, lo_x, z_x, z_mp, z_mp, z_mp, lo_mp, pvec, mu_l, wup_h, wup_l, aup)


def _out_kernel(x_ref, oa_ref, or_ref, wa_ref, wr_ref, ge_ref, be_ref, gp_ref, bp_ref, o_ref):
  h = _ln_rows(x_ref[...], ge_ref[...], be_ref[...])
  y = (jnp.dot(oa_ref[...], wa_ref[...], preferred_element_type=F32)
       + jnp.dot(or_ref[...], wr_ref[...], preferred_element_type=F32))
  o_ref[...] = _ln_rows(DEEPNORM_ALPHA * h + y, gp_ref[...], bp_ref[...])


def _out_proj(x2d, oa, orw, wa, wr, ge, be, gp, bp, tm):
  m, d = x2d.shape
  vec = pl.BlockSpec((1, d), lambda i: (0, 0))
  return pl.pallas_call(
      _out_kernel,
      name="out_proj",
      grid=(m // tm,),
      in_specs=[
          pl.BlockSpec((tm, d), lambda i: (i, 0)),
          pl.BlockSpec((tm, A_WIDTH), lambda i: (i, 0)),
          pl.BlockSpec((tm, R_WIDTH), lambda i: (i, 0)),
          pl.BlockSpec((A_WIDTH, d), lambda i: (0, 0), pipeline_mode=pl.Buffered(1)),
          pl.BlockSpec((R_WIDTH, d), lambda i: (0, 0), pipeline_mode=pl.Buffered(1)),
          vec, vec, vec, vec,
      ],
      out_specs=pl.BlockSpec((tm, d), lambda i: (i, 0)),
      out_shape=jax.ShapeDtypeStruct((m, d), F32),
      compiler_params=pltpu.CompilerParams(
          dimension_semantics=("parallel",),
          vmem_limit_bytes=VMEM_LIMIT),
  )(x2d, oa, orw, wa, wr, ge, be, gp, bp)


def kernel(x, meta_tokens, ln_emb_g, ln_emb_b, rel_bias, w_in, w_out, lambda_q1, lambda_k1, lambda_q2,
           lambda_k2, subln_g, rw_mu, rw_w0, rw_w_up, rw_a0, rw_a_up, rw_k_k, rw_k_a, rw_r_k, rw_gn_g,
           rw_gn_b, ln_post_g, ln_post_b):
  b, s, d = x.shape
  l = 0
  wi = w_in[l]
  c_lo = 4 * A_WIDTH + 3 * R_WIDTH
  c_gr = c_lo + DECAY_LORA + ICLR_LORA
  lora_pad = LORA_PAD - DECAY_LORA - ICLR_LORA
  c_v = 2 * A_WIDTH
  w_all = _cast_bf16(wi, 256)
  w_gr = wi[:, c_gr:].astype(BF16)
  w_vt = wi[:, c_v:c_v + A_WIDTH].T.astype(BF16)
  w_lora = jnp.pad(wi[:, c_lo:c_gr], ((0, 0), (0, lora_pad))).astype(BF16)

  ge, be = ln_emb_g.reshape(1, d), ln_emb_b.reshape(1, d)
  x2d = x.reshape(b * s, d)
  z_x, vt_x, lo_x = _ln_matmul(x2d, ge, be, w_all, w_gr, w_vt, w_lora, 1024)
  z_x, lo_x = z_x.reshape(b, s, -1), lo_x.reshape(b, s, LORA_PAD)
  z_m, vt_m, lo_m = _ln_matmul(meta_tokens, ge, be, w_all, w_gr, w_vt, w_lora, N_META)

  bias_d, bias_s, bias_m = _bias_tiles(rel_bias)
  lam_p = jnp.stack([lambda_q1[l], lambda_k1[l], lambda_q2[l], lambda_k2[l]], axis=0)
  o_attn = _attention(z_x, vt_x, z_m, vt_m, bias_d, bias_s, bias_m, lam_p, subln_g[l].reshape(1, A_V_DIM))

  mu = rw_mu[l]
  zeros = jnp.zeros((R_WIDTH,), F32)
  pvec = jnp.stack([mu[:R_WIDTH], mu[R_WIDTH:2 * R_WIDTH], mu[2 * R_WIDTH:3 * R_WIDTH], rw_w0[l], rw_a0[l],
                    rw_k_k[l], rw_k_a[l], rw_r_k[l].reshape(R_WIDTH), rw_gn_g[l], rw_gn_b[l]]
                   + [zeros] * 6, axis=0)
  mu_l = jnp.pad(mu[3 * R_WIDTH:], (0, lora_pad)).reshape(1, LORA_PAD)
  wup = jnp.pad(rw_w_up[l], ((0, LORA_PAD - DECAY_LORA), (0, 0)))
  wup_h = wup.astype(BF16)
  wup_l = (wup - wup_h.astype(F32)).astype(BF16)
  aup = jnp.pad(rw_a_up[l], ((DECAY_LORA, lora_pad), (0, 0))).astype(BF16)
  front = ((RW_TB - N_META, 0), (0, 0))
  o_rwkv = _rwkv(z_x, lo_x, jnp.pad(z_m, front), jnp.pad(lo_m, front), pvec, mu_l, wup_h, wup_l, aup)

  wo = w_out[l].astype(BF16)
  out = _out_proj(x2d, o_attn.reshape(b * s, A_WIDTH), o_rwkv.reshape(b * s, R_WIDTH),
                  wo[:A_WIDTH], wo[A_WIDTH:], ge, be,
                  ln_post_g[l].reshape(1, d), ln_post_b[l].reshape(1, d), 512)
  return out.reshape(b, s, d)
```

```python
import math

import numpy as np
import jax
import jax.numpy as jnp
from jax import lax
from jax.experimental import pallas as pl
from jax.experimental.pallas import tpu as pltpu

N_META = 16
A_HEADS = 8
A_V_DIM = 128
A_QK_DIM = 64
A_WIDTH = A_HEADS * A_V_DIM
R_HEAD = 64
R_WIDTH = 1024
R_PAIRS = R_WIDTH // (2 * R_HEAD)
DECAY_LORA = 96
ICLR_LORA = 96
LORA_PAD = 256
N_BUCKETS = 32
MAX_DISTANCE = 128
LN_EPS = 1e-5
SUBLN_EPS = 1e-5
GN_EPS = 64e-5
DEPTH = 1
DEEPNORM_ALPHA = (2 * DEPTH) ** 0.25
LAM_INIT = 0.8 - 0.6 * math.exp(-0.3 * 0)
NEG = -1e30

ATT_T = 256
ATT_G = 8
ONES_ROWS = 16
IN_TN = 1024
Z_Q, Z_K, Z_GA, Z_RR, Z_RK, Z_RV, Z_GR = (i * 1024 for i in range(7))
LOG2E = math.log2(math.e)
Q_SCALE = A_QK_DIM ** -0.5 * LOG2E
RW_TB = 128
RW_C = 64
RW_P = 8
VMEM_LIMIT = 56 * 1024 * 1024

F32 = jnp.float32
BF16 = jnp.bfloat16


def _ln_rows(x, g, b):
  mu = jnp.mean(x, axis=-1, keepdims=True)
  xc = x - mu
  var = jnp.mean(xc * xc, axis=-1, keepdims=True)
  return xc * lax.rsqrt(var + LN_EPS) * g + b


def _ln_mm_kernel(x_ref, g_ref, b_ref, wm_ref, wgr_ref, wvt_ref, wl_ref, om_ref, ovt_ref, ol_ref, hn_ref):
  j = pl.program_id(1)
  n_main = pl.num_programs(1) - 2

  @pl.when(j == 0)
  def _():
    hn_ref[...] = _ln_rows(x_ref[...], g_ref[...], b_ref[...]).astype(BF16)

  @pl.when(j < n_main - 1)
  def _():
    scale = jnp.where(j == Z_Q // IN_TN, Q_SCALE, 1.0)
    z = jnp.dot(hn_ref[...], wm_ref[...], preferred_element_type=F32)
    om_ref[...] = (z * scale).astype(om_ref.dtype)

  @pl.when(j == n_main - 1)
  def _():
    om_ref[...] = jnp.dot(hn_ref[...], wgr_ref[...], preferred_element_type=F32).astype(om_ref.dtype)

  @pl.when(j == n_main)
  def _():
    zt = _dot_nt(wvt_ref[...], hn_ref[...])
    tv = ovt_ref.shape[2]
    for c in range(ovt_ref.shape[0]):
      ovt_ref[c] = zt[:, c * tv:(c + 1) * tv].astype(ovt_ref.dtype)

  @pl.when(j == n_main + 1)
  def _():
    ol_ref[...] = jnp.dot(hn_ref[...], wl_ref[...], preferred_element_type=F32)


def _ln_matmul(x2d, g, b, w_all, w_gr, w_vt, w_lora, tm):
  m, d = x2d.shape
  tn = IN_TN
  nj = Z_GR // tn + 1
  n = nj * tn
  last = nj - 1
  tv = min(ATT_T, tm)
  v_tile = 2 * A_WIDTH // tn
  once = dict(pipeline_mode=pl.Buffered(1))

  def w_map(i, j):
    jj = jnp.minimum(j, last - 1)
    return (0, jnp.where(jj >= v_tile, jj + 1, jj))

  return pl.pallas_call(
      _ln_mm_kernel,
      name="ln_inproj",
      grid=(m // tm, nj + 2),
      in_specs=[
          pl.BlockSpec((tm, d), lambda i, j: (i, 0)),
          pl.BlockSpec((1, d), lambda i, j: (0, 0)),
          pl.BlockSpec((1, d), lambda i, j: (0, 0)),
          pl.BlockSpec((d, tn), w_map),
          pl.BlockSpec((d, tn), lambda i, j: (0, 0), **once),
          pl.BlockSpec((A_WIDTH, d), lambda i, j: (0, 0), **once),
          pl.BlockSpec((d, LORA_PAD), lambda i, j: (0, 0), **once),
      ],
      out_specs=[
          pl.BlockSpec((tm, tn), lambda i, j: (i, jnp.minimum(j, last))),
          pl.BlockSpec((tm // tv, A_WIDTH, tv), lambda i, j: (i, 0, 0)),
          pl.BlockSpec((tm, LORA_PAD), lambda i, j: (i, 0)),
      ],
      out_shape=[
          jax.ShapeDtypeStruct((m, n), BF16),
          jax.ShapeDtypeStruct((m // tv, A_WIDTH, tv), BF16),
          jax.ShapeDtypeStruct((m, LORA_PAD), F32),
      ],
      scratch_shapes=[pltpu.VMEM((tm, d), BF16)],
      compiler_params=pltpu.CompilerParams(
          dimension_semantics=("parallel", "arbitrary"),
          vmem_limit_bytes=VMEM_LIMIT),
  )(x2d, g, b, w_all, w_gr, w_vt, w_lora)


def _bucket_thresholds():
  n = np.arange(0, 4 * MAX_DISTANCE, dtype=np.int64)
  max_exact = N_BUCKETS // 2
  nf = np.maximum(n, 1).astype(np.float32)
  large = max_exact + (np.log(nf / np.float32(max_exact)) / np.float32(math.log(MAX_DISTANCE / max_exact))
                       * np.float32(N_BUCKETS - max_exact)).astype(np.int32)
  large = np.minimum(large, N_BUCKETS - 1)
  bucket = np.where(n < max_exact, n, large)
  assert np.all(np.diff(bucket) >= 0) and bucket[-1] == N_BUCKETS - 1
  return [int(np.argmax(bucket >= b)) for b in range(N_BUCKETS)]


_THR = _bucket_thresholds()


def _bias_kernel(rb_ref, diag_ref, sub_ref, meta_ref):
  h = pl.program_id(0)
  far = rb_ref[N_BUCKETS - 1, h]

  def bias_of(n):
    out = jnp.full(n.shape, (rb_ref[0, h] - far) * LOG2E, F32)
    for b in range(1, N_BUCKETS):
      out = jnp.where(n >= _THR[b], (rb_ref[b, h] - far) * LOG2E, out)
    return out

  t = ATT_T
  kj = lax.broadcasted_iota(jnp.int32, (t, t), 0)
  qi = lax.broadcasted_iota(jnp.int32, (t, t), 1)
  d = qi - kj
  diag_ref[...] = jnp.where(d >= 0, bias_of(d), NEG)
  sub_ref[...] = bias_of(d + t)
  km = lax.broadcasted_iota(jnp.int32, (N_META, t), 0)
  qm = lax.broadcasted_iota(jnp.int32, (N_META, t), 1)
  meta_ref[...] = bias_of(qm - km + N_META)


def _bias_tiles(rel_bias):
  t = ATT_T
  return pl.pallas_call(
      _bias_kernel,
      name="bias_tiles",
      grid=(A_HEADS,),
      in_specs=[pl.BlockSpec(memory_space=pltpu.SMEM)],
      out_specs=[
          pl.BlockSpec((None, t, t), lambda h: (h, 0, 0)),
          pl.BlockSpec((None, t, t), lambda h: (h, 0, 0)),
          pl.BlockSpec((None, N_META, t), lambda h: (h, 0, 0)),
      ],
      out_shape=[
          jax.ShapeDtypeStruct((A_HEADS, t, t), F32),
          jax.ShapeDtypeStruct((A_HEADS, t, t), F32),
          jax.ShapeDtypeStruct((A_HEADS, N_META, t), F32),
      ],
  )(rel_bias)


def _dot_nt(a, b):
  return lax.dot_general(a, b, (((1,), (1,)), ((), ())), preferred_element_type=F32)


def _attn_kernel(q_ref, kx_ref, vt_ref, km_ref, vmt_ref, ga_ref, bd_ref, bs_ref, bm_ref,
                 lam_ref, sg_ref, o_ref, m_ref, alpha_ref, acc_ref, pt_ref):
  t = ATT_T
  g = ATT_G
  nc = 2 * g
  dv = A_V_DIM
  qi = pl.program_id(2)
  lane = lax.broadcasted_iota(jnp.int32, (t, 128), 1)
  qs = []
  for hh in range(g):
    q = q_ref[:, hh * 128:(hh + 1) * 128]
    zero = jnp.zeros_like(q)
    qs += [jnp.where(lane < A_QK_DIM, q, zero), jnp.where(lane >= A_QK_DIM, q, zero)]

  m_ref[...] = jnp.full(m_ref.shape, NEG, F32)
  acc_ref[...] = jnp.zeros(acc_ref.shape, F32)
  alpha_ref[...] = jnp.ones(alpha_ref.shape, F32)
  pt_ref[...] = jnp.zeros(pt_ref.shape, BF16)

  def v_tile(j):
    ones = jnp.ones((ONES_ROWS, t), BF16)
    return [jnp.concatenate([vt_ref[j, hh * dv:(hh + 1) * dv, :], ones], axis=0) for hh in range(g)]

  def k_tile(j):
    off = pl.multiple_of(j * t, t)
    return [kx_ref[pl.ds(off, t), hh * 128:(hh + 1) * 128] for hh in range(g)]

  def softmax_stage(c, m_prev, s_list):
    m_new = m_prev
    for s in s_list:
      m_new = jnp.maximum(m_new, jnp.max(s, axis=0, keepdims=True))
    m_ref[c] = m_new
    alpha_ref[c] = jnp.exp2(m_prev - m_new)
    return [jnp.exp2(s - m_new).astype(BF16) for s in s_list]

  def step(j_cur, j_prev, biases):
    ks, vts = k_tile(j_cur), v_tile(j_prev)
    pv, st = [], []
    for c in range(nc):
      pv.append(jnp.dot(vts[c // 2], pt_ref[c], preferred_element_type=F32))
      s = _dot_nt(ks[c // 2], qs[c])
      st.append(s if biases is None else s + biases[c // 2])
    for c in range(nc):
      acc_ref[c] = alpha_ref[c] * acc_ref[c] + pv[c]
    for c in range(nc):
      pt_ref[c], = softmax_stage(c, m_ref[c], [st[c]])

  n_far = jnp.maximum(qi - 1, 0)

  def far_body(j, carry):
    step(2 * j, jnp.maximum(2 * j - 1, 0), None)
    step(2 * j + 1, 2 * j, None)
    return carry

  lax.fori_loop(0, n_far // 2, far_body, 0)

  @pl.when(n_far % 2 == 1)
  def _():
    step(n_far - 1, jnp.maximum(n_far - 2, 0), None)

  @pl.when(qi >= 1)
  def _():
    step(qi - 1, jnp.maximum(qi - 2, 0), [bs_ref[hh] for hh in range(g)])

  vts = v_tile(jnp.maximum(qi - 1, 0))
  ks = k_tile(qi)
  pv, st, sm = [], [], []
  for c in range(nc):
    hh = c // 2
    pv.append(jnp.dot(vts[hh], pt_ref[c], preferred_element_type=F32))
    st.append(_dot_nt(ks[hh], qs[c]) + bd_ref[hh])
    sm.append(_dot_nt(km_ref[:, hh * 128:(hh + 1) * 128], qs[c]) + jnp.where(qi == 0, bm_ref[hh], 0.0))
  for c in range(nc):
    acc_ref[c] = alpha_ref[c] * acc_ref[c] + pv[c]
  pts = [softmax_stage(c, m_ref[c], [st[c], sm[c]]) for c in range(nc)]
  vts = v_tile(qi)
  ones_m = jnp.ones((ONES_ROWS, N_META), BF16)
  vtm = [jnp.concatenate([vmt_ref[0, hh * dv:(hh + 1) * dv, :], ones_m], axis=0) for hh in range(g)]
  lp = lam_ref[...]
  lam = (jnp.exp(jnp.sum(lp[0:1] * lp[1:2], axis=1, keepdims=True))
         - jnp.exp(jnp.sum(lp[2:3] * lp[3:4], axis=1, keepdims=True)) + LAM_INIT)
  for hh in range(g):
    a = []
    for c in (2 * hh, 2 * hh + 1):
      pv = (jnp.dot(vts[hh], pts[c][0], preferred_element_type=F32)
            + jnp.dot(vtm[hh], pts[c][1], preferred_element_type=F32))
      a.append(alpha_ref[c] * acc_ref[c] + pv)
    a0, a1 = a
    ot = a0[:dv] / a0[dv:dv + 1] - lam * (a1[:dv] / a1[dv:dv + 1])
    ot = ot * lax.rsqrt(jnp.mean(ot * ot, axis=0, keepdims=True) + SUBLN_EPS)
    o = ot.T * (sg_ref[...] * (1.0 - LAM_INIT))
    gate = ga_ref[:, hh * dv:(hh + 1) * dv].astype(F32)
    o_ref[:, hh * dv:(hh + 1) * dv] = (o * (gate / (1.0 + jnp.exp(-gate)))).astype(o_ref.dtype)


def _attention(z_x, vt_x, z_m, vt_m, bias_d, bias_s, bias_m, lam_p, subln_g):
  b, s, _ = z_x.shape
  t = ATT_T
  g = ATT_G
  w = 128 * g
  hb = A_HEADS // g
  kb, gb = Z_K // w, Z_GA // w
  return pl.pallas_call(
      _attn_kernel,
      name="diff_attn",
      grid=(b, hb, s // t),
      in_specs=[
          pl.BlockSpec((None, t, w), lambda bi, hi, qi: (bi, qi, hi)),
          pl.BlockSpec((None, s, w), lambda bi, hi, qi: (bi, 0, kb + hi), pipeline_mode=pl.Buffered(1)),
          pl.BlockSpec((None, s // t, w, t), lambda bi, hi, qi: (bi, 0, hi, 0), pipeline_mode=pl.Buffered(1)),
          pl.BlockSpec((N_META, w), lambda bi, hi, qi: (0, kb + hi)),
          pl.BlockSpec((1, w, N_META), lambda bi, hi, qi: (0, hi, 0)),
          pl.BlockSpec((None, t, w), lambda bi, hi, qi: (bi, qi, gb + hi)),
          pl.BlockSpec((g, t, t), lambda bi, hi, qi: (hi, 0, 0)),
          pl.BlockSpec((g, t, t), lambda bi, hi, qi: (hi, 0, 0)),
          pl.BlockSpec((g, N_META, t), lambda bi, hi, qi: (hi, 0, 0)),
          pl.BlockSpec((4, A_QK_DIM), lambda bi, hi, qi: (0, 0)),
          pl.BlockSpec((1, A_V_DIM), lambda bi, hi, qi: (0, 0)),
      ],
      out_specs=pl.BlockSpec((None, t, w), lambda bi, hi, qi: (bi, qi, hi)),
      out_shape=jax.ShapeDtypeStruct((b, s, A_WIDTH), BF16),
      scratch_shapes=[
          pltpu.VMEM((2 * g, 1, t), F32),
          pltpu.VMEM((2 * g, 1, t), F32),
          pltpu.VMEM((2 * g, A_V_DIM + ONES_ROWS, t), F32),
          pltpu.VMEM((2 * g, t, t), BF16),
      ],
      compiler_params=pltpu.CompilerParams(
          dimension_semantics=("parallel", "parallel", "arbitrary"),
          vmem_limit_bytes=VMEM_LIMIT),
  )(z_x, z_x, vt_x.reshape(b, s // t, A_WIDTH, t), z_m, vt_m, z_x, bias_d, bias_s, bias_m, lam_p, subln_g)


def _seg_sum(x):
  lane = lax.broadcasted_iota(jnp.int32, x.shape, 1)
  first = lane < R_HEAD
  lo = jnp.sum(jnp.where(first, x, 0.0), axis=1, keepdims=True)
  hi = jnp.sum(jnp.where(first, 0.0, x), axis=1, keepdims=True)
  return jnp.where(first, lo, hi)


def _split_bf16(x):
  hi = x.astype(BF16)
  return hi, x - hi.astype(F32)


def _dot_tn(a, b):
  return lax.dot_general(a, b, (((0,), (0,)), ((), ())), preferred_element_type=F32)


def _bdot(a, b):
  return lax.dot_general(a, b, (((2,), (1,)), ((0,), (0,))), preferred_element_type=F32)


def _bdot_nt(a, b):
  return lax.dot_general(a, b, (((2,), (2,)), ((0,), (0,))), preferred_element_type=F32)


def _rwkv_kernel(rx_ref, kx_ref, vx_ref, lx_ref, gr_ref, rm_ref, kmt_ref, vmt_ref, lm_ref,
                 pv_ref, mul_ref, wuph_ref, wupl_ref, aup_ref, o_ref, s_ref, prev_ref, prevl_ref):
  tb, c = RW_TB, RW_C
  nh = R_HEAD
  c2 = 2 * c
  ti = pl.program_id(2)
  is_meta = ti == 0

  @pl.when(is_meta)
  def _():
    s_ref[...] = jnp.zeros_like(s_ref)
    prev_ref[...] = jnp.zeros_like(prev_ref)
    prevl_ref[...] = jnp.zeros_like(prevl_ref)

  row = lax.broadcasted_iota(jnp.int32, (tb, 128), 0)
  rowl = lax.broadcasted_iota(jnp.int32, (tb, LORA_PAD), 0)

  def shifted(z, prev, mu, rows):
    z_prev = jnp.where(rows == 0, prev, pltpu.roll(z, 1, 0))
    return z + (z_prev - z) * mu

  z_l = jnp.where(is_meta, lm_ref[...], lx_ref[...])
  lo = shifted(z_l, prevl_ref[...], mul_ref[...], rowl)
  prevl_ref[...] = z_l[tb - 1:tb]
  th_h, th_l = _split_bf16(jnp.tanh(lo))
  w_lora = (jnp.dot(th_h, wuph_ref[...], preferred_element_type=F32)
            + jnp.dot(th_h, wupl_ref[...], preferred_element_type=F32)
            + jnp.dot(th_l.astype(BF16), wuph_ref[...], preferred_element_type=F32))
  a_lora = jnp.dot(lo.astype(BF16), aup_ref[...], preferred_element_type=F32)

  ii = lax.broadcasted_iota(jnp.int32, (tb, tb), 0)
  jj = lax.broadcasted_iota(jnp.int32, (tb, tb), 1)
  shift = int(math.log2(c))
  same = lax.shift_right_logical(ii, shift) == lax.shift_right_logical(jj, shift)
  cum_op = jnp.where(same, jnp.where(jj <= ii, 1.0, 0.0), 0.0).astype(BF16)

  ci = lax.broadcasted_iota(jnp.int32, (c2, c2), 0)
  cj = lax.broadcasted_iota(jnp.int32, (c2, c2), 1)
  diag = ci == cj
  strict2 = jnp.concatenate([cj < ci, cj < ci], axis=1)
  incl2 = jnp.concatenate([cj <= ci, cj <= ci], axis=1)
  first = lax.broadcasted_iota(jnp.int32, (c, 128), 1) < nh

  def stack(x):
    return jnp.concatenate([jnp.where(first, x, 0.0), jnp.where(first, 0.0, x)], axis=0)

  ncc = tb // c
  chains = {name: [] for name in ("at", "rt", "bt", "kt", "bh", "kh", "vv", "gd")}
  post = []
  for p in range(RW_P):
    ls = slice(p * 128, (p + 1) * 128)
    pv = pv_ref[:, ls]
    mu_r, mu_k, mu_v = pv[0:1], pv[1:2], pv[2:3]
    w0, a0, k_k, k_a, r_k, gn_g, gn_b = pv[3:4], pv[4:5], pv[5:6], pv[6:7], pv[7:8], pv[8:9], pv[9:10]

    z_r = jnp.where(is_meta, rm_ref[:, ls], rx_ref[:, ls]).astype(F32)
    z_k = jnp.where(is_meta, kmt_ref[:, ls], kx_ref[:, ls]).astype(F32)
    z_v = jnp.where(is_meta, vmt_ref[:, ls], vx_ref[:, ls]).astype(F32)
    r = shifted(z_r, prev_ref[0:1, ls], mu_r, row)
    k = shifted(z_k, prev_ref[1:2, ls], mu_k, row)
    v = shifted(z_v, prev_ref[2:3, ls], mu_v, row)
    prev_ref[0:1, ls] = z_r[tb - 1:tb]
    prev_ref[1:2, ls] = z_k[tb - 1:tb]
    prev_ref[2:3, ls] = z_v[tb - 1:tb]

    u = -(w0 + w_lora[:, ls])
    softplus = jnp.maximum(u, 0.0) + jnp.log(1.0 + jnp.exp(-jnp.abs(u)))
    logw = -jnp.exp(-softplus - 0.5) * LOG2E
    a = 1.0 / (1.0 + jnp.exp(-(a0 + a_lora[:, ls])))
    kk = k * k_k
    kk = kk / jnp.maximum(jnp.sqrt(_seg_sum(kk * kk)), 1e-12)
    k_mod = k * (1.0 + (a - 1.0) * k_a)
    bonus = _seg_sum(r * k_mod * r_k) * v

    lw_h, lw_r = _split_bf16(logw)
    lw_m, lw_l = _split_bf16(lw_r)
    cum3 = jnp.dot(cum_op, jnp.concatenate([lw_h, lw_m, lw_l.astype(BF16)], axis=1),
                   preferred_element_type=F32)
    cum = cum3[:, :128] + cum3[:, 128:256] + cum3[:, 256:]
    tot = jnp.concatenate([jnp.broadcast_to(cum[cc * c + c - 1:cc * c + c], (c, 128)) for cc in range(ncc)],
                          axis=0)
    p_inv = jnp.exp2(-cum)
    a_t = -kk * jnp.exp2(cum - logw)
    kka = kk * a
    b_t = kka * p_inv
    k_t = k_mod * p_inv
    r_t = r * jnp.exp2(cum)
    p_end = jnp.exp2(tot - cum)
    b_h = kka * p_end
    k_h = k_mod * p_end
    g_diag = jnp.exp2(tot)

    for cc in range(ncc):
      rs = slice(cc * c, (cc + 1) * c)
      for name, val in (("at", a_t), ("rt", r_t), ("bt", b_t), ("kt", k_t), ("bh", b_h), ("kh", k_h),
                        ("vv", v)):
        chains[name].append(stack(val[rs]))
      chains["gd"].append(g_diag[cc * c:cc * c + 1])
    post.append((bonus, gn_g, gn_b))

  nb = RW_P * ncc
  at, rt, bt, kt, bh, kh, vv = (jnp.stack(chains[name]) for name in ("at", "rt", "bt", "kt", "bh", "kh", "vv"))
  at_b, vv_b, bh_b = at.astype(BF16), vv.astype(BF16), bh.astype(BF16)
  bk = jnp.concatenate([bt, kt], axis=1).astype(BF16)
  top = jnp.where(strict2, _bdot_nt(at_b, bk), 0.0)
  lblk = jnp.where(incl2, _bdot_nt(rt.astype(BF16), bk), 0.0)
  nm, mak = top[:, :, :c2], top[:, :, c2:]
  tinv = jnp.where(diag, 1.0, nm)
  npow = nm.astype(BF16)
  for _ in range(5):
    npow = _bdot(npow, npow).astype(BF16)
    tinv = tinv + _bdot(tinv.astype(BF16), npow)
  x1 = _bdot(mak.astype(BF16), vv_b)
  wu_b = _bdot(tinv.astype(BF16), jnp.concatenate([at_b, x1.astype(BF16)], axis=2)).astype(BF16)
  rhs = jnp.concatenate([wu_b, jnp.concatenate([jnp.zeros_like(vv_b), vv_b], axis=2)], axis=1)
  qy = _bdot(lblk.astype(BF16), rhs)
  q_h = (rt + qy[:, :, :c2]).astype(BF16)
  y0 = qy[:, :, c2:]
  uv = jnp.concatenate([wu_b[:, :, c2:], vv_b], axis=1)
  bkh = jnp.concatenate([bh_b, kh.astype(BF16)], axis=1)
  g_m = [(jnp.where(diag, chains["gd"][n], 0.0) + _dot_tn(wu_b[n, :, :c2], bh_b[n])).astype(BF16)
         for n in range(nb)]
  h_m = [_dot_tn(uv[n], bkh[n]) for n in range(nb)]

  states = [s_ref[p] for p in range(RW_P)]
  y_rows = [[] for _ in range(RW_P)]
  for cc in range(ncc):
    for p in range(RW_P):
      n = p * ncc + cc
      s_old_b = states[p].astype(BF16)
      y2 = _dot_nt(q_h[n], s_old_b) + y0[n]
      states[p] = jnp.dot(s_old_b, g_m[n], preferred_element_type=F32) + h_m[n]
      y_rows[p].append(y2[:c] + y2[c:])

  for p in range(RW_P):
    ls = slice(p * 128, (p + 1) * 128)
    s_ref[p] = states[p]
    bonus, gn_g, gn_b = post[p]
    y = jnp.concatenate(y_rows[p], axis=0)
    mean = _seg_sum(y) * (1.0 / nh)
    yc = y - mean
    var = _seg_sum(yc * yc) * (1.0 / nh)
    yn = yc * lax.rsqrt(var + GN_EPS) * gn_g + gn_b
    g = gr_ref[:, ls].astype(F32)
    o_ref[:, ls] = ((yn + bonus) * (g / (1.0 + jnp.exp(-g)))).astype(o_ref.dtype)


def _rwkv(z_x, lo_x, z_mp, lo_mp, pvec, mu_l, wup_h, wup_l, aup):
  b, s, _ = z_x.shape
  tb = RW_TB
  nt = s // tb + 1
  pw = 128 * RW_P

  def xmap(col):
    return lambda bi, hp, ti: (bi, jnp.maximum(ti - 1, 0), col // pw + hp)

  def mmap(col):
    return lambda bi, hp, ti: (0, col // pw + hp)

  return pl.pallas_call(
      _rwkv_kernel,
      name="rwkv7",
      grid=(b, R_PAIRS // RW_P, nt),
      in_specs=[
          pl.BlockSpec((None, tb, pw), xmap(Z_RR)),
          pl.BlockSpec((None, tb, pw), xmap(Z_RK)),
          pl.BlockSpec((None, tb, pw), xmap(Z_RV)),
          pl.BlockSpec((None, tb, LORA_PAD), lambda bi, hp, ti: (bi, jnp.maximum(ti - 1, 0), 0)),
          pl.BlockSpec((None, tb, pw), xmap(Z_GR)),
          pl.BlockSpec((tb, pw), mmap(Z_RR)),
          pl.BlockSpec((tb, pw), mmap(Z_RK)),
          pl.BlockSpec((tb, pw), mmap(Z_RV)),
          pl.BlockSpec((tb, LORA_PAD), lambda bi, hp, ti: (0, 0)),
          pl.BlockSpec((16, pw), lambda bi, hp, ti: (0, hp)),
          pl.BlockSpec((1, LORA_PAD), lambda bi, hp, ti: (0, 0)),
          pl.BlockSpec((LORA_PAD, pw), lambda bi, hp, ti: (0, hp)),
          pl.BlockSpec((LORA_PAD, pw), lambda bi, hp, ti: (0, hp)),
          pl.BlockSpec((LORA_PAD, pw), lambda bi, hp, ti: (0, hp)),
      ],
      out_specs=pl.BlockSpec((None, tb, pw), lambda bi, hp, ti: (bi, jnp.maximum(ti - 1, 0), hp)),
      out_shape=jax.ShapeDtypeStruct((b, s, R_WIDTH), BF16),
      scratch_shapes=[
          pltpu.VMEM((RW_P, 2 * R_HEAD, 2 * R_HEAD), F32),
          pltpu.VMEM((8, pw), F32),
          pltpu.VMEM((1, LORA_PAD), F32),
      ],
      compiler_params=pltpu.CompilerParams(
          dimension_semantics=("parallel", "parallel", "arbitrary"),
          vmem_limit_bytes=VMEM_LIMIT),
  )(z_x, z_x, z_x, lo_x, z_x, z_mp, z_mp, z_mp, lo_mp, pvec, mu_l, wup_h, wup_l, aup)


def _out_kernel(x_ref, oa_ref, or_ref, wa_ref, wr_ref, ge_ref, be_ref, gp_ref, bp_ref, o_ref):
  h = _ln_rows(x_ref[...], ge_ref[...], be_ref[...])
  y = (jnp.dot(oa_ref[...], wa_ref[...], preferred_element_type=F32)
       + jnp.dot(or_ref[...], wr_ref[...], preferred_element_type=F32))
  o_ref[...] = _ln_rows(DEEPNORM_ALPHA * h + y, gp_ref[...], bp_ref[...])


def _out_proj(x2d, oa, orw, wa, wr, ge, be, gp, bp, tm):
  m, d = x2d.shape
  vec = pl.BlockSpec((1, d), lambda i: (0, 0))
  return pl.pallas_call(
      _out_kernel,
      name="out_proj",
      grid=(m // tm,),
      in_specs=[
          pl.BlockSpec((tm, d), lambda i: (i, 0)),
          pl.BlockSpec((tm, A_WIDTH), lambda i: (i, 0)),
          pl.BlockSpec((tm, R_WIDTH), lambda i: (i, 0)),
          pl.BlockSpec((A_WIDTH, d), lambda i: (0, 0), pipeline_mode=pl.Buffered(1)),
          pl.BlockSpec((R_WIDTH, d), lambda i: (0, 0), pipeline_mode=pl.Buffered(1)),
          vec, vec, vec, vec,
      ],
      out_specs=pl.BlockSpec((tm, d), lambda i: (i, 0)),
      out_shape=jax.ShapeDtypeStruct((m, d), F32),
      compiler_params=pltpu.CompilerParams(
          dimension_semantics=("parallel",),
          vmem_limit_bytes=VMEM_LIMIT),
  )(x2d, oa, orw, wa, wr, ge, be, gp, bp)


def kernel(x, meta_tokens, ln_emb_g, ln_emb_b, rel_bias, w_in, w_out, lambda_q1, lambda_k1, lambda_q2,
           lambda_k2, subln_g, rw_mu, rw_w0, rw_w_up, rw_a0, rw_a_up, rw_k_k, rw_k_a, rw_r_k, rw_gn_g,
           rw_gn_b, ln_post_g, ln_post_b):
  b, s, d = x.shape
  l = 0
  wi = w_in[l]
  c_lo = 4 * A_WIDTH + 3 * R_WIDTH
  c_gr = c_lo + DECAY_LORA + ICLR_LORA
  lora_pad = LORA_PAD - DECAY_LORA - ICLR_LORA
  c_v = 2 * A_WIDTH
  w_all = wi.astype(BF16)
  w_gr = wi[:, c_gr:].astype(BF16)
  w_vt = wi[:, c_v:c_v + A_WIDTH].T.astype(BF16)
  w_lora = jnp.pad(wi[:, c_lo:c_gr], ((0, 0), (0, lora_pad))).astype(BF16)

  ge, be = ln_emb_g.reshape(1, d), ln_emb_b.reshape(1, d)
  x2d = x.reshape(b * s, d)
  z_x, vt_x, lo_x = _ln_matmul(x2d, ge, be, w_all, w_gr, w_vt, w_lora, 1024)
  z_x, lo_x = z_x.reshape(b, s, -1), lo_x.reshape(b, s, LORA_PAD)
  z_m, vt_m, lo_m = _ln_matmul(meta_tokens, ge, be, w_all, w_gr, w_vt, w_lora, N_META)

  bias_d, bias_s, bias_m = _bias_tiles(rel_bias)
  lam_p = jnp.stack([lambda_q1[l], lambda_k1[l], lambda_q2[l], lambda_k2[l]], axis=0)
  o_attn = _attention(z_x, vt_x, z_m, vt_m, bias_d, bias_s, bias_m, lam_p, subln_g[l].reshape(1, A_V_DIM))

  mu = rw_mu[l]
  zeros = jnp.zeros((R_WIDTH,), F32)
  pvec = jnp.stack([mu[:R_WIDTH], mu[R_WIDTH:2 * R_WIDTH], mu[2 * R_WIDTH:3 * R_WIDTH], rw_w0[l], rw_a0[l],
                    rw_k_k[l], rw_k_a[l], rw_r_k[l].reshape(R_WIDTH), rw_gn_g[l], rw_gn_b[l]]
                   + [zeros] * 6, axis=0)
  mu_l = jnp.pad(mu[3 * R_WIDTH:], (0, lora_pad)).reshape(1, LORA_PAD)
  wup = jnp.pad(rw_w_up[l], ((0, LORA_PAD - DECAY_LORA), (0, 0)))
  wup_h = wup.astype(BF16)
  wup_l = (wup - wup_h.astype(F32)).astype(BF16)
  aup = jnp.pad(rw_a_up[l], ((DECAY_LORA, lora_pad), (0, 0))).astype(BF16)
  front = ((RW_TB - N_META, 0), (0, 0))
  o_rwkv = _rwkv(z_x, lo_x, jnp.pad(z_m, front), jnp.pad(lo_m, front), pvec, mu_l, wup_h, wup_l, aup)

  wo = w_out[l].astype(BF16)
  out = _out_proj(x2d, o_attn.reshape(b * s, A_WIDTH), o_rwkv.reshape(b * s, R_WIDTH),
                  wo[:A_WIDTH], wo[A_WIDTH:], ge, be,
                  ln_post_g[l].reshape(1, d), ln_post_b[l].reshape(1, d), 512)
  return out.reshape(b, s, d)
```

```python
import math

import numpy as np
import jax
import jax.numpy as jnp
from jax import lax
from jax.experimental import pallas as pl
from jax.experimental.pallas import tpu as pltpu

N_META = 16
A_HEADS = 8
A_V_DIM = 128
A_QK_DIM = 64
A_WIDTH = A_HEADS * A_V_DIM
R_HEAD = 64
R_WIDTH = 1024
R_PAIR = 2 * R_HEAD
R_PAIRS = R_WIDTH // R_PAIR
A_QK_W = 2 * A_QK_DIM
DECAY_LORA = 96
ICLR_LORA = 96
LORA_PAD = 256
N_BUCKETS = 32
MAX_DISTANCE = 128
LN_EPS = 1e-5
SUBLN_EPS = 1e-5
GN_EPS = 64e-5
DEPTH = 1
DEEPNORM_ALPHA = (2 * DEPTH) ** 0.25
LAM_INIT = 0.8 - 0.6 * math.exp(-0.3 * 0)
NEG = -1e30

ATT_T = 256
ATT_G = 8
ONES_ROWS = 16
IN_TN = 1024
Z_Q, Z_K, Z_GA, Z_RR, Z_RK, Z_RV, Z_GR = (i * 1024 for i in range(7))
LOG2E = math.log2(math.e)
Q_SCALE = A_QK_DIM ** -0.5 * LOG2E
RW_TB = 128
RW_C = 64
RW_P = 8
VMEM_LIMIT = 56 * 1024 * 1024

F32 = jnp.float32
BF16 = jnp.bfloat16


def _ln_rows(x, g, b):
  mu = jnp.mean(x, axis=-1, keepdims=True)
  xc = x - mu
  var = jnp.mean(xc * xc, axis=-1, keepdims=True)
  return xc * lax.rsqrt(var + LN_EPS) * g + b


def _ln_mm_kernel(x_ref, g_ref, b_ref, wm_ref, wgr_ref, wvt_ref, wl_ref, om_ref, ovt_ref, ol_ref, hn_ref):
  j = pl.program_id(1)
  n_main = pl.num_programs(1) - 1

  @pl.when(j == 0)
  def _():
    hn_ref[...] = _ln_rows(x_ref[...], g_ref[...], b_ref[...]).astype(BF16)

  @pl.when(j < n_main - 1)
  def _():
    scale = jnp.where(j == Z_Q // IN_TN, Q_SCALE, 1.0)
    z = jnp.dot(hn_ref[...], wm_ref[...], preferred_element_type=F32)
    om_ref[...] = (z * scale).astype(om_ref.dtype)

  @pl.when(j == n_main - 1)
  def _():
    om_ref[...] = jnp.dot(hn_ref[...], wgr_ref[...], preferred_element_type=F32).astype(om_ref.dtype)
    ol_ref[...] = jnp.dot(hn_ref[...], wl_ref[...], preferred_element_type=F32)

  @pl.when(j == n_main)
  def _():
    zt = _dot_nt(wvt_ref[...], hn_ref[...])
    tv = ovt_ref.shape[2]
    for c in range(ovt_ref.shape[0]):
      ovt_ref[c] = zt[:, c * tv:(c + 1) * tv].astype(ovt_ref.dtype)


def _ln_matmul(x2d, g, b, w_all, w_gr, w_vt, w_lora, tm):
  m, d = x2d.shape
  tn = IN_TN
  nj = Z_GR // tn + 1
  n = nj * tn
  last = nj - 1
  tv = min(ATT_T, tm)
  v_tile = 2 * A_WIDTH // tn
  once = dict(pipeline_mode=pl.Buffered(1))

  def w_map(i, j):
    jj = jnp.minimum(j, last - 1)
    return (0, jnp.where(jj >= v_tile, jj + 1, jj))

  return pl.pallas_call(
      _ln_mm_kernel,
      name="ln_inproj",
      grid=(m // tm, nj + 1),
      in_specs=[
          pl.BlockSpec((tm, d), lambda i, j: (i, 0)),
          pl.BlockSpec((1, d), lambda i, j: (0, 0)),
          pl.BlockSpec((1, d), lambda i, j: (0, 0)),
          pl.BlockSpec((d, tn), w_map),
          pl.BlockSpec((d, tn), lambda i, j: (0, 0), **once),
          pl.BlockSpec((A_WIDTH, d), lambda i, j: (0, 0), **once),
          pl.BlockSpec((d, LORA_PAD), lambda i, j: (0, 0), **once),
      ],
      out_specs=[
          pl.BlockSpec((tm, tn), lambda i, j: (i, jnp.minimum(j, last))),
          pl.BlockSpec((tm // tv, A_WIDTH, tv), lambda i, j: (i, 0, 0)),
          pl.BlockSpec((tm, LORA_PAD), lambda i, j: (i, 0)),
      ],
      out_shape=[
          jax.ShapeDtypeStruct((m, n), BF16),
          jax.ShapeDtypeStruct((m // tv, A_WIDTH, tv), BF16),
          jax.ShapeDtypeStruct((m, LORA_PAD), F32),
      ],
      scratch_shapes=[pltpu.VMEM((tm, d), BF16)],
      compiler_params=pltpu.CompilerParams(
          dimension_semantics=("parallel", "arbitrary"),
          vmem_limit_bytes=VMEM_LIMIT),
  )(x2d, g, b, w_all, w_gr, w_vt, w_lora)


def _bucket_thresholds():
  n = np.arange(0, 4 * MAX_DISTANCE, dtype=np.int64)
  max_exact = N_BUCKETS // 2
  nf = np.maximum(n, 1).astype(np.float32)
  large = max_exact + (np.log(nf / np.float32(max_exact)) / np.float32(math.log(MAX_DISTANCE / max_exact))
                       * np.float32(N_BUCKETS - max_exact)).astype(np.int32)
  large = np.minimum(large, N_BUCKETS - 1)
  bucket = np.where(n < max_exact, n, large)
  assert np.all(np.diff(bucket) >= 0) and bucket[-1] == N_BUCKETS - 1
  return [int(np.argmax(bucket >= b)) for b in range(N_BUCKETS)]


_THR = _bucket_thresholds()


def _bias_kernel(rb_ref, diag_ref, sub_ref, meta_ref):
  h = pl.program_id(0)
  far = rb_ref[N_BUCKETS - 1, h]

  def bias_of(n):
    out = jnp.full(n.shape, (rb_ref[0, h] - far) * LOG2E, F32)
    for b in range(1, N_BUCKETS):
      out = jnp.where(n >= _THR[b], (rb_ref[b, h] - far) * LOG2E, out)
    return out

  t = ATT_T
  kj = lax.broadcasted_iota(jnp.int32, (t, t), 0)
  qi = lax.broadcasted_iota(jnp.int32, (t, t), 1)
  d = qi - kj
  diag_ref[...] = jnp.where(d >= 0, bias_of(d), NEG)
  sub_ref[...] = bias_of(d + t)
  km = lax.broadcasted_iota(jnp.int32, (N_META, t), 0)
  qm = lax.broadcasted_iota(jnp.int32, (N_META, t), 1)
  meta_ref[...] = bias_of(qm - km + N_META)


def _bias_tiles(rel_bias):
  t = ATT_T
  return pl.pallas_call(
      _bias_kernel,
      name="bias_tiles",
      grid=(A_HEADS,),
      in_specs=[pl.BlockSpec(memory_space=pltpu.SMEM)],
      out_specs=[
          pl.BlockSpec((None, t, t), lambda h: (h, 0, 0)),
          pl.BlockSpec((None, t, t), lambda h: (h, 0, 0)),
          pl.BlockSpec((None, N_META, t), lambda h: (h, 0, 0)),
      ],
      out_shape=[
          jax.ShapeDtypeStruct((A_HEADS, t, t), F32),
          jax.ShapeDtypeStruct((A_HEADS, t, t), F32),
          jax.ShapeDtypeStruct((A_HEADS, N_META, t), F32),
      ],
  )(rel_bias)


def _dot_nt(a, b):
  return lax.dot_general(a, b, (((1,), (1,)), ((), ())), preferred_element_type=F32)


def _attn_kernel(q_ref, kx_ref, vt_ref, km_ref, vmt_ref, ga_ref, bd_ref, bs_ref, bm_ref,
                 lam_ref, sg_ref, o_ref, m_ref, alpha_ref, acc_ref, pt_ref):
  t = ATT_T
  g = ATT_G
  nc = 2 * g
  dv = A_V_DIM
  qi = pl.program_id(2)
  lane = lax.broadcasted_iota(jnp.int32, (t, A_QK_W), 1)
  qs = []
  for hh in range(g):
    q = q_ref[:, hh * A_QK_W:(hh + 1) * A_QK_W]
    zero = jnp.zeros_like(q)
    qs += [jnp.where(lane < A_QK_DIM, q, zero), jnp.where(lane >= A_QK_DIM, q, zero)]

  m_ref[...] = jnp.full(m_ref.shape, NEG, F32)
  acc_ref[...] = jnp.zeros(acc_ref.shape, F32)
  alpha_ref[...] = jnp.ones(alpha_ref.shape, F32)
  pt_ref[...] = jnp.zeros(pt_ref.shape, BF16)

  def v_tile(j):
    ones = jnp.ones((ONES_ROWS, t), BF16)
    return [jnp.concatenate([vt_ref[j, hh * dv:(hh + 1) * dv, :], ones], axis=0) for hh in range(g)]

  def k_tile(j):
    off = pl.multiple_of(j * t, t)
    return [kx_ref[pl.ds(off, t), hh * A_QK_W:(hh + 1) * A_QK_W] for hh in range(g)]

  def softmax_stage(c, m_prev, s_list):
    m_new = m_prev
    for s in s_list:
      m_new = jnp.maximum(m_new, jnp.max(s, axis=0, keepdims=True))
    m_ref[c] = m_new
    alpha_ref[c] = jnp.exp2(m_prev - m_new)
    return [jnp.exp2(s - m_new).astype(BF16) for s in s_list]

  def step(j_cur, j_prev, biases):
    ks, vts = k_tile(j_cur), v_tile(j_prev)
    pv, st = [], []
    for c in range(nc):
      pv.append(jnp.dot(vts[c // 2], pt_ref[c], preferred_element_type=F32))
      s = _dot_nt(ks[c // 2], qs[c])
      st.append(s if biases is None else s + biases[c // 2])
    for c in range(nc):
      acc_ref[c] = alpha_ref[c] * acc_ref[c] + pv[c]
    for c in range(nc):
      pt_ref[c], = softmax_stage(c, m_ref[c], [st[c]])

  n_far = jnp.maximum(qi - 1, 0)

  def far_body(j, carry):
    step(2 * j, jnp.maximum(2 * j - 1, 0), None)
    step(2 * j + 1, 2 * j, None)
    return carry

  lax.fori_loop(0, n_far // 2, far_body, 0)

  @pl.when(n_far % 2 == 1)
  def _():
    step(n_far - 1, jnp.maximum(n_far - 2, 0), None)

  @pl.when(qi >= 1)
  def _():
    step(qi - 1, jnp.maximum(qi - 2, 0), [bs_ref[hh] for hh in range(g)])

  vts = v_tile(jnp.maximum(qi - 1, 0))
  ks = k_tile(qi)
  pv, st, sm = [], [], []
  for c in range(nc):
    hh = c // 2
    pv.append(jnp.dot(vts[hh], pt_ref[c], preferred_element_type=F32))
    st.append(_dot_nt(ks[hh], qs[c]) + bd_ref[hh])
    sm.append(_dot_nt(km_ref[:, hh * A_QK_W:(hh + 1) * A_QK_W], qs[c]) + jnp.where(qi == 0, bm_ref[hh], 0.0))
  for c in range(nc):
    acc_ref[c] = alpha_ref[c] * acc_ref[c] + pv[c]
  pts = [softmax_stage(c, m_ref[c], [st[c], sm[c]]) for c in range(nc)]
  vts = v_tile(qi)
  ones_m = jnp.ones((ONES_ROWS, N_META), BF16)
  vtm = [jnp.concatenate([vmt_ref[0, hh * dv:(hh + 1) * dv, :], ones_m], axis=0) for hh in range(g)]
  lp = lam_ref[...]
  lam = (jnp.exp(jnp.sum(lp[0:1] * lp[1:2], axis=1, keepdims=True))
         - jnp.exp(jnp.sum(lp[2:3] * lp[3:4], axis=1, keepdims=True)) + LAM_INIT)
  for hh in range(g):
    a = []
    for c in (2 * hh, 2 * hh + 1):
      pv = (jnp.dot(vts[hh], pts[c][0], preferred_element_type=F32)
            + jnp.dot(vtm[hh], pts[c][1], preferred_element_type=F32))
      a.append(alpha_ref[c] * acc_ref[c] + pv)
    a0, a1 = a
    ot = a0[:dv] / a0[dv:dv + 1] - lam * (a1[:dv] / a1[dv:dv + 1])
    ot = ot * lax.rsqrt(jnp.mean(ot * ot, axis=0, keepdims=True) + SUBLN_EPS)
    o = ot.T * (sg_ref[...] * (1.0 - LAM_INIT))
    gate = ga_ref[:, hh * dv:(hh + 1) * dv].astype(F32)
    o_ref[:, hh * dv:(hh + 1) * dv] = (o * (gate / (1.0 + jnp.exp(-gate)))).astype(o_ref.dtype)


def _attention(z_x, vt_x, z_m, vt_m, bias_d, bias_s, bias_m, lam_p, subln_g):
  b, s, _ = z_x.shape
  t = ATT_T
  g = ATT_G
  w = A_QK_W * g
  hb = A_HEADS // g
  kb, gb = Z_K // w, Z_GA // w
  return pl.pallas_call(
      _attn_kernel,
      name="diff_attn",
      grid=(b, hb, s // t),
      in_specs=[
          pl.BlockSpec((None, t, w), lambda bi, hi, qi: (bi, qi, hi)),
          pl.BlockSpec((None, s, w), lambda bi, hi, qi: (bi, 0, kb + hi), pipeline_mode=pl.Buffered(1)),
          pl.BlockSpec((None, s // t, w, t), lambda bi, hi, qi: (bi, 0, hi, 0), pipeline_mode=pl.Buffered(1)),
          pl.BlockSpec((N_META, w), lambda bi, hi, qi: (0, kb + hi)),
          pl.BlockSpec((1, w, N_META), lambda bi, hi, qi: (0, hi, 0)),
          pl.BlockSpec((None, t, w), lambda bi, hi, qi: (bi, qi, gb + hi)),
          pl.BlockSpec((g, t, t), lambda bi, hi, qi: (hi, 0, 0)),
          pl.BlockSpec((g, t, t), lambda bi, hi, qi: (hi, 0, 0)),
          pl.BlockSpec((g, N_META, t), lambda bi, hi, qi: (hi, 0, 0)),
          pl.BlockSpec((4, A_QK_DIM), lambda bi, hi, qi: (0, 0)),
          pl.BlockSpec((1, A_V_DIM), lambda bi, hi, qi: (0, 0)),
      ],
      out_specs=pl.BlockSpec((None, t, w), lambda bi, hi, qi: (bi, qi, hi)),
      out_shape=jax.ShapeDtypeStruct((b, s, A_WIDTH), BF16),
      scratch_shapes=[
          pltpu.VMEM((2 * g, 1, t), F32),
          pltpu.VMEM((2 * g, 1, t), F32),
          pltpu.VMEM((2 * g, A_V_DIM + ONES_ROWS, t), F32),
          pltpu.VMEM((2 * g, t, t), BF16),
      ],
      compiler_params=pltpu.CompilerParams(
          dimension_semantics=("parallel", "parallel", "arbitrary"),
          vmem_limit_bytes=VMEM_LIMIT),
  )(z_x, z_x, vt_x.reshape(b, s // t, A_WIDTH, t), z_m, vt_m, z_x, bias_d, bias_s, bias_m, lam_p, subln_g)


def _seg_sum(x):
  lane = lax.broadcasted_iota(jnp.int32, x.shape, 1)
  first = lane < R_HEAD
  lo = jnp.sum(jnp.where(first, x, 0.0), axis=1, keepdims=True)
  hi = jnp.sum(jnp.where(first, 0.0, x), axis=1, keepdims=True)
  return jnp.where(first, lo, hi)


def _split_bf16(x):
  hi = x.astype(BF16)
  return hi, x - hi.astype(F32)


def _dot_tn(a, b):
  return lax.dot_general(a, b, (((0,), (0,)), ((), ())), preferred_element_type=F32)


def _bdot(a, b):
  return lax.dot_general(a, b, (((2,), (1,)), ((0,), (0,))), preferred_element_type=F32)


def _bdot_nt(a, b):
  return lax.dot_general(a, b, (((2,), (2,)), ((0,), (0,))), preferred_element_type=F32)


def _rwkv_kernel(rx_ref, kx_ref, vx_ref, lx_ref, gr_ref, rm_ref, kmt_ref, vmt_ref, lm_ref,
                 pv_ref, mul_ref, wuph_ref, wupl_ref, aup_ref, o_ref, s_ref, prev_ref, prevl_ref):
  tb, c = RW_TB, RW_C
  nh = R_HEAD
  c2 = 2 * c
  ti = pl.program_id(2)
  is_meta = ti == 0

  @pl.when(is_meta)
  def _():
    s_ref[...] = jnp.zeros_like(s_ref)
    prev_ref[...] = jnp.zeros_like(prev_ref)
    prevl_ref[...] = jnp.zeros_like(prevl_ref)

  row = lax.broadcasted_iota(jnp.int32, (tb, R_PAIR), 0)
  rowl = lax.broadcasted_iota(jnp.int32, (tb, LORA_PAD), 0)

  def shifted(z, prev, mu, rows):
    z_prev = jnp.where(rows == 0, prev, pltpu.roll(z, 1, 0))
    return z + (z_prev - z) * mu

  z_l = jnp.where(is_meta, lm_ref[...], lx_ref[...])
  lo = shifted(z_l, prevl_ref[...], mul_ref[...], rowl)
  prevl_ref[...] = z_l[tb - 1:tb]
  th_h, th_l = _split_bf16(jnp.tanh(lo))
  w_lora = (jnp.dot(th_h, wuph_ref[...], preferred_element_type=F32)
            + jnp.dot(th_h, wupl_ref[...], preferred_element_type=F32)
            + jnp.dot(th_l.astype(BF16), wuph_ref[...], preferred_element_type=F32))
  a_lora = jnp.dot(lo.astype(BF16), aup_ref[...], preferred_element_type=F32)

  ii = lax.broadcasted_iota(jnp.int32, (tb, tb), 0)
  jj = lax.broadcasted_iota(jnp.int32, (tb, tb), 1)
  shift = int(math.log2(c))
  same = lax.shift_right_logical(ii, shift) == lax.shift_right_logical(jj, shift)
  cum_op = jnp.where(same, jnp.where(jj <= ii, 1.0, 0.0), 0.0).astype(BF16)

  ci = lax.broadcasted_iota(jnp.int32, (c2, c2), 0)
  cj = lax.broadcasted_iota(jnp.int32, (c2, c2), 1)
  diag = ci == cj
  strict2 = jnp.concatenate([cj < ci, cj < ci], axis=1)
  incl2 = jnp.concatenate([cj <= ci, cj <= ci], axis=1)
  first = lax.broadcasted_iota(jnp.int32, (c, R_PAIR), 1) < nh

  def stack(x):
    return jnp.concatenate([jnp.where(first, x, 0.0), jnp.where(first, 0.0, x)], axis=0)

  ncc = tb // c
  chains = {name: [] for name in ("at", "rt", "bt", "kt", "bh", "kh", "vv", "gd")}
  post = []
  for p in range(RW_P):
    ls = slice(p * R_PAIR, (p + 1) * R_PAIR)
    pv = pv_ref[:, ls]
    mu_r, mu_k, mu_v = pv[0:1], pv[1:2], pv[2:3]
    w0, a0, k_k, k_a, r_k, gn_g, gn_b = pv[3:4], pv[4:5], pv[5:6], pv[6:7], pv[7:8], pv[8:9], pv[9:10]

    z_r = jnp.where(is_meta, rm_ref[:, ls], rx_ref[:, ls]).astype(F32)
    z_k = jnp.where(is_meta, kmt_ref[:, ls], kx_ref[:, ls]).astype(F32)
    z_v = jnp.where(is_meta, vmt_ref[:, ls], vx_ref[:, ls]).astype(F32)
    r = shifted(z_r, prev_ref[0:1, ls], mu_r, row)
    k = shifted(z_k, prev_ref[1:2, ls], mu_k, row)
    v = shifted(z_v, prev_ref[2:3, ls], mu_v, row)
    prev_ref[0:1, ls] = z_r[tb - 1:tb]
    prev_ref[1:2, ls] = z_k[tb - 1:tb]
    prev_ref[2:3, ls] = z_v[tb - 1:tb]

    u = -(w0 + w_lora[:, ls])
    softplus = jnp.maximum(u, 0.0) + jnp.log(1.0 + jnp.exp(-jnp.abs(u)))
    logw = -jnp.exp(-softplus - 0.5) * LOG2E
    a = 1.0 / (1.0 + jnp.exp(-(a0 + a_lora[:, ls])))
    kk = k * k_k
    kk = kk / jnp.maximum(jnp.sqrt(_seg_sum(kk * kk)), 1e-12)
    k_mod = k * (1.0 + (a - 1.0) * k_a)
    bonus = _seg_sum(r * k_mod * r_k) * v

    lw_h, lw_r = _split_bf16(logw)
    lw_m, lw_l = _split_bf16(lw_r)
    cum3 = jnp.dot(cum_op, jnp.concatenate([lw_h, lw_m, lw_l.astype(BF16)], axis=1),
                   preferred_element_type=F32)
    cum = cum3[:, :R_PAIR] + cum3[:, R_PAIR:2 * R_PAIR] + cum3[:, 2 * R_PAIR:]
    tot = jnp.concatenate([jnp.broadcast_to(cum[cc * c + c - 1:cc * c + c], (c, R_PAIR)) for cc in range(ncc)],
                          axis=0)
    p_inv = jnp.exp2(-cum)
    a_t = -kk * jnp.exp2(cum - logw)
    kka = kk * a
    b_t = kka * p_inv
    k_t = k_mod * p_inv
    r_t = r * jnp.exp2(cum)
    p_end = jnp.exp2(tot - cum)
    b_h = kka * p_end
    k_h = k_mod * p_end
    g_diag = jnp.exp2(tot)

    for cc in range(ncc):
      rs = slice(cc * c, (cc + 1) * c)
      for name, val in (("at", a_t), ("rt", r_t), ("bt", b_t), ("kt", k_t), ("bh", b_h), ("kh", k_h),
                        ("vv", v)):
        chains[name].append(stack(val[rs]))
      chains["gd"].append(g_diag[cc * c:cc * c + 1])
    post.append((bonus, gn_g, gn_b))

  nb = RW_P * ncc
  at, rt, bt, kt, bh, kh, vv = (jnp.stack(chains[name]) for name in ("at", "rt", "bt", "kt", "bh", "kh", "vv"))
  at_b, vv_b, bh_b = at.astype(BF16), vv.astype(BF16), bh.astype(BF16)
  bk = jnp.concatenate([bt, kt], axis=1).astype(BF16)
  top = jnp.where(strict2, _bdot_nt(at_b, bk), 0.0)
  lblk = jnp.where(incl2, _bdot_nt(rt.astype(BF16), bk), 0.0)
  nm, mak = top[:, :, :c2], top[:, :, c2:]
  tinv = jnp.where(diag, 1.0, nm)
  npow = nm.astype(BF16)
  for _ in range(5):
    npow = _bdot(npow, npow).astype(BF16)
    tinv = tinv + _bdot(tinv.astype(BF16), npow)
  x1 = _bdot(mak.astype(BF16), vv_b)
  wu_b = _bdot(tinv.astype(BF16), jnp.concatenate([at_b, x1.astype(BF16)], axis=2)).astype(BF16)
  rhs = jnp.concatenate([wu_b, jnp.concatenate([jnp.zeros_like(vv_b), vv_b], axis=2)], axis=1)
  qy = _bdot(lblk.astype(BF16), rhs)
  q_h = (rt + qy[:, :, :c2]).astype(BF16)
  y0 = qy[:, :, c2:]
  uv = jnp.concatenate([wu_b[:, :, c2:], vv_b], axis=1)
  bkh = jnp.concatenate([bh_b, kh.astype(BF16)], axis=1)
  g_m = [(jnp.where(diag, chains["gd"][n], 0.0) + _dot_tn(wu_b[n, :, :c2], bh_b[n])).astype(BF16)
         for n in range(nb)]
  h_m = [_dot_tn(uv[n], bkh[n]) for n in range(nb)]

  states = [s_ref[p] for p in range(RW_P)]
  y_rows = [[] for _ in range(RW_P)]
  for cc in range(ncc):
    for p in range(RW_P):
      n = p * ncc + cc
      s_old_b = states[p].astype(BF16)
      y2 = _dot_nt(q_h[n], s_old_b) + y0[n]
      states[p] = jnp.dot(s_old_b, g_m[n], preferred_element_type=F32) + h_m[n]
      y_rows[p].append(y2[:c] + y2[c:])

  for p in range(RW_P):
    ls = slice(p * R_PAIR, (p + 1) * R_PAIR)
    s_ref[p] = states[p]
    bonus, gn_g, gn_b = post[p]
    y = jnp.concatenate(y_rows[p], axis=0)
    mean = _seg_sum(y) * (1.0 / nh)
    yc = y - mean
    var = _seg_sum(yc * yc) * (1.0 / nh)
    yn = yc * lax.rsqrt(var + GN_EPS) * gn_g + gn_b
    g = gr_ref[:, ls].astype(F32)
    o_ref[:, ls] = ((yn + bonus) * (g / (1.0 + jnp.exp(-g)))).astype(o_ref.dtype)


def _rwkv(z_x, lo_x, z_mp, lo_mp, pvec, mu_l, wup_h, wup_l, aup):
  b, s, _ = z_x.shape
  tb = RW_TB
  nt = s // tb + 1
  pw = R_PAIR * RW_P

  def xmap(col):
    return lambda bi, hp, ti: (bi, jnp.maximum(ti - 1, 0), col // pw + hp)

  def mmap(col):
    return lambda bi, hp, ti: (0, col // pw + hp)

  return pl.pallas_call(
      _rwkv_kernel,
      name="rwkv7",
      grid=(b, R_PAIRS // RW_P, nt),
      in_specs=[
          pl.BlockSpec((None, tb, pw), xmap(Z_RR)),
          pl.BlockSpec((None, tb, pw), xmap(Z_RK)),
          pl.BlockSpec((None, tb, pw), xmap(Z_RV)),
          pl.BlockSpec((None, tb, LORA_PAD), lambda bi, hp, ti: (bi, jnp.maximum(ti - 1, 0), 0)),
          pl.BlockSpec((None, tb, pw), xmap(Z_GR)),
          pl.BlockSpec((tb, pw), mmap(Z_RR)),
          pl.BlockSpec((tb, pw), mmap(Z_RK)),
          pl.BlockSpec((tb, pw), mmap(Z_RV)),
          pl.BlockSpec((tb, LORA_PAD), lambda bi, hp, ti: (0, 0)),
          pl.BlockSpec((16, pw), lambda bi, hp, ti: (0, hp)),
          pl.BlockSpec((1, LORA_PAD), lambda bi, hp, ti: (0, 0)),
          pl.BlockSpec((LORA_PAD, pw), lambda bi, hp, ti: (0, hp)),
          pl.BlockSpec((LORA_PAD, pw), lambda bi, hp, ti: (0, hp)),
          pl.BlockSpec((LORA_PAD, pw), lambda bi, hp, ti: (0, hp)),
      ],
      out_specs=pl.BlockSpec((None, tb, pw), lambda bi, hp, ti: (bi, jnp.maximum(ti - 1, 0), hp)),
      out_shape=jax.ShapeDtypeStruct((b, s, R_WIDTH), BF16),
      scratch_shapes=[
          pltpu.VMEM((RW_P, 2 * R_HEAD, 2 * R_HEAD), F32),
          pltpu.VMEM((8, pw), F32),
          pltpu.VMEM((1, LORA_PAD), F32),
      ],
      compiler_params=pltpu.CompilerParams(
          dimension_semantics=("parallel", "parallel", "arbitrary"),
          vmem_limit_bytes=VMEM_LIMIT),
  )(z_x, z_x, z_x, lo_x, z_x, z_mp, z_mp, z_mp, lo_mp, pvec, mu_l, wup_h, wup_l, aup)


def _out_kernel(x_ref, oa_ref, or_ref, wa_ref, wr_ref, ge_ref, be_ref, gp_ref, bp_ref, o_ref):
  h = _ln_rows(x_ref[...], ge_ref[...], be_ref[...])
  y = (jnp.dot(oa_ref[...], wa_ref[...], preferred_element_type=F32)
       + jnp.dot(or_ref[...], wr_ref[...], preferred_element_type=F32))
  o_ref[...] = _ln_rows(DEEPNORM_ALPHA * h + y, gp_ref[...], bp_ref[...])


def _out_proj(x2d, oa, orw, wa, wr, ge, be, gp, bp, tm):
  m, d = x2d.shape
  vec = pl.BlockSpec((1, d), lambda i: (0, 0))
  return pl.pallas_call(
      _out_kernel,
      name="out_proj",
      grid=(m // tm,),
      in_specs=[
          pl.BlockSpec((tm, d), lambda i: (i, 0)),
          pl.BlockSpec((tm, A_WIDTH), lambda i: (i, 0)),
          pl.BlockSpec((tm, R_WIDTH), lambda i: (i, 0)),
          pl.BlockSpec((A_WIDTH, d), lambda i: (0, 0), pipeline_mode=pl.Buffered(1)),
          pl.BlockSpec((R_WIDTH, d), lambda i: (0, 0), pipeline_mode=pl.Buffered(1)),
          vec, vec, vec, vec,
      ],
      out_specs=pl.BlockSpec((tm, d), lambda i: (i, 0)),
      out_shape=jax.ShapeDtypeStruct((m, d), F32),
      compiler_params=pltpu.CompilerParams(
          dimension_semantics=("parallel",),
          vmem_limit_bytes=VMEM_LIMIT),
  )(x2d, oa, orw, wa, wr, ge, be, gp, bp)


def kernel(x, meta_tokens, ln_emb_g, ln_emb_b, rel_bias, w_in, w_out, lambda_q1, lambda_k1, lambda_q2,
           lambda_k2, subln_g, rw_mu, rw_w0, rw_w_up, rw_a0, rw_a_up, rw_k_k, rw_k_a, rw_r_k, rw_gn_g,
           rw_gn_b, ln_post_g, ln_post_b):
  b, s, d = x.shape
  l = 0
  wi = w_in[l]
  c_lo = 4 * A_WIDTH + 3 * R_WIDTH
  c_gr = c_lo + DECAY_LORA + ICLR_LORA
  lora_pad = LORA_PAD - DECAY_LORA - ICLR_LORA
  c_v = 2 * A_WIDTH
  w_all = wi.astype(BF16)
  w_gr = wi[:, c_gr:].astype(BF16)
  w_vt = wi[:, c_v:c_v + A_WIDTH].T.astype(BF16)
  w_lora = jnp.pad(wi[:, c_lo:c_gr], ((0, 0), (0, lora_pad))).astype(BF16)

  ge, be = ln_emb_g.reshape(1, d), ln_emb_b.reshape(1, d)
  x2d = x.reshape(b * s, d)
  z_x, vt_x, lo_x = _ln_matmul(x2d, ge, be, w_all, w_gr, w_vt, w_lora, 1024)
  z_x, lo_x = z_x.reshape(b, s, -1), lo_x.reshape(b, s, LORA_PAD)
  z_m, vt_m, lo_m = _ln_matmul(meta_tokens, ge, be, w_all, w_gr, w_vt, w_lora, N_META)

  bias_d, bias_s, bias_m = _bias_tiles(rel_bias)
  lam_p = jnp.stack([lambda_q1[l], lambda_k1[l], lambda_q2[l], lambda_k2[l]], axis=0)
  o_attn = _attention(z_x, vt_x, z_m, vt_m, bias_d, bias_s, bias_m, lam_p, subln_g[l].reshape(1, A_V_DIM))

  mu = rw_mu[l]
  zeros = jnp.zeros((R_WIDTH,), F32)
  pvec = jnp.stack([mu[:R_WIDTH], mu[R_WIDTH:2 * R_WIDTH], mu[2 * R_WIDTH:3 * R_WIDTH], rw_w0[l], rw_a0[l],
                    rw_k_k[l], rw_k_a[l], rw_r_k[l].reshape(R_WIDTH), rw_gn_g[l], rw_gn_b[l]]
                   + [zeros] * 6, axis=0)
  mu_l = jnp.pad(mu[3 * R_WIDTH:], (0, lora_pad)).reshape(1, LORA_PAD)
  wup = jnp.pad(rw_w_up[l], ((0, LORA_PAD - DECAY_LORA), (0, 0)))
  wup_h = wup.astype(BF16)
  wup_l = (wup - wup_h.astype(F32)).astype(BF16)
  aup = jnp.pad(rw_a_up[l], ((DECAY_LORA, lora_pad), (0, 0))).astype(BF16)
  front = ((RW_TB - N_META, 0), (0, 0))
  o_rwkv = _rwkv(z_x, lo_x, jnp.pad(z_m, front), jnp.pad(lo_m, front), pvec, mu_l, wup_h, wup_l, aup)

  wo = w_out[l].astype(BF16)
  out = _out_proj(x2d, o_attn.reshape(b * s, A_WIDTH), o_rwkv.reshape(b * s, R_WIDTH),
                  wo[:A_WIDTH], wo[A_WIDTH:], ge, be,
                  ln_post_g[l].reshape(1, d), ln_post_b[l].reshape(1, d), 512)
  return out.reshape(b, s, d)
```

```python
import math

import numpy as np
import jax
import jax.numpy as jnp
from jax import lax
from jax.experimental import pallas as pl
from jax.experimental.pallas import tpu as pltpu

N_META = 16
A_HEADS = 8
A_V_DIM = 128
A_QK_DIM = 64
A_WIDTH = A_HEADS * A_V_DIM
R_HEAD = 64
R_WIDTH = 1024
R_PAIR = 2 * R_HEAD
R_PAIRS = R_WIDTH // R_PAIR
A_QK_W = 2 * A_QK_DIM
DECAY_LORA = 96
ICLR_LORA = 96
LORA_PAD = 256
N_BUCKETS = 32
MAX_DISTANCE = 128
LN_EPS = 1e-5
SUBLN_EPS = 1e-5
GN_EPS = 64e-5
DEPTH = 1
DEEPNORM_ALPHA = (2 * DEPTH) ** 0.25
LAM_INIT = 0.8 - 0.6 * math.exp(-0.3 * 0)
NEG = -1e30

ATT_T = 256
ATT_G = 8
ONES_ROWS = 16
IN_TN = 1024
Z_Q, Z_K, Z_GA, Z_RR, Z_RK, Z_RV, Z_GR = (i * 1024 for i in range(7))
LOG2E = math.log2(math.e)
Q_SCALE = A_QK_DIM ** -0.5 * LOG2E
RW_TB = 128
RW_C = 64
RW_P = 8
VMEM_LIMIT = 56 * 1024 * 1024

F32 = jnp.float32
BF16 = jnp.bfloat16


def _ln_rows(x, g, b):
  mu = jnp.mean(x, axis=-1, keepdims=True)
  xc = x - mu
  var = jnp.mean(xc * xc, axis=-1, keepdims=True)
  return xc * lax.rsqrt(var + LN_EPS) * g + b


def _ln_mm_kernel(x_ref, g_ref, b_ref, wm_ref, wgr_ref, wvt_ref, wl_ref, om_ref, ovt_ref, ol_ref, hn_ref):
  j = pl.program_id(1)
  n_main = pl.num_programs(1) - 1

  @pl.when(j == 0)
  def _():
    hn_ref[...] = _ln_rows(x_ref[...], g_ref[...], b_ref[...]).astype(BF16)

  @pl.when(j < n_main - 1)
  def _():
    scale = jnp.where(j == Z_Q // IN_TN, Q_SCALE, 1.0)
    z = jnp.dot(hn_ref[...], wm_ref[...], preferred_element_type=F32)
    om_ref[...] = (z * scale).astype(om_ref.dtype)

  @pl.when(j == n_main - 1)
  def _():
    om_ref[...] = jnp.dot(hn_ref[...], wgr_ref[...], preferred_element_type=F32).astype(om_ref.dtype)
    ol_ref[...] = jnp.dot(hn_ref[...], wl_ref[...], preferred_element_type=F32)

  @pl.when(j == n_main)
  def _():
    zt = _dot_nt(wvt_ref[...], hn_ref[...])
    tv = ovt_ref.shape[2]
    for c in range(ovt_ref.shape[0]):
      ovt_ref[c] = zt[:, c * tv:(c + 1) * tv].astype(ovt_ref.dtype)


def _ln_matmul(x2d, g, b, w_all, w_gr, w_vt, w_lora, tm):
  m, d = x2d.shape
  tn = IN_TN
  nj = Z_GR // tn + 1
  n = nj * tn
  last = nj - 1
  tv = min(ATT_T, tm)
  v_tile = 2 * A_WIDTH // tn
  once = dict(pipeline_mode=pl.Buffered(1))

  def w_map(i, j):
    jj = jnp.minimum(j, last - 1)
    return (0, jnp.where(jj >= v_tile, jj + 1, jj))

  return pl.pallas_call(
      _ln_mm_kernel,
      name="ln_inproj",
      grid=(m // tm, nj + 1),
      in_specs=[
          pl.BlockSpec((tm, d), lambda i, j: (i, 0)),
          pl.BlockSpec((1, d), lambda i, j: (0, 0)),
          pl.BlockSpec((1, d), lambda i, j: (0, 0)),
          pl.BlockSpec((d, tn), w_map),
          pl.BlockSpec((d, tn), lambda i, j: (0, 0), **once),
          pl.BlockSpec((A_WIDTH, d), lambda i, j: (0, 0), **once),
          pl.BlockSpec((d, LORA_PAD), lambda i, j: (0, 0), **once),
      ],
      out_specs=[
          pl.BlockSpec((tm, tn), lambda i, j: (i, jnp.minimum(j, last))),
          pl.BlockSpec((tm // tv, A_WIDTH, tv), lambda i, j: (i, 0, 0)),
          pl.BlockSpec((tm, LORA_PAD), lambda i, j: (i, 0)),
      ],
      out_shape=[
          jax.ShapeDtypeStruct((m, n), BF16),
          jax.ShapeDtypeStruct((m // tv, A_WIDTH, tv), BF16),
          jax.ShapeDtypeStruct((m, LORA_PAD), F32),
      ],
      scratch_shapes=[pltpu.VMEM((tm, d), BF16)],
      compiler_params=pltpu.CompilerParams(
          dimension_semantics=("parallel", "arbitrary"),
          vmem_limit_bytes=VMEM_LIMIT),
  )(x2d, g, b, w_all, w_gr, w_vt, w_lora)


def _bucket_thresholds():
  n = np.arange(0, 4 * MAX_DISTANCE, dtype=np.int64)
  max_exact = N_BUCKETS // 2
  nf = np.maximum(n, 1).astype(np.float32)
  large = max_exact + (np.log(nf / np.float32(max_exact)) / np.float32(math.log(MAX_DISTANCE / max_exact))
                       * np.float32(N_BUCKETS - max_exact)).astype(np.int32)
  large = np.minimum(large, N_BUCKETS - 1)
  bucket = np.where(n < max_exact, n, large)
  assert np.all(np.diff(bucket) >= 0) and bucket[-1] == N_BUCKETS - 1
  return [int(np.argmax(bucket >= b)) for b in range(N_BUCKETS)]


_THR = _bucket_thresholds()


def _bias_kernel(rb_ref, diag_ref, sub_ref, meta_ref):
  h = pl.program_id(0)
  far = rb_ref[N_BUCKETS - 1, h]

  def bias_of(n):
    out = jnp.full(n.shape, (rb_ref[0, h] - far) * LOG2E, F32)
    for b in range(1, N_BUCKETS):
      out = jnp.where(n >= _THR[b], (rb_ref[b, h] - far) * LOG2E, out)
    return out

  t = ATT_T
  kj = lax.broadcasted_iota(jnp.int32, (t, t), 0)
  qi = lax.broadcasted_iota(jnp.int32, (t, t), 1)
  d = qi - kj
  diag_ref[...] = jnp.where(d >= 0, bias_of(d), NEG)
  sub_ref[...] = bias_of(d + t)
  km = lax.broadcasted_iota(jnp.int32, (N_META, t), 0)
  qm = lax.broadcasted_iota(jnp.int32, (N_META, t), 1)
  meta_ref[...] = bias_of(qm - km + N_META)


def _bias_tiles(rel_bias):
  t = ATT_T
  return pl.pallas_call(
      _bias_kernel,
      name="bias_tiles",
      grid=(A_HEADS,),
      in_specs=[pl.BlockSpec(memory_space=pltpu.SMEM)],
      out_specs=[
          pl.BlockSpec((None, t, t), lambda h: (h, 0, 0)),
          pl.BlockSpec((None, t, t), lambda h: (h, 0, 0)),
          pl.BlockSpec((None, N_META, t), lambda h: (h, 0, 0)),
      ],
      out_shape=[
          jax.ShapeDtypeStruct((A_HEADS, t, t), F32),
          jax.ShapeDtypeStruct((A_HEADS, t, t), F32),
          jax.ShapeDtypeStruct((A_HEADS, N_META, t), F32),
      ],
  )(rel_bias)


def _dot_nt(a, b):
  return lax.dot_general(a, b, (((1,), (1,)), ((), ())), preferred_element_type=F32)


def _attn_kernel(q_ref, kx_ref, vt_ref, km_ref, vmt_ref, ga_ref, bd_ref, bs_ref, bm_ref,
                 lam_ref, sg_ref, o_ref, m_ref, alpha_ref, acc_ref, pt_ref, ptm_ref):
  t = ATT_T
  g = ATT_G
  nc = 2 * g
  dv = A_V_DIM
  qi = pl.program_id(2)
  nq = pl.num_programs(2) - 1

  def v_tile(j):
    ones = jnp.ones((ONES_ROWS, t), BF16)
    return [jnp.concatenate([vt_ref[j, hh * dv:(hh + 1) * dv, :], ones], axis=0) for hh in range(g)]

  def k_tile(j):
    off = pl.multiple_of(j * t, t)
    return [kx_ref[pl.ds(off, t), hh * A_QK_W:(hh + 1) * A_QK_W] for hh in range(g)]

  def meta_v():
    ones_m = jnp.ones((ONES_ROWS, N_META), BF16)
    return [jnp.concatenate([vmt_ref[0, hh * dv:(hh + 1) * dv, :], ones_m], axis=0) for hh in range(g)]

  def queries():
    lane = lax.broadcasted_iota(jnp.int32, (t, A_QK_W), 1)
    qs = []
    for hh in range(g):
      q = q_ref[:, hh * A_QK_W:(hh + 1) * A_QK_W]
      zero = jnp.zeros_like(q)
      qs += [jnp.where(lane < A_QK_DIM, q, zero), jnp.where(lane >= A_QK_DIM, q, zero)]
    return qs

  def softmax_stage(c, m_prev, s_list):
    m_new = m_prev
    for s in s_list:
      m_new = jnp.maximum(m_new, jnp.max(s, axis=0, keepdims=True))
    m_ref[c] = m_new
    alpha_ref[c] = jnp.exp2(m_prev - m_new)
    return [jnp.exp2(s - m_new).astype(BF16) for s in s_list]

  def pending(segments, c):
    pv = None
    for vts, p_ref in segments:
      d = jnp.dot(vts[c // 2], p_ref[c], preferred_element_type=F32)
      pv = d if pv is None else pv + d
    return pv

  def finish_previous(segments):
    lp = lam_ref[...]
    lam = (jnp.exp(jnp.sum(lp[0:1] * lp[1:2], axis=1, keepdims=True))
           - jnp.exp(jnp.sum(lp[2:3] * lp[3:4], axis=1, keepdims=True)) + LAM_INIT)
    for hh in range(g):
      a0, a1 = (alpha_ref[c] * acc_ref[c] + pending(segments, c) for c in (2 * hh, 2 * hh + 1))
      ot = a0[:dv] / a0[dv:dv + 1] - lam * (a1[:dv] / a1[dv:dv + 1])
      ot = ot * lax.rsqrt(jnp.mean(ot * ot, axis=0, keepdims=True) + SUBLN_EPS)
      o = ot.T * (sg_ref[...] * (1.0 - LAM_INIT))
      gate = ga_ref[:, hh * dv:(hh + 1) * dv].astype(F32)
      o_ref[:, hh * dv:(hh + 1) * dv] = (o * (gate / (1.0 + jnp.exp(-gate)))).astype(o_ref.dtype)

  def start_tile(qs, previous):
    if previous is not None:
      finish_previous(previous)
    ks = k_tile(qi)
    st, sm = [], []
    for c in range(nc):
      hh = c // 2
      st.append(_dot_nt(ks[hh], qs[c]) + bd_ref[hh])
      sm.append(_dot_nt(km_ref[:, hh * A_QK_W:(hh + 1) * A_QK_W], qs[c]) + jnp.where(qi == 0, bm_ref[hh], 0.0))
    acc_ref[...] = jnp.zeros(acc_ref.shape, F32)
    m_start = jnp.full((1, t), NEG, F32)
    for c in range(nc):
      pt_ref[c], ptm_ref[c] = softmax_stage(c, m_start, [st[c], sm[c]])

  def step(qs, j_cur, biases, segments):
    ks = k_tile(j_cur)
    pv, st = [], []
    for c in range(nc):
      pv.append(pending(segments, c))
      s = _dot_nt(ks[c // 2], qs[c])
      st.append(s if biases is None else s + biases[c // 2])
    for c in range(nc):
      acc_ref[c] = alpha_ref[c] * acc_ref[c] + pv[c]
    for c in range(nc):
      pt_ref[c], = softmax_stage(c, m_ref[c], [st[c]])

  def below_diagonal(qs):
    step(qs, qi - 1, [bs_ref[hh] for hh in range(g)], [(v_tile(qi), pt_ref), (meta_v(), ptm_ref)])

  @pl.when(qi == 0)
  def _():
    start_tile(queries(), None)

  @pl.when(qi == 1)
  def _():
    qs = queries()
    start_tile(qs, [(v_tile(0), pt_ref), (meta_v(), ptm_ref)])
    below_diagonal(qs)

  @pl.when(jnp.logical_and(qi >= 2, qi < nq))
  def _():
    qs = queries()
    start_tile(qs, [(v_tile(jnp.maximum(qi - 3, 0)), pt_ref)])
    below_diagonal(qs)
    n_far = qi - 1

    def prev_of(j):
      return jnp.where(j == 0, qi - 1, j - 1)

    def far_body(i, carry):
      step(qs, 2 * i, None, [(v_tile(prev_of(2 * i)), pt_ref)])
      step(qs, 2 * i + 1, None, [(v_tile(2 * i), pt_ref)])
      return carry

    lax.fori_loop(0, n_far // 2, far_body, 0)

    @pl.when(n_far % 2 == 1)
    def _():
      step(qs, n_far - 1, None, [(v_tile(prev_of(n_far - 1)), pt_ref)])

  @pl.when(qi == nq)
  def _():
    finish_previous([(v_tile(nq - 3), pt_ref)])


def _attention(z_x, vt_x, z_m, vt_m, bias_d, bias_s, bias_m, lam_p, subln_g):
  b, s, _ = z_x.shape
  t = ATT_T
  g = ATT_G
  nq = s // t
  assert nq >= 3
  w = A_QK_W * g
  hb = A_HEADS // g
  kb, gb = Z_K // w, Z_GA // w

  def cur(qi):
    return jnp.minimum(qi, nq - 1)

  def prev(qi):
    return jnp.maximum(qi - 1, 0)

  return pl.pallas_call(
      _attn_kernel,
      name="diff_attn",
      grid=(b, hb, nq + 1),
      in_specs=[
          pl.BlockSpec((None, t, w), lambda bi, hi, qi: (bi, cur(qi), hi)),
          pl.BlockSpec((None, s, w), lambda bi, hi, qi: (bi, 0, kb + hi), pipeline_mode=pl.Buffered(1)),
          pl.BlockSpec((None, s // t, w, t), lambda bi, hi, qi: (bi, 0, hi, 0), pipeline_mode=pl.Buffered(1)),
          pl.BlockSpec((N_META, w), lambda bi, hi, qi: (0, kb + hi)),
          pl.BlockSpec((1, w, N_META), lambda bi, hi, qi: (0, hi, 0)),
          pl.BlockSpec((None, t, w), lambda bi, hi, qi: (bi, prev(qi), gb + hi)),
          pl.BlockSpec((g, t, t), lambda bi, hi, qi: (hi, 0, 0)),
          pl.BlockSpec((g, t, t), lambda bi, hi, qi: (hi, 0, 0)),
          pl.BlockSpec((g, N_META, t), lambda bi, hi, qi: (hi, 0, 0)),
          pl.BlockSpec((4, A_QK_DIM), lambda bi, hi, qi: (0, 0)),
          pl.BlockSpec((1, A_V_DIM), lambda bi, hi, qi: (0, 0)),
      ],
      out_specs=pl.BlockSpec((None, t, w), lambda bi, hi, qi: (bi, prev(qi), hi)),
      out_shape=jax.ShapeDtypeStruct((b, s, A_WIDTH), BF16),
      scratch_shapes=[
          pltpu.VMEM((2 * g, 1, t), F32),
          pltpu.VMEM((2 * g, 1, t), F32),
          pltpu.VMEM((2 * g, A_V_DIM + ONES_ROWS, t), F32),
          pltpu.VMEM((2 * g, t, t), BF16),
          pltpu.VMEM((2 * g, N_META, t), BF16),
      ],
      compiler_params=pltpu.CompilerParams(
          dimension_semantics=("parallel", "parallel", "arbitrary"),
          vmem_limit_bytes=VMEM_LIMIT),
  )(z_x, z_x, vt_x.reshape(b, s // t, A_WIDTH, t), z_m, vt_m, z_x, bias_d, bias_s, bias_m, lam_p, subln_g)


def _seg_sum(x):
  lane = lax.broadcasted_iota(jnp.int32, x.shape, 1)
  first = lane < R_HEAD
  lo = jnp.sum(jnp.where(first, x, 0.0), axis=1, keepdims=True)
  hi = jnp.sum(jnp.where(first, 0.0, x), axis=1, keepdims=True)
  return jnp.where(first, lo, hi)


def _split_bf16(x):
  hi = x.astype(BF16)
  return hi, x - hi.astype(F32)


def _dot_tn(a, b):
  return lax.dot_general(a, b, (((0,), (0,)), ((), ())), preferred_element_type=F32)


def _bdot(a, b):
  return lax.dot_general(a, b, (((2,), (1,)), ((0,), (0,))), preferred_element_type=F32)


def _bdot_nt(a, b):
  return lax.dot_general(a, b, (((2,), (2,)), ((0,), (0,))), preferred_element_type=F32)


def _rwkv_kernel(rx_ref, kx_ref, vx_ref, lx_ref, gr_ref, rm_ref, kmt_ref, vmt_ref, lm_ref,
                 pv_ref, mul_ref, wuph_ref, wupl_ref, aup_ref, o_ref, s_ref, prev_ref, prevl_ref):
  tb, c = RW_TB, RW_C
  nh = R_HEAD
  c2 = 2 * c
  ti = pl.program_id(2)
  is_meta = ti == 0

  @pl.when(is_meta)
  def _():
    s_ref[...] = jnp.zeros_like(s_ref)
    prev_ref[...] = jnp.zeros_like(prev_ref)
    prevl_ref[...] = jnp.zeros_like(prevl_ref)

  row = lax.broadcasted_iota(jnp.int32, (tb, R_PAIR), 0)
  rowl = lax.broadcasted_iota(jnp.int32, (tb, LORA_PAD), 0)

  def shifted(z, prev, mu, rows):
    z_prev = jnp.where(rows == 0, prev, pltpu.roll(z, 1, 0))
    return z + (z_prev - z) * mu

  z_l = jnp.where(is_meta, lm_ref[...], lx_ref[...])
  lo = shifted(z_l, prevl_ref[...], mul_ref[...], rowl)
  prevl_ref[...] = z_l[tb - 1:tb]
  th_h, th_l = _split_bf16(jnp.tanh(lo))
  w_lora = (jnp.dot(th_h, wuph_ref[...], preferred_element_type=F32)
            + jnp.dot(th_h, wupl_ref[...], preferred_element_type=F32)
            + jnp.dot(th_l.astype(BF16), wuph_ref[...], preferred_element_type=F32))
  a_lora = jnp.dot(lo.astype(BF16), aup_ref[...], preferred_element_type=F32)

  ii = lax.broadcasted_iota(jnp.int32, (tb, tb), 0)
  jj = lax.broadcasted_iota(jnp.int32, (tb, tb), 1)
  shift = int(math.log2(c))
  same = lax.shift_right_logical(ii, shift) == lax.shift_right_logical(jj, shift)
  cum_op = jnp.where(same, jnp.where(jj <= ii, 1.0, 0.0), 0.0).astype(BF16)

  ci = lax.broadcasted_iota(jnp.int32, (c2, c2), 0)
  cj = lax.broadcasted_iota(jnp.int32, (c2, c2), 1)
  diag = ci == cj
  strict2 = jnp.concatenate([cj < ci, cj < ci], axis=1)
  incl2 = jnp.concatenate([cj <= ci, cj <= ci], axis=1)
  first = lax.broadcasted_iota(jnp.int32, (c, R_PAIR), 1) < nh

  def stack(x):
    return jnp.concatenate([jnp.where(first, x, 0.0), jnp.where(first, 0.0, x)], axis=0)

  ncc = tb // c
  chains = {name: [] for name in ("at", "rt", "bt", "kt", "bh", "kh", "vv", "gd")}
  post = []
  for p in range(RW_P):
    ls = slice(p * R_PAIR, (p + 1) * R_PAIR)
    pv = pv_ref[:, ls]
    mu_r, mu_k, mu_v = pv[0:1], pv[1:2], pv[2:3]
    w0, a0, k_k, k_a, r_k, gn_g, gn_b = pv[3:4], pv[4:5], pv[5:6], pv[6:7], pv[7:8], pv[8:9], pv[9:10]

    z_r = jnp.where(is_meta, rm_ref[:, ls], rx_ref[:, ls]).astype(F32)
    z_k = jnp.where(is_meta, kmt_ref[:, ls], kx_ref[:, ls]).astype(F32)
    z_v = jnp.where(is_meta, vmt_ref[:, ls], vx_ref[:, ls]).astype(F32)
    r = shifted(z_r, prev_ref[0:1, ls], mu_r, row)
    k = shifted(z_k, prev_ref[1:2, ls], mu_k, row)
    v = shifted(z_v, prev_ref[2:3, ls], mu_v, row)
    prev_ref[0:1, ls] = z_r[tb - 1:tb]
    prev_ref[1:2, ls] = z_k[tb - 1:tb]
    prev_ref[2:3, ls] = z_v[tb - 1:tb]

    u = -(w0 + w_lora[:, ls])
    softplus = jnp.maximum(u, 0.0) + jnp.log(1.0 + jnp.exp(-jnp.abs(u)))
    logw = -jnp.exp(-softplus - 0.5) * LOG2E
    a = 1.0 / (1.0 + jnp.exp(-(a0 + a_lora[:, ls])))
    kk = k * k_k
    kk = kk / jnp.maximum(jnp.sqrt(_seg_sum(kk * kk)), 1e-12)
    k_mod = k * (1.0 + (a - 1.0) * k_a)
    bonus = _seg_sum(r * k_mod * r_k) * v

    lw_h, lw_r = _split_bf16(logw)
    lw_m, lw_l = _split_bf16(lw_r)
    cum3 = jnp.dot(cum_op, jnp.concatenate([lw_h, lw_m, lw_l.astype(BF16)], axis=1),
                   preferred_element_type=F32)
    cum = cum3[:, :R_PAIR] + cum3[:, R_PAIR:2 * R_PAIR] + cum3[:, 2 * R_PAIR:]
    tot = jnp.concatenate([jnp.broadcast_to(cum[cc * c + c - 1:cc * c + c], (c, R_PAIR)) for cc in range(ncc)],
                          axis=0)
    p_inv = jnp.exp2(-cum)
    a_t = -kk * jnp.exp2(cum - logw)
    kka = kk * a
    b_t = kka * p_inv
    k_t = k_mod * p_inv
    r_t = r * jnp.exp2(cum)
    p_end = jnp.exp2(tot - cum)
    b_h = kka * p_end
    k_h = k_mod * p_end
    g_diag = jnp.exp2(tot)

    for cc in range(ncc):
      rs = slice(cc * c, (cc + 1) * c)
      for name, val in (("at", a_t), ("rt", r_t), ("bt", b_t), ("kt", k_t), ("bh", b_h), ("kh", k_h),
                        ("vv", v)):
        chains[name].append(stack(val[rs]))
      chains["gd"].append(g_diag[cc * c:cc * c + 1])
    post.append((bonus, gn_g, gn_b))

  nb = RW_P * ncc
  at, rt, bt, kt, bh, kh, vv = (jnp.stack(chains[name]) for name in ("at", "rt", "bt", "kt", "bh", "kh", "vv"))
  at_b, vv_b, bh_b = at.astype(BF16), vv.astype(BF16), bh.astype(BF16)
  bk = jnp.concatenate([bt, kt], axis=1).astype(BF16)
  top = jnp.where(strict2, _bdot_nt(at_b, bk), 0.0)
  lblk = jnp.where(incl2, _bdot_nt(rt.astype(BF16), bk), 0.0)
  nm, mak = top[:, :, :c2], top[:, :, c2:]
  tinv = jnp.where(diag, 1.0, nm)
  npow = nm.astype(BF16)
  for _ in range(5):
    npow = _bdot(npow, npow).astype(BF16)
    tinv = tinv + _bdot(tinv.astype(BF16), npow)
  x1 = _bdot(mak.astype(BF16), vv_b)
  wu_b = _bdot(tinv.astype(BF16), jnp.concatenate([at_b, x1.astype(BF16)], axis=2)).astype(BF16)
  rhs = jnp.concatenate([wu_b, jnp.concatenate([jnp.zeros_like(vv_b), vv_b], axis=2)], axis=1)
  qy = _bdot(lblk.astype(BF16), rhs)
  q_h = (rt + qy[:, :, :c2]).astype(BF16)
  y0 = qy[:, :, c2:]
  uv = jnp.concatenate([wu_b[:, :, c2:], vv_b], axis=1)
  bkh = jnp.concatenate([bh_b, kh.astype(BF16)], axis=1)
  g_m = [(jnp.where(diag, chains["gd"][n], 0.0) + _dot_tn(wu_b[n, :, :c2], bh_b[n])).astype(BF16)
         for n in range(nb)]
  h_m = [_dot_tn(uv[n], bkh[n]) for n in range(nb)]

  states = [s_ref[p] for p in range(RW_P)]
  y_rows = [[] for _ in range(RW_P)]
  for cc in range(ncc):
    for p in range(RW_P):
      n = p * ncc + cc
      s_old_b = states[p].astype(BF16)
      y2 = _dot_nt(q_h[n], s_old_b) + y0[n]
      states[p] = jnp.dot(s_old_b, g_m[n], preferred_element_type=F32) + h_m[n]
      y_rows[p].append(y2[:c] + y2[c:])

  for p in range(RW_P):
    ls = slice(p * R_PAIR, (p + 1) * R_PAIR)
    s_ref[p] = states[p]
    bonus, gn_g, gn_b = post[p]
    y = jnp.concatenate(y_rows[p], axis=0)
    mean = _seg_sum(y) * (1.0 / nh)
    yc = y - mean
    var = _seg_sum(yc * yc) * (1.0 / nh)
    yn = yc * lax.rsqrt(var + GN_EPS) * gn_g + gn_b
    g = gr_ref[:, ls].astype(F32)
    o_ref[:, ls] = ((yn + bonus) * (g / (1.0 + jnp.exp(-g)))).astype(o_ref.dtype)


def _rwkv(z_x, lo_x, z_mp, lo_mp, pvec, mu_l, wup_h, wup_l, aup):
  b, s, _ = z_x.shape
  tb = RW_TB
  nt = s // tb + 1
  pw = R_PAIR * RW_P

  def xmap(col):
    return lambda bi, hp, ti: (bi, jnp.maximum(ti - 1, 0), col // pw + hp)

  def mmap(col):
    return lambda bi, hp, ti: (0, col // pw + hp)

  return pl.pallas_call(
      _rwkv_kernel,
      name="rwkv7",
      grid=(b, R_PAIRS // RW_P, nt),
      in_specs=[
          pl.BlockSpec((None, tb, pw), xmap(Z_RR)),
          pl.BlockSpec((None, tb, pw), xmap(Z_RK)),
          pl.BlockSpec((None, tb, pw), xmap(Z_RV)),
          pl.BlockSpec((None, tb, LORA_PAD), lambda bi, hp, ti: (bi, jnp.maximum(ti - 1, 0), 0)),
          pl.BlockSpec((None, tb, pw), xmap(Z_GR)),
          pl.BlockSpec((tb, pw), mmap(Z_RR)),
          pl.BlockSpec((tb, pw), mmap(Z_RK)),
          pl.BlockSpec((tb, pw), mmap(Z_RV)),
          pl.BlockSpec((tb, LORA_PAD), lambda bi, hp, ti: (0, 0)),
          pl.BlockSpec((16, pw), lambda bi, hp, ti: (0, hp)),
          pl.BlockSpec((1, LORA_PAD), lambda bi, hp, ti: (0, 0)),
          pl.BlockSpec((LORA_PAD, pw), lambda bi, hp, ti: (0, hp)),
          pl.BlockSpec((LORA_PAD, pw), lambda bi, hp, ti: (0, hp)),
          pl.BlockSpec((LORA_PAD, pw), lambda bi, hp, ti: (0, hp)),
      ],
      out_specs=pl.BlockSpec((None, tb, pw), lambda bi, hp, ti: (bi, jnp.maximum(ti - 1, 0), hp)),
      out_shape=jax.ShapeDtypeStruct((b, s, R_WIDTH), BF16),
      scratch_shapes=[
          pltpu.VMEM((RW_P, 2 * R_HEAD, 2 * R_HEAD), F32),
          pltpu.VMEM((8, pw), F32),
          pltpu.VMEM((1, LORA_PAD), F32),
      ],
      compiler_params=pltpu.CompilerParams(
          dimension_semantics=("parallel", "parallel", "arbitrary"),
          vmem_limit_bytes=VMEM_LIMIT),
  )(z_x, z_x, z_x, lo_x, z_x, z_mp, z_mp, z_mp, lo_mp, pvec, mu_l, wup_h, wup_l, aup)


def _out_kernel(x_ref, oa_ref, or_ref, wa_ref, wr_ref, ge_ref, be_ref, gp_ref, bp_ref, o_ref):
  h = _ln_rows(x_ref[...], ge_ref[...], be_ref[...])
  y = (jnp.dot(oa_ref[...], wa_ref[...], preferred_element_type=F32)
       + jnp.dot(or_ref[...], wr_ref[...], preferred_element_type=F32))
  o_ref[...] = _ln_rows(DEEPNORM_ALPHA * h + y, gp_ref[...], bp_ref[...])


def _out_proj(x2d, oa, orw, wa, wr, ge, be, gp, bp, tm):
  m, d = x2d.shape
  vec = pl.BlockSpec((1, d), lambda i: (0, 0))
  return pl.pallas_call(
      _out_kernel,
      name="out_proj",
      grid=(m // tm,),
      in_specs=[
          pl.BlockSpec((tm, d), lambda i: (i, 0)),
          pl.BlockSpec((tm, A_WIDTH), lambda i: (i, 0)),
          pl.BlockSpec((tm, R_WIDTH), lambda i: (i, 0)),
          pl.BlockSpec((A_WIDTH, d), lambda i: (0, 0), pipeline_mode=pl.Buffered(1)),
          pl.BlockSpec((R_WIDTH, d), lambda i: (0, 0), pipeline_mode=pl.Buffered(1)),
          vec, vec, vec, vec,
      ],
      out_specs=pl.BlockSpec((tm, d), lambda i: (i, 0)),
      out_shape=jax.ShapeDtypeStruct((m, d), F32),
      compiler_params=pltpu.CompilerParams(
          dimension_semantics=("parallel",),
          vmem_limit_bytes=VMEM_LIMIT),
  )(x2d, oa, orw, wa, wr, ge, be, gp, bp)


def kernel(x, meta_tokens, ln_emb_g, ln_emb_b, rel_bias, w_in, w_out, lambda_q1, lambda_k1, lambda_q2,
           lambda_k2, subln_g, rw_mu, rw_w0, rw_w_up, rw_a0, rw_a_up, rw_k_k, rw_k_a, rw_r_k, rw_gn_g,
           rw_gn_b, ln_post_g, ln_post_b):
  b, s, d = x.shape
  l = 0
  wi = w_in[l]
  c_lo = 4 * A_WIDTH + 3 * R_WIDTH
  c_gr = c_lo + DECAY_LORA + ICLR_LORA
  lora_pad = LORA_PAD - DECAY_LORA - ICLR_LORA
  c_v = 2 * A_WIDTH
  w_all = wi.astype(BF16)
  w_gr = wi[:, c_gr:].astype(BF16)
  w_vt = wi[:, c_v:c_v + A_WIDTH].T.astype(BF16)
  w_lora = jnp.pad(wi[:, c_lo:c_gr], ((0, 0), (0, lora_pad))).astype(BF16)

  ge, be = ln_emb_g.reshape(1, d), ln_emb_b.reshape(1, d)
  x2d = x.reshape(b * s, d)
  z_x, vt_x, lo_x = _ln_matmul(x2d, ge, be, w_all, w_gr, w_vt, w_lora, 1024)
  z_x, lo_x = z_x.reshape(b, s, -1), lo_x.reshape(b, s, LORA_PAD)
  z_m, vt_m, lo_m = _ln_matmul(meta_tokens, ge, be, w_all, w_gr, w_vt, w_lora, N_META)

  bias_d, bias_s, bias_m = _bias_tiles(rel_bias)
  lam_p = jnp.stack([lambda_q1[l], lambda_k1[l], lambda_q2[l], lambda_k2[l]], axis=0)
  o_attn = _attention(z_x, vt_x, z_m, vt_m, bias_d, bias_s, bias_m, lam_p, subln_g[l].reshape(1, A_V_DIM))

  mu = rw_mu[l]
  zeros = jnp.zeros((R_WIDTH,), F32)
  pvec = jnp.stack([mu[:R_WIDTH], mu[R_WIDTH:2 * R_WIDTH], mu[2 * R_WIDTH:3 * R_WIDTH], rw_w0[l], rw_a0[l],
                    rw_k_k[l], rw_k_a[l], rw_r_k[l].reshape(R_WIDTH), rw_gn_g[l], rw_gn_b[l]]
                   + [zeros] * 6, axis=0)
  mu_l = jnp.pad(mu[3 * R_WIDTH:], (0, lora_pad)).reshape(1, LORA_PAD)
  wup = jnp.pad(rw_w_up[l], ((0, LORA_PAD - DECAY_LORA), (0, 0)))
  wup_h = wup.astype(BF16)
  wup_l = (wup - wup_h.astype(F32)).astype(BF16)
  aup = jnp.pad(rw_a_up[l], ((DECAY_LORA, lora_pad), (0, 0))).astype(BF16)
  front = ((RW_TB - N_META, 0), (0, 0))
  o_rwkv = _rwkv(z_x, lo_x, jnp.pad(z_m, front), jnp.pad(lo_m, front), pvec, mu_l, wup_h, wup_l, aup)

  wo = w_out[l].astype(BF16)
  out = _out_proj(x2d, o_attn.reshape(b * s, A_WIDTH), o_rwkv.reshape(b * s, R_WIDTH),
                  wo[:A_WIDTH], wo[A_WIDTH:], ge, be,
                  ln_post_g[l].reshape(1, d), ln_post_b[l].reshape(1, d), 512)
  return out.reshape(b, s, d)
```

```python
import math

import numpy as np
import jax
import jax.numpy as jnp
from jax import lax
from jax.experimental import pallas as pl
from jax.experimental.pallas import tpu as pltpu

N_META = 16
A_HEADS = 8
A_V_DIM = 128
A_QK_DIM = 64
A_WIDTH = A_HEADS * A_V_DIM
R_HEAD = 64
R_WIDTH = 1024
R_PAIR = 2 * R_HEAD
R_PAIRS = R_WIDTH // R_PAIR
A_QK_W = 2 * A_QK_DIM
DECAY_LORA = 96
ICLR_LORA = 96
LORA_PAD = 256
N_BUCKETS = 32
MAX_DISTANCE = 128
LN_EPS = 1e-5
SUBLN_EPS = 1e-5
GN_EPS = 64e-5
DEPTH = 1
DEEPNORM_ALPHA = (2 * DEPTH) ** 0.25
LAM_INIT = 0.8 - 0.6 * math.exp(-0.3 * 0)
NEG = -1e30

ATT_T = 256
ATT_G = 8
ONES_ROWS = 16
IN_TN = 1024
Z_Q, Z_K, Z_GA, Z_RR, Z_RK, Z_RV, Z_GR = (i * 1024 for i in range(7))
LOG2E = math.log2(math.e)
Q_SCALE = A_QK_DIM ** -0.5 * LOG2E
RW_TB = 128
RW_C = 64
RW_P = 8
VMEM_LIMIT = 56 * 1024 * 1024

F32 = jnp.float32
BF16 = jnp.bfloat16


def _ln_rows(x, g, b):
  mu = jnp.mean(x, axis=-1, keepdims=True)
  xc = x - mu
  var = jnp.mean(xc * xc, axis=-1, keepdims=True)
  return xc * lax.rsqrt(var + LN_EPS) * g + b


def _ln_mm_kernel(x_ref, mt_ref, g_ref, b_ref, wm_ref, wgr_ref, wvt_ref, wl_ref,
                  om_ref, ovt_ref, ol_ref, omm_ref, ovtm_ref, olm_ref, hn_ref):
  j = pl.program_id(1)
  n_main = pl.num_programs(1) - 1
  tm = x_ref.shape[0]

  @pl.when(j == 0)
  def _():
    hn_ref[:tm] = _ln_rows(x_ref[...], g_ref[...], b_ref[...]).astype(BF16)
    hn_ref[tm:] = _ln_rows(mt_ref[...], g_ref[...], b_ref[...]).astype(BF16)

  @pl.when(j < n_main - 1)
  def _():
    scale = jnp.where(j == Z_Q // IN_TN, Q_SCALE, 1.0)
    z = jnp.dot(hn_ref[...], wm_ref[...], preferred_element_type=F32) * scale
    om_ref[...] = z[:tm].astype(om_ref.dtype)
    omm_ref[...] = z[tm:].astype(omm_ref.dtype)

  @pl.when(j == n_main - 1)
  def _():
    z = jnp.dot(hn_ref[...], wgr_ref[...], preferred_element_type=F32)
    om_ref[...] = z[:tm].astype(om_ref.dtype)
    omm_ref[...] = z[tm:].astype(omm_ref.dtype)
    zl = jnp.dot(hn_ref[...], wl_ref[...], preferred_element_type=F32)
    ol_ref[...] = zl[:tm]
    olm_ref[...] = zl[tm:]

  @pl.when(j == n_main)
  def _():
    zt = _dot_nt(wvt_ref[...], hn_ref[:tm])
    tv = ovt_ref.shape[2]
    for c in range(ovt_ref.shape[0]):
      ovt_ref[c] = zt[:, c * tv:(c + 1) * tv].astype(ovt_ref.dtype)

    @pl.when(pl.program_id(0) == 0)
    def _():
      ovtm_ref[0] = _dot_nt(wvt_ref[...], hn_ref[tm:]).astype(ovtm_ref.dtype)


def _ln_matmul(x2d, meta, g, b, w_all, w_gr, w_vt, w_lora, tm):
  m, d = x2d.shape
  nm = meta.shape[0]
  tn = IN_TN
  nj = Z_GR // tn + 1
  n = nj * tn
  last = nj - 1
  tv = ATT_T
  v_tile = 2 * A_WIDTH // tn
  once = dict(pipeline_mode=pl.Buffered(1))

  def w_map(i, j):
    jj = jnp.minimum(j, last - 1)
    return (0, jnp.where(jj >= v_tile, jj + 1, jj))

  return pl.pallas_call(
      _ln_mm_kernel,
      name="ln_inproj",
      grid=(m // tm, nj + 1),
      in_specs=[
          pl.BlockSpec((tm, d), lambda i, j: (i, 0)),
          pl.BlockSpec((nm, d), lambda i, j: (0, 0)),
          pl.BlockSpec((1, d), lambda i, j: (0, 0)),
          pl.BlockSpec((1, d), lambda i, j: (0, 0)),
          pl.BlockSpec((d, tn), w_map),
          pl.BlockSpec((d, tn), lambda i, j: (0, 0), **once),
          pl.BlockSpec((A_WIDTH, d), lambda i, j: (0, 0), **once),
          pl.BlockSpec((d, LORA_PAD), lambda i, j: (0, 0), **once),
      ],
      out_specs=[
          pl.BlockSpec((tm, tn), lambda i, j: (i, jnp.minimum(j, last))),
          pl.BlockSpec((tm // tv, A_WIDTH, tv), lambda i, j: (i, 0, 0)),
          pl.BlockSpec((tm, LORA_PAD), lambda i, j: (i, 0)),
          pl.BlockSpec((nm, tn), lambda i, j: (0, jnp.minimum(j, last))),
          pl.BlockSpec((1, A_WIDTH, nm), lambda i, j: (0, 0, 0)),
          pl.BlockSpec((nm, LORA_PAD), lambda i, j: (0, 0)),
      ],
      out_shape=[
          jax.ShapeDtypeStruct((m, n), BF16),
          jax.ShapeDtypeStruct((m // tv, A_WIDTH, tv), BF16),
          jax.ShapeDtypeStruct((m, LORA_PAD), F32),
          jax.ShapeDtypeStruct((nm, n), BF16),
          jax.ShapeDtypeStruct((1, A_WIDTH, nm), BF16),
          jax.ShapeDtypeStruct((nm, LORA_PAD), F32),
      ],
      scratch_shapes=[pltpu.VMEM((tm + nm, d), BF16)],
      compiler_params=pltpu.CompilerParams(
          dimension_semantics=("arbitrary", "arbitrary"),
          vmem_limit_bytes=VMEM_LIMIT),
  )(x2d, meta, g, b, w_all, w_gr, w_vt, w_lora)


def _bucket_thresholds():
  n = np.arange(0, 4 * MAX_DISTANCE, dtype=np.int64)
  max_exact = N_BUCKETS // 2
  nf = np.maximum(n, 1).astype(np.float32)
  large = max_exact + (np.log(nf / np.float32(max_exact)) / np.float32(math.log(MAX_DISTANCE / max_exact))
                       * np.float32(N_BUCKETS - max_exact)).astype(np.int32)
  large = np.minimum(large, N_BUCKETS - 1)
  bucket = np.where(n < max_exact, n, large)
  assert np.all(np.diff(bucket) >= 0) and bucket[-1] == N_BUCKETS - 1
  return [int(np.argmax(bucket >= b)) for b in range(N_BUCKETS)]


_THR = _bucket_thresholds()


def _bias_kernel(rb_ref, diag_ref, sub_ref, meta_ref):
  h = pl.program_id(0)
  far = rb_ref[N_BUCKETS - 1, h]

  def bias_of(n):
    out = jnp.full(n.shape, (rb_ref[0, h] - far) * LOG2E, F32)
    for b in range(1, N_BUCKETS):
      out = jnp.where(n >= _THR[b], (rb_ref[b, h] - far) * LOG2E, out)
    return out

  t = ATT_T
  kj = lax.broadcasted_iota(jnp.int32, (t, t), 0)
  qi = lax.broadcasted_iota(jnp.int32, (t, t), 1)
  d = qi - kj
  diag_ref[...] = jnp.where(d >= 0, bias_of(d), NEG)
  sub_ref[...] = bias_of(d + t)
  km = lax.broadcasted_iota(jnp.int32, (N_META, t), 0)
  qm = lax.broadcasted_iota(jnp.int32, (N_META, t), 1)
  meta_ref[...] = bias_of(qm - km + N_META)


def _bias_tiles(rel_bias):
  t = ATT_T
  return pl.pallas_call(
      _bias_kernel,
      name="bias_tiles",
      grid=(A_HEADS,),
      in_specs=[pl.BlockSpec(memory_space=pltpu.SMEM)],
      out_specs=[
          pl.BlockSpec((None, t, t), lambda h: (h, 0, 0)),
          pl.BlockSpec((None, t, t), lambda h: (h, 0, 0)),
          pl.BlockSpec((None, N_META, t), lambda h: (h, 0, 0)),
      ],
      out_shape=[
          jax.ShapeDtypeStruct((A_HEADS, t, t), F32),
          jax.ShapeDtypeStruct((A_HEADS, t, t), F32),
          jax.ShapeDtypeStruct((A_HEADS, N_META, t), F32),
      ],
  )(rel_bias)


def _dot_nt(a, b):
  return lax.dot_general(a, b, (((1,), (1,)), ((), ())), preferred_element_type=F32)


def _attn_kernel(q_ref, kx_ref, vt_ref, km_ref, vmt_ref, ga_ref, bd_ref, bs_ref, bm_ref,
                 lam_ref, sg_ref, o_ref, m_ref, alpha_ref, acc_ref, pt_ref, ptm_ref):
  t = ATT_T
  g = ATT_G
  nc = 2 * g
  dv = A_V_DIM
  qi = pl.program_id(2)
  nq = pl.num_programs(2) - 1

  def v_tile(j):
    ones = jnp.ones((ONES_ROWS, t), BF16)
    return [jnp.concatenate([vt_ref[j, hh * dv:(hh + 1) * dv, :], ones], axis=0) for hh in range(g)]

  def k_tile(j):
    off = pl.multiple_of(j * t, t)
    return [kx_ref[pl.ds(off, t), hh * A_QK_W:(hh + 1) * A_QK_W] for hh in range(g)]

  def meta_v():
    ones_m = jnp.ones((ONES_ROWS, N_META), BF16)
    return [jnp.concatenate([vmt_ref[0, hh * dv:(hh + 1) * dv, :], ones_m], axis=0) for hh in range(g)]

  def queries():
    lane = lax.broadcasted_iota(jnp.int32, (t, A_QK_W), 1)
    qs = []
    for hh in range(g):
      q = q_ref[:, hh * A_QK_W:(hh + 1) * A_QK_W]
      zero = jnp.zeros_like(q)
      qs += [jnp.where(lane < A_QK_DIM, q, zero), jnp.where(lane >= A_QK_DIM, q, zero)]
    return qs

  def softmax_stage(c, m_prev, s_list):
    m_new = m_prev
    for s in s_list:
      m_new = jnp.maximum(m_new, jnp.max(s, axis=0, keepdims=True))
    m_ref[c] = m_new
    alpha_ref[c] = jnp.exp2(m_prev - m_new)
    return [jnp.exp2(s - m_new).astype(BF16) for s in s_list]

  def pending(segments, c):
    pv = None
    for vts, p_ref in segments:
      d = jnp.dot(vts[c // 2], p_ref[c], preferred_element_type=F32)
      pv = d if pv is None else pv + d
    return pv

  def finish_previous(segments):
    lp = lam_ref[...]
    lam = (jnp.exp(jnp.sum(lp[0:1] * lp[1:2], axis=1, keepdims=True))
           - jnp.exp(jnp.sum(lp[2:3] * lp[3:4], axis=1, keepdims=True)) + LAM_INIT)
    for hh in range(g):
      a0, a1 = (alpha_ref[c] * acc_ref[c] + pending(segments, c) for c in (2 * hh, 2 * hh + 1))
      ot = a0[:dv] / a0[dv:dv + 1] - lam * (a1[:dv] / a1[dv:dv + 1])
      ot = ot * lax.rsqrt(jnp.mean(ot * ot, axis=0, keepdims=True) + SUBLN_EPS)
      o = ot.T * (sg_ref[...] * (1.0 - LAM_INIT))
      gate = ga_ref[:, hh * dv:(hh + 1) * dv].astype(F32)
      o_ref[:, hh * dv:(hh + 1) * dv] = (o * (gate / (1.0 + jnp.exp(-gate)))).astype(o_ref.dtype)

  def start_tile(qs, previous):
    if previous is not None:
      finish_previous(previous)
    ks = k_tile(qi)
    st, sm = [], []
    for c in range(nc):
      hh = c // 2
      st.append(_dot_nt(ks[hh], qs[c]) + bd_ref[hh])
      sm.append(_dot_nt(km_ref[:, hh * A_QK_W:(hh + 1) * A_QK_W], qs[c]) + jnp.where(qi == 0, bm_ref[hh], 0.0))
    acc_ref[...] = jnp.zeros(acc_ref.shape, F32)
    m_start = jnp.full((1, t), NEG, F32)
    for c in range(nc):
      pt_ref[c], ptm_ref[c] = softmax_stage(c, m_start, [st[c], sm[c]])

  def step(qs, j_cur, biases, segments):
    ks = k_tile(j_cur)
    pv, st = [], []
    for c in range(nc):
      pv.append(pending(segments, c))
      s = _dot_nt(ks[c // 2], qs[c])
      st.append(s if biases is None else s + biases[c // 2])
    for c in range(nc):
      acc_ref[c] = alpha_ref[c] * acc_ref[c] + pv[c]
    for c in range(nc):
      pt_ref[c], = softmax_stage(c, m_ref[c], [st[c]])

  def below_diagonal(qs):
    step(qs, qi - 1, [bs_ref[hh] for hh in range(g)], [(v_tile(qi), pt_ref), (meta_v(), ptm_ref)])

  @pl.when(qi == 0)
  def _():
    start_tile(queries(), None)

  @pl.when(qi == 1)
  def _():
    qs = queries()
    start_tile(qs, [(v_tile(0), pt_ref), (meta_v(), ptm_ref)])
    below_diagonal(qs)

  @pl.when(jnp.logical_and(qi >= 2, qi < nq))
  def _():
    qs = queries()
    start_tile(qs, [(v_tile(jnp.maximum(qi - 3, 0)), pt_ref)])
    below_diagonal(qs)
    n_far = qi - 1

    def prev_of(j):
      return jnp.where(j == 0, qi - 1, j - 1)

    def far_body(i, carry):
      step(qs, 2 * i, None, [(v_tile(prev_of(2 * i)), pt_ref)])
      step(qs, 2 * i + 1, None, [(v_tile(2 * i), pt_ref)])
      return carry

    lax.fori_loop(0, n_far // 2, far_body, 0)

    @pl.when(n_far % 2 == 1)
    def _():
      step(qs, n_far - 1, None, [(v_tile(prev_of(n_far - 1)), pt_ref)])

  @pl.when(qi == nq)
  def _():
    finish_previous([(v_tile(nq - 3), pt_ref)])


def _attention(z_x, vt_x, z_m, vt_m, bias_d, bias_s, bias_m, lam_p, subln_g):
  b, s, _ = z_x.shape
  t = ATT_T
  g = ATT_G
  nq = s // t
  assert nq >= 3
  w = A_QK_W * g
  hb = A_HEADS // g
  kb, gb = Z_K // w, Z_GA // w

  def cur(qi):
    return jnp.minimum(qi, nq - 1)

  def prev(qi):
    return jnp.maximum(qi - 1, 0)

  return pl.pallas_call(
      _attn_kernel,
      name="diff_attn",
      grid=(b, hb, nq + 1),
      in_specs=[
          pl.BlockSpec((None, t, w), lambda bi, hi, qi: (bi, cur(qi), hi)),
          pl.BlockSpec((None, s, w), lambda bi, hi, qi: (bi, 0, kb + hi), pipeline_mode=pl.Buffered(1)),
          pl.BlockSpec((None, s // t, w, t), lambda bi, hi, qi: (bi, 0, hi, 0), pipeline_mode=pl.Buffered(1)),
          pl.BlockSpec((N_META, w), lambda bi, hi, qi: (0, kb + hi)),
          pl.BlockSpec((1, w, N_META), lambda bi, hi, qi: (0, hi, 0)),
          pl.BlockSpec((None, t, w), lambda bi, hi, qi: (bi, prev(qi), gb + hi)),
          pl.BlockSpec((g, t, t), lambda bi, hi, qi: (hi, 0, 0)),
          pl.BlockSpec((g, t, t), lambda bi, hi, qi: (hi, 0, 0)),
          pl.BlockSpec((g, N_META, t), lambda bi, hi, qi: (hi, 0, 0)),
          pl.BlockSpec((4, A_QK_DIM), lambda bi, hi, qi: (0, 0)),
          pl.BlockSpec((1, A_V_DIM), lambda bi, hi, qi: (0, 0)),
      ],
      out_specs=pl.BlockSpec((None, t, w), lambda bi, hi, qi: (bi, prev(qi), hi)),
      out_shape=jax.ShapeDtypeStruct((b, s, A_WIDTH), BF16),
      scratch_shapes=[
          pltpu.VMEM((2 * g, 1, t), F32),
          pltpu.VMEM((2 * g, 1, t), F32),
          pltpu.VMEM((2 * g, A_V_DIM + ONES_ROWS, t), F32),
          pltpu.VMEM((2 * g, t, t), BF16),
          pltpu.VMEM((2 * g, N_META, t), BF16),
      ],
      compiler_params=pltpu.CompilerParams(
          dimension_semantics=("parallel", "parallel", "arbitrary"),
          vmem_limit_bytes=VMEM_LIMIT),
  )(z_x, z_x, vt_x.reshape(b, s // t, A_WIDTH, t), z_m, vt_m, z_x, bias_d, bias_s, bias_m, lam_p, subln_g)


def _seg_sum(x):
  lane = lax.broadcasted_iota(jnp.int32, x.shape, 1)
  first = lane < R_HEAD
  lo = jnp.sum(jnp.where(first, x, 0.0), axis=1, keepdims=True)
  hi = jnp.sum(jnp.where(first, 0.0, x), axis=1, keepdims=True)
  return jnp.where(first, lo, hi)


def _split_bf16(x):
  hi = x.astype(BF16)
  return hi, x - hi.astype(F32)


def _dot_tn(a, b):
  return lax.dot_general(a, b, (((0,), (0,)), ((), ())), preferred_element_type=F32)


def _bdot(a, b):
  return lax.dot_general(a, b, (((2,), (1,)), ((0,), (0,))), preferred_element_type=F32)


def _bdot_nt(a, b):
  return lax.dot_general(a, b, (((2,), (2,)), ((0,), (0,))), preferred_element_type=F32)


def _rwkv_kernel(rx_ref, kx_ref, vx_ref, lx_ref, gr_ref, rm_ref, kmt_ref, vmt_ref, lm_ref,
                 pv_ref, mul_ref, wuph_ref, wupl_ref, aup_ref, o_ref, s_ref, prev_ref, prevl_ref):
  tb, c = RW_TB, RW_C
  nh = R_HEAD
  c2 = 2 * c
  ti = pl.program_id(2)
  is_meta = ti == 0

  @pl.when(is_meta)
  def _():
    s_ref[...] = jnp.zeros_like(s_ref)
    prev_ref[...] = jnp.zeros_like(prev_ref)
    prevl_ref[...] = jnp.zeros_like(prevl_ref)

  row = lax.broadcasted_iota(jnp.int32, (tb, R_PAIR), 0)
  rowl = lax.broadcasted_iota(jnp.int32, (tb, LORA_PAD), 0)

  def shifted(z, prev, mu, rows):
    z_prev = jnp.where(rows == 0, prev, pltpu.roll(z, 1, 0))
    return z + (z_prev - z) * mu

  z_l = jnp.where(is_meta, lm_ref[...], lx_ref[...])
  lo = shifted(z_l, prevl_ref[...], mul_ref[...], rowl)
  prevl_ref[...] = z_l[tb - 1:tb]
  th_h, th_l = _split_bf16(jnp.tanh(lo))
  w_lora = (jnp.dot(th_h, wuph_ref[...], preferred_element_type=F32)
            + jnp.dot(th_h, wupl_ref[...], preferred_element_type=F32)
            + jnp.dot(th_l.astype(BF16), wuph_ref[...], preferred_element_type=F32))
  a_lora = jnp.dot(lo.astype(BF16), aup_ref[...], preferred_element_type=F32)

  ii = lax.broadcasted_iota(jnp.int32, (tb, tb), 0)
  jj = lax.broadcasted_iota(jnp.int32, (tb, tb), 1)
  shift = int(math.log2(c))
  same = lax.shift_right_logical(ii, shift) == lax.shift_right_logical(jj, shift)
  cum_op = jnp.where(same, jnp.where(jj <= ii, 1.0, 0.0), 0.0).astype(BF16)

  ci = lax.broadcasted_iota(jnp.int32, (c2, c2), 0)
  cj = lax.broadcasted_iota(jnp.int32, (c2, c2), 1)
  diag = ci == cj
  strict2 = jnp.concatenate([cj < ci, cj < ci], axis=1)
  incl2 = jnp.concatenate([cj <= ci, cj <= ci], axis=1)
  first = lax.broadcasted_iota(jnp.int32, (c, R_PAIR), 1) < nh

  def stack(x):
    return jnp.concatenate([jnp.where(first, x, 0.0), jnp.where(first, 0.0, x)], axis=0)

  ncc = tb // c
  chains = {name: [] for name in ("at", "rt", "bt", "kt", "bh", "kh", "vv", "gd")}
  post = []
  for p in range(RW_P):
    ls = slice(p * R_PAIR, (p + 1) * R_PAIR)
    pv = pv_ref[:, ls]
    mu_r, mu_k, mu_v = pv[0:1], pv[1:2], pv[2:3]
    w0, a0, k_k, k_a, r_k, gn_g, gn_b = pv[3:4], pv[4:5], pv[5:6], pv[6:7], pv[7:8], pv[8:9], pv[9:10]

    z_r = jnp.where(is_meta, rm_ref[:, ls], rx_ref[:, ls]).astype(F32)
    z_k = jnp.where(is_meta, kmt_ref[:, ls], kx_ref[:, ls]).astype(F32)
    z_v = jnp.where(is_meta, vmt_ref[:, ls], vx_ref[:, ls]).astype(F32)
    r = shifted(z_r, prev_ref[0:1, ls], mu_r, row)
    k = shifted(z_k, prev_ref[1:2, ls], mu_k, row)
    v = shifted(z_v, prev_ref[2:3, ls], mu_v, row)
    prev_ref[0:1, ls] = z_r[tb - 1:tb]
    prev_ref[1:2, ls] = z_k[tb - 1:tb]
    prev_ref[2:3, ls] = z_v[tb - 1:tb]

    u = -(w0 + w_lora[:, ls])
    softplus = jnp.maximum(u, 0.0) + jnp.log(1.0 + jnp.exp(-jnp.abs(u)))
    logw = -jnp.exp(-softplus - 0.5) * LOG2E
    a = 1.0 / (1.0 + jnp.exp(-(a0 + a_lora[:, ls])))
    kk = k * k_k
    kk = kk / jnp.maximum(jnp.sqrt(_seg_sum(kk * kk)), 1e-12)
    k_mod = k * (1.0 + (a - 1.0) * k_a)
    bonus = _seg_sum(r * k_mod * r_k) * v

    lw_h, lw_r = _split_bf16(logw)
    lw_m, lw_l = _split_bf16(lw_r)
    cum3 = jnp.dot(cum_op, jnp.concatenate([lw_h, lw_m, lw_l.astype(BF16)], axis=1),
                   preferred_element_type=F32)
    cum = cum3[:, :R_PAIR] + cum3[:, R_PAIR:2 * R_PAIR] + cum3[:, 2 * R_PAIR:]
    tot = jnp.concatenate([jnp.broadcast_to(cum[cc * c + c - 1:cc * c + c], (c, R_PAIR)) for cc in range(ncc)],
                          axis=0)
    p_inv = jnp.exp2(-cum)
    a_t = -kk * jnp.exp2(cum - logw)
    kka = kk * a
    b_t = kka * p_inv
    k_t = k_mod * p_inv
    r_t = r * jnp.exp2(cum)
    p_end = jnp.exp2(tot - cum)
    b_h = kka * p_end
    k_h = k_mod * p_end
    g_diag = jnp.exp2(tot)

    for cc in range(ncc):
      rs = slice(cc * c, (cc + 1) * c)
      for name, val in (("at", a_t), ("rt", r_t), ("bt", b_t), ("kt", k_t), ("bh", b_h), ("kh", k_h),
                        ("vv", v)):
        chains[name].append(stack(val[rs]))
      chains["gd"].append(g_diag[cc * c:cc * c + 1])
    post.append((bonus, gn_g, gn_b))

  nb = RW_P * ncc
  at, rt, bt, kt, bh, kh, vv = (jnp.stack(chains[name]) for name in ("at", "rt", "bt", "kt", "bh", "kh", "vv"))
  at_b, vv_b, bh_b = at.astype(BF16), vv.astype(BF16), bh.astype(BF16)
  bk = jnp.concatenate([bt, kt], axis=1).astype(BF16)
  top = jnp.where(strict2, _bdot_nt(at_b, bk), 0.0)
  lblk = jnp.where(incl2, _bdot_nt(rt.astype(BF16), bk), 0.0)
  nm, mak = top[:, :, :c2], top[:, :, c2:]
  tinv = jnp.where(diag, 1.0, nm)
  npow = nm.astype(BF16)
  for _ in range(5):
    npow = _bdot(npow, npow).astype(BF16)
    tinv = tinv + _bdot(tinv.astype(BF16), npow)
  x1 = _bdot(mak.astype(BF16), vv_b)
  wu_b = _bdot(tinv.astype(BF16), jnp.concatenate([at_b, x1.astype(BF16)], axis=2)).astype(BF16)
  rhs = jnp.concatenate([wu_b, jnp.concatenate([jnp.zeros_like(vv_b), vv_b], axis=2)], axis=1)
  qy = _bdot(lblk.astype(BF16), rhs)
  q_h = (rt + qy[:, :, :c2]).astype(BF16)
  y0 = qy[:, :, c2:]
  uv = jnp.concatenate([wu_b[:, :, c2:], vv_b], axis=1)
  bkh = jnp.concatenate([bh_b, kh.astype(BF16)], axis=1)
  g_m = [(jnp.where(diag, chains["gd"][n], 0.0) + _dot_tn(wu_b[n, :, :c2], bh_b[n])).astype(BF16)
         for n in range(nb)]
  h_m = [_dot_tn(uv[n], bkh[n]) for n in range(nb)]

  states = [s_ref[p] for p in range(RW_P)]
  y_rows = [[] for _ in range(RW_P)]
  for cc in range(ncc):
    for p in range(RW_P):
      n = p * ncc + cc
      s_old_b = states[p].astype(BF16)
      y2 = _dot_nt(q_h[n], s_old_b) + y0[n]
      states[p] = jnp.dot(s_old_b, g_m[n], preferred_element_type=F32) + h_m[n]
      y_rows[p].append(y2[:c] + y2[c:])

  for p in range(RW_P):
    ls = slice(p * R_PAIR, (p + 1) * R_PAIR)
    s_ref[p] = states[p]
    bonus, gn_g, gn_b = post[p]
    y = jnp.concatenate(y_rows[p], axis=0)
    mean = _seg_sum(y) * (1.0 / nh)
    yc = y - mean
    var = _seg_sum(yc * yc) * (1.0 / nh)
    yn = yc * lax.rsqrt(var + GN_EPS) * gn_g + gn_b
    g = gr_ref[:, ls].astype(F32)
    o_ref[:, ls] = ((yn + bonus) * (g / (1.0 + jnp.exp(-g)))).astype(o_ref.dtype)


def _rwkv(z_x, lo_x, z_mp, lo_mp, pvec, mu_l, wup_h, wup_l, aup):
  b, s, _ = z_x.shape
  tb = RW_TB
  nt = s // tb + 1
  pw = R_PAIR * RW_P

  def xmap(col):
    return lambda bi, hp, ti: (bi, jnp.maximum(ti - 1, 0), col // pw + hp)

  def mmap(col):
    return lambda bi, hp, ti: (0, col // pw + hp)

  return pl.pallas_call(
      _rwkv_kernel,
      name="rwkv7",
      grid=(b, R_PAIRS // RW_P, nt),
      in_specs=[
          pl.BlockSpec((None, tb, pw), xmap(Z_RR)),
          pl.BlockSpec((None, tb, pw), xmap(Z_RK)),
          pl.BlockSpec((None, tb, pw), xmap(Z_RV)),
          pl.BlockSpec((None, tb, LORA_PAD), lambda bi, hp, ti: (bi, jnp.maximum(ti - 1, 0), 0)),
          pl.BlockSpec((None, tb, pw), xmap(Z_GR)),
          pl.BlockSpec((tb, pw), mmap(Z_RR)),
          pl.BlockSpec((tb, pw), mmap(Z_RK)),
          pl.BlockSpec((tb, pw), mmap(Z_RV)),
          pl.BlockSpec((tb, LORA_PAD), lambda bi, hp, ti: (0, 0)),
          pl.BlockSpec((16, pw), lambda bi, hp, ti: (0, hp)),
          pl.BlockSpec((1, LORA_PAD), lambda bi, hp, ti: (0, 0)),
          pl.BlockSpec((LORA_PAD, pw), lambda bi, hp, ti: (0, hp)),
          pl.BlockSpec((LORA_PAD, pw), lambda bi, hp, ti: (0, hp)),
          pl.BlockSpec((LORA_PAD, pw), lambda bi, hp, ti: (0, hp)),
      ],
      out_specs=pl.BlockSpec((None, tb, pw), lambda bi, hp, ti: (bi, jnp.maximum(ti - 1, 0), hp)),
      out_shape=jax.ShapeDtypeStruct((b, s, R_WIDTH), BF16),
      scratch_shapes=[
          pltpu.VMEM((RW_P, 2 * R_HEAD, 2 * R_HEAD), F32),
          pltpu.VMEM((8, pw), F32),
          pltpu.VMEM((1, LORA_PAD), F32),
      ],
      compiler_params=pltpu.CompilerParams(
          dimension_semantics=("parallel", "parallel", "arbitrary"),
          vmem_limit_bytes=VMEM_LIMIT),
  )(z_x, z_x, z_x, lo_x, z_x, z_mp, z_mp, z_mp, lo_mp, pvec, mu_l, wup_h, wup_l, aup)


def _out_kernel(x_ref, oa_ref, or_ref, wa_ref, wr_ref, ge_ref, be_ref, gp_ref, bp_ref, o_ref):
  h = _ln_rows(x_ref[...], ge_ref[...], be_ref[...])
  y = (jnp.dot(oa_ref[...], wa_ref[...], preferred_element_type=F32)
       + jnp.dot(or_ref[...], wr_ref[...], preferred_element_type=F32))
  o_ref[...] = _ln_rows(DEEPNORM_ALPHA * h + y, gp_ref[...], bp_ref[...])


def _out_proj(x2d, oa, orw, wa, wr, ge, be, gp, bp, tm):
  m, d = x2d.shape
  vec = pl.BlockSpec((1, d), lambda i: (0, 0))
  return pl.pallas_call(
      _out_kernel,
      name="out_proj",
      grid=(m // tm,),
      in_specs=[
          pl.BlockSpec((tm, d), lambda i: (i, 0)),
          pl.BlockSpec((tm, A_WIDTH), lambda i: (i, 0)),
          pl.BlockSpec((tm, R_WIDTH), lambda i: (i, 0)),
          pl.BlockSpec((A_WIDTH, d), lambda i: (0, 0), pipeline_mode=pl.Buffered(1)),
          pl.BlockSpec((R_WIDTH, d), lambda i: (0, 0), pipeline_mode=pl.Buffered(1)),
          vec, vec, vec, vec,
      ],
      out_specs=pl.BlockSpec((tm, d), lambda i: (i, 0)),
      out_shape=jax.ShapeDtypeStruct((m, d), F32),
      compiler_params=pltpu.CompilerParams(
          dimension_semantics=("parallel",),
          vmem_limit_bytes=VMEM_LIMIT),
  )(x2d, oa, orw, wa, wr, ge, be, gp, bp)


def kernel(x, meta_tokens, ln_emb_g, ln_emb_b, rel_bias, w_in, w_out, lambda_q1, lambda_k1, lambda_q2,
           lambda_k2, subln_g, rw_mu, rw_w0, rw_w_up, rw_a0, rw_a_up, rw_k_k, rw_k_a, rw_r_k, rw_gn_g,
           rw_gn_b, ln_post_g, ln_post_b):
  b, s, d = x.shape
  l = 0
  wi = w_in[l]
  c_lo = 4 * A_WIDTH + 3 * R_WIDTH
  c_gr = c_lo + DECAY_LORA + ICLR_LORA
  lora_pad = LORA_PAD - DECAY_LORA - ICLR_LORA
  c_v = 2 * A_WIDTH
  w_all = wi.astype(BF16)
  w_gr = wi[:, c_gr:].astype(BF16)
  w_vt = wi[:, c_v:c_v + A_WIDTH].T.astype(BF16)
  w_lora = jnp.pad(wi[:, c_lo:c_gr], ((0, 0), (0, lora_pad))).astype(BF16)

  ge, be = ln_emb_g.reshape(1, d), ln_emb_b.reshape(1, d)
  x2d = x.reshape(b * s, d)
  z_x, vt_x, lo_x, z_m, vt_m, lo_m = _ln_matmul(x2d, meta_tokens, ge, be, w_all, w_gr, w_vt, w_lora, 1024)
  z_x, lo_x = z_x.reshape(b, s, -1), lo_x.reshape(b, s, LORA_PAD)

  bias_d, bias_s, bias_m = _bias_tiles(rel_bias)
  lam_p = jnp.stack([lambda_q1[l], lambda_k1[l], lambda_q2[l], lambda_k2[l]], axis=0)
  o_attn = _attention(z_x, vt_x, z_m, vt_m, bias_d, bias_s, bias_m, lam_p, subln_g[l].reshape(1, A_V_DIM))

  mu = rw_mu[l]
  zeros = jnp.zeros((R_WIDTH,), F32)
  pvec = jnp.stack([mu[:R_WIDTH], mu[R_WIDTH:2 * R_WIDTH], mu[2 * R_WIDTH:3 * R_WIDTH], rw_w0[l], rw_a0[l],
                    rw_k_k[l], rw_k_a[l], rw_r_k[l].reshape(R_WIDTH), rw_gn_g[l], rw_gn_b[l]]
                   + [zeros] * 6, axis=0)
  mu_l = jnp.pad(mu[3 * R_WIDTH:], (0, lora_pad)).reshape(1, LORA_PAD)
  wup = jnp.pad(rw_w_up[l], ((0, LORA_PAD - DECAY_LORA), (0, 0)))
  wup_h = wup.astype(BF16)
  wup_l = (wup - wup_h.astype(F32)).astype(BF16)
  aup = jnp.pad(rw_a_up[l], ((DECAY_LORA, lora_pad), (0, 0))).astype(BF16)
  front = ((RW_TB - N_META, 0), (0, 0))
  o_rwkv = _rwkv(z_x, lo_x, jnp.pad(z_m, front), jnp.pad(lo_m, front), pvec, mu_l, wup_h, wup_l, aup)

  wo = w_out[l].astype(BF16)
  out = _out_proj(x2d, o_attn.reshape(b * s, A_WIDTH), o_rwkv.reshape(b * s, R_WIDTH),
                  wo[:A_WIDTH], wo[A_WIDTH:], ge, be,
                  ln_post_g[l].reshape(1, d), ln_post_b[l].reshape(1, d), 512)
  return out.reshape(b, s, d)
```

```python
import math

import numpy as np
import jax
import jax.numpy as jnp
from jax import lax
from jax.experimental import pallas as pl
from jax.experimental.pallas import tpu as pltpu

N_META = 16
A_HEADS = 8
A_V_DIM = 128
A_QK_DIM = 64
A_WIDTH = A_HEADS * A_V_DIM
R_HEAD = 64
R_WIDTH = 1024
R_PAIR = 2 * R_HEAD
R_PAIRS = R_WIDTH // R_PAIR
A_QK_W = 2 * A_QK_DIM
DECAY_LORA = 96
ICLR_LORA = 96
LORA_PAD = 256
N_BUCKETS = 32
MAX_DISTANCE = 128
LN_EPS = 1e-5
SUBLN_EPS = 1e-5
GN_EPS = 64e-5
DEPTH = 1
DEEPNORM_ALPHA = (2 * DEPTH) ** 0.25
LAM_INIT = 0.8 - 0.6 * math.exp(-0.3 * 0)
NEG = -1e30

ATT_T = 256
ATT_G = 8
ONES_ROWS = 16
IN_TN = 1024
Z_Q, Z_K, Z_GA, Z_RR, Z_RK, Z_RV, Z_GR = (i * 1024 for i in range(7))
LOG2E = math.log2(math.e)
Q_SCALE = A_QK_DIM ** -0.5 * LOG2E
RW_TB = 128
RW_C = 64
RW_P = 8
VMEM_LIMIT = 56 * 1024 * 1024

F32 = jnp.float32
BF16 = jnp.bfloat16


def _ln_rows(x, g, b):
  mu = jnp.mean(x, axis=-1, keepdims=True)
  xc = x - mu
  var = jnp.mean(xc * xc, axis=-1, keepdims=True)
  return xc * lax.rsqrt(var + LN_EPS) * g + b


def _ln_mm_kernel(x_ref, g_ref, b_ref, wm_ref, wgr_ref, wvt_ref, wl_ref, om_ref, ovt_ref, ol_ref, hn_ref):
  j = pl.program_id(1)
  n_main = pl.num_programs(1) - 1

  @pl.when(j == 0)
  def _():
    hn_ref[...] = _ln_rows(x_ref[...], g_ref[...], b_ref[...]).astype(BF16)

  @pl.when(j < n_main - 1)
  def _():
    scale = jnp.where(j == Z_Q // IN_TN, Q_SCALE, 1.0)
    z = jnp.dot(hn_ref[...], wm_ref[...], preferred_element_type=F32)
    om_ref[...] = (z * scale).astype(om_ref.dtype)

  @pl.when(j == n_main - 1)
  def _():
    zt = _dot_nt(wvt_ref[...], hn_ref[...])
    tv = ovt_ref.shape[2]
    for c in range(ovt_ref.shape[0]):
      ovt_ref[c] = zt[:, c * tv:(c + 1) * tv].astype(ovt_ref.dtype)

  @pl.when(j == n_main)
  def _():
    om_ref[...] = jnp.dot(hn_ref[...], wgr_ref[...], preferred_element_type=F32).astype(om_ref.dtype)
    ol_ref[...] = jnp.dot(hn_ref[...], wl_ref[...], preferred_element_type=F32)


def _ln_matmul(x2d, g, b, w_all, w_gr, w_vt, w_lora, tm):
  m, d = x2d.shape
  tn = IN_TN
  nj = Z_GR // tn + 1
  n = nj * tn
  last = nj - 1
  tv = min(ATT_T, tm)
  v_tile = 2 * A_WIDTH // tn
  once = dict(pipeline_mode=pl.Buffered(1))

  def w_map(i, j):
    jj = jnp.minimum(j, last - 1)
    return (0, jnp.where(jj >= v_tile, jj + 1, jj))

  return pl.pallas_call(
      _ln_mm_kernel,
      name="ln_inproj",
      grid=(m // tm, nj + 1),
      in_specs=[
          pl.BlockSpec((tm, d), lambda i, j: (i, 0)),
          pl.BlockSpec((1, d), lambda i, j: (0, 0)),
          pl.BlockSpec((1, d), lambda i, j: (0, 0)),
          pl.BlockSpec((d, tn), w_map),
          pl.BlockSpec((d, tn), lambda i, j: (0, 0), **once),
          pl.BlockSpec((A_WIDTH, d), lambda i, j: (0, 0), **once),
          pl.BlockSpec((d, LORA_PAD), lambda i, j: (0, 0), **once),
      ],
      out_specs=[
          pl.BlockSpec((tm, tn), lambda i, j: (i, jnp.where(j > last, last, jnp.minimum(j, last - 1)))),
          pl.BlockSpec((tm // tv, A_WIDTH, tv), lambda i, j: (i, 0, 0)),
          pl.BlockSpec((tm, LORA_PAD), lambda i, j: (i, 0)),
      ],
      out_shape=[
          jax.ShapeDtypeStruct((m, n), BF16),
          jax.ShapeDtypeStruct((m // tv, A_WIDTH, tv), BF16),
          jax.ShapeDtypeStruct((m, LORA_PAD), F32),
      ],
      scratch_shapes=[pltpu.VMEM((tm, d), BF16)],
      compiler_params=pltpu.CompilerParams(
          dimension_semantics=("parallel", "arbitrary"),
          vmem_limit_bytes=VMEM_LIMIT),
  )(x2d, g, b, w_all, w_gr, w_vt, w_lora)


def _bucket_thresholds():
  n = np.arange(0, 4 * MAX_DISTANCE, dtype=np.int64)
  max_exact = N_BUCKETS // 2
  nf = np.maximum(n, 1).astype(np.float32)
  large = max_exact + (np.log(nf / np.float32(max_exact)) / np.float32(math.log(MAX_DISTANCE / max_exact))
                       * np.float32(N_BUCKETS - max_exact)).astype(np.int32)
  large = np.minimum(large, N_BUCKETS - 1)
  bucket = np.where(n < max_exact, n, large)
  assert np.all(np.diff(bucket) >= 0) and bucket[-1] == N_BUCKETS - 1
  return [int(np.argmax(bucket >= b)) for b in range(N_BUCKETS)]


_THR = _bucket_thresholds()


def _bias_kernel(rb_ref, diag_ref, sub_ref, meta_ref):
  h = pl.program_id(0)
  far = rb_ref[N_BUCKETS - 1, h]

  def bias_of(n):
    out = jnp.full(n.shape, (rb_ref[0, h] - far) * LOG2E, F32)
    for b in range(1, N_BUCKETS):
      out = jnp.where(n >= _THR[b], (rb_ref[b, h] - far) * LOG2E, out)
    return out

  t = ATT_T
  kj = lax.broadcasted_iota(jnp.int32, (t, t), 0)
  qi = lax.broadcasted_iota(jnp.int32, (t, t), 1)
  d = qi - kj
  diag_ref[...] = jnp.where(d >= 0, bias_of(d), NEG)
  sub_ref[...] = bias_of(d + t)
  km = lax.broadcasted_iota(jnp.int32, (N_META, t), 0)
  qm = lax.broadcasted_iota(jnp.int32, (N_META, t), 1)
  meta_ref[...] = bias_of(qm - km + N_META)


def _bias_tiles(rel_bias):
  t = ATT_T
  return pl.pallas_call(
      _bias_kernel,
      name="bias_tiles",
      grid=(A_HEADS,),
      in_specs=[pl.BlockSpec(memory_space=pltpu.SMEM)],
      out_specs=[
          pl.BlockSpec((None, t, t), lambda h: (h, 0, 0)),
          pl.BlockSpec((None, t, t), lambda h: (h, 0, 0)),
          pl.BlockSpec((None, N_META, t), lambda h: (h, 0, 0)),
      ],
      out_shape=[
          jax.ShapeDtypeStruct((A_HEADS, t, t), F32),
          jax.ShapeDtypeStruct((A_HEADS, t, t), F32),
          jax.ShapeDtypeStruct((A_HEADS, N_META, t), F32),
      ],
  )(rel_bias)


def _dot_nt(a, b):
  return lax.dot_general(a, b, (((1,), (1,)), ((), ())), preferred_element_type=F32)


def _attn_kernel(q_ref, kx_ref, vt_ref, km_ref, vmt_ref, ga_ref, bd_ref, bs_ref, bm_ref,
                 lam_ref, sg_ref, o_ref, m_ref, alpha_ref, acc_ref, pt_ref, ptm_ref):
  t = ATT_T
  g = ATT_G
  nc = 2 * g
  dv = A_V_DIM
  qi = pl.program_id(2)
  nq = pl.num_programs(2) - 1

  def v_tile(j):
    ones = jnp.ones((ONES_ROWS, t), BF16)
    return [jnp.concatenate([vt_ref[j, hh * dv:(hh + 1) * dv, :], ones], axis=0) for hh in range(g)]

  def k_tile(j):
    off = pl.multiple_of(j * t, t)
    return [kx_ref[pl.ds(off, t), hh * A_QK_W:(hh + 1) * A_QK_W] for hh in range(g)]

  def meta_v():
    ones_m = jnp.ones((ONES_ROWS, N_META), BF16)
    return [jnp.concatenate([vmt_ref[0, hh * dv:(hh + 1) * dv, :], ones_m], axis=0) for hh in range(g)]

  def queries():
    lane = lax.broadcasted_iota(jnp.int32, (t, A_QK_W), 1)
    qs = []
    for hh in range(g):
      q = q_ref[:, hh * A_QK_W:(hh + 1) * A_QK_W]
      zero = jnp.zeros_like(q)
      qs += [jnp.where(lane < A_QK_DIM, q, zero), jnp.where(lane >= A_QK_DIM, q, zero)]
    return qs

  def softmax_stage(c, m_prev, s_list):
    m_new = m_prev
    for s in s_list:
      m_new = jnp.maximum(m_new, jnp.max(s, axis=0, keepdims=True))
    m_ref[c] = m_new
    alpha_ref[c] = jnp.exp2(m_prev - m_new)
    return [jnp.exp2(s - m_new).astype(BF16) for s in s_list]

  def pending(segments, c):
    pv = None
    for vts, p_ref in segments:
      d = jnp.dot(vts[c // 2], p_ref[c], preferred_element_type=F32)
      pv = d if pv is None else pv + d
    return pv

  def finish_previous(segments):
    lp = lam_ref[...]
    lam = (jnp.exp(jnp.sum(lp[0:1] * lp[1:2], axis=1, keepdims=True))
           - jnp.exp(jnp.sum(lp[2:3] * lp[3:4], axis=1, keepdims=True)) + LAM_INIT)
    for hh in range(g):
      a0, a1 = (alpha_ref[c] * acc_ref[c] + pending(segments, c) for c in (2 * hh, 2 * hh + 1))
      ot = a0[:dv] / a0[dv:dv + 1] - lam * (a1[:dv] / a1[dv:dv + 1])
      ot = ot * lax.rsqrt(jnp.mean(ot * ot, axis=0, keepdims=True) + SUBLN_EPS)
      o = ot.T * (sg_ref[...] * (1.0 - LAM_INIT))
      gate = ga_ref[:, hh * dv:(hh + 1) * dv].astype(F32)
      o_ref[:, hh * dv:(hh + 1) * dv] = (o * (gate / (1.0 + jnp.exp(-gate)))).astype(o_ref.dtype)

  def start_tile(qs, previous):
    if previous is not None:
      finish_previous(previous)
    ks = k_tile(qi)
    st, sm = [], []
    for c in range(nc):
      hh = c // 2
      st.append(_dot_nt(ks[hh], qs[c]) + bd_ref[hh])
      sm.append(_dot_nt(km_ref[:, hh * A_QK_W:(hh + 1) * A_QK_W], qs[c]) + jnp.where(qi == 0, bm_ref[hh], 0.0))
    acc_ref[...] = jnp.zeros(acc_ref.shape, F32)
    m_start = jnp.full((1, t), NEG, F32)
    for c in range(nc):
      pt_ref[c], ptm_ref[c] = softmax_stage(c, m_start, [st[c], sm[c]])

  def step(qs, j_cur, biases, segments):
    ks = k_tile(j_cur)
    pv, st = [], []
    for c in range(nc):
      pv.append(pending(segments, c))
      s = _dot_nt(ks[c // 2], qs[c])
      st.append(s if biases is None else s + biases[c // 2])
    for c in range(nc):
      acc_ref[c] = alpha_ref[c] * acc_ref[c] + pv[c]
    for c in range(nc):
      pt_ref[c], = softmax_stage(c, m_ref[c], [st[c]])

  def below_diagonal(qs):
    step(qs, qi - 1, [bs_ref[hh] for hh in range(g)], [(v_tile(qi), pt_ref), (meta_v(), ptm_ref)])

  @pl.when(qi == 0)
  def _():
    start_tile(queries(), None)

  @pl.when(qi == 1)
  def _():
    qs = queries()
    start_tile(qs, [(v_tile(0), pt_ref), (meta_v(), ptm_ref)])
    below_diagonal(qs)

  @pl.when(jnp.logical_and(qi >= 2, qi < nq))
  def _():
    qs = queries()
    start_tile(qs, [(v_tile(jnp.maximum(qi - 3, 0)), pt_ref)])
    below_diagonal(qs)
    n_far = qi - 1

    def prev_of(j):
      return jnp.where(j == 0, qi - 1, j - 1)

    def far_body(i, carry):
      step(qs, 2 * i, None, [(v_tile(prev_of(2 * i)), pt_ref)])
      step(qs, 2 * i + 1, None, [(v_tile(2 * i), pt_ref)])
      return carry

    lax.fori_loop(0, n_far // 2, far_body, 0)

    @pl.when(n_far % 2 == 1)
    def _():
      step(qs, n_far - 1, None, [(v_tile(prev_of(n_far - 1)), pt_ref)])

  @pl.when(qi == nq)
  def _():
    finish_previous([(v_tile(nq - 3), pt_ref)])


def _attention(z_x, vt_x, z_m, vt_m, bias_d, bias_s, bias_m, lam_p, subln_g):
  b, s, _ = z_x.shape
  t = ATT_T
  g = ATT_G
  nq = s // t
  assert nq >= 3
  w = A_QK_W * g
  hb = A_HEADS // g
  kb, gb = Z_K // w, Z_GA // w

  def cur(qi):
    return jnp.minimum(qi, nq - 1)

  def prev(qi):
    return jnp.maximum(qi - 1, 0)

  return pl.pallas_call(
      _attn_kernel,
      name="diff_attn",
      grid=(b, hb, nq + 1),
      in_specs=[
          pl.BlockSpec((None, t, w), lambda bi, hi, qi: (bi, cur(qi), hi)),
          pl.BlockSpec((None, s, w), lambda bi, hi, qi: (bi, 0, kb + hi), pipeline_mode=pl.Buffered(1)),
          pl.BlockSpec((None, s // t, w, t), lambda bi, hi, qi: (bi, 0, hi, 0), pipeline_mode=pl.Buffered(1)),
          pl.BlockSpec((N_META, w), lambda bi, hi, qi: (0, kb + hi)),
          pl.BlockSpec((1, w, N_META), lambda bi, hi, qi: (0, hi, 0)),
          pl.BlockSpec((None, t, w), lambda bi, hi, qi: (bi, prev(qi), gb + hi)),
          pl.BlockSpec((g, t, t), lambda bi, hi, qi: (hi, 0, 0)),
          pl.BlockSpec((g, t, t), lambda bi, hi, qi: (hi, 0, 0)),
          pl.BlockSpec((g, N_META, t), lambda bi, hi, qi: (hi, 0, 0)),
          pl.BlockSpec((4, A_QK_DIM), lambda bi, hi, qi: (0, 0)),
          pl.BlockSpec((1, A_V_DIM), lambda bi, hi, qi: (0, 0)),
      ],
      out_specs=pl.BlockSpec((None, t, w), lambda bi, hi, qi: (bi, prev(qi), hi)),
      out_shape=jax.ShapeDtypeStruct((b, s, A_WIDTH), BF16),
      scratch_shapes=[
          pltpu.VMEM((2 * g, 1, t), F32),
          pltpu.VMEM((2 * g, 1, t), F32),
          pltpu.VMEM((2 * g, A_V_DIM + ONES_ROWS, t), F32),
          pltpu.VMEM((2 * g, t, t), BF16),
          pltpu.VMEM((2 * g, N_META, t), BF16),
      ],
      compiler_params=pltpu.CompilerParams(
          dimension_semantics=("parallel", "parallel", "arbitrary"),
          vmem_limit_bytes=VMEM_LIMIT),
  )(z_x, z_x, vt_x.reshape(b, s // t, A_WIDTH, t), z_m, vt_m, z_x, bias_d, bias_s, bias_m, lam_p, subln_g)


def _seg_sum(x):
  lane = lax.broadcasted_iota(jnp.int32, x.shape, 1)
  first = lane < R_HEAD
  lo = jnp.sum(jnp.where(first, x, 0.0), axis=1, keepdims=True)
  hi = jnp.sum(jnp.where(first, 0.0, x), axis=1, keepdims=True)
  return jnp.where(first, lo, hi)


def _split_bf16(x):
  hi = x.astype(BF16)
  return hi, x - hi.astype(F32)


def _dot_tn(a, b):
  return lax.dot_general(a, b, (((0,), (0,)), ((), ())), preferred_element_type=F32)


def _bdot(a, b):
  return lax.dot_general(a, b, (((2,), (1,)), ((0,), (0,))), preferred_element_type=F32)


def _bdot_nt(a, b):
  return lax.dot_general(a, b, (((2,), (2,)), ((0,), (0,))), preferred_element_type=F32)


def _rwkv_kernel(rx_ref, kx_ref, vx_ref, lx_ref, gr_ref, rm_ref, kmt_ref, vmt_ref, lm_ref,
                 pv_ref, mul_ref, wuph_ref, wupl_ref, aup_ref, o_ref, s_ref, prev_ref, prevl_ref):
  tb, c = RW_TB, RW_C
  nh = R_HEAD
  c2 = 2 * c
  ti = pl.program_id(2)
  is_meta = ti == 0

  @pl.when(is_meta)
  def _():
    s_ref[...] = jnp.zeros_like(s_ref)
    prev_ref[...] = jnp.zeros_like(prev_ref)
    prevl_ref[...] = jnp.zeros_like(prevl_ref)

  row = lax.broadcasted_iota(jnp.int32, (tb, R_PAIR), 0)
  rowl = lax.broadcasted_iota(jnp.int32, (tb, LORA_PAD), 0)

  def shifted(z, prev, mu, rows):
    z_prev = jnp.where(rows == 0, prev, pltpu.roll(z, 1, 0))
    return z + (z_prev - z) * mu

  z_l = jnp.where(is_meta, lm_ref[...], lx_ref[...])
  lo = shifted(z_l, prevl_ref[...], mul_ref[...], rowl)
  prevl_ref[...] = z_l[tb - 1:tb]
  th_h, th_l = _split_bf16(jnp.tanh(lo))
  w_lora = (jnp.dot(th_h, wuph_ref[...], preferred_element_type=F32)
            + jnp.dot(th_h, wupl_ref[...], preferred_element_type=F32)
            + jnp.dot(th_l.astype(BF16), wuph_ref[...], preferred_element_type=F32))
  a_lora = jnp.dot(lo.astype(BF16), aup_ref[...], preferred_element_type=F32)

  ii = lax.broadcasted_iota(jnp.int32, (tb, tb), 0)
  jj = lax.broadcasted_iota(jnp.int32, (tb, tb), 1)
  shift = int(math.log2(c))
  same = lax.shift_right_logical(ii, shift) == lax.shift_right_logical(jj, shift)
  cum_op = jnp.where(same, jnp.where(jj <= ii, 1.0, 0.0), 0.0).astype(BF16)

  ci = lax.broadcasted_iota(jnp.int32, (c2, c2), 0)
  cj = lax.broadcasted_iota(jnp.int32, (c2, c2), 1)
  diag = ci == cj
  strict2 = jnp.concatenate([cj < ci, cj < ci], axis=1)
  incl2 = jnp.concatenate([cj <= ci, cj <= ci], axis=1)
  first = lax.broadcasted_iota(jnp.int32, (c, R_PAIR), 1) < nh

  def stack(x):
    return jnp.concatenate([jnp.where(first, x, 0.0), jnp.where(first, 0.0, x)], axis=0)

  ncc = tb // c
  chains = {name: [] for name in ("at", "rt", "bt", "kt", "bh", "kh", "vv", "gd")}
  post = []
  for p in range(RW_P):
    ls = slice(p * R_PAIR, (p + 1) * R_PAIR)
    pv = pv_ref[:, ls]
    mu_r, mu_k, mu_v = pv[0:1], pv[1:2], pv[2:3]
    w0, a0, k_k, k_a, r_k, gn_g, gn_b = pv[3:4], pv[4:5], pv[5:6], pv[6:7], pv[7:8], pv[8:9], pv[9:10]

    z_r = jnp.where(is_meta, rm_ref[:, ls], rx_ref[:, ls]).astype(F32)
    z_k = jnp.where(is_meta, kmt_ref[:, ls], kx_ref[:, ls]).astype(F32)
    z_v = jnp.where(is_meta, vmt_ref[:, ls], vx_ref[:, ls]).astype(F32)
    r = shifted(z_r, prev_ref[0:1, ls], mu_r, row)
    k = shifted(z_k, prev_ref[1:2, ls], mu_k, row)
    v = shifted(z_v, prev_ref[2:3, ls], mu_v, row)
    prev_ref[0:1, ls] = z_r[tb - 1:tb]
    prev_ref[1:2, ls] = z_k[tb - 1:tb]
    prev_ref[2:3, ls] = z_v[tb - 1:tb]

    u = -(w0 + w_lora[:, ls])
    softplus = jnp.maximum(u, 0.0) + jnp.log(1.0 + jnp.exp(-jnp.abs(u)))
    logw = -jnp.exp(-softplus - 0.5) * LOG2E
    a = 1.0 / (1.0 + jnp.exp(-(a0 + a_lora[:, ls])))
    kk = k * k_k
    kk = kk / jnp.maximum(jnp.sqrt(_seg_sum(kk * kk)), 1e-12)
    k_mod = k * (1.0 + (a - 1.0) * k_a)
    bonus = _seg_sum(r * k_mod * r_k) * v

    lw_h, lw_r = _split_bf16(logw)
    lw_m, lw_l = _split_bf16(lw_r)
    cum3 = jnp.dot(cum_op, jnp.concatenate([lw_h, lw_m, lw_l.astype(BF16)], axis=1),
                   preferred_element_type=F32)
    cum = cum3[:, :R_PAIR] + cum3[:, R_PAIR:2 * R_PAIR] + cum3[:, 2 * R_PAIR:]
    tot = jnp.concatenate([jnp.broadcast_to(cum[cc * c + c - 1:cc * c + c], (c, R_PAIR)) for cc in range(ncc)],
                          axis=0)
    p_inv = jnp.exp2(-cum)
    a_t = -kk * jnp.exp2(cum - logw)
    kka = kk * a
    b_t = kka * p_inv
    k_t = k_mod * p_inv
    r_t = r * jnp.exp2(cum)
    p_end = jnp.exp2(tot - cum)
    b_h = kka * p_end
    k_h = k_mod * p_end
    g_diag = jnp.exp2(tot)

    for cc in range(ncc):
      rs = slice(cc * c, (cc + 1) * c)
      for name, val in (("at", a_t), ("rt", r_t), ("bt", b_t), ("kt", k_t), ("bh", b_h), ("kh", k_h),
                        ("vv", v)):
        chains[name].append(stack(val[rs]))
      chains["gd"].append(g_diag[cc * c:cc * c + 1])
    post.append((bonus, gn_g, gn_b))

  nb = RW_P * ncc
  at, rt, bt, kt, bh, kh, vv = (jnp.stack(chains[name]) for name in ("at", "rt", "bt", "kt", "bh", "kh", "vv"))
  at_b, vv_b, bh_b = at.astype(BF16), vv.astype(BF16), bh.astype(BF16)
  bk = jnp.concatenate([bt, kt], axis=1).astype(BF16)
  top = jnp.where(strict2, _bdot_nt(at_b, bk), 0.0)
  lblk = jnp.where(incl2, _bdot_nt(rt.astype(BF16), bk), 0.0)
  nm, mak = top[:, :, :c2], top[:, :, c2:]
  tinv = jnp.where(diag, 1.0, nm)
  npow = nm.astype(BF16)
  for _ in range(5):
    npow = _bdot(npow, npow).astype(BF16)
    tinv = tinv + _bdot(tinv.astype(BF16), npow)
  x1 = _bdot(mak.astype(BF16), vv_b)
  wu_b = _bdot(tinv.astype(BF16), jnp.concatenate([at_b, x1.astype(BF16)], axis=2)).astype(BF16)
  rhs = jnp.concatenate([wu_b, jnp.concatenate([jnp.zeros_like(vv_b), vv_b], axis=2)], axis=1)
  qy = _bdot(lblk.astype(BF16), rhs)
  q_h = (rt + qy[:, :, :c2]).astype(BF16)
  y0 = qy[:, :, c2:]
  uv = jnp.concatenate([wu_b[:, :, c2:], vv_b], axis=1)
  bkh = jnp.concatenate([bh_b, kh.astype(BF16)], axis=1)
  g_m = [(jnp.where(diag, chains["gd"][n], 0.0) + _dot_tn(wu_b[n, :, :c2], bh_b[n])).astype(BF16)
         for n in range(nb)]
  h_m = [_dot_tn(uv[n], bkh[n]) for n in range(nb)]

  states = [s_ref[p] for p in range(RW_P)]
  y_rows = [[] for _ in range(RW_P)]
  for cc in range(ncc):
    for p in range(RW_P):
      n = p * ncc + cc
      s_old_b = states[p].astype(BF16)
      y2 = _dot_nt(q_h[n], s_old_b) + y0[n]
      states[p] = jnp.dot(s_old_b, g_m[n], preferred_element_type=F32) + h_m[n]
      y_rows[p].append(y2[:c] + y2[c:])

  for p in range(RW_P):
    ls = slice(p * R_PAIR, (p + 1) * R_PAIR)
    s_ref[p] = states[p]
    bonus, gn_g, gn_b = post[p]
    y = jnp.concatenate(y_rows[p], axis=0)
    mean = _seg_sum(y) * (1.0 / nh)
    yc = y - mean
    var = _seg_sum(yc * yc) * (1.0 / nh)
    yn = yc * lax.rsqrt(var + GN_EPS) * gn_g + gn_b
    g = gr_ref[:, ls].astype(F32)
    o_ref[:, ls] = ((yn + bonus) * (g / (1.0 + jnp.exp(-g)))).astype(o_ref.dtype)


def _rwkv(z_x, lo_x, z_mp, lo_mp, pvec, mu_l, wup_h, wup_l, aup):
  b, s, _ = z_x.shape
  tb = RW_TB
  nt = s // tb + 1
  pw = R_PAIR * RW_P

  def xmap(col):
    return lambda bi, hp, ti: (bi, jnp.maximum(ti - 1, 0), col // pw + hp)

  def mmap(col):
    return lambda bi, hp, ti: (0, col // pw + hp)

  return pl.pallas_call(
      _rwkv_kernel,
      name="rwkv7",
      grid=(b, R_PAIRS // RW_P, nt),
      in_specs=[
          pl.BlockSpec((None, tb, pw), xmap(Z_RR)),
          pl.BlockSpec((None, tb, pw), xmap(Z_RK)),
          pl.BlockSpec((None, tb, pw), xmap(Z_RV)),
          pl.BlockSpec((None, tb, LORA_PAD), lambda bi, hp, ti: (bi, jnp.maximum(ti - 1, 0), 0)),
          pl.BlockSpec((None, tb, pw), xmap(Z_GR)),
          pl.BlockSpec((tb, pw), mmap(Z_RR)),
          pl.BlockSpec((tb, pw), mmap(Z_RK)),
          pl.BlockSpec((tb, pw), mmap(Z_RV)),
          pl.BlockSpec((tb, LORA_PAD), lambda bi, hp, ti: (0, 0)),
          pl.BlockSpec((16, pw), lambda bi, hp, ti: (0, hp)),
          pl.BlockSpec((1, LORA_PAD), lambda bi, hp, ti: (0, 0)),
          pl.BlockSpec((LORA_PAD, pw), lambda bi, hp, ti: (0, hp)),
          pl.BlockSpec((LORA_PAD, pw), lambda bi, hp, ti: (0, hp)),
          pl.BlockSpec((LORA_PAD, pw), lambda bi, hp, ti: (0, hp)),
      ],
      out_specs=pl.BlockSpec((None, tb, pw), lambda bi, hp, ti: (bi, jnp.maximum(ti - 1, 0), hp)),
      out_shape=jax.ShapeDtypeStruct((b, s, R_WIDTH), BF16),
      scratch_shapes=[
          pltpu.VMEM((RW_P, 2 * R_HEAD, 2 * R_HEAD), F32),
          pltpu.VMEM((8, pw), F32),
          pltpu.VMEM((1, LORA_PAD), F32),
      ],
      compiler_params=pltpu.CompilerParams(
          dimension_semantics=("parallel", "parallel", "arbitrary"),
          vmem_limit_bytes=VMEM_LIMIT),
  )(z_x, z_x, z_x, lo_x, z_x, z_mp, z_mp, z_mp, lo_mp, pvec, mu_l, wup_h, wup_l, aup)


def _out_kernel(x_ref, oa_ref, or_ref, wa_ref, wr_ref, ge_ref, be_ref, gp_ref, bp_ref, o_ref):
  h = _ln_rows(x_ref[...], ge_ref[...], be_ref[...])
  y = (jnp.dot(oa_ref[...], wa_ref[...], preferred_element_type=F32)
       + jnp.dot(or_ref[...], wr_ref[...], preferred_element_type=F32))
  o_ref[...] = _ln_rows(DEEPNORM_ALPHA * h + y, gp_ref[...], bp_ref[...])


def _out_proj(x2d, oa, orw, wa, wr, ge, be, gp, bp, tm):
  m, d = x2d.shape
  vec = pl.BlockSpec((1, d), lambda i: (0, 0))
  return pl.pallas_call(
      _out_kernel,
      name="out_proj",
      grid=(m // tm,),
      in_specs=[
          pl.BlockSpec((tm, d), lambda i: (i, 0)),
          pl.BlockSpec((tm, A_WIDTH), lambda i: (i, 0)),
          pl.BlockSpec((tm, R_WIDTH), lambda i: (i, 0)),
          pl.BlockSpec((A_WIDTH, d), lambda i: (0, 0), pipeline_mode=pl.Buffered(1)),
          pl.BlockSpec((R_WIDTH, d), lambda i: (0, 0), pipeline_mode=pl.Buffered(1)),
          vec, vec, vec, vec,
      ],
      out_specs=pl.BlockSpec((tm, d), lambda i: (i, 0)),
      out_shape=jax.ShapeDtypeStruct((m, d), F32),
      compiler_params=pltpu.CompilerParams(
          dimension_semantics=("parallel",),
          vmem_limit_bytes=VMEM_LIMIT),
  )(x2d, oa, orw, wa, wr, ge, be, gp, bp)


def kernel(x, meta_tokens, ln_emb_g, ln_emb_b, rel_bias, w_in, w_out, lambda_q1, lambda_k1, lambda_q2,
           lambda_k2, subln_g, rw_mu, rw_w0, rw_w_up, rw_a0, rw_a_up, rw_k_k, rw_k_a, rw_r_k, rw_gn_g,
           rw_gn_b, ln_post_g, ln_post_b):
  b, s, d = x.shape
  l = 0
  wi = w_in[l]
  c_lo = 4 * A_WIDTH + 3 * R_WIDTH
  c_gr = c_lo + DECAY_LORA + ICLR_LORA
  lora_pad = LORA_PAD - DECAY_LORA - ICLR_LORA
  c_v = 2 * A_WIDTH
  w_all = wi.astype(BF16)
  w_gr = wi[:, c_gr:].astype(BF16)
  w_vt = wi[:, c_v:c_v + A_WIDTH].T.astype(BF16)
  w_lora = jnp.pad(wi[:, c_lo:c_gr], ((0, 0), (0, lora_pad))).astype(BF16)

  ge, be = ln_emb_g.reshape(1, d), ln_emb_b.reshape(1, d)
  x2d = x.reshape(b * s, d)
  z_x, vt_x, lo_x = _ln_matmul(x2d, ge, be, w_all, w_gr, w_vt, w_lora, 1024)
  z_x, lo_x = z_x.reshape(b, s, -1), lo_x.reshape(b, s, LORA_PAD)
  z_m, vt_m, lo_m = _ln_matmul(meta_tokens, ge, be, w_all, w_gr, w_vt, w_lora, N_META)

  bias_d, bias_s, bias_m = _bias_tiles(rel_bias)
  lam_p = jnp.stack([lambda_q1[l], lambda_k1[l], lambda_q2[l], lambda_k2[l]], axis=0)
  o_attn = _attention(z_x, vt_x, z_m, vt_m, bias_d, bias_s, bias_m, lam_p, subln_g[l].reshape(1, A_V_DIM))

  mu = rw_mu[l]
  zeros = jnp.zeros((R_WIDTH,), F32)
  pvec = jnp.stack([mu[:R_WIDTH], mu[R_WIDTH:2 * R_WIDTH], mu[2 * R_WIDTH:3 * R_WIDTH], rw_w0[l], rw_a0[l],
                    rw_k_k[l], rw_k_a[l], rw_r_k[l].reshape(R_WIDTH), rw_gn_g[l], rw_gn_b[l]]
                   + [zeros] * 6, axis=0)
  mu_l = jnp.pad(mu[3 * R_WIDTH:], (0, lora_pad)).reshape(1, LORA_PAD)
  wup = jnp.pad(rw_w_up[l], ((0, LORA_PAD - DECAY_LORA), (0, 0)))
  wup_h = wup.astype(BF16)
  wup_l = (wup - wup_h.astype(F32)).astype(BF16)
  aup = jnp.pad(rw_a_up[l], ((DECAY_LORA, lora_pad), (0, 0))).astype(BF16)
  front = ((RW_TB - N_META, 0), (0, 0))
  o_rwkv = _rwkv(z_x, lo_x, jnp.pad(z_m, front), jnp.pad(lo_m, front), pvec, mu_l, wup_h, wup_l, aup)

  wo = w_out[l].astype(BF16)
  out = _out_proj(x2d, o_attn.reshape(b * s, A_WIDTH), o_rwkv.reshape(b * s, R_WIDTH),
                  wo[:A_WIDTH], wo[A_WIDTH:], ge, be,
                  ln_post_g[l].reshape(1, d), ln_post_b[l].reshape(1, d), 512)
  return out.reshape(b, s, d)
```

```python
import math

import numpy as np
import jax
import jax.numpy as jnp
from jax import lax
from jax.experimental import pallas as pl
from jax.experimental.pallas import tpu as pltpu

N_META = 16
A_HEADS = 8
A_V_DIM = 128
A_QK_DIM = 64
A_WIDTH = A_HEADS * A_V_DIM
R_HEAD = 64
R_WIDTH = 1024
R_PAIR = 2 * R_HEAD
R_PAIRS = R_WIDTH // R_PAIR
A_QK_W = 2 * A_QK_DIM
DECAY_LORA = 96
ICLR_LORA = 96
LORA_PAD = 256
N_BUCKETS = 32
MAX_DISTANCE = 128
LN_EPS = 1e-5
SUBLN_EPS = 1e-5
GN_EPS = 64e-5
DEPTH = 1
DEEPNORM_ALPHA = (2 * DEPTH) ** 0.25
LAM_INIT = 0.8 - 0.6 * math.exp(-0.3 * 0)
NEG = -1e30

ATT_T = 256
ATT_G = 8
ONES_ROWS = 16
IN_TN = 1024
Z_Q, Z_K, Z_GA, Z_RR, Z_RK, Z_RV, Z_GR = (i * 1024 for i in range(7))
LOG2E = math.log2(math.e)
Q_SCALE = A_QK_DIM ** -0.5 * LOG2E
RW_TB = 128
RW_C = 64
RW_P = 8
VMEM_LIMIT = 56 * 1024 * 1024

F32 = jnp.float32
BF16 = jnp.bfloat16


def _ln_rows(x, g, b):
  mu = jnp.mean(x, axis=-1, keepdims=True)
  xc = x - mu
  var = jnp.mean(xc * xc, axis=-1, keepdims=True)
  return xc * lax.rsqrt(var + LN_EPS) * g + b


def _ln_mm_kernel(x_ref, g_ref, b_ref, wm_ref, wgr_ref, wvt_ref, wl_ref, om_ref, ovt_ref, ol_ref, hn_ref):
  j = pl.program_id(1)
  n_main = pl.num_programs(1) - 1

  @pl.when(j == 0)
  def _():
    hn_ref[...] = _ln_rows(x_ref[...], g_ref[...], b_ref[...]).astype(BF16)

  @pl.when(j < n_main - 1)
  def _():
    scale = jnp.where(j == Z_Q // IN_TN, Q_SCALE, 1.0)
    z = _dot_nt(hn_ref[...], wm_ref[...])
    om_ref[...] = (z * scale).astype(om_ref.dtype)

  @pl.when(j == n_main - 1)
  def _():
    om_ref[...] = _dot_nt(hn_ref[...], wgr_ref[...]).astype(om_ref.dtype)
    ol_ref[...] = _dot_nt(hn_ref[...], wl_ref[...])

  @pl.when(j == n_main)
  def _():
    zt = _dot_nt(wvt_ref[...], hn_ref[...])
    tv = ovt_ref.shape[2]
    for c in range(ovt_ref.shape[0]):
      ovt_ref[c] = zt[:, c * tv:(c + 1) * tv].astype(ovt_ref.dtype)


def _ln_matmul(x2d, g, b, wt_all, wt_gr, wt_lora, tm):
  m, d = x2d.shape
  tn = IN_TN
  nj = Z_GR // tn + 1
  n = nj * tn
  last = nj - 1
  tv = min(ATT_T, tm)
  v_tile = 2 * A_WIDTH // tn
  once = dict(pipeline_mode=pl.Buffered(1))

  def w_map(i, j):
    jj = jnp.minimum(j, last - 1)
    return (jnp.where(jj >= v_tile, jj + 1, jj), 0)

  return pl.pallas_call(
      _ln_mm_kernel,
      name="ln_inproj",
      grid=(m // tm, nj + 1),
      in_specs=[
          pl.BlockSpec((tm, d), lambda i, j: (i, 0)),
          pl.BlockSpec((1, d), lambda i, j: (0, 0)),
          pl.BlockSpec((1, d), lambda i, j: (0, 0)),
          pl.BlockSpec((tn, d), w_map),
          pl.BlockSpec((tn, d), lambda i, j: (0, 0), **once),
          pl.BlockSpec((A_WIDTH, d), lambda i, j: (v_tile, 0), **once),
          pl.BlockSpec((LORA_PAD, d), lambda i, j: (0, 0), **once),
      ],
      out_specs=[
          pl.BlockSpec((tm, tn), lambda i, j: (i, jnp.minimum(j, last))),
          pl.BlockSpec((tm // tv, A_WIDTH, tv), lambda i, j: (i, 0, 0)),
          pl.BlockSpec((tm, LORA_PAD), lambda i, j: (i, 0)),
      ],
      out_shape=[
          jax.ShapeDtypeStruct((m, n), BF16),
          jax.ShapeDtypeStruct((m // tv, A_WIDTH, tv), BF16),
          jax.ShapeDtypeStruct((m, LORA_PAD), F32),
      ],
      scratch_shapes=[pltpu.VMEM((tm, d), BF16)],
      compiler_params=pltpu.CompilerParams(
          dimension_semantics=("parallel", "arbitrary"),
          vmem_limit_bytes=VMEM_LIMIT),
  )(x2d, g, b, wt_all, wt_gr, wt_all, wt_lora)


def _bucket_thresholds():
  n = np.arange(0, 4 * MAX_DISTANCE, dtype=np.int64)
  max_exact = N_BUCKETS // 2
  nf = np.maximum(n, 1).astype(np.float32)
  large = max_exact + (np.log(nf / np.float32(max_exact)) / np.float32(math.log(MAX_DISTANCE / max_exact))
                       * np.float32(N_BUCKETS - max_exact)).astype(np.int32)
  large = np.minimum(large, N_BUCKETS - 1)
  bucket = np.where(n < max_exact, n, large)
  assert np.all(np.diff(bucket) >= 0) and bucket[-1] == N_BUCKETS - 1
  return [int(np.argmax(bucket >= b)) for b in range(N_BUCKETS)]


_THR = _bucket_thresholds()


def _bias_kernel(rb_ref, diag_ref, sub_ref, meta_ref):
  h = pl.program_id(0)
  far = rb_ref[N_BUCKETS - 1, h]

  def bias_of(n):
    out = jnp.full(n.shape, (rb_ref[0, h] - far) * LOG2E, F32)
    for b in range(1, N_BUCKETS):
      out = jnp.where(n >= _THR[b], (rb_ref[b, h] - far) * LOG2E, out)
    return out

  t = ATT_T
  kj = lax.broadcasted_iota(jnp.int32, (t, t), 0)
  qi = lax.broadcasted_iota(jnp.int32, (t, t), 1)
  d = qi - kj
  diag_ref[...] = jnp.where(d >= 0, bias_of(d), NEG)
  sub_ref[...] = bias_of(d + t)
  km = lax.broadcasted_iota(jnp.int32, (N_META, t), 0)
  qm = lax.broadcasted_iota(jnp.int32, (N_META, t), 1)
  meta_ref[...] = bias_of(qm - km + N_META)


def _bias_tiles(rel_bias):
  t = ATT_T
  return pl.pallas_call(
      _bias_kernel,
      name="bias_tiles",
      grid=(A_HEADS,),
      in_specs=[pl.BlockSpec(memory_space=pltpu.SMEM)],
      out_specs=[
          pl.BlockSpec((None, t, t), lambda h: (h, 0, 0)),
          pl.BlockSpec((None, t, t), lambda h: (h, 0, 0)),
          pl.BlockSpec((None, N_META, t), lambda h: (h, 0, 0)),
      ],
      out_shape=[
          jax.ShapeDtypeStruct((A_HEADS, t, t), F32),
          jax.ShapeDtypeStruct((A_HEADS, t, t), F32),
          jax.ShapeDtypeStruct((A_HEADS, N_META, t), F32),
      ],
  )(rel_bias)


def _dot_nt(a, b):
  return lax.dot_general(a, b, (((1,), (1,)), ((), ())), preferred_element_type=F32)


def _attn_kernel(q_ref, kx_ref, vt_ref, km_ref, vmt_ref, ga_ref, bd_ref, bs_ref, bm_ref,
                 lam_ref, sg_ref, o_ref, m_ref, alpha_ref, acc_ref, pt_ref, ptm_ref):
  t = ATT_T
  g = ATT_G
  nc = 2 * g
  dv = A_V_DIM
  qi = pl.program_id(2)
  nq = pl.num_programs(2) - 1

  def v_tile(j):
    ones = jnp.ones((ONES_ROWS, t), BF16)
    return [jnp.concatenate([vt_ref[j, hh * dv:(hh + 1) * dv, :], ones], axis=0) for hh in range(g)]

  def k_tile(j):
    off = pl.multiple_of(j * t, t)
    return [kx_ref[pl.ds(off, t), hh * A_QK_W:(hh + 1) * A_QK_W] for hh in range(g)]

  def meta_v():
    ones_m = jnp.ones((ONES_ROWS, N_META), BF16)
    return [jnp.concatenate([vmt_ref[0, hh * dv:(hh + 1) * dv, :], ones_m], axis=0) for hh in range(g)]

  def queries():
    lane = lax.broadcasted_iota(jnp.int32, (t, A_QK_W), 1)
    qs = []
    for hh in range(g):
      q = q_ref[:, hh * A_QK_W:(hh + 1) * A_QK_W]
      zero = jnp.zeros_like(q)
      qs += [jnp.where(lane < A_QK_DIM, q, zero), jnp.where(lane >= A_QK_DIM, q, zero)]
    return qs

  def softmax_stage(c, m_prev, s_list):
    m_new = m_prev
    for s in s_list:
      m_new = jnp.maximum(m_new, jnp.max(s, axis=0, keepdims=True))
    m_ref[c] = m_new
    alpha_ref[c] = jnp.exp2(m_prev - m_new)
    return [jnp.exp2(s - m_new).astype(BF16) for s in s_list]

  def pending(segments, c):
    pv = None
    for vts, p_ref in segments:
      d = jnp.dot(vts[c // 2], p_ref[c], preferred_element_type=F32)
      pv = d if pv is None else pv + d
    return pv

  def finish_previous(segments):
    lp = lam_ref[...]
    lam = (jnp.exp(jnp.sum(lp[0:1] * lp[1:2], axis=1, keepdims=True))
           - jnp.exp(jnp.sum(lp[2:3] * lp[3:4], axis=1, keepdims=True)) + LAM_INIT)
    for hh in range(g):
      a0, a1 = (alpha_ref[c] * acc_ref[c] + pending(segments, c) for c in (2 * hh, 2 * hh + 1))
      ot = a0[:dv] / a0[dv:dv + 1] - lam * (a1[:dv] / a1[dv:dv + 1])
      ot = ot * lax.rsqrt(jnp.mean(ot * ot, axis=0, keepdims=True) + SUBLN_EPS)
      o = ot.T * (sg_ref[...] * (1.0 - LAM_INIT))
      gate = ga_ref[:, hh * dv:(hh + 1) * dv].astype(F32)
      o_ref[:, hh * dv:(hh + 1) * dv] = (o * (gate / (1.0 + jnp.exp(-gate)))).astype(o_ref.dtype)

  def start_tile(qs, previous):
    if previous is not None:
      finish_previous(previous)
    ks = k_tile(qi)
    st, sm = [], []
    for c in range(nc):
      hh = c // 2
      st.append(_dot_nt(ks[hh], qs[c]) + bd_ref[hh])
      sm.append(_dot_nt(km_ref[:, hh * A_QK_W:(hh + 1) * A_QK_W], qs[c]) + jnp.where(qi == 0, bm_ref[hh], 0.0))
    acc_ref[...] = jnp.zeros(acc_ref.shape, F32)
    m_start = jnp.full((1, t), NEG, F32)
    for c in range(nc):
      pt_ref[c], ptm_ref[c] = softmax_stage(c, m_start, [st[c], sm[c]])

  def step(qs, j_cur, biases, segments):
    ks = k_tile(j_cur)
    pv, st = [], []
    for c in range(nc):
      pv.append(pending(segments, c))
      s = _dot_nt(ks[c // 2], qs[c])
      st.append(s if biases is None else s + biases[c // 2])
    for c in range(nc):
      acc_ref[c] = alpha_ref[c] * acc_ref[c] + pv[c]
    for c in range(nc):
      pt_ref[c], = softmax_stage(c, m_ref[c], [st[c]])

  def below_diagonal(qs):
    step(qs, qi - 1, [bs_ref[hh] for hh in range(g)], [(v_tile(qi), pt_ref), (meta_v(), ptm_ref)])

  @pl.when(qi == 0)
  def _():
    start_tile(queries(), None)

  @pl.when(qi == 1)
  def _():
    qs = queries()
    start_tile(qs, [(v_tile(0), pt_ref), (meta_v(), ptm_ref)])
    below_diagonal(qs)

  @pl.when(jnp.logical_and(qi >= 2, qi < nq))
  def _():
    qs = queries()
    start_tile(qs, [(v_tile(jnp.maximum(qi - 3, 0)), pt_ref)])
    below_diagonal(qs)
    n_far = qi - 1

    def prev_of(j):
      return jnp.where(j == 0, qi - 1, j - 1)

    def far_body(i, carry):
      step(qs, 2 * i, None, [(v_tile(prev_of(2 * i)), pt_ref)])
      step(qs, 2 * i + 1, None, [(v_tile(2 * i), pt_ref)])
      return carry

    lax.fori_loop(0, n_far // 2, far_body, 0)

    @pl.when(n_far % 2 == 1)
    def _():
      step(qs, n_far - 1, None, [(v_tile(prev_of(n_far - 1)), pt_ref)])

  @pl.when(qi == nq)
  def _():
    finish_previous([(v_tile(nq - 3), pt_ref)])


def _attention(z_x, vt_x, z_m, vt_m, bias_d, bias_s, bias_m, lam_p, subln_g):
  b, s, _ = z_x.shape
  t = ATT_T
  g = ATT_G
  nq = s // t
  assert nq >= 3
  w = A_QK_W * g
  hb = A_HEADS // g
  kb, gb = Z_K // w, Z_GA // w

  def cur(qi):
    return jnp.minimum(qi, nq - 1)

  def prev(qi):
    return jnp.maximum(qi - 1, 0)

  return pl.pallas_call(
      _attn_kernel,
      name="diff_attn",
      grid=(b, hb, nq + 1),
      in_specs=[
          pl.BlockSpec((None, t, w), lambda bi, hi, qi: (bi, cur(qi), hi)),
          pl.BlockSpec((None, s, w), lambda bi, hi, qi: (bi, 0, kb + hi), pipeline_mode=pl.Buffered(1)),
          pl.BlockSpec((None, s // t, w, t), lambda bi, hi, qi: (bi, 0, hi, 0), pipeline_mode=pl.Buffered(1)),
          pl.BlockSpec((N_META, w), lambda bi, hi, qi: (0, kb + hi)),
          pl.BlockSpec((1, w, N_META), lambda bi, hi, qi: (0, hi, 0)),
          pl.BlockSpec((None, t, w), lambda bi, hi, qi: (bi, prev(qi), gb + hi)),
          pl.BlockSpec((g, t, t), lambda bi, hi, qi: (hi, 0, 0)),
          pl.BlockSpec((g, t, t), lambda bi, hi, qi: (hi, 0, 0)),
          pl.BlockSpec((g, N_META, t), lambda bi, hi, qi: (hi, 0, 0)),
          pl.BlockSpec((4, A_QK_DIM), lambda bi, hi, qi: (0, 0)),
          pl.BlockSpec((1, A_V_DIM), lambda bi, hi, qi: (0, 0)),
      ],
      out_specs=pl.BlockSpec((None, t, w), lambda bi, hi, qi: (bi, prev(qi), hi)),
      out_shape=jax.ShapeDtypeStruct((b, s, A_WIDTH), BF16),
      scratch_shapes=[
          pltpu.VMEM((2 * g, 1, t), F32),
          pltpu.VMEM((2 * g, 1, t), F32),
          pltpu.VMEM((2 * g, A_V_DIM + ONES_ROWS, t), F32),
          pltpu.VMEM((2 * g, t, t), BF16),
          pltpu.VMEM((2 * g, N_META, t), BF16),
      ],
      compiler_params=pltpu.CompilerParams(
          dimension_semantics=("parallel", "parallel", "arbitrary"),
          vmem_limit_bytes=VMEM_LIMIT),
  )(z_x, z_x, vt_x.reshape(b, s // t, A_WIDTH, t), z_m, vt_m, z_x, bias_d, bias_s, bias_m, lam_p, subln_g)


def _seg_sum(x):
  lane = lax.broadcasted_iota(jnp.int32, x.shape, 1)
  first = lane < R_HEAD
  lo = jnp.sum(jnp.where(first, x, 0.0), axis=1, keepdims=True)
  hi = jnp.sum(jnp.where(first, 0.0, x), axis=1, keepdims=True)
  return jnp.where(first, lo, hi)


def _split_bf16(x):
  hi = x.astype(BF16)
  return hi, x - hi.astype(F32)


def _dot_tn(a, b):
  return lax.dot_general(a, b, (((0,), (0,)), ((), ())), preferred_element_type=F32)


def _bdot(a, b):
  return lax.dot_general(a, b, (((2,), (1,)), ((0,), (0,))), preferred_element_type=F32)


def _bdot_nt(a, b):
  return lax.dot_general(a, b, (((2,), (2,)), ((0,), (0,))), preferred_element_type=F32)


def _rwkv_kernel(rx_ref, kx_ref, vx_ref, lx_ref, gr_ref, rm_ref, kmt_ref, vmt_ref, lm_ref,
                 pv_ref, mul_ref, wuph_ref, wupl_ref, aup_ref, o_ref, s_ref, prev_ref, prevl_ref):
  tb, c = RW_TB, RW_C
  nh = R_HEAD
  c2 = 2 * c
  ti = pl.program_id(2)
  is_meta = ti == 0

  @pl.when(is_meta)
  def _():
    s_ref[...] = jnp.zeros_like(s_ref)
    prev_ref[...] = jnp.zeros_like(prev_ref)
    prevl_ref[...] = jnp.zeros_like(prevl_ref)

  row = lax.broadcasted_iota(jnp.int32, (tb, R_PAIR), 0)
  rowl = lax.broadcasted_iota(jnp.int32, (tb, LORA_PAD), 0)

  def shifted(z, prev, mu, rows):
    z_prev = jnp.where(rows == 0, prev, pltpu.roll(z, 1, 0))
    return z + (z_prev - z) * mu

  z_l = jnp.where(is_meta, lm_ref[...], lx_ref[...])
  lo = shifted(z_l, prevl_ref[...], mul_ref[...], rowl)
  prevl_ref[...] = z_l[tb - 1:tb]
  th_h, th_l = _split_bf16(jnp.tanh(lo))
  w_lora = (jnp.dot(th_h, wuph_ref[...], preferred_element_type=F32)
            + jnp.dot(th_h, wupl_ref[...], preferred_element_type=F32)
            + jnp.dot(th_l.astype(BF16), wuph_ref[...], preferred_element_type=F32))
  a_lora = jnp.dot(lo.astype(BF16), aup_ref[...], preferred_element_type=F32)

  ii = lax.broadcasted_iota(jnp.int32, (tb, tb), 0)
  jj = lax.broadcasted_iota(jnp.int32, (tb, tb), 1)
  shift = int(math.log2(c))
  same = lax.shift_right_logical(ii, shift) == lax.shift_right_logical(jj, shift)
  cum_op = jnp.where(same, jnp.where(jj <= ii, 1.0, 0.0), 0.0).astype(BF16)

  ci = lax.broadcasted_iota(jnp.int32, (c2, c2), 0)
  cj = lax.broadcasted_iota(jnp.int32, (c2, c2), 1)
  diag = ci == cj
  strict2 = jnp.concatenate([cj < ci, cj < ci], axis=1)
  incl2 = jnp.concatenate([cj <= ci, cj <= ci], axis=1)
  first = lax.broadcasted_iota(jnp.int32, (c, R_PAIR), 1) < nh

  def stack(x):
    return jnp.concatenate([jnp.where(first, x, 0.0), jnp.where(first, 0.0, x)], axis=0)

  ncc = tb // c
  chains = {name: [] for name in ("at", "rt", "bt", "kt", "bh", "kh", "vv", "gd")}
  post = []
  for p in range(RW_P):
    ls = slice(p * R_PAIR, (p + 1) * R_PAIR)
    pv = pv_ref[:, ls]
    mu_r, mu_k, mu_v = pv[0:1], pv[1:2], pv[2:3]
    w0, a0, k_k, k_a, r_k, gn_g, gn_b = pv[3:4], pv[4:5], pv[5:6], pv[6:7], pv[7:8], pv[8:9], pv[9:10]

    z_r = jnp.where(is_meta, rm_ref[:, ls], rx_ref[:, ls]).astype(F32)
    z_k = jnp.where(is_meta, kmt_ref[:, ls], kx_ref[:, ls]).astype(F32)
    z_v = jnp.where(is_meta, vmt_ref[:, ls], vx_ref[:, ls]).astype(F32)
    r = shifted(z_r, prev_ref[0:1, ls], mu_r, row)
    k = shifted(z_k, prev_ref[1:2, ls], mu_k, row)
    v = shifted(z_v, prev_ref[2:3, ls], mu_v, row)
    prev_ref[0:1, ls] = z_r[tb - 1:tb]
    prev_ref[1:2, ls] = z_k[tb - 1:tb]
    prev_ref[2:3, ls] = z_v[tb - 1:tb]

    u = -(w0 + w_lora[:, ls])
    softplus = jnp.maximum(u, 0.0) + jnp.log(1.0 + jnp.exp(-jnp.abs(u)))
    logw = -jnp.exp(-softplus - 0.5) * LOG2E
    a = 1.0 / (1.0 + jnp.exp(-(a0 + a_lora[:, ls])))
    kk = k * k_k
    kk = kk / jnp.maximum(jnp.sqrt(_seg_sum(kk * kk)), 1e-12)
    k_mod = k * (1.0 + (a - 1.0) * k_a)
    bonus = _seg_sum(r * k_mod * r_k) * v

    lw_h, lw_r = _split_bf16(logw)
    lw_m, lw_l = _split_bf16(lw_r)
    cum3 = jnp.dot(cum_op, jnp.concatenate([lw_h, lw_m, lw_l.astype(BF16)], axis=1),
                   preferred_element_type=F32)
    cum = cum3[:, :R_PAIR] + cum3[:, R_PAIR:2 * R_PAIR] + cum3[:, 2 * R_PAIR:]
    tot = jnp.concatenate([jnp.broadcast_to(cum[cc * c + c - 1:cc * c + c], (c, R_PAIR)) for cc in range(ncc)],
                          axis=0)
    p_inv = jnp.exp2(-cum)
    a_t = -kk * jnp.exp2(cum - logw)
    kka = kk * a
    b_t = kka * p_inv
    k_t = k_mod * p_inv
    r_t = r * jnp.exp2(cum)
    p_end = jnp.exp2(tot - cum)
    b_h = kka * p_end
    k_h = k_mod * p_end
    g_diag = jnp.exp2(tot)

    for cc in range(ncc):
      rs = slice(cc * c, (cc + 1) * c)
      for name, val in (("at", a_t), ("rt", r_t), ("bt", b_t), ("kt", k_t), ("bh", b_h), ("kh", k_h),
                        ("vv", v)):
        chains[name].append(stack(val[rs]))
      chains["gd"].append(g_diag[cc * c:cc * c + 1])
    post.append((bonus, gn_g, gn_b))

  nb = RW_P * ncc
  at, rt, bt, kt, bh, kh, vv = (jnp.stack(chains[name]) for name in ("at", "rt", "bt", "kt", "bh", "kh", "vv"))
  at_b, vv_b, bh_b = at.astype(BF16), vv.astype(BF16), bh.astype(BF16)
  bk = jnp.concatenate([bt, kt], axis=1).astype(BF16)
  top = jnp.where(strict2, _bdot_nt(at_b, bk), 0.0)
  lblk = jnp.where(incl2, _bdot_nt(rt.astype(BF16), bk), 0.0)
  nm, mak = top[:, :, :c2], top[:, :, c2:]
  tinv = jnp.where(diag, 1.0, nm)
  npow = nm.astype(BF16)
  for _ in range(5):
    npow = _bdot(npow, npow).astype(BF16)
    tinv = tinv + _bdot(tinv.astype(BF16), npow)
  x1 = _bdot(mak.astype(BF16), vv_b)
  wu_b = _bdot(tinv.astype(BF16), jnp.concatenate([at_b, x1.astype(BF16)], axis=2)).astype(BF16)
  rhs = jnp.concatenate([wu_b, jnp.concatenate([jnp.zeros_like(vv_b), vv_b], axis=2)], axis=1)
  qy = _bdot(lblk.astype(BF16), rhs)
  q_h = (rt + qy[:, :, :c2]).astype(BF16)
  y0 = qy[:, :, c2:]
  uv = jnp.concatenate([wu_b[:, :, c2:], vv_b], axis=1)
  bkh = jnp.concatenate([bh_b, kh.astype(BF16)], axis=1)
  g_m = [(jnp.where(diag, chains["gd"][n], 0.0) + _dot_tn(wu_b[n, :, :c2], bh_b[n])).astype(BF16)
         for n in range(nb)]
  h_m = [_dot_tn(uv[n], bkh[n]) for n in range(nb)]

  states = [s_ref[p] for p in range(RW_P)]
  y_rows = [[] for _ in range(RW_P)]
  for cc in range(ncc):
    for p in range(RW_P):
      n = p * ncc + cc
      s_old_b = states[p].astype(BF16)
      y2 = _dot_nt(q_h[n], s_old_b) + y0[n]
      states[p] = jnp.dot(s_old_b, g_m[n], preferred_element_type=F32) + h_m[n]
      y_rows[p].append(y2[:c] + y2[c:])

  for p in range(RW_P):
    ls = slice(p * R_PAIR, (p + 1) * R_PAIR)
    s_ref[p] = states[p]
    bonus, gn_g, gn_b = post[p]
    y = jnp.concatenate(y_rows[p], axis=0)
    mean = _seg_sum(y) * (1.0 / nh)
    yc = y - mean
    var = _seg_sum(yc * yc) * (1.0 / nh)
    yn = yc * lax.rsqrt(var + GN_EPS) * gn_g + gn_b
    g = gr_ref[:, ls].astype(F32)
    o_ref[:, ls] = ((yn + bonus) * (g / (1.0 + jnp.exp(-g)))).astype(o_ref.dtype)


def _rwkv(z_x, lo_x, z_mp, lo_mp, pvec, mu_l, wup_h, wup_l, aup):
  b, s, _ = z_x.shape
  tb = RW_TB
  nt = s // tb + 1
  pw = R_PAIR * RW_P

  def xmap(col):
    return lambda bi, hp, ti: (bi, jnp.maximum(ti - 1, 0), col // pw + hp)

  def mmap(col):
    return lambda bi, hp, ti: (0, col // pw + hp)

  return pl.pallas_call(
      _rwkv_kernel,
      name="rwkv7",
      grid=(b, R_PAIRS // RW_P, nt),
      in_specs=[
          pl.BlockSpec((None, tb, pw), xmap(Z_RR)),
          pl.BlockSpec((None, tb, pw), xmap(Z_RK)),
          pl.BlockSpec((None, tb, pw), xmap(Z_RV)),
          pl.BlockSpec((None, tb, LORA_PAD), lambda bi, hp, ti: (bi, jnp.maximum(ti - 1, 0), 0)),
          pl.BlockSpec((None, tb, pw), xmap(Z_GR)),
          pl.BlockSpec((tb, pw), mmap(Z_RR)),
          pl.BlockSpec((tb, pw), mmap(Z_RK)),
          pl.BlockSpec((tb, pw), mmap(Z_RV)),
          pl.BlockSpec((tb, LORA_PAD), lambda bi, hp, ti: (0, 0)),
          pl.BlockSpec((16, pw), lambda bi, hp, ti: (0, hp)),
          pl.BlockSpec((1, LORA_PAD), lambda bi, hp, ti: (0, 0)),
          pl.BlockSpec((LORA_PAD, pw), lambda bi, hp, ti: (0, hp)),
          pl.BlockSpec((LORA_PAD, pw), lambda bi, hp, ti: (0, hp)),
          pl.BlockSpec((LORA_PAD, pw), lambda bi, hp, ti: (0, hp)),
      ],
      out_specs=pl.BlockSpec((None, tb, pw), lambda bi, hp, ti: (bi, jnp.maximum(ti - 1, 0), hp)),
      out_shape=jax.ShapeDtypeStruct((b, s, R_WIDTH), BF16),
      scratch_shapes=[
          pltpu.VMEM((RW_P, 2 * R_HEAD, 2 * R_HEAD), F32),
          pltpu.VMEM((8, pw), F32),
          pltpu.VMEM((1, LORA_PAD), F32),
      ],
      compiler_params=pltpu.CompilerParams(
          dimension_semantics=("parallel", "parallel", "arbitrary"),
          vmem_limit_bytes=VMEM_LIMIT),
  )(z_x, z_x, z_x, lo_x, z_x, z_mp, z_mp, z_mp, lo_mp, pvec, mu_l, wup_h, wup_l, aup)


def _out_kernel(x_ref, oa_ref, or_ref, wa_ref, wr_ref, ge_ref, be_ref, gp_ref, bp_ref, o_ref):
  h = _ln_rows(x_ref[...], ge_ref[...], be_ref[...])
  y = (jnp.dot(oa_ref[...], wa_ref[...], preferred_element_type=F32)
       + jnp.dot(or_ref[...], wr_ref[...], preferred_element_type=F32))
  o_ref[...] = _ln_rows(DEEPNORM_ALPHA * h + y, gp_ref[...], bp_ref[...])


def _out_proj(x2d, oa, orw, wa, wr, ge, be, gp, bp, tm):
  m, d = x2d.shape
  vec = pl.BlockSpec((1, d), lambda i: (0, 0))
  return pl.pallas_call(
      _out_kernel,
      name="out_proj",
      grid=(m // tm,),
      in_specs=[
          pl.BlockSpec((tm, d), lambda i: (i, 0)),
          pl.BlockSpec((tm, A_WIDTH), lambda i: (i, 0)),
          pl.BlockSpec((tm, R_WIDTH), lambda i: (i, 0)),
          pl.BlockSpec((A_WIDTH, d), lambda i: (0, 0), pipeline_mode=pl.Buffered(1)),
          pl.BlockSpec((R_WIDTH, d), lambda i: (0, 0), pipeline_mode=pl.Buffered(1)),
          vec, vec, vec, vec,
      ],
      out_specs=pl.BlockSpec((tm, d), lambda i: (i, 0)),
      out_shape=jax.ShapeDtypeStruct((m, d), F32),
      compiler_params=pltpu.CompilerParams(
          dimension_semantics=("parallel",),
          vmem_limit_bytes=VMEM_LIMIT),
  )(x2d, oa, orw, wa, wr, ge, be, gp, bp)


def kernel(x, meta_tokens, ln_emb_g, ln_emb_b, rel_bias, w_in, w_out, lambda_q1, lambda_k1, lambda_q2,
           lambda_k2, subln_g, rw_mu, rw_w0, rw_w_up, rw_a0, rw_a_up, rw_k_k, rw_k_a, rw_r_k, rw_gn_g,
           rw_gn_b, ln_post_g, ln_post_b):
  b, s, d = x.shape
  l = 0
  wi = w_in[l]
  c_lo = 4 * A_WIDTH + 3 * R_WIDTH
  c_gr = c_lo + DECAY_LORA + ICLR_LORA
  lora_pad = LORA_PAD - DECAY_LORA - ICLR_LORA
  wt_all = wi.T.astype(BF16)
  wt_gr = wt_all[c_gr:]
  wt_lora = jnp.pad(wt_all[c_lo:c_gr], ((0, lora_pad), (0, 0)))

  ge, be = ln_emb_g.reshape(1, d), ln_emb_b.reshape(1, d)
  x2d = x.reshape(b * s, d)
  z_x, vt_x, lo_x = _ln_matmul(x2d, ge, be, wt_all, wt_gr, wt_lora, 1024)
  z_x, lo_x = z_x.reshape(b, s, -1), lo_x.reshape(b, s, LORA_PAD)
  z_m, vt_m, lo_m = _ln_matmul(meta_tokens, ge, be, wt_all, wt_gr, wt_lora, N_META)

  bias_d, bias_s, bias_m = _bias_tiles(rel_bias)
  lam_p = jnp.stack([lambda_q1[l], lambda_k1[l], lambda_q2[l], lambda_k2[l]], axis=0)
  o_attn = _attention(z_x, vt_x, z_m, vt_m, bias_d, bias_s, bias_m, lam_p, subln_g[l].reshape(1, A_V_DIM))

  mu = rw_mu[l]
  zeros = jnp.zeros((R_WIDTH,), F32)
  pvec = jnp.stack([mu[:R_WIDTH], mu[R_WIDTH:2 * R_WIDTH], mu[2 * R_WIDTH:3 * R_WIDTH], rw_w0[l], rw_a0[l],
                    rw_k_k[l], rw_k_a[l], rw_r_k[l].reshape(R_WIDTH), rw_gn_g[l], rw_gn_b[l]]
                   + [zeros] * 6, axis=0)
  mu_l = jnp.pad(mu[3 * R_WIDTH:], (0, lora_pad)).reshape(1, LORA_PAD)
  wup = jnp.pad(rw_w_up[l], ((0, LORA_PAD - DECAY_LORA), (0, 0)))
  wup_h = wup.astype(BF16)
  wup_l = (wup - wup_h.astype(F32)).astype(BF16)
  aup = jnp.pad(rw_a_up[l], ((DECAY_LORA, lora_pad), (0, 0))).astype(BF16)
  front = ((RW_TB - N_META, 0), (0, 0))
  o_rwkv = _rwkv(z_x, lo_x, jnp.pad(z_m, front), jnp.pad(lo_m, front), pvec, mu_l, wup_h, wup_l, aup)

  wo = w_out[l].astype(BF16)
  out = _out_proj(x2d, o_attn.reshape(b * s, A_WIDTH), o_rwkv.reshape(b * s, R_WIDTH),
                  wo[:A_WIDTH], wo[A_WIDTH:], ge, be,
                  ln_post_g[l].reshape(1, d), ln_post_b[l].reshape(1, d), 512)
  return out.reshape(b, s, d)
```

```python
import math

import numpy as np
import jax
import jax.numpy as jnp
from jax import lax
from jax.experimental import pallas as pl
from jax.experimental.pallas import tpu as pltpu

N_META = 16
A_HEADS = 8
A_V_DIM = 128
A_QK_DIM = 64
A_WIDTH = A_HEADS * A_V_DIM
R_HEAD = 64
R_WIDTH = 1024
R_PAIR = 2 * R_HEAD
R_PAIRS = R_WIDTH // R_PAIR
A_QK_W = 2 * A_QK_DIM
DECAY_LORA = 96
ICLR_LORA = 96
LORA_PAD = 256
N_BUCKETS = 32
MAX_DISTANCE = 128
LN_EPS = 1e-5
SUBLN_EPS = 1e-5
GN_EPS = 64e-5
DEPTH = 1
DEEPNORM_ALPHA = (2 * DEPTH) ** 0.25
LAM_INIT = 0.8 - 0.6 * math.exp(-0.3 * 0)
NEG = -1e30

ATT_T = 256
ATT_G = 8
ONES_ROWS = 16
IN_TM = 1024
IN_TN = 1024
OUT_TM = 512
Z_Q, Z_K, Z_GA, Z_RR, Z_RK, Z_RV, Z_GR = (i * 1024 for i in range(7))
LOG2E = math.log2(math.e)
Q_SCALE = A_QK_DIM ** -0.5 * LOG2E
RW_TB = 128
RW_C = 64
RW_P = 8
VMEM_LIMIT = 56 * 1024 * 1024

F32 = jnp.float32
BF16 = jnp.bfloat16


def _ln_rows(x, g, b):
  mu = jnp.mean(x, axis=-1, keepdims=True)
  xc = x - mu
  var = jnp.mean(xc * xc, axis=-1, keepdims=True)
  return xc * lax.rsqrt(var + LN_EPS) * g + b


def _ln_mm_kernel(x_ref, g_ref, b_ref, wm_ref, wgr_ref, wvt_ref, wl_ref, om_ref, ovt_ref, ol_ref, hn_ref):
  j = pl.program_id(1)
  n_main = pl.num_programs(1) - 1

  @pl.when(j == 0)
  def _():
    hn_ref[...] = _ln_rows(x_ref[...], g_ref[...], b_ref[...]).astype(BF16)

  @pl.when(j < n_main - 1)
  def _():
    scale = jnp.where(j == Z_Q // IN_TN, Q_SCALE, 1.0)
    z = _dot_nt(hn_ref[...], wm_ref[...])
    om_ref[...] = (z * scale).astype(om_ref.dtype)

  @pl.when(j == n_main - 1)
  def _():
    om_ref[...] = _dot_nt(hn_ref[...], wgr_ref[...]).astype(om_ref.dtype)
    ol_ref[...] = _dot_nt(hn_ref[...], wl_ref[...])

  @pl.when(j == n_main)
  def _():
    zt = _dot_nt(wvt_ref[...], hn_ref[...])
    tv = ovt_ref.shape[2]
    for c in range(ovt_ref.shape[0]):
      ovt_ref[c] = zt[:, c * tv:(c + 1) * tv].astype(ovt_ref.dtype)


def _ln_matmul(x2d, g, b, wt_all, wt_gr, wt_lora, tm):
  m, d = x2d.shape
  tn = IN_TN
  nj = Z_GR // tn + 1
  n = nj * tn
  last = nj - 1
  tv = min(ATT_T, tm)
  v_tile = 2 * A_WIDTH // tn
  once = dict(pipeline_mode=pl.Buffered(1))

  def w_map(i, j):
    jj = jnp.minimum(j, last - 1)
    return (jnp.where(jj >= v_tile, jj + 1, jj), 0)

  return pl.pallas_call(
      _ln_mm_kernel,
      name="ln_inproj",
      grid=(m // tm, nj + 1),
      in_specs=[
          pl.BlockSpec((tm, d), lambda i, j: (i, 0)),
          pl.BlockSpec((1, d), lambda i, j: (0, 0)),
          pl.BlockSpec((1, d), lambda i, j: (0, 0)),
          pl.BlockSpec((tn, d), w_map),
          pl.BlockSpec((tn, d), lambda i, j: (0, 0), **once),
          pl.BlockSpec((A_WIDTH, d), lambda i, j: (v_tile, 0), **once),
          pl.BlockSpec((LORA_PAD, d), lambda i, j: (0, 0), **once),
      ],
      out_specs=[
          pl.BlockSpec((tm, tn), lambda i, j: (i, jnp.minimum(j, last))),
          pl.BlockSpec((tm // tv, A_WIDTH, tv), lambda i, j: (i, 0, 0)),
          pl.BlockSpec((tm, LORA_PAD), lambda i, j: (i, 0)),
      ],
      out_shape=[
          jax.ShapeDtypeStruct((m, n), BF16),
          jax.ShapeDtypeStruct((m // tv, A_WIDTH, tv), BF16),
          jax.ShapeDtypeStruct((m, LORA_PAD), F32),
      ],
      scratch_shapes=[pltpu.VMEM((tm, d), BF16)],
      compiler_params=pltpu.CompilerParams(
          dimension_semantics=("parallel", "arbitrary"),
          vmem_limit_bytes=VMEM_LIMIT),
  )(x2d, g, b, wt_all, wt_gr, wt_all, wt_lora)


def _bucket_thresholds():
  n = np.arange(0, 4 * MAX_DISTANCE, dtype=np.int64)
  max_exact = N_BUCKETS // 2
  nf = np.maximum(n, 1).astype(np.float32)
  large = max_exact + (np.log(nf / np.float32(max_exact)) / np.float32(math.log(MAX_DISTANCE / max_exact))
                       * np.float32(N_BUCKETS - max_exact)).astype(np.int32)
  large = np.minimum(large, N_BUCKETS - 1)
  bucket = np.where(n < max_exact, n, large)
  assert np.all(np.diff(bucket) >= 0) and bucket[-1] == N_BUCKETS - 1
  return [int(np.argmax(bucket >= b)) for b in range(N_BUCKETS)]


_THR = _bucket_thresholds()


def _bias_kernel(rb_ref, diag_ref, sub_ref, meta_ref):
  h = pl.program_id(0)
  far = rb_ref[N_BUCKETS - 1, h]

  def bias_of(n):
    out = jnp.full(n.shape, (rb_ref[0, h] - far) * LOG2E, F32)
    for b in range(1, N_BUCKETS):
      out = jnp.where(n >= _THR[b], (rb_ref[b, h] - far) * LOG2E, out)
    return out

  t = ATT_T
  kj = lax.broadcasted_iota(jnp.int32, (t, t), 0)
  qi = lax.broadcasted_iota(jnp.int32, (t, t), 1)
  d = qi - kj
  diag_ref[...] = jnp.where(d >= 0, bias_of(d), NEG)
  sub_ref[...] = bias_of(d + t)
  km = lax.broadcasted_iota(jnp.int32, (N_META, t), 0)
  qm = lax.broadcasted_iota(jnp.int32, (N_META, t), 1)
  meta_ref[...] = bias_of(qm - km + N_META)


def _bias_tiles(rel_bias):
  t = ATT_T
  return pl.pallas_call(
      _bias_kernel,
      name="bias_tiles",
      grid=(A_HEADS,),
      in_specs=[pl.BlockSpec(memory_space=pltpu.SMEM)],
      out_specs=[
          pl.BlockSpec((None, t, t), lambda h: (h, 0, 0)),
          pl.BlockSpec((None, t, t), lambda h: (h, 0, 0)),
          pl.BlockSpec((None, N_META, t), lambda h: (h, 0, 0)),
      ],
      out_shape=[
          jax.ShapeDtypeStruct((A_HEADS, t, t), F32),
          jax.ShapeDtypeStruct((A_HEADS, t, t), F32),
          jax.ShapeDtypeStruct((A_HEADS, N_META, t), F32),
      ],
  )(rel_bias)


def _dot_nt(a, b):
  return lax.dot_general(a, b, (((1,), (1,)), ((), ())), preferred_element_type=F32)


def _attn_kernel(q_ref, kx_ref, vt_ref, km_ref, vmt_ref, ga_ref, bd_ref, bs_ref, bm_ref,
                 lam_ref, sg_ref, o_ref, m_ref, alpha_ref, acc_ref, pt_ref, ptm_ref):
  t = ATT_T
  g = ATT_G
  nc = 2 * g
  dv = A_V_DIM
  qi = pl.program_id(2)
  nq = pl.num_programs(2) - 1

  def v_tile(j):
    ones = jnp.ones((ONES_ROWS, t), BF16)
    return [jnp.concatenate([vt_ref[j, hh * dv:(hh + 1) * dv, :], ones], axis=0) for hh in range(g)]

  def k_tile(j):
    off = pl.multiple_of(j * t, t)
    return [kx_ref[pl.ds(off, t), hh * A_QK_W:(hh + 1) * A_QK_W] for hh in range(g)]

  def meta_v():
    ones_m = jnp.ones((ONES_ROWS, N_META), BF16)
    return [jnp.concatenate([vmt_ref[0, hh * dv:(hh + 1) * dv, :], ones_m], axis=0) for hh in range(g)]

  def queries():
    lane = lax.broadcasted_iota(jnp.int32, (t, A_QK_W), 1)
    qs = []
    for hh in range(g):
      q = q_ref[:, hh * A_QK_W:(hh + 1) * A_QK_W]
      zero = jnp.zeros_like(q)
      qs += [jnp.where(lane < A_QK_DIM, q, zero), jnp.where(lane >= A_QK_DIM, q, zero)]
    return qs

  def softmax_stage(c, m_prev, s_list):
    m_new = m_prev
    for s in s_list:
      m_new = jnp.maximum(m_new, jnp.max(s, axis=0, keepdims=True))
    m_ref[c] = m_new
    alpha_ref[c] = jnp.exp2(m_prev - m_new)
    return [jnp.exp2(s - m_new).astype(BF16) for s in s_list]

  def pending(segments, c):
    pv = None
    for vts, p_ref in segments:
      d = jnp.dot(vts[c // 2], p_ref[c], preferred_element_type=F32)
      pv = d if pv is None else pv + d
    return pv

  def finish_previous(segments):
    lp = lam_ref[...]
    lam = (jnp.exp(jnp.sum(lp[0:1] * lp[1:2], axis=1, keepdims=True))
           - jnp.exp(jnp.sum(lp[2:3] * lp[3:4], axis=1, keepdims=True)) + LAM_INIT)
    for hh in range(g):
      a0, a1 = (alpha_ref[c] * acc_ref[c] + pending(segments, c) for c in (2 * hh, 2 * hh + 1))
      ot = a0[:dv] / a0[dv:dv + 1] - lam * (a1[:dv] / a1[dv:dv + 1])
      ot = ot * lax.rsqrt(jnp.mean(ot * ot, axis=0, keepdims=True) + SUBLN_EPS)
      o = ot.T * (sg_ref[...] * (1.0 - LAM_INIT))
      gate = ga_ref[:, hh * dv:(hh + 1) * dv].astype(F32)
      o_ref[:, hh * dv:(hh + 1) * dv] = (o * (gate / (1.0 + jnp.exp(-gate)))).astype(o_ref.dtype)

  def start_tile(qs, previous):
    if previous is not None:
      finish_previous(previous)
    ks = k_tile(qi)
    st, sm = [], []
    for c in range(nc):
      hh = c // 2
      st.append(_dot_nt(ks[hh], qs[c]) + bd_ref[hh])
      sm.append(_dot_nt(km_ref[:, hh * A_QK_W:(hh + 1) * A_QK_W], qs[c]) + jnp.where(qi == 0, bm_ref[hh], 0.0))
    acc_ref[...] = jnp.zeros(acc_ref.shape, F32)
    m_start = jnp.full((1, t), NEG, F32)
    for c in range(nc):
      pt_ref[c], ptm_ref[c] = softmax_stage(c, m_start, [st[c], sm[c]])

  def step(qs, j_cur, biases, segments):
    ks = k_tile(j_cur)
    pv, st = [], []
    for c in range(nc):
      pv.append(pending(segments, c))
      s = _dot_nt(ks[c // 2], qs[c])
      st.append(s if biases is None else s + biases[c // 2])
    for c in range(nc):
      acc_ref[c] = alpha_ref[c] * acc_ref[c] + pv[c]
    for c in range(nc):
      pt_ref[c], = softmax_stage(c, m_ref[c], [st[c]])

  def below_diagonal(qs):
    step(qs, qi - 1, [bs_ref[hh] for hh in range(g)], [(v_tile(qi), pt_ref), (meta_v(), ptm_ref)])

  @pl.when(qi == 0)
  def _():
    start_tile(queries(), None)

  @pl.when(qi == 1)
  def _():
    qs = queries()
    start_tile(qs, [(v_tile(0), pt_ref), (meta_v(), ptm_ref)])
    below_diagonal(qs)

  @pl.when(jnp.logical_and(qi >= 2, qi < nq))
  def _():
    qs = queries()
    start_tile(qs, [(v_tile(jnp.maximum(qi - 3, 0)), pt_ref)])
    below_diagonal(qs)
    n_far = qi - 1

    def prev_of(j):
      return jnp.where(j == 0, qi - 1, j - 1)

    def far_body(i, carry):
      step(qs, 2 * i, None, [(v_tile(prev_of(2 * i)), pt_ref)])
      step(qs, 2 * i + 1, None, [(v_tile(2 * i), pt_ref)])
      return carry

    lax.fori_loop(0, n_far // 2, far_body, 0)

    @pl.when(n_far % 2 == 1)
    def _():
      step(qs, n_far - 1, None, [(v_tile(prev_of(n_far - 1)), pt_ref)])

  @pl.when(qi == nq)
  def _():
    finish_previous([(v_tile(nq - 3), pt_ref)])


def _attention(z_x, vt_x, z_m, vt_m, bias_d, bias_s, bias_m, lam_p, subln_g):
  b, s, _ = z_x.shape
  t = ATT_T
  g = ATT_G
  nq = s // t
  assert nq >= 3
  w = A_QK_W * g
  hb = A_HEADS // g
  kb, gb = Z_K // w, Z_GA // w

  def cur(qi):
    return jnp.minimum(qi, nq - 1)

  def prev(qi):
    return jnp.maximum(qi - 1, 0)

  return pl.pallas_call(
      _attn_kernel,
      name="diff_attn",
      grid=(b, hb, nq + 1),
      in_specs=[
          pl.BlockSpec((None, t, w), lambda bi, hi, qi: (bi, cur(qi), hi)),
          pl.BlockSpec((None, s, w), lambda bi, hi, qi: (bi, 0, kb + hi)),
          pl.BlockSpec((None, s // t, w, t), lambda bi, hi, qi: (bi, 0, hi, 0)),
          pl.BlockSpec((N_META, w), lambda bi, hi, qi: (0, kb + hi)),
          pl.BlockSpec((1, w, N_META), lambda bi, hi, qi: (0, hi, 0)),
          pl.BlockSpec((None, t, w), lambda bi, hi, qi: (bi, prev(qi), gb + hi)),
          pl.BlockSpec((g, t, t), lambda bi, hi, qi: (hi, 0, 0)),
          pl.BlockSpec((g, t, t), lambda bi, hi, qi: (hi, 0, 0)),
          pl.BlockSpec((g, N_META, t), lambda bi, hi, qi: (hi, 0, 0)),
          pl.BlockSpec((4, A_QK_DIM), lambda bi, hi, qi: (0, 0)),
          pl.BlockSpec((1, A_V_DIM), lambda bi, hi, qi: (0, 0)),
      ],
      out_specs=pl.BlockSpec((None, t, w), lambda bi, hi, qi: (bi, prev(qi), hi)),
      out_shape=jax.ShapeDtypeStruct((b, s, A_WIDTH), BF16),
      scratch_shapes=[
          pltpu.VMEM((2 * g, 1, t), F32),
          pltpu.VMEM((2 * g, 1, t), F32),
          pltpu.VMEM((2 * g, A_V_DIM + ONES_ROWS, t), F32),
          pltpu.VMEM((2 * g, t, t), BF16),
          pltpu.VMEM((2 * g, N_META, t), BF16),
      ],
      compiler_params=pltpu.CompilerParams(
          dimension_semantics=("parallel", "parallel", "arbitrary"),
          vmem_limit_bytes=VMEM_LIMIT),
  )(z_x, z_x, vt_x.reshape(b, s // t, A_WIDTH, t), z_m, vt_m, z_x, bias_d, bias_s, bias_m, lam_p, subln_g)


def _seg_sum(x):
  lane = lax.broadcasted_iota(jnp.int32, x.shape, 1)
  first = lane < R_HEAD
  lo = jnp.sum(jnp.where(first, x, 0.0), axis=1, keepdims=True)
  hi = jnp.sum(jnp.where(first, 0.0, x), axis=1, keepdims=True)
  return jnp.where(first, lo, hi)


def _split_bf16(x):
  hi = x.astype(BF16)
  return hi, x - hi.astype(F32)


def _dot_tn(a, b):
  return lax.dot_general(a, b, (((0,), (0,)), ((), ())), preferred_element_type=F32)


def _bdot(a, b):
  return lax.dot_general(a, b, (((2,), (1,)), ((0,), (0,))), preferred_element_type=F32)


def _bdot_nt(a, b):
  return lax.dot_general(a, b, (((2,), (2,)), ((0,), (0,))), preferred_element_type=F32)


def _rwkv_kernel(rx_ref, kx_ref, vx_ref, lx_ref, gr_ref, rm_ref, kmt_ref, vmt_ref, lm_ref,
                 pv_ref, mul_ref, wuph_ref, wupl_ref, aup_ref, o_ref, s_ref, prev_ref, prevl_ref):
  tb, c = RW_TB, RW_C
  nh = R_HEAD
  c2 = 2 * c
  ti = pl.program_id(2)
  is_meta = ti == 0

  @pl.when(is_meta)
  def _():
    s_ref[...] = jnp.zeros_like(s_ref)
    prev_ref[...] = jnp.zeros_like(prev_ref)
    prevl_ref[...] = jnp.zeros_like(prevl_ref)

  row = lax.broadcasted_iota(jnp.int32, (tb, R_PAIR), 0)
  rowl = lax.broadcasted_iota(jnp.int32, (tb, LORA_PAD), 0)

  def shifted(z, prev, mu, rows):
    z_prev = jnp.where(rows == 0, prev, pltpu.roll(z, 1, 0))
    return z + (z_prev - z) * mu

  z_l = jnp.where(is_meta, lm_ref[...], lx_ref[...])
  lo = shifted(z_l, prevl_ref[...], mul_ref[...], rowl)
  prevl_ref[...] = z_l[tb - 1:tb]
  th_h, th_l = _split_bf16(jnp.tanh(lo))
  w_lora = (jnp.dot(th_h, wuph_ref[...], preferred_element_type=F32)
            + jnp.dot(th_h, wupl_ref[...], preferred_element_type=F32)
            + jnp.dot(th_l.astype(BF16), wuph_ref[...], preferred_element_type=F32))
  a_lora = jnp.dot(lo.astype(BF16), aup_ref[...], preferred_element_type=F32)

  ii = lax.broadcasted_iota(jnp.int32, (tb, tb), 0)
  jj = lax.broadcasted_iota(jnp.int32, (tb, tb), 1)
  shift = int(math.log2(c))
  same = lax.shift_right_logical(ii, shift) == lax.shift_right_logical(jj, shift)
  cum_op = jnp.where(same, jnp.where(jj <= ii, 1.0, 0.0), 0.0).astype(BF16)

  ci = lax.broadcasted_iota(jnp.int32, (c2, c2), 0)
  cj = lax.broadcasted_iota(jnp.int32, (c2, c2), 1)
  diag = ci == cj
  strict2 = jnp.concatenate([cj < ci, cj < ci], axis=1)
  incl2 = jnp.concatenate([cj <= ci, cj <= ci], axis=1)
  first = lax.broadcasted_iota(jnp.int32, (c, R_PAIR), 1) < nh

  def stack(x):
    return jnp.concatenate([jnp.where(first, x, 0.0), jnp.where(first, 0.0, x)], axis=0)

  ncc = tb // c
  chains = {name: [] for name in ("at", "rt", "bt", "kt", "bh", "kh", "vv", "gd")}
  post = []
  for p in range(RW_P):
    ls = slice(p * R_PAIR, (p + 1) * R_PAIR)
    pv = pv_ref[:, ls]
    mu_r, mu_k, mu_v = pv[0:1], pv[1:2], pv[2:3]
    w0, a0, k_k, k_a, r_k, gn_g, gn_b = pv[3:4], pv[4:5], pv[5:6], pv[6:7], pv[7:8], pv[8:9], pv[9:10]

    z_r = jnp.where(is_meta, rm_ref[:, ls], rx_ref[:, ls]).astype(F32)
    z_k = jnp.where(is_meta, kmt_ref[:, ls], kx_ref[:, ls]).astype(F32)
    z_v = jnp.where(is_meta, vmt_ref[:, ls], vx_ref[:, ls]).astype(F32)
    r = shifted(z_r, prev_ref[0:1, ls], mu_r, row)
    k = shifted(z_k, prev_ref[1:2, ls], mu_k, row)
    v = shifted(z_v, prev_ref[2:3, ls], mu_v, row)
    prev_ref[0:1, ls] = z_r[tb - 1:tb]
    prev_ref[1:2, ls] = z_k[tb - 1:tb]
    prev_ref[2:3, ls] = z_v[tb - 1:tb]

    u = -(w0 + w_lora[:, ls])
    softplus = jnp.maximum(u, 0.0) + jnp.log(1.0 + jnp.exp(-jnp.abs(u)))
    logw = -jnp.exp(-softplus - 0.5) * LOG2E
    a = 1.0 / (1.0 + jnp.exp(-(a0 + a_lora[:, ls])))
    kk = k * k_k
    kk = kk / jnp.maximum(jnp.sqrt(_seg_sum(kk * kk)), 1e-12)
    k_mod = k * (1.0 + (a - 1.0) * k_a)
    bonus = _seg_sum(r * k_mod * r_k) * v

    lw_h, lw_r = _split_bf16(logw)
    lw_m, lw_l = _split_bf16(lw_r)
    cum3 = jnp.dot(cum_op, jnp.concatenate([lw_h, lw_m, lw_l.astype(BF16)], axis=1),
                   preferred_element_type=F32)
    cum = cum3[:, :R_PAIR] + cum3[:, R_PAIR:2 * R_PAIR] + cum3[:, 2 * R_PAIR:]
    tot = jnp.concatenate([jnp.broadcast_to(cum[cc * c + c - 1:cc * c + c], (c, R_PAIR)) for cc in range(ncc)],
                          axis=0)
    p_inv = jnp.exp2(-cum)
    a_t = -kk * jnp.exp2(cum - logw)
    kka = kk * a
    b_t = kka * p_inv
    k_t = k_mod * p_inv
    r_t = r * jnp.exp2(cum)
    p_end = jnp.exp2(tot - cum)
    b_h = kka * p_end
    k_h = k_mod * p_end
    g_diag = jnp.exp2(tot)

    for cc in range(ncc):
      rs = slice(cc * c, (cc + 1) * c)
      for name, val in (("at", a_t), ("rt", r_t), ("bt", b_t), ("kt", k_t), ("bh", b_h), ("kh", k_h),
                        ("vv", v)):
        chains[name].append(stack(val[rs]))
      chains["gd"].append(g_diag[cc * c:cc * c + 1])
    post.append((bonus, gn_g, gn_b))

  nb = RW_P * ncc
  at, rt, bt, kt, bh, kh, vv = (jnp.stack(chains[name]) for name in ("at", "rt", "bt", "kt", "bh", "kh", "vv"))
  at_b, vv_b, bh_b = at.astype(BF16), vv.astype(BF16), bh.astype(BF16)
  bk = jnp.concatenate([bt, kt], axis=1).astype(BF16)
  top = jnp.where(strict2, _bdot_nt(at_b, bk), 0.0)
  lblk = jnp.where(incl2, _bdot_nt(rt.astype(BF16), bk), 0.0)
  nm, mak = top[:, :, :c2], top[:, :, c2:]
  tinv = jnp.where(diag, 1.0, nm)
  npow = nm.astype(BF16)
  for _ in range(5):
    npow = _bdot(npow, npow).astype(BF16)
    tinv = tinv + _bdot(tinv.astype(BF16), npow)
  x1 = _bdot(mak.astype(BF16), vv_b)
  wu_b = _bdot(tinv.astype(BF16), jnp.concatenate([at_b, x1.astype(BF16)], axis=2)).astype(BF16)
  rhs = jnp.concatenate([wu_b, jnp.concatenate([jnp.zeros_like(vv_b), vv_b], axis=2)], axis=1)
  qy = _bdot(lblk.astype(BF16), rhs)
  q_h = (rt + qy[:, :, :c2]).astype(BF16)
  y0 = qy[:, :, c2:]
  uv = jnp.concatenate([wu_b[:, :, c2:], vv_b], axis=1)
  bkh = jnp.concatenate([bh_b, kh.astype(BF16)], axis=1)
  g_m = [(jnp.where(diag, chains["gd"][n], 0.0) + _dot_tn(wu_b[n, :, :c2], bh_b[n])).astype(BF16)
         for n in range(nb)]
  h_m = [_dot_tn(uv[n], bkh[n]) for n in range(nb)]

  states = [s_ref[p] for p in range(RW_P)]
  y_rows = [[] for _ in range(RW_P)]
  for cc in range(ncc):
    for p in range(RW_P):
      n = p * ncc + cc
      s_old_b = states[p].astype(BF16)
      y2 = _dot_nt(q_h[n], s_old_b) + y0[n]
      states[p] = jnp.dot(s_old_b, g_m[n], preferred_element_type=F32) + h_m[n]
      y_rows[p].append(y2[:c] + y2[c:])

  for p in range(RW_P):
    ls = slice(p * R_PAIR, (p + 1) * R_PAIR)
    s_ref[p] = states[p]
    bonus, gn_g, gn_b = post[p]
    y = jnp.concatenate(y_rows[p], axis=0)
    mean = _seg_sum(y) * (1.0 / nh)
    yc = y - mean
    var = _seg_sum(yc * yc) * (1.0 / nh)
    yn = yc * lax.rsqrt(var + GN_EPS) * gn_g + gn_b
    g = gr_ref[:, ls].astype(F32)
    o_ref[:, ls] = ((yn + bonus) * (g / (1.0 + jnp.exp(-g)))).astype(o_ref.dtype)


def _rwkv(z_x, lo_x, z_mp, lo_mp, pvec, mu_l, wup_h, wup_l, aup):
  b, s, _ = z_x.shape
  tb = RW_TB
  nt = s // tb + 1
  pw = R_PAIR * RW_P

  def xmap(col):
    return lambda bi, hp, ti: (bi, jnp.maximum(ti - 1, 0), col // pw + hp)

  def mmap(col):
    return lambda bi, hp, ti: (0, col // pw + hp)

  return pl.pallas_call(
      _rwkv_kernel,
      name="rwkv7",
      grid=(b, R_PAIRS // RW_P, nt),
      in_specs=[
          pl.BlockSpec((None, tb, pw), xmap(Z_RR)),
          pl.BlockSpec((None, tb, pw), xmap(Z_RK)),
          pl.BlockSpec((None, tb, pw), xmap(Z_RV)),
          pl.BlockSpec((None, tb, LORA_PAD), lambda bi, hp, ti: (bi, jnp.maximum(ti - 1, 0), 0)),
          pl.BlockSpec((None, tb, pw), xmap(Z_GR)),
          pl.BlockSpec((tb, pw), mmap(Z_RR)),
          pl.BlockSpec((tb, pw), mmap(Z_RK)),
          pl.BlockSpec((tb, pw), mmap(Z_RV)),
          pl.BlockSpec((tb, LORA_PAD), lambda bi, hp, ti: (0, 0)),
          pl.BlockSpec((16, pw), lambda bi, hp, ti: (0, hp)),
          pl.BlockSpec((1, LORA_PAD), lambda bi, hp, ti: (0, 0)),
          pl.BlockSpec((LORA_PAD, pw), lambda bi, hp, ti: (0, hp)),
          pl.BlockSpec((LORA_PAD, pw), lambda bi, hp, ti: (0, hp)),
          pl.BlockSpec((LORA_PAD, pw), lambda bi, hp, ti: (0, hp)),
      ],
      out_specs=pl.BlockSpec((None, tb, pw), lambda bi, hp, ti: (bi, jnp.maximum(ti - 1, 0), hp)),
      out_shape=jax.ShapeDtypeStruct((b, s, R_WIDTH), BF16),
      scratch_shapes=[
          pltpu.VMEM((RW_P, 2 * R_HEAD, 2 * R_HEAD), F32),
          pltpu.VMEM((8, pw), F32),
          pltpu.VMEM((1, LORA_PAD), F32),
      ],
      compiler_params=pltpu.CompilerParams(
          dimension_semantics=("parallel", "parallel", "arbitrary"),
          vmem_limit_bytes=VMEM_LIMIT),
  )(z_x, z_x, z_x, lo_x, z_x, z_mp, z_mp, z_mp, lo_mp, pvec, mu_l, wup_h, wup_l, aup)


def _out_kernel(x_ref, oa_ref, or_ref, wa_ref, wr_ref, ge_ref, be_ref, gp_ref, bp_ref, o_ref):
  h = _ln_rows(x_ref[...], ge_ref[...], be_ref[...])
  y = (jnp.dot(oa_ref[...], wa_ref[...], preferred_element_type=F32)
       + jnp.dot(or_ref[...], wr_ref[...], preferred_element_type=F32))
  o_ref[...] = _ln_rows(DEEPNORM_ALPHA * h + y, gp_ref[...], bp_ref[...])


def _out_proj(x2d, oa, orw, wa, wr, ge, be, gp, bp, tm):
  m, d = x2d.shape
  vec = pl.BlockSpec((1, d), lambda i: (0, 0))
  return pl.pallas_call(
      _out_kernel,
      name="out_proj",
      grid=(m // tm,),
      in_specs=[
          pl.BlockSpec((tm, d), lambda i: (i, 0)),
          pl.BlockSpec((tm, A_WIDTH), lambda i: (i, 0)),
          pl.BlockSpec((tm, R_WIDTH), lambda i: (i, 0)),
          pl.BlockSpec((A_WIDTH, d), lambda i: (0, 0), pipeline_mode=pl.Buffered(1)),
          pl.BlockSpec((R_WIDTH, d), lambda i: (0, 0), pipeline_mode=pl.Buffered(1)),
          vec, vec, vec, vec,
      ],
      out_specs=pl.BlockSpec((tm, d), lambda i: (i, 0)),
      out_shape=jax.ShapeDtypeStruct((m, d), F32),
      compiler_params=pltpu.CompilerParams(
          dimension_semantics=("parallel",),
          vmem_limit_bytes=VMEM_LIMIT),
  )(x2d, oa, orw, wa, wr, ge, be, gp, bp)


def kernel(x, meta_tokens, ln_emb_g, ln_emb_b, rel_bias, w_in, w_out, lambda_q1, lambda_k1, lambda_q2,
           lambda_k2, subln_g, rw_mu, rw_w0, rw_w_up, rw_a0, rw_a_up, rw_k_k, rw_k_a, rw_r_k, rw_gn_g,
           rw_gn_b, ln_post_g, ln_post_b):
  b, s, d = x.shape
  assert w_in.shape[0] == DEPTH == 1 and w_in.shape[1:] == (d, 4 * A_WIDTH + 4 * R_WIDTH + DECAY_LORA + ICLR_LORA)
  assert s % IN_TM == 0 and (b * s) % OUT_TM == 0 and meta_tokens.shape == (N_META, d)
  l = 0
  wi = w_in[l]
  c_lo = 4 * A_WIDTH + 3 * R_WIDTH
  c_gr = c_lo + DECAY_LORA + ICLR_LORA
  lora_pad = LORA_PAD - DECAY_LORA - ICLR_LORA
  wt_all = wi.T.astype(BF16)
  wt_gr = wt_all[c_gr:]
  wt_lora = jnp.pad(wt_all[c_lo:c_gr], ((0, lora_pad), (0, 0)))

  ge, be = ln_emb_g.reshape(1, d), ln_emb_b.reshape(1, d)
  x2d = x.reshape(b * s, d)
  z_x, vt_x, lo_x = _ln_matmul(x2d, ge, be, wt_all, wt_gr, wt_lora, IN_TM)
  z_x, lo_x = z_x.reshape(b, s, -1), lo_x.reshape(b, s, LORA_PAD)
  z_m, vt_m, lo_m = _ln_matmul(meta_tokens, ge, be, wt_all, wt_gr, wt_lora, N_META)

  bias_d, bias_s, bias_m = _bias_tiles(rel_bias)
  lam_p = jnp.stack([lambda_q1[l], lambda_k1[l], lambda_q2[l], lambda_k2[l]], axis=0)
  o_attn = _attention(z_x, vt_x, z_m, vt_m, bias_d, bias_s, bias_m, lam_p, subln_g[l].reshape(1, A_V_DIM))

  mu = rw_mu[l]
  zeros = jnp.zeros((R_WIDTH,), F32)
  pvec = jnp.stack([mu[:R_WIDTH], mu[R_WIDTH:2 * R_WIDTH], mu[2 * R_WIDTH:3 * R_WIDTH], rw_w0[l], rw_a0[l],
                    rw_k_k[l], rw_k_a[l], rw_r_k[l].reshape(R_WIDTH), rw_gn_g[l], rw_gn_b[l]]
                   + [zeros] * 6, axis=0)
  mu_l = jnp.pad(mu[3 * R_WIDTH:], (0, lora_pad)).reshape(1, LORA_PAD)
  wup = jnp.pad(rw_w_up[l], ((0, LORA_PAD - DECAY_LORA), (0, 0)))
  wup_h = wup.astype(BF16)
  wup_l = (wup - wup_h.astype(F32)).astype(BF16)
  aup = jnp.pad(rw_a_up[l], ((DECAY_LORA, lora_pad), (0, 0))).astype(BF16)
  front = ((RW_TB - N_META, 0), (0, 0))
  o_rwkv = _rwkv(z_x, lo_x, jnp.pad(z_m, front), jnp.pad(lo_m, front), pvec, mu_l, wup_h, wup_l, aup)

  wo = w_out[l].astype(BF16)
  out = _out_proj(x2d, o_attn.reshape(b * s, A_WIDTH), o_rwkv.reshape(b * s, R_WIDTH),
                  wo[:A_WIDTH], wo[A_WIDTH:], ge, be,
                  ln_post_g[l].reshape(1, d), ln_post_b[l].reshape(1, d), OUT_TM)
  return out.reshape(b, s, d)
```

```python
import functools
import math

import numpy as np
import jax
import jax.numpy as jnp
from jax import lax
from jax.experimental import pallas as pl
from jax.experimental.pallas import tpu as pltpu

N_META = 16
A_HEADS = 8
A_V_DIM = 128
A_QK_DIM = 64
A_WIDTH = A_HEADS * A_V_DIM
R_HEAD = 64
R_WIDTH = 1024
R_PAIR = 2 * R_HEAD
R_PAIRS = R_WIDTH // R_PAIR
A_QK_W = 2 * A_QK_DIM
DECAY_LORA = 96
ICLR_LORA = 96
LORA_PAD = 256
N_BUCKETS = 32
MAX_DISTANCE = 128
LN_EPS = 1e-5
SUBLN_EPS = 1e-5
GN_EPS = 64e-5
DEPTH = 1
DEEPNORM_ALPHA = (2 * DEPTH) ** 0.25
LAM_INIT = 0.8 - 0.6 * math.exp(-0.3 * 0)
NEG = -1e30

ATT_T = 256
ATT_G = 8
ONES_ROWS = 16
IN_TM = 1024
IN_TN = 1024
OUT_TM = 512
Z_Q, Z_K, Z_GA, Z_RR, Z_RK, Z_RV, Z_GR = (i * 1024 for i in range(7))
LOG2E = math.log2(math.e)
Q_SCALE = A_QK_DIM ** -0.5 * LOG2E
RW_TB = 128
RW_C = 64
RW_P = 8
META_LANES = 128
VMEM_LIMIT = 56 * 1024 * 1024

F32 = jnp.float32
BF16 = jnp.bfloat16


def _ln_rows(x, g, b):
  mu = jnp.mean(x, axis=-1, keepdims=True)
  xc = x - mu
  var = jnp.mean(xc * xc, axis=-1, keepdims=True)
  return xc * lax.rsqrt(var + LN_EPS) * g + b


def _ln_mm_kernel(x_ref, g_ref, b_ref, wm_ref, wgr_ref, wvt_ref, wl_ref, om_ref, ovt_ref, ol_ref, hn_ref):
  j = pl.program_id(1)
  n_main = pl.num_programs(1) - 1

  @pl.when(j == 0)
  def _():
    hn_ref[...] = _ln_rows(x_ref[...], g_ref[...], b_ref[...]).astype(BF16)

  @pl.when(j < n_main - 1)
  def _():
    scale = jnp.where(j == Z_Q // IN_TN, Q_SCALE, 1.0)
    z = _dot_nt(hn_ref[...], wm_ref[...])
    om_ref[...] = (z * scale).astype(om_ref.dtype)

  @pl.when(j == n_main - 1)
  def _():
    om_ref[...] = _dot_nt(hn_ref[...], wgr_ref[...]).astype(om_ref.dtype)
    ol_ref[...] = _dot_nt(hn_ref[...], wl_ref[...])

  @pl.when(j == n_main)
  def _():
    zt = _dot_nt(wvt_ref[...], hn_ref[...])
    tv = ovt_ref.shape[2]
    for c in range(ovt_ref.shape[0]):
      ovt_ref[c] = zt[:, c * tv:(c + 1) * tv].astype(ovt_ref.dtype)


def _cast_meta_kernel(n_a, xm_ref, g_ref, b_ref, wa_ref, wb_ref, oa_ref, ob_ref, za_ref, zb_ref, hn_ref):
  j = pl.program_id(0)

  @pl.when(j == 0)
  def _():
    hn_ref[...] = jnp.zeros_like(hn_ref)
    hn_ref[0:N_META] = _ln_rows(xm_ref[...], g_ref[...], b_ref[...]).astype(BF16)

  def project(w_ref, o_ref, z_ref):
    w = w_ref[...].astype(BF16)
    o_ref[...] = w
    z_ref[...] = _dot_nt(w, hn_ref[...])

  @pl.when(j < n_a)
  def _():
    project(wa_ref, oa_ref, za_ref)

  @pl.when(j >= n_a)
  def _():
    project(wb_ref, ob_ref, zb_ref)


def _cast_project_meta(wt, xm, g, b):
  n, d = wt.shape
  tn = IN_TN
  n_a = n // tn
  rem = n - n_a * tn
  tb = math.gcd(rem, tn)
  assert rem > 0 and tb % 16 == 0
  n_b = rem // tb
  vec = pl.BlockSpec((1, d), lambda j: (0, 0))

  def a_map(j):
    return (jnp.minimum(j, n_a - 1), 0)

  def b_map(j):
    return (jnp.clip(j - n_a, 0, n_b - 1), 0)

  return pl.pallas_call(
      functools.partial(_cast_meta_kernel, n_a),
      name="cast_meta",
      grid=(n_a + n_b,),
      in_specs=[
          pl.BlockSpec((N_META, d), lambda j: (0, 0)),
          vec, vec,
          pl.BlockSpec((tn, d), a_map),
          pl.BlockSpec((tb, d), lambda j: (n_a * tn // tb + jnp.clip(j - n_a, 0, n_b - 1), 0)),
      ],
      out_specs=[
          pl.BlockSpec((tn, d), a_map),
          pl.BlockSpec((tb, d), b_map),
          pl.BlockSpec((tn, META_LANES), a_map),
          pl.BlockSpec((tb, META_LANES), b_map),
      ],
      out_shape=[
          jax.ShapeDtypeStruct((n_a * tn, d), BF16),
          jax.ShapeDtypeStruct((rem, d), BF16),
          jax.ShapeDtypeStruct((n_a * tn, META_LANES), F32),
          jax.ShapeDtypeStruct((rem, META_LANES), F32),
      ],
      scratch_shapes=[pltpu.VMEM((META_LANES, d), BF16)],
      compiler_params=pltpu.CompilerParams(
          dimension_semantics=("arbitrary",),
          vmem_limit_bytes=VMEM_LIMIT),
  )(xm, g, b, wt, wt)


def _ln_matmul(x2d, g, b, wt_all, wt_gr, wt_lora):
  m, d = x2d.shape
  tm = IN_TM
  tn = IN_TN
  nj = Z_GR // tn + 1
  n = nj * tn
  last = nj - 1
  tv = ATT_T
  v_tile = 2 * A_WIDTH // tn
  once = dict(pipeline_mode=pl.Buffered(1))

  def w_map(i, j):
    jj = jnp.minimum(j, last - 1)
    return (jnp.where(jj >= v_tile, jj + 1, jj), 0)

  return pl.pallas_call(
      _ln_mm_kernel,
      name="ln_inproj",
      grid=(m // tm, nj + 1),
      in_specs=[
          pl.BlockSpec((tm, d), lambda i, j: (i, 0)),
          pl.BlockSpec((1, d), lambda i, j: (0, 0)),
          pl.BlockSpec((1, d), lambda i, j: (0, 0)),
          pl.BlockSpec((tn, d), w_map),
          pl.BlockSpec((tn, d), lambda i, j: (0, 0), **once),
          pl.BlockSpec((A_WIDTH, d), lambda i, j: (v_tile, 0), **once),
          pl.BlockSpec((LORA_PAD, d), lambda i, j: (0, 0), **once),
      ],
      out_specs=[
          pl.BlockSpec((tm, tn), lambda i, j: (i, jnp.minimum(j, last))),
          pl.BlockSpec((tm // tv, A_WIDTH, tv), lambda i, j: (i, 0, 0)),
          pl.BlockSpec((tm, LORA_PAD), lambda i, j: (i, 0)),
      ],
      out_shape=[
          jax.ShapeDtypeStruct((m, n), BF16),
          jax.ShapeDtypeStruct((m // tv, A_WIDTH, tv), BF16),
          jax.ShapeDtypeStruct((m, LORA_PAD), F32),
      ],
      scratch_shapes=[pltpu.VMEM((tm, d), BF16)],
      compiler_params=pltpu.CompilerParams(
          dimension_semantics=("parallel", "arbitrary"),
          vmem_limit_bytes=VMEM_LIMIT),
  )(x2d, g, b, wt_all, wt_gr, wt_all, wt_lora)


def _bucket_thresholds():
  n = np.arange(0, 4 * MAX_DISTANCE, dtype=np.int64)
  max_exact = N_BUCKETS // 2
  nf = np.maximum(n, 1).astype(np.float32)
  large = max_exact + (np.log(nf / np.float32(max_exact)) / np.float32(math.log(MAX_DISTANCE / max_exact))
                       * np.float32(N_BUCKETS - max_exact)).astype(np.int32)
  large = np.minimum(large, N_BUCKETS - 1)
  bucket = np.where(n < max_exact, n, large)
  assert np.all(np.diff(bucket) >= 0) and bucket[-1] == N_BUCKETS - 1
  return [int(np.argmax(bucket >= b)) for b in range(N_BUCKETS)]


_THR = _bucket_thresholds()


def _bias_kernel(rb_ref, diag_ref, sub_ref, meta_ref):
  h = pl.program_id(0)
  far = rb_ref[N_BUCKETS - 1, h]

  def bias_of(n):
    out = jnp.full(n.shape, (rb_ref[0, h] - far) * LOG2E, F32)
    for b in range(1, N_BUCKETS):
      out = jnp.where(n >= _THR[b], (rb_ref[b, h] - far) * LOG2E, out)
    return out

  t = ATT_T
  kj = lax.broadcasted_iota(jnp.int32, (t, t), 0)
  qi = lax.broadcasted_iota(jnp.int32, (t, t), 1)
  d = qi - kj
  diag_ref[...] = jnp.where(d >= 0, bias_of(d), NEG)
  sub_ref[...] = bias_of(d + t)
  km = lax.broadcasted_iota(jnp.int32, (N_META, t), 0)
  qm = lax.broadcasted_iota(jnp.int32, (N_META, t), 1)
  meta_ref[...] = bias_of(qm - km + N_META)


def _bias_tiles(rel_bias):
  t = ATT_T
  return pl.pallas_call(
      _bias_kernel,
      name="bias_tiles",
      grid=(A_HEADS,),
      in_specs=[pl.BlockSpec(memory_space=pltpu.SMEM)],
      out_specs=[
          pl.BlockSpec((None, t, t), lambda h: (h, 0, 0)),
          pl.BlockSpec((None, t, t), lambda h: (h, 0, 0)),
          pl.BlockSpec((None, N_META, t), lambda h: (h, 0, 0)),
      ],
      out_shape=[
          jax.ShapeDtypeStruct((A_HEADS, t, t), F32),
          jax.ShapeDtypeStruct((A_HEADS, t, t), F32),
          jax.ShapeDtypeStruct((A_HEADS, N_META, t), F32),
      ],
  )(rel_bias)


def _dot_nt(a, b):
  return lax.dot_general(a, b, (((1,), (1,)), ((), ())), preferred_element_type=F32)


def _attn_kernel(q_ref, kx_ref, vt_ref, km_ref, vmt_ref, ga_ref, bd_ref, bs_ref, bm_ref,
                 lam_ref, sg_ref, o_ref, m_ref, alpha_ref, acc_ref, pt_ref, ptm_ref):
  t = ATT_T
  g = ATT_G
  nc = 2 * g
  dv = A_V_DIM
  qi = pl.program_id(2)
  nq = pl.num_programs(2) - 1

  def v_tile(j):
    ones = jnp.ones((ONES_ROWS, t), BF16)
    return [jnp.concatenate([vt_ref[j, hh * dv:(hh + 1) * dv, :], ones], axis=0) for hh in range(g)]

  def k_tile(j):
    off = pl.multiple_of(j * t, t)
    return [kx_ref[pl.ds(off, t), hh * A_QK_W:(hh + 1) * A_QK_W] for hh in range(g)]

  def meta_v():
    ones_m = jnp.ones((ONES_ROWS, N_META), BF16)
    return [jnp.concatenate([vmt_ref[0, hh * dv:(hh + 1) * dv, :], ones_m], axis=0) for hh in range(g)]

  def queries():
    lane = lax.broadcasted_iota(jnp.int32, (t, A_QK_W), 1)
    qs = []
    for hh in range(g):
      q = q_ref[:, hh * A_QK_W:(hh + 1) * A_QK_W]
      zero = jnp.zeros_like(q)
      qs += [jnp.where(lane < A_QK_DIM, q, zero), jnp.where(lane >= A_QK_DIM, q, zero)]
    return qs

  def softmax_stage(c, m_prev, s_list):
    m_new = m_prev
    for s in s_list:
      m_new = jnp.maximum(m_new, jnp.max(s, axis=0, keepdims=True))
    m_ref[c] = m_new
    alpha_ref[c] = jnp.exp2(m_prev - m_new)
    return [jnp.exp2(s - m_new).astype(BF16) for s in s_list]

  def pending(segments, c):
    pv = None
    for vts, p_ref in segments:
      d = jnp.dot(vts[c // 2], p_ref[c], preferred_element_type=F32)
      pv = d if pv is None else pv + d
    return pv

  def finish_previous(segments):
    lp = lam_ref[...]
    lam = (jnp.exp(jnp.sum(lp[0:1] * lp[1:2], axis=1, keepdims=True))
           - jnp.exp(jnp.sum(lp[2:3] * lp[3:4], axis=1, keepdims=True)) + LAM_INIT)
    for hh in range(g):
      a0, a1 = (alpha_ref[c] * acc_ref[c] + pending(segments, c) for c in (2 * hh, 2 * hh + 1))
      ot = a0[:dv] / a0[dv:dv + 1] - lam * (a1[:dv] / a1[dv:dv + 1])
      ot = ot * lax.rsqrt(jnp.mean(ot * ot, axis=0, keepdims=True) + SUBLN_EPS)
      o = ot.T * (sg_ref[...] * (1.0 - LAM_INIT))
      gate = ga_ref[:, hh * dv:(hh + 1) * dv].astype(F32)
      o_ref[:, hh * dv:(hh + 1) * dv] = (o * (gate / (1.0 + jnp.exp(-gate)))).astype(o_ref.dtype)

  def start_tile(qs, previous):
    if previous is not None:
      finish_previous(previous)
    ks = k_tile(qi)
    st, sm = [], []
    for c in range(nc):
      hh = c // 2
      st.append(_dot_nt(ks[hh], qs[c]) + bd_ref[hh])
      sm.append(_dot_nt(km_ref[:, hh * A_QK_W:(hh + 1) * A_QK_W], qs[c]) + jnp.where(qi == 0, bm_ref[hh], 0.0))
    acc_ref[...] = jnp.zeros(acc_ref.shape, F32)
    m_start = jnp.full((1, t), NEG, F32)
    for c in range(nc):
      pt_ref[c], ptm_ref[c] = softmax_stage(c, m_start, [st[c], sm[c]])

  def step(qs, j_cur, biases, segments):
    ks = k_tile(j_cur)
    pv, st = [], []
    for c in range(nc):
      pv.append(pending(segments, c))
      s = _dot_nt(ks[c // 2], qs[c])
      st.append(s if biases is None else s + biases[c // 2])
    for c in range(nc):
      acc_ref[c] = alpha_ref[c] * acc_ref[c] + pv[c]
    for c in range(nc):
      pt_ref[c], = softmax_stage(c, m_ref[c], [st[c]])

  def below_diagonal(qs):
    step(qs, qi - 1, [bs_ref[hh] for hh in range(g)], [(v_tile(qi), pt_ref), (meta_v(), ptm_ref)])

  @pl.when(qi == 0)
  def _():
    start_tile(queries(), None)

  @pl.when(qi == 1)
  def _():
    qs = queries()
    start_tile(qs, [(v_tile(0), pt_ref), (meta_v(), ptm_ref)])
    below_diagonal(qs)

  @pl.when(jnp.logical_and(qi >= 2, qi < nq))
  def _():
    qs = queries()
    start_tile(qs, [(v_tile(jnp.maximum(qi - 3, 0)), pt_ref)])
    below_diagonal(qs)
    n_far = qi - 1

    def prev_of(j):
      return jnp.where(j == 0, qi - 1, j - 1)

    def far_body(i, carry):
      step(qs, 2 * i, None, [(v_tile(prev_of(2 * i)), pt_ref)])
      step(qs, 2 * i + 1, None, [(v_tile(2 * i), pt_ref)])
      return carry

    lax.fori_loop(0, n_far // 2, far_body, 0)

    @pl.when(n_far % 2 == 1)
    def _():
      step(qs, n_far - 1, None, [(v_tile(prev_of(n_far - 1)), pt_ref)])

  @pl.when(qi == nq)
  def _():
    finish_previous([(v_tile(nq - 3), pt_ref)])


def _attention(z_x, vt_x, z_m, vt_m, bias_d, bias_s, bias_m, lam_p, subln_g):
  b, s, _ = z_x.shape
  t = ATT_T
  g = ATT_G
  nq = s // t
  assert nq >= 3
  w = A_QK_W * g
  hb = A_HEADS // g
  kb, gb = Z_K // w, Z_GA // w

  def cur(qi):
    return jnp.minimum(qi, nq - 1)

  def prev(qi):
    return jnp.maximum(qi - 1, 0)

  return pl.pallas_call(
      _attn_kernel,
      name="diff_attn",
      grid=(b, hb, nq + 1),
      in_specs=[
          pl.BlockSpec((None, t, w), lambda bi, hi, qi: (bi, cur(qi), hi)),
          pl.BlockSpec((None, s, w), lambda bi, hi, qi: (bi, 0, kb + hi)),
          pl.BlockSpec((None, s // t, w, t), lambda bi, hi, qi: (bi, 0, hi, 0)),
          pl.BlockSpec((N_META, w), lambda bi, hi, qi: (0, kb + hi)),
          pl.BlockSpec((1, w, N_META), lambda bi, hi, qi: (0, hi, 0)),
          pl.BlockSpec((None, t, w), lambda bi, hi, qi: (bi, prev(qi), gb + hi)),
          pl.BlockSpec((g, t, t), lambda bi, hi, qi: (hi, 0, 0)),
          pl.BlockSpec((g, t, t), lambda bi, hi, qi: (hi, 0, 0)),
          pl.BlockSpec((g, N_META, t), lambda bi, hi, qi: (hi, 0, 0)),
          pl.BlockSpec((4, A_QK_DIM), lambda bi, hi, qi: (0, 0)),
          pl.BlockSpec((1, A_V_DIM), lambda bi, hi, qi: (0, 0)),
      ],
      out_specs=pl.BlockSpec((None, t, w), lambda bi, hi, qi: (bi, prev(qi), hi)),
      out_shape=jax.ShapeDtypeStruct((b, s, A_WIDTH), BF16),
      scratch_shapes=[
          pltpu.VMEM((2 * g, 1, t), F32),
          pltpu.VMEM((2 * g, 1, t), F32),
          pltpu.VMEM((2 * g, A_V_DIM + ONES_ROWS, t), F32),
          pltpu.VMEM((2 * g, t, t), BF16),
          pltpu.VMEM((2 * g, N_META, t), BF16),
      ],
      compiler_params=pltpu.CompilerParams(
          dimension_semantics=("parallel", "parallel", "arbitrary"),
          vmem_limit_bytes=VMEM_LIMIT),
  )(z_x, z_x, vt_x.reshape(b, s // t, A_WIDTH, t), z_m, vt_m, z_x, bias_d, bias_s, bias_m, lam_p, subln_g)


def _seg_sum(x):
  lane = lax.broadcasted_iota(jnp.int32, x.shape, 1)
  first = lane < R_HEAD
  lo = jnp.sum(jnp.where(first, x, 0.0), axis=1, keepdims=True)
  hi = jnp.sum(jnp.where(first, 0.0, x), axis=1, keepdims=True)
  return jnp.where(first, lo, hi)


def _split_bf16(x):
  hi = x.astype(BF16)
  return hi, x - hi.astype(F32)


def _dot_tn(a, b):
  return lax.dot_general(a, b, (((0,), (0,)), ((), ())), preferred_element_type=F32)


def _bdot(a, b):
  return lax.dot_general(a, b, (((2,), (1,)), ((0,), (0,))), preferred_element_type=F32)


def _bdot_nt(a, b):
  return lax.dot_general(a, b, (((2,), (2,)), ((0,), (0,))), preferred_element_type=F32)


def _rwkv_kernel(rx_ref, kx_ref, vx_ref, lx_ref, gr_ref, rm_ref, kmt_ref, vmt_ref, lm_ref,
                 pv_ref, mul_ref, wuph_ref, wupl_ref, aup_ref, o_ref, s_ref, prev_ref, prevl_ref):
  tb, c = RW_TB, RW_C
  nh = R_HEAD
  c2 = 2 * c
  ti = pl.program_id(2)
  is_meta = ti == 0

  @pl.when(is_meta)
  def _():
    s_ref[...] = jnp.zeros_like(s_ref)
    prev_ref[...] = jnp.zeros_like(prev_ref)
    prevl_ref[...] = jnp.zeros_like(prevl_ref)

  row = lax.broadcasted_iota(jnp.int32, (tb, R_PAIR), 0)
  rowl = lax.broadcasted_iota(jnp.int32, (tb, LORA_PAD), 0)

  def shifted(z, prev, mu, rows):
    z_prev = jnp.where(rows == 0, prev, pltpu.roll(z, 1, 0))
    return z + (z_prev - z) * mu

  z_l = jnp.where(is_meta, lm_ref[...], lx_ref[...])
  lo = shifted(z_l, prevl_ref[...], mul_ref[...], rowl)
  prevl_ref[...] = z_l[tb - 1:tb]
  th_h, th_l = _split_bf16(jnp.tanh(lo))
  w_lora = (jnp.dot(th_h, wuph_ref[...], preferred_element_type=F32)
            + jnp.dot(th_h, wupl_ref[...], preferred_element_type=F32)
            + jnp.dot(th_l.astype(BF16), wuph_ref[...], preferred_element_type=F32))
  a_lora = jnp.dot(lo.astype(BF16), aup_ref[...], preferred_element_type=F32)

  ii = lax.broadcasted_iota(jnp.int32, (tb, tb), 0)
  jj = lax.broadcasted_iota(jnp.int32, (tb, tb), 1)
  shift = int(math.log2(c))
  same = lax.shift_right_logical(ii, shift) == lax.shift_right_logical(jj, shift)
  cum_op = jnp.where(same, jnp.where(jj <= ii, 1.0, 0.0), 0.0).astype(BF16)

  ci = lax.broadcasted_iota(jnp.int32, (c2, c2), 0)
  cj = lax.broadcasted_iota(jnp.int32, (c2, c2), 1)
  diag = ci == cj
  strict2 = jnp.concatenate([cj < ci, cj < ci], axis=1)
  incl2 = jnp.concatenate([cj <= ci, cj <= ci], axis=1)
  first = lax.broadcasted_iota(jnp.int32, (c, R_PAIR), 1) < nh

  def stack(x):
    return jnp.concatenate([jnp.where(first, x, 0.0), jnp.where(first, 0.0, x)], axis=0)

  ncc = tb // c
  chains = {name: [] for name in ("at", "rt", "bt", "kt", "bh", "kh", "vv", "gd")}
  post = []
  for p in range(RW_P):
    ls = slice(p * R_PAIR, (p + 1) * R_PAIR)
    pv = pv_ref[:, ls]
    mu_r, mu_k, mu_v = pv[0:1], pv[1:2], pv[2:3]
    w0, a0, k_k, k_a, r_k, gn_g, gn_b = pv[3:4], pv[4:5], pv[5:6], pv[6:7], pv[7:8], pv[8:9], pv[9:10]

    z_r = jnp.where(is_meta, rm_ref[:, ls], rx_ref[:, ls]).astype(F32)
    z_k = jnp.where(is_meta, kmt_ref[:, ls], kx_ref[:, ls]).astype(F32)
    z_v = jnp.where(is_meta, vmt_ref[:, ls], vx_ref[:, ls]).astype(F32)
    r = shifted(z_r, prev_ref[0:1, ls], mu_r, row)
    k = shifted(z_k, prev_ref[1:2, ls], mu_k, row)
    v = shifted(z_v, prev_ref[2:3, ls], mu_v, row)
    prev_ref[0:1, ls] = z_r[tb - 1:tb]
    prev_ref[1:2, ls] = z_k[tb - 1:tb]
    prev_ref[2:3, ls] = z_v[tb - 1:tb]

    u = -(w0 + w_lora[:, ls])
    softplus = jnp.maximum(u, 0.0) + jnp.log(1.0 + jnp.exp(-jnp.abs(u)))
    logw = -jnp.exp(-softplus - 0.5) * LOG2E
    a = 1.0 / (1.0 + jnp.exp(-(a0 + a_lora[:, ls])))
    kk = k * k_k
    kk = kk / jnp.maximum(jnp.sqrt(_seg_sum(kk * kk)), 1e-12)
    k_mod = k * (1.0 + (a - 1.0) * k_a)
    bonus = _seg_sum(r * k_mod * r_k) * v

    lw_h, lw_r = _split_bf16(logw)
    lw_m, lw_l = _split_bf16(lw_r)
    cum3 = jnp.dot(cum_op, jnp.concatenate([lw_h, lw_m, lw_l.astype(BF16)], axis=1),
                   preferred_element_type=F32)
    cum = cum3[:, :R_PAIR] + cum3[:, R_PAIR:2 * R_PAIR] + cum3[:, 2 * R_PAIR:]
    tot = jnp.concatenate([jnp.broadcast_to(cum[cc * c + c - 1:cc * c + c], (c, R_PAIR)) for cc in range(ncc)],
                          axis=0)
    p_inv = jnp.exp2(-cum)
    a_t = -kk * jnp.exp2(cum - logw)
    kka = kk * a
    b_t = kka * p_inv
    k_t = k_mod * p_inv
    r_t = r * jnp.exp2(cum)
    p_end = jnp.exp2(tot - cum)
    b_h = kka * p_end
    k_h = k_mod * p_end
    g_diag = jnp.exp2(tot)

    for cc in range(ncc):
      rs = slice(cc * c, (cc + 1) * c)
      for name, val in (("at", a_t), ("rt", r_t), ("bt", b_t), ("kt", k_t), ("bh", b_h), ("kh", k_h),
                        ("vv", v)):
        chains[name].append(stack(val[rs]))
      chains["gd"].append(g_diag[cc * c:cc * c + 1])
    post.append((bonus, gn_g, gn_b))

  nb = RW_P * ncc
  at, rt, bt, kt, bh, kh, vv = (jnp.stack(chains[name]) for name in ("at", "rt", "bt", "kt", "bh", "kh", "vv"))
  at_b, vv_b, bh_b = at.astype(BF16), vv.astype(BF16), bh.astype(BF16)
  bk = jnp.concatenate([bt, kt], axis=1).astype(BF16)
  top = jnp.where(strict2, _bdot_nt(at_b, bk), 0.0)
  lblk = jnp.where(incl2, _bdot_nt(rt.astype(BF16), bk), 0.0)
  nm, mak = top[:, :, :c2], top[:, :, c2:]
  tinv = jnp.where(diag, 1.0, nm)
  npow = nm.astype(BF16)
  for _ in range(5):
    npow = _bdot(npow, npow).astype(BF16)
    tinv = tinv + _bdot(tinv.astype(BF16), npow)
  x1 = _bdot(mak.astype(BF16), vv_b)
  wu_b = _bdot(tinv.astype(BF16), jnp.concatenate([at_b, x1.astype(BF16)], axis=2)).astype(BF16)
  rhs = jnp.concatenate([wu_b, jnp.concatenate([jnp.zeros_like(vv_b), vv_b], axis=2)], axis=1)
  qy = _bdot(lblk.astype(BF16), rhs)
  q_h = (rt + qy[:, :, :c2]).astype(BF16)
  y0 = qy[:, :, c2:]
  uv = jnp.concatenate([wu_b[:, :, c2:], vv_b], axis=1)
  bkh = jnp.concatenate([bh_b, kh.astype(BF16)], axis=1)
  g_m = [(jnp.where(diag, chains["gd"][n], 0.0) + _dot_tn(wu_b[n, :, :c2], bh_b[n])).astype(BF16)
         for n in range(nb)]
  h_m = [_dot_tn(uv[n], bkh[n]) for n in range(nb)]

  states = [s_ref[p] for p in range(RW_P)]
  y_rows = [[] for _ in range(RW_P)]
  for cc in range(ncc):
    for p in range(RW_P):
      n = p * ncc + cc
      s_old_b = states[p].astype(BF16)
      y2 = _dot_nt(q_h[n], s_old_b) + y0[n]
      states[p] = jnp.dot(s_old_b, g_m[n], preferred_element_type=F32) + h_m[n]
      y_rows[p].append(y2[:c] + y2[c:])

  for p in range(RW_P):
    ls = slice(p * R_PAIR, (p + 1) * R_PAIR)
    s_ref[p] = states[p]
    bonus, gn_g, gn_b = post[p]
    y = jnp.concatenate(y_rows[p], axis=0)
    mean = _seg_sum(y) * (1.0 / nh)
    yc = y - mean
    var = _seg_sum(yc * yc) * (1.0 / nh)
    yn = yc * lax.rsqrt(var + GN_EPS) * gn_g + gn_b
    g = gr_ref[:, ls].astype(F32)
    o_ref[:, ls] = ((yn + bonus) * (g / (1.0 + jnp.exp(-g)))).astype(o_ref.dtype)


def _rwkv(z_x, lo_x, z_mp, lo_mp, pvec, mu_l, wup_h, wup_l, aup):
  b, s, _ = z_x.shape
  tb = RW_TB
  nt = s // tb + 1
  pw = R_PAIR * RW_P

  def xmap(col):
    return lambda bi, hp, ti: (bi, jnp.maximum(ti - 1, 0), col // pw + hp)

  def mmap(col):
    return lambda bi, hp, ti: (0, col // pw + hp)

  return pl.pallas_call(
      _rwkv_kernel,
      name="rwkv7",
      grid=(b, R_PAIRS // RW_P, nt),
      in_specs=[
          pl.BlockSpec((None, tb, pw), xmap(Z_RR)),
          pl.BlockSpec((None, tb, pw), xmap(Z_RK)),
          pl.BlockSpec((None, tb, pw), xmap(Z_RV)),
          pl.BlockSpec((None, tb, LORA_PAD), lambda bi, hp, ti: (bi, jnp.maximum(ti - 1, 0), 0)),
          pl.BlockSpec((None, tb, pw), xmap(Z_GR)),
          pl.BlockSpec((tb, pw), mmap(Z_RR)),
          pl.BlockSpec((tb, pw), mmap(Z_RK)),
          pl.BlockSpec((tb, pw), mmap(Z_RV)),
          pl.BlockSpec((tb, LORA_PAD), lambda bi, hp, ti: (0, 0)),
          pl.BlockSpec((16, pw), lambda bi, hp, ti: (0, hp)),
          pl.BlockSpec((1, LORA_PAD), lambda bi, hp, ti: (0, 0)),
          pl.BlockSpec((LORA_PAD, pw), lambda bi, hp, ti: (0, hp)),
          pl.BlockSpec((LORA_PAD, pw), lambda bi, hp, ti: (0, hp)),
          pl.BlockSpec((LORA_PAD, pw), lambda bi, hp, ti: (0, hp)),
      ],
      out_specs=pl.BlockSpec((None, tb, pw), lambda bi, hp, ti: (bi, jnp.maximum(ti - 1, 0), hp)),
      out_shape=jax.ShapeDtypeStruct((b, s, R_WIDTH), BF16),
      scratch_shapes=[
          pltpu.VMEM((RW_P, 2 * R_HEAD, 2 * R_HEAD), F32),
          pltpu.VMEM((8, pw), F32),
          pltpu.VMEM((1, LORA_PAD), F32),
      ],
      compiler_params=pltpu.CompilerParams(
          dimension_semantics=("parallel", "parallel", "arbitrary"),
          vmem_limit_bytes=VMEM_LIMIT),
  )(z_x, z_x, z_x, lo_x, z_x, z_mp, z_mp, z_mp, lo_mp, pvec, mu_l, wup_h, wup_l, aup)


def _out_kernel(x_ref, oa_ref, or_ref, wa_ref, wr_ref, ge_ref, be_ref, gp_ref, bp_ref, o_ref):
  h = _ln_rows(x_ref[...], ge_ref[...], be_ref[...])
  y = (jnp.dot(oa_ref[...], wa_ref[...], preferred_element_type=F32)
       + jnp.dot(or_ref[...], wr_ref[...], preferred_element_type=F32))
  o_ref[...] = _ln_rows(DEEPNORM_ALPHA * h + y, gp_ref[...], bp_ref[...])


def _out_proj(x2d, oa, orw, wa, wr, ge, be, gp, bp, tm):
  m, d = x2d.shape
  vec = pl.BlockSpec((1, d), lambda i: (0, 0))
  return pl.pallas_call(
      _out_kernel,
      name="out_proj",
      grid=(m // tm,),
      in_specs=[
          pl.BlockSpec((tm, d), lambda i: (i, 0)),
          pl.BlockSpec((tm, A_WIDTH), lambda i: (i, 0)),
          pl.BlockSpec((tm, R_WIDTH), lambda i: (i, 0)),
          pl.BlockSpec((A_WIDTH, d), lambda i: (0, 0), pipeline_mode=pl.Buffered(1)),
          pl.BlockSpec((R_WIDTH, d), lambda i: (0, 0), pipeline_mode=pl.Buffered(1)),
          vec, vec, vec, vec,
      ],
      out_specs=pl.BlockSpec((tm, d), lambda i: (i, 0)),
      out_shape=jax.ShapeDtypeStruct((m, d), F32),
      compiler_params=pltpu.CompilerParams(
          dimension_semantics=("parallel",),
          vmem_limit_bytes=VMEM_LIMIT),
  )(x2d, oa, orw, wa, wr, ge, be, gp, bp)


def kernel(x, meta_tokens, ln_emb_g, ln_emb_b, rel_bias, w_in, w_out, lambda_q1, lambda_k1, lambda_q2,
           lambda_k2, subln_g, rw_mu, rw_w0, rw_w_up, rw_a0, rw_a_up, rw_k_k, rw_k_a, rw_r_k, rw_gn_g,
           rw_gn_b, ln_post_g, ln_post_b):
  b, s, d = x.shape
  assert w_in.shape[0] == DEPTH == 1 and w_in.shape[1:] == (d, 4 * A_WIDTH + 4 * R_WIDTH + DECAY_LORA + ICLR_LORA)
  assert s % IN_TM == 0 and (b * s) % OUT_TM == 0 and meta_tokens.shape == (N_META, d)
  l = 0
  wi = w_in[l]
  c_lo = 4 * A_WIDTH + 3 * R_WIDTH
  c_gr = c_lo + DECAY_LORA + ICLR_LORA
  lora_pad = LORA_PAD - DECAY_LORA - ICLR_LORA
  ge, be = ln_emb_g.reshape(1, d), ln_emb_b.reshape(1, d)
  wt_all, wt_tail, zt_main, zt_tail = _cast_project_meta(wi.T, meta_tokens, ge, be)
  assert c_gr <= wt_all.shape[0]
  wt_gr = jnp.concatenate([wt_all[c_gr:], wt_tail], axis=0)
  wt_lora = jnp.pad(wt_all[c_lo:c_gr], ((0, lora_pad), (0, 0)))
  zt = jnp.concatenate([zt_main, zt_tail], axis=0)[:, :N_META]
  zm = zt.T
  z_m = jnp.concatenate([zm[:, :A_WIDTH] * Q_SCALE, zm[:, A_WIDTH:2 * A_WIDTH], zm[:, 3 * A_WIDTH:c_lo],
                         zm[:, c_gr:]], axis=1).astype(BF16)
  vt_m = zt[2 * A_WIDTH:3 * A_WIDTH].astype(BF16).reshape(1, A_WIDTH, N_META)
  lo_m = jnp.pad(zm[:, c_lo:c_gr], ((0, 0), (0, lora_pad)))

  x2d = x.reshape(b * s, d)
  z_x, vt_x, lo_x = _ln_matmul(x2d, ge, be, wt_all, wt_gr, wt_lora)
  z_x, lo_x = z_x.reshape(b, s, -1), lo_x.reshape(b, s, LORA_PAD)

  bias_d, bias_s, bias_m = _bias_tiles(rel_bias)
  lam_p = jnp.stack([lambda_q1[l], lambda_k1[l], lambda_q2[l], lambda_k2[l]], axis=0)
  o_attn = _attention(z_x, vt_x, z_m, vt_m, bias_d, bias_s, bias_m, lam_p, subln_g[l].reshape(1, A_V_DIM))

  mu = rw_mu[l]
  zeros = jnp.zeros((R_WIDTH,), F32)
  pvec = jnp.stack([mu[:R_WIDTH], mu[R_WIDTH:2 * R_WIDTH], mu[2 * R_WIDTH:3 * R_WIDTH], rw_w0[l], rw_a0[l],
                    rw_k_k[l], rw_k_a[l], rw_r_k[l].reshape(R_WIDTH), rw_gn_g[l], rw_gn_b[l]]
                   + [zeros] * 6, axis=0)
  mu_l = jnp.pad(mu[3 * R_WIDTH:], (0, lora_pad)).reshape(1, LORA_PAD)
  wup = jnp.pad(rw_w_up[l], ((0, LORA_PAD - DECAY_LORA), (0, 0)))
  wup_h = wup.astype(BF16)
  wup_l = (wup - wup_h.astype(F32)).astype(BF16)
  aup = jnp.pad(rw_a_up[l], ((DECAY_LORA, lora_pad), (0, 0))).astype(BF16)
  front = ((RW_TB - N_META, 0), (0, 0))
  o_rwkv = _rwkv(z_x, lo_x, jnp.pad(z_m, front), jnp.pad(lo_m, front), pvec, mu_l, wup_h, wup_l, aup)

  wo = w_out[l].astype(BF16)
  out = _out_proj(x2d, o_attn.reshape(b * s, A_WIDTH), o_rwkv.reshape(b * s, R_WIDTH),
                  wo[:A_WIDTH], wo[A_WIDTH:], ge, be,
                  ln_post_g[l].reshape(1, d), ln_post_b[l].reshape(1, d), OUT_TM)
  return out.reshape(b, s, d)
```

```python
import functools
import math

import numpy as np
import jax
import jax.numpy as jnp
from jax import lax
from jax.experimental import pallas as pl
from jax.experimental.pallas import tpu as pltpu

N_META = 16
A_HEADS = 8
A_V_DIM = 128
A_QK_DIM = 64
A_WIDTH = A_HEADS * A_V_DIM
R_HEAD = 64
R_WIDTH = 1024
R_PAIR = 2 * R_HEAD
R_PAIRS = R_WIDTH // R_PAIR
A_QK_W = 2 * A_QK_DIM
DECAY_LORA = 96
ICLR_LORA = 96
LORA_PAD = 256
N_BUCKETS = 32
MAX_DISTANCE = 128
LN_EPS = 1e-5
SUBLN_EPS = 1e-5
GN_EPS = 64e-5
DEPTH = 1
DEEPNORM_ALPHA = (2 * DEPTH) ** 0.25
LAM_INIT = 0.8 - 0.6 * math.exp(-0.3 * 0)
NEG = -1e30

ATT_T = 256
ATT_G = 8
ONES_ROWS = 16
IN_TM = 1024
IN_TN = 1024
OUT_TM = 512
Z_Q, Z_K, Z_GA, Z_RR, Z_RK, Z_RV, Z_GR = (i * 1024 for i in range(7))
LOG2E = math.log2(math.e)
Q_SCALE = A_QK_DIM ** -0.5 * LOG2E
RW_TB = 128
RW_C = 64
RW_P = 8
META_LANES = 128
VMEM_LIMIT = 56 * 1024 * 1024

F32 = jnp.float32
BF16 = jnp.bfloat16


def _ln_rows(x, g, b):
  mu = jnp.mean(x, axis=-1, keepdims=True)
  xc = x - mu
  var = jnp.mean(xc * xc, axis=-1, keepdims=True)
  return xc * lax.rsqrt(var + LN_EPS) * g + b


def _ln_mm_kernel(x_ref, g_ref, b_ref, wm_ref, wgr_ref, wvt_ref, wl_ref, om_ref, ovt_ref, ol_ref, hn_ref):
  j = pl.program_id(1)
  n_main = pl.num_programs(1) - 1

  @pl.when(j == 0)
  def _():
    hn_ref[...] = _ln_rows(x_ref[...], g_ref[...], b_ref[...]).astype(BF16)

  @pl.when(j < n_main - 1)
  def _():
    scale = jnp.where(j == Z_Q // IN_TN, Q_SCALE, 1.0)
    z = _dot_nt(hn_ref[...], wm_ref[...])
    om_ref[...] = (z * scale).astype(om_ref.dtype)

  @pl.when(j == n_main - 1)
  def _():
    om_ref[...] = _dot_nt(hn_ref[...], wgr_ref[...]).astype(om_ref.dtype)
    ol_ref[...] = _dot_nt(hn_ref[...], wl_ref[...])

  @pl.when(j == n_main)
  def _():
    zt = _dot_nt(wvt_ref[...], hn_ref[...])
    tv = ovt_ref.shape[2]
    for c in range(ovt_ref.shape[0]):
      ovt_ref[c] = zt[:, c * tv:(c + 1) * tv].astype(ovt_ref.dtype)


def _cast_meta_kernel(n_a, n_lo, xm_ref, g_ref, b_ref, wa_ref, wb_ref, oa_ref, ogr_ref, olo_ref, za_ref, zb_ref,
                      hn_ref):
  j = pl.program_id(0)
  tn = wa_ref.shape[0]
  tb = wb_ref.shape[0]

  @pl.when(j == 0)
  def _():
    hn_ref[...] = jnp.zeros_like(hn_ref)
    hn_ref[0:N_META] = _ln_rows(xm_ref[...], g_ref[...], b_ref[...]).astype(BF16)

  def meta_rows(w):
    return _dot_nt(w, hn_ref[...])

  @pl.when(j < n_a - 1)
  def _():
    w = wa_ref[...].astype(BF16)
    oa_ref[...] = w
    za_ref[...] = meta_rows(w).T[:N_META]

  @pl.when(j == n_a - 1)
  def _():
    w = wa_ref[...].astype(BF16)
    olo_ref[0:n_lo] = w[:n_lo]
    olo_ref[n_lo:] = jnp.zeros((olo_ref.shape[0] - n_lo, w.shape[1]), BF16)
    ogr_ref[0:tn - n_lo] = w[n_lo:]
    za_ref[...] = meta_rows(w).T[:N_META]

  @pl.when(j >= n_a)
  def _():
    w = wb_ref[...].astype(BF16)
    ogr_ref[pl.ds(pl.multiple_of(tn - n_lo + (j - n_a) * tb, tb), tb)] = w
    zb_ref[...] = meta_rows(w)


def _cast_project_meta(wt, xm, g, b, c_lo):
  n, d = wt.shape
  tn = IN_TN
  n_a = n // tn
  rem = n - n_a * tn
  tb = math.gcd(rem, tn)
  n_b = rem // tb
  n_lo = n - tn - c_lo
  assert c_lo == (n_a - 1) * tn and n_lo == rem and 0 < n_lo <= LORA_PAD and tb % 16 == 0 and n_lo % 16 == 0
  vec = pl.BlockSpec((1, d), lambda j: (0, 0))
  whole = lambda j: (0, 0)

  return pl.pallas_call(
      functools.partial(_cast_meta_kernel, n_a, n_lo),
      name="cast_meta",
      grid=(n_a + n_b,),
      in_specs=[
          pl.BlockSpec((N_META, d), whole),
          vec, vec,
          pl.BlockSpec((tn, d), lambda j: (jnp.minimum(j, n_a - 1), 0)),
          pl.BlockSpec((tb, d), lambda j: (n_a * tn // tb + jnp.clip(j - n_a, 0, n_b - 1), 0)),
      ],
      out_specs=[
          pl.BlockSpec((tn, d), lambda j: (jnp.minimum(j, n_a - 2), 0)),
          pl.BlockSpec((tn, d), whole),
          pl.BlockSpec((LORA_PAD, d), whole),
          pl.BlockSpec((N_META, tn), lambda j: (0, jnp.minimum(j, n_a - 1))),
          pl.BlockSpec((tb, META_LANES), lambda j: (jnp.clip(j - n_a, 0, n_b - 1), 0)),
      ],
      out_shape=[
          jax.ShapeDtypeStruct((c_lo, d), BF16),
          jax.ShapeDtypeStruct((tn, d), BF16),
          jax.ShapeDtypeStruct((LORA_PAD, d), BF16),
          jax.ShapeDtypeStruct((N_META, n_a * tn), F32),
          jax.ShapeDtypeStruct((rem, META_LANES), F32),
      ],
      scratch_shapes=[pltpu.VMEM((META_LANES, d), BF16)],
      compiler_params=pltpu.CompilerParams(
          dimension_semantics=("arbitrary",),
          vmem_limit_bytes=VMEM_LIMIT),
  )(xm, g, b, wt, wt)


def _ln_matmul(x2d, g, b, wt_all, wt_gr, wt_lora):
  m, d = x2d.shape
  tm = IN_TM
  tn = IN_TN
  nj = Z_GR // tn + 1
  n = nj * tn
  last = nj - 1
  tv = ATT_T
  v_tile = 2 * A_WIDTH // tn
  once = dict(pipeline_mode=pl.Buffered(1))

  def w_map(i, j):
    jj = jnp.minimum(j, last - 1)
    return (jnp.where(jj >= v_tile, jj + 1, jj), 0)

  return pl.pallas_call(
      _ln_mm_kernel,
      name="ln_inproj",
      grid=(m // tm, nj + 1),
      in_specs=[
          pl.BlockSpec((tm, d), lambda i, j: (i, 0)),
          pl.BlockSpec((1, d), lambda i, j: (0, 0)),
          pl.BlockSpec((1, d), lambda i, j: (0, 0)),
          pl.BlockSpec((tn, d), w_map),
          pl.BlockSpec((tn, d), lambda i, j: (0, 0), **once),
          pl.BlockSpec((A_WIDTH, d), lambda i, j: (v_tile, 0), **once),
          pl.BlockSpec((LORA_PAD, d), lambda i, j: (0, 0), **once),
      ],
      out_specs=[
          pl.BlockSpec((tm, tn), lambda i, j: (i, jnp.minimum(j, last))),
          pl.BlockSpec((tm // tv, A_WIDTH, tv), lambda i, j: (i, 0, 0)),
          pl.BlockSpec((tm, LORA_PAD), lambda i, j: (i, 0)),
      ],
      out_shape=[
          jax.ShapeDtypeStruct((m, n), BF16),
          jax.ShapeDtypeStruct((m // tv, A_WIDTH, tv), BF16),
          jax.ShapeDtypeStruct((m, LORA_PAD), F32),
      ],
      scratch_shapes=[pltpu.VMEM((tm, d), BF16)],
      compiler_params=pltpu.CompilerParams(
          dimension_semantics=("parallel", "arbitrary"),
          vmem_limit_bytes=VMEM_LIMIT),
  )(x2d, g, b, wt_all, wt_gr, wt_all, wt_lora)


def _bucket_thresholds():
  n = np.arange(0, 4 * MAX_DISTANCE, dtype=np.int64)
  max_exact = N_BUCKETS // 2
  nf = np.maximum(n, 1).astype(np.float32)
  large = max_exact + (np.log(nf / np.float32(max_exact)) / np.float32(math.log(MAX_DISTANCE / max_exact))
                       * np.float32(N_BUCKETS - max_exact)).astype(np.int32)
  large = np.minimum(large, N_BUCKETS - 1)
  bucket = np.where(n < max_exact, n, large)
  assert np.all(np.diff(bucket) >= 0) and bucket[-1] == N_BUCKETS - 1
  return [int(np.argmax(bucket >= b)) for b in range(N_BUCKETS)]


_THR = _bucket_thresholds()


def _bias_kernel(rb_ref, diag_ref, sub_ref, meta_ref):
  h = pl.program_id(0)
  far = rb_ref[N_BUCKETS - 1, h]

  def bias_of(n):
    out = jnp.full(n.shape, (rb_ref[0, h] - far) * LOG2E, F32)
    for b in range(1, N_BUCKETS):
      out = jnp.where(n >= _THR[b], (rb_ref[b, h] - far) * LOG2E, out)
    return out

  t = ATT_T
  kj = lax.broadcasted_iota(jnp.int32, (t, t), 0)
  qi = lax.broadcasted_iota(jnp.int32, (t, t), 1)
  d = qi - kj
  diag_ref[...] = jnp.where(d >= 0, bias_of(d), NEG)
  sub_ref[...] = bias_of(d + t)
  km = lax.broadcasted_iota(jnp.int32, (N_META, t), 0)
  qm = lax.broadcasted_iota(jnp.int32, (N_META, t), 1)
  meta_ref[...] = bias_of(qm - km + N_META)


def _bias_tiles(rel_bias):
  t = ATT_T
  return pl.pallas_call(
      _bias_kernel,
      name="bias_tiles",
      grid=(A_HEADS,),
      in_specs=[pl.BlockSpec(memory_space=pltpu.SMEM)],
      out_specs=[
          pl.BlockSpec((None, t, t), lambda h: (h, 0, 0)),
          pl.BlockSpec((None, t, t), lambda h: (h, 0, 0)),
          pl.BlockSpec((None, N_META, t), lambda h: (h, 0, 0)),
      ],
      out_shape=[
          jax.ShapeDtypeStruct((A_HEADS, t, t), F32),
          jax.ShapeDtypeStruct((A_HEADS, t, t), F32),
          jax.ShapeDtypeStruct((A_HEADS, N_META, t), F32),
      ],
  )(rel_bias)


def _dot_nt(a, b):
  return lax.dot_general(a, b, (((1,), (1,)), ((), ())), preferred_element_type=F32)


def _attn_kernel(q_ref, kx_ref, vt_ref, km_ref, vmt_ref, ga_ref, bd_ref, bs_ref, bm_ref,
                 lam_ref, sg_ref, o_ref, m_ref, alpha_ref, acc_ref, pt_ref, ptm_ref):
  t = ATT_T
  g = ATT_G
  nc = 2 * g
  dv = A_V_DIM
  qi = pl.program_id(2)
  nq = pl.num_programs(2) - 1

  def v_tile(j):
    ones = jnp.ones((ONES_ROWS, t), BF16)
    return [jnp.concatenate([vt_ref[j, hh * dv:(hh + 1) * dv, :], ones], axis=0) for hh in range(g)]

  def k_tile(j):
    off = pl.multiple_of(j * t, t)
    return [kx_ref[pl.ds(off, t), hh * A_QK_W:(hh + 1) * A_QK_W] for hh in range(g)]

  def meta_v():
    ones_m = jnp.ones((ONES_ROWS, N_META), BF16)
    return [jnp.concatenate([vmt_ref[0, hh * dv:(hh + 1) * dv, :], ones_m], axis=0) for hh in range(g)]

  def queries():
    lane = lax.broadcasted_iota(jnp.int32, (t, A_QK_W), 1)
    qs = []
    for hh in range(g):
      q = q_ref[:, hh * A_QK_W:(hh + 1) * A_QK_W]
      zero = jnp.zeros_like(q)
      qs += [jnp.where(lane < A_QK_DIM, q, zero), jnp.where(lane >= A_QK_DIM, q, zero)]
    return qs

  def softmax_stage(c, m_prev, s_list):
    m_new = m_prev
    for s in s_list:
      m_new = jnp.maximum(m_new, jnp.max(s, axis=0, keepdims=True))
    m_ref[c] = m_new
    alpha_ref[c] = jnp.exp2(m_prev - m_new)
    return [jnp.exp2(s - m_new).astype(BF16) for s in s_list]

  def pending(segments, c):
    pv = None
    for vts, p_ref in segments:
      d = jnp.dot(vts[c // 2], p_ref[c], preferred_element_type=F32)
      pv = d if pv is None else pv + d
    return pv

  def finish_previous(segments):
    lp = lam_ref[...]
    lam = (jnp.exp(jnp.sum(lp[0:1] * lp[1:2], axis=1, keepdims=True))
           - jnp.exp(jnp.sum(lp[2:3] * lp[3:4], axis=1, keepdims=True)) + LAM_INIT)
    for hh in range(g):
      a0, a1 = (alpha_ref[c] * acc_ref[c] + pending(segments, c) for c in (2 * hh, 2 * hh + 1))
      ot = a0[:dv] / a0[dv:dv + 1] - lam * (a1[:dv] / a1[dv:dv + 1])
      ot = ot * lax.rsqrt(jnp.mean(ot * ot, axis=0, keepdims=True) + SUBLN_EPS)
      o = ot.T * (sg_ref[...] * (1.0 - LAM_INIT))
      gate = ga_ref[:, hh * dv:(hh + 1) * dv].astype(F32)
      o_ref[:, hh * dv:(hh + 1) * dv] = (o * (gate / (1.0 + jnp.exp(-gate)))).astype(o_ref.dtype)

  def start_tile(qs, previous):
    if previous is not None:
      finish_previous(previous)
    ks = k_tile(qi)
    st, sm = [], []
    for c in range(nc):
      hh = c // 2
      st.append(_dot_nt(ks[hh], qs[c]) + bd_ref[hh])
      sm.append(_dot_nt(km_ref[:, hh * A_QK_W:(hh + 1) * A_QK_W], qs[c]) + jnp.where(qi == 0, bm_ref[hh], 0.0))
    acc_ref[...] = jnp.zeros(acc_ref.shape, F32)
    m_start = jnp.full((1, t), NEG, F32)
    for c in range(nc):
      pt_ref[c], ptm_ref[c] = softmax_stage(c, m_start, [st[c], sm[c]])

  def step(qs, j_cur, biases, segments):
    ks = k_tile(j_cur)
    pv, st = [], []
    for c in range(nc):
      pv.append(pending(segments, c))
      s = _dot_nt(ks[c // 2], qs[c])
      st.append(s if biases is None else s + biases[c // 2])
    for c in range(nc):
      acc_ref[c] = alpha_ref[c] * acc_ref[c] + pv[c]
    for c in range(nc):
      pt_ref[c], = softmax_stage(c, m_ref[c], [st[c]])

  def below_diagonal(qs):
    step(qs, qi - 1, [bs_ref[hh] for hh in range(g)], [(v_tile(qi), pt_ref), (meta_v(), ptm_ref)])

  @pl.when(qi == 0)
  def _():
    start_tile(queries(), None)

  @pl.when(qi == 1)
  def _():
    qs = queries()
    start_tile(qs, [(v_tile(0), pt_ref), (meta_v(), ptm_ref)])
    below_diagonal(qs)

  @pl.when(jnp.logical_and(qi >= 2, qi < nq))
  def _():
    qs = queries()
    start_tile(qs, [(v_tile(jnp.maximum(qi - 3, 0)), pt_ref)])
    below_diagonal(qs)
    n_far = qi - 1

    def prev_of(j):
      return jnp.where(j == 0, qi - 1, j - 1)

    def far_body(i, carry):
      step(qs, 2 * i, None, [(v_tile(prev_of(2 * i)), pt_ref)])
      step(qs, 2 * i + 1, None, [(v_tile(2 * i), pt_ref)])
      return carry

    lax.fori_loop(0, n_far // 2, far_body, 0)

    @pl.when(n_far % 2 == 1)
    def _():
      step(qs, n_far - 1, None, [(v_tile(prev_of(n_far - 1)), pt_ref)])

  @pl.when(qi == nq)
  def _():
    finish_previous([(v_tile(nq - 3), pt_ref)])


def _attention(z_x, vt_x, z_m, vt_m, bias_d, bias_s, bias_m, lam_p, subln_g):
  b, s, _ = z_x.shape
  t = ATT_T
  g = ATT_G
  nq = s // t
  assert nq >= 3
  w = A_QK_W * g
  hb = A_HEADS // g
  kb, gb = Z_K // w, Z_GA // w

  def cur(qi):
    return jnp.minimum(qi, nq - 1)

  def prev(qi):
    return jnp.maximum(qi - 1, 0)

  return pl.pallas_call(
      _attn_kernel,
      name="diff_attn",
      grid=(b, hb, nq + 1),
      in_specs=[
          pl.BlockSpec((None, t, w), lambda bi, hi, qi: (bi, cur(qi), hi)),
          pl.BlockSpec((None, s, w), lambda bi, hi, qi: (bi, 0, kb + hi)),
          pl.BlockSpec((None, s // t, w, t), lambda bi, hi, qi: (bi, 0, hi, 0)),
          pl.BlockSpec((N_META, w), lambda bi, hi, qi: (0, kb + hi)),
          pl.BlockSpec((1, w, N_META), lambda bi, hi, qi: (0, hi, 0)),
          pl.BlockSpec((None, t, w), lambda bi, hi, qi: (bi, prev(qi), gb + hi)),
          pl.BlockSpec((g, t, t), lambda bi, hi, qi: (hi, 0, 0)),
          pl.BlockSpec((g, t, t), lambda bi, hi, qi: (hi, 0, 0)),
          pl.BlockSpec((g, N_META, t), lambda bi, hi, qi: (hi, 0, 0)),
          pl.BlockSpec((4, A_QK_DIM), lambda bi, hi, qi: (0, 0)),
          pl.BlockSpec((1, A_V_DIM), lambda bi, hi, qi: (0, 0)),
      ],
      out_specs=pl.BlockSpec((None, t, w), lambda bi, hi, qi: (bi, prev(qi), hi)),
      out_shape=jax.ShapeDtypeStruct((b, s, A_WIDTH), BF16),
      scratch_shapes=[
          pltpu.VMEM((2 * g, 1, t), F32),
          pltpu.VMEM((2 * g, 1, t), F32),
          pltpu.VMEM((2 * g, A_V_DIM + ONES_ROWS, t), F32),
          pltpu.VMEM((2 * g, t, t), BF16),
          pltpu.VMEM((2 * g, N_META, t), BF16),
      ],
      compiler_params=pltpu.CompilerParams(
          dimension_semantics=("parallel", "parallel", "arbitrary"),
          vmem_limit_bytes=VMEM_LIMIT),
  )(z_x, z_x, vt_x.reshape(b, s // t, A_WIDTH, t), z_m, vt_m, z_x, bias_d, bias_s, bias_m, lam_p, subln_g)


def _seg_sum(x):
  lane = lax.broadcasted_iota(jnp.int32, x.shape, 1)
  first = lane < R_HEAD
  lo = jnp.sum(jnp.where(first, x, 0.0), axis=1, keepdims=True)
  hi = jnp.sum(jnp.where(first, 0.0, x), axis=1, keepdims=True)
  return jnp.where(first, lo, hi)


def _split_bf16(x):
  hi = x.astype(BF16)
  return hi, x - hi.astype(F32)


def _dot_tn(a, b):
  return lax.dot_general(a, b, (((0,), (0,)), ((), ())), preferred_element_type=F32)


def _bdot(a, b):
  return lax.dot_general(a, b, (((2,), (1,)), ((0,), (0,))), preferred_element_type=F32)


def _bdot_nt(a, b):
  return lax.dot_general(a, b, (((2,), (2,)), ((0,), (0,))), preferred_element_type=F32)


def _rwkv_kernel(rx_ref, kx_ref, vx_ref, lx_ref, gr_ref, rm_ref, kmt_ref, vmt_ref, lm_ref,
                 pv_ref, mul_ref, wuph_ref, wupl_ref, aup_ref, o_ref, s_ref, prev_ref, prevl_ref):
  tb, c = RW_TB, RW_C
  nh = R_HEAD
  c2 = 2 * c
  ti = pl.program_id(2)
  is_meta = ti == 0

  @pl.when(is_meta)
  def _():
    s_ref[...] = jnp.zeros_like(s_ref)
    prev_ref[...] = jnp.zeros_like(prev_ref)
    prevl_ref[...] = jnp.zeros_like(prevl_ref)

  row = lax.broadcasted_iota(jnp.int32, (tb, R_PAIR), 0)
  rowl = lax.broadcasted_iota(jnp.int32, (tb, LORA_PAD), 0)

  def shifted(z, prev, mu, rows):
    z_prev = jnp.where(rows == 0, prev, pltpu.roll(z, 1, 0))
    return z + (z_prev - z) * mu

  z_l = jnp.where(is_meta, lm_ref[...], lx_ref[...])
  lo = shifted(z_l, prevl_ref[...], mul_ref[...], rowl)
  prevl_ref[...] = z_l[tb - 1:tb]
  th_h, th_l = _split_bf16(jnp.tanh(lo))
  w_lora = (jnp.dot(th_h, wuph_ref[...], preferred_element_type=F32)
            + jnp.dot(th_h, wupl_ref[...], preferred_element_type=F32)
            + jnp.dot(th_l.astype(BF16), wuph_ref[...], preferred_element_type=F32))
  a_lora = jnp.dot(lo.astype(BF16), aup_ref[...], preferred_element_type=F32)

  ii = lax.broadcasted_iota(jnp.int32, (tb, tb), 0)
  jj = lax.broadcasted_iota(jnp.int32, (tb, tb), 1)
  shift = int(math.log2(c))
  same = lax.shift_right_logical(ii, shift) == lax.shift_right_logical(jj, shift)
  cum_op = jnp.where(same, jnp.where(jj <= ii, 1.0, 0.0), 0.0).astype(BF16)

  ci = lax.broadcasted_iota(jnp.int32, (c2, c2), 0)
  cj = lax.broadcasted_iota(jnp.int32, (c2, c2), 1)
  diag = ci == cj
  strict2 = jnp.concatenate([cj < ci, cj < ci], axis=1)
  incl2 = jnp.concatenate([cj <= ci, cj <= ci], axis=1)
  first = lax.broadcasted_iota(jnp.int32, (c, R_PAIR), 1) < nh

  def stack(x):
    return jnp.concatenate([jnp.where(first, x, 0.0), jnp.where(first, 0.0, x)], axis=0)

  ncc = tb // c
  chains = {name: [] for name in ("at", "rt", "bt", "kt", "bh", "kh", "vv", "gd")}
  post = []
  for p in range(RW_P):
    ls = slice(p * R_PAIR, (p + 1) * R_PAIR)
    pv = pv_ref[:, ls]
    mu_r, mu_k, mu_v = pv[0:1], pv[1:2], pv[2:3]
    w0, a0, k_k, k_a, r_k, gn_g, gn_b = pv[3:4], pv[4:5], pv[5:6], pv[6:7], pv[7:8], pv[8:9], pv[9:10]

    z_r = jnp.where(is_meta, rm_ref[:, ls], rx_ref[:, ls]).astype(F32)
    z_k = jnp.where(is_meta, kmt_ref[:, ls], kx_ref[:, ls]).astype(F32)
    z_v = jnp.where(is_meta, vmt_ref[:, ls], vx_ref[:, ls]).astype(F32)
    r = shifted(z_r, prev_ref[0:1, ls], mu_r, row)
    k = shifted(z_k, prev_ref[1:2, ls], mu_k, row)
    v = shifted(z_v, prev_ref[2:3, ls], mu_v, row)
    prev_ref[0:1, ls] = z_r[tb - 1:tb]
    prev_ref[1:2, ls] = z_k[tb - 1:tb]
    prev_ref[2:3, ls] = z_v[tb - 1:tb]

    u = -(w0 + w_lora[:, ls])
    softplus = jnp.maximum(u, 0.0) + jnp.log(1.0 + jnp.exp(-jnp.abs(u)))
    logw = -jnp.exp(-softplus - 0.5) * LOG2E
    a = 1.0 / (1.0 + jnp.exp(-(a0 + a_lora[:, ls])))
    kk = k * k_k
    kk = kk / jnp.maximum(jnp.sqrt(_seg_sum(kk * kk)), 1e-12)
    k_mod = k * (1.0 + (a - 1.0) * k_a)
    bonus = _seg_sum(r * k_mod * r_k) * v

    lw_h, lw_r = _split_bf16(logw)
    lw_m, lw_l = _split_bf16(lw_r)
    cum3 = jnp.dot(cum_op, jnp.concatenate([lw_h, lw_m, lw_l.astype(BF16)], axis=1),
                   preferred_element_type=F32)
    cum = cum3[:, :R_PAIR] + cum3[:, R_PAIR:2 * R_PAIR] + cum3[:, 2 * R_PAIR:]
    tot = jnp.concatenate([jnp.broadcast_to(cum[cc * c + c - 1:cc * c + c], (c, R_PAIR)) for cc in range(ncc)],
                          axis=0)
    p_inv = jnp.exp2(-cum)
    a_t = -kk * jnp.exp2(cum - logw)
    kka = kk * a
    b_t = kka * p_inv
    k_t = k_mod * p_inv
    r_t = r * jnp.exp2(cum)
    p_end = jnp.exp2(tot - cum)
    b_h = kka * p_end
    k_h = k_mod * p_end
    g_diag = jnp.exp2(tot)

    for cc in range(ncc):
      rs = slice(cc * c, (cc + 1) * c)
      for name, val in (("at", a_t), ("rt", r_t), ("bt", b_t), ("kt", k_t), ("bh", b_h), ("kh", k_h),
                        ("vv", v)):
        chains[name].append(stack(val[rs]))
      chains["gd"].append(g_diag[cc * c:cc * c + 1])
    post.append((bonus, gn_g, gn_b))

  nb = RW_P * ncc
  at, rt, bt, kt, bh, kh, vv = (jnp.stack(chains[name]) for name in ("at", "rt", "bt", "kt", "bh", "kh", "vv"))
  at_b, vv_b, bh_b = at.astype(BF16), vv.astype(BF16), bh.astype(BF16)
  bk = jnp.concatenate([bt, kt], axis=1).astype(BF16)
  top = jnp.where(strict2, _bdot_nt(at_b, bk), 0.0)
  lblk = jnp.where(incl2, _bdot_nt(rt.astype(BF16), bk), 0.0)
  nm, mak = top[:, :, :c2], top[:, :, c2:]
  tinv = jnp.where(diag, 1.0, nm)
  npow = nm.astype(BF16)
  for _ in range(5):
    npow = _bdot(npow, npow).astype(BF16)
    tinv = tinv + _bdot(tinv.astype(BF16), npow)
  x1 = _bdot(mak.astype(BF16), vv_b)
  wu_b = _bdot(tinv.astype(BF16), jnp.concatenate([at_b, x1.astype(BF16)], axis=2)).astype(BF16)
  rhs = jnp.concatenate([wu_b, jnp.concatenate([jnp.zeros_like(vv_b), vv_b], axis=2)], axis=1)
  qy = _bdot(lblk.astype(BF16), rhs)
  q_h = (rt + qy[:, :, :c2]).astype(BF16)
  y0 = qy[:, :, c2:]
  uv = jnp.concatenate([wu_b[:, :, c2:], vv_b], axis=1)
  bkh = jnp.concatenate([bh_b, kh.astype(BF16)], axis=1)
  g_m = [(jnp.where(diag, chains["gd"][n], 0.0) + _dot_tn(wu_b[n, :, :c2], bh_b[n])).astype(BF16)
         for n in range(nb)]
  h_m = [_dot_tn(uv[n], bkh[n]) for n in range(nb)]

  states = [s_ref[p] for p in range(RW_P)]
  y_rows = [[] for _ in range(RW_P)]
  for cc in range(ncc):
    for p in range(RW_P):
      n = p * ncc + cc
      s_old_b = states[p].astype(BF16)
      y2 = _dot_nt(q_h[n], s_old_b) + y0[n]
      states[p] = jnp.dot(s_old_b, g_m[n], preferred_element_type=F32) + h_m[n]
      y_rows[p].append(y2[:c] + y2[c:])

  for p in range(RW_P):
    ls = slice(p * R_PAIR, (p + 1) * R_PAIR)
    s_ref[p] = states[p]
    bonus, gn_g, gn_b = post[p]
    y = jnp.concatenate(y_rows[p], axis=0)
    mean = _seg_sum(y) * (1.0 / nh)
    yc = y - mean
    var = _seg_sum(yc * yc) * (1.0 / nh)
    yn = yc * lax.rsqrt(var + GN_EPS) * gn_g + gn_b
    g = gr_ref[:, ls].astype(F32)
    o_ref[:, ls] = ((yn + bonus) * (g / (1.0 + jnp.exp(-g)))).astype(o_ref.dtype)


def _rwkv(z_x, lo_x, z_mp, lo_mp, pvec, mu_l, wup_h, wup_l, aup):
  b, s, _ = z_x.shape
  tb = RW_TB
  nt = s // tb + 1
  pw = R_PAIR * RW_P

  def xmap(col):
    return lambda bi, hp, ti: (bi, jnp.maximum(ti - 1, 0), col // pw + hp)

  def mmap(col):
    return lambda bi, hp, ti: (0, col // pw + hp)

  return pl.pallas_call(
      _rwkv_kernel,
      name="rwkv7",
      grid=(b, R_PAIRS // RW_P, nt),
      in_specs=[
          pl.BlockSpec((None, tb, pw), xmap(Z_RR)),
          pl.BlockSpec((None, tb, pw), xmap(Z_RK)),
          pl.BlockSpec((None, tb, pw), xmap(Z_RV)),
          pl.BlockSpec((None, tb, LORA_PAD), lambda bi, hp, ti: (bi, jnp.maximum(ti - 1, 0), 0)),
          pl.BlockSpec((None, tb, pw), xmap(Z_GR)),
          pl.BlockSpec((tb, pw), mmap(Z_RR)),
          pl.BlockSpec((tb, pw), mmap(Z_RK)),
          pl.BlockSpec((tb, pw), mmap(Z_RV)),
          pl.BlockSpec((tb, LORA_PAD), lambda bi, hp, ti: (0, 0)),
          pl.BlockSpec((16, pw), lambda bi, hp, ti: (0, hp)),
          pl.BlockSpec((1, LORA_PAD), lambda bi, hp, ti: (0, 0)),
          pl.BlockSpec((LORA_PAD, pw), lambda bi, hp, ti: (0, hp)),
          pl.BlockSpec((LORA_PAD, pw), lambda bi, hp, ti: (0, hp)),
          pl.BlockSpec((LORA_PAD, pw), lambda bi, hp, ti: (0, hp)),
      ],
      out_specs=pl.BlockSpec((None, tb, pw), lambda bi, hp, ti: (bi, jnp.maximum(ti - 1, 0), hp)),
      out_shape=jax.ShapeDtypeStruct((b, s, R_WIDTH), BF16),
      scratch_shapes=[
          pltpu.VMEM((RW_P, 2 * R_HEAD, 2 * R_HEAD), F32),
          pltpu.VMEM((8, pw), F32),
          pltpu.VMEM((1, LORA_PAD), F32),
      ],
      compiler_params=pltpu.CompilerParams(
          dimension_semantics=("parallel", "parallel", "arbitrary"),
          vmem_limit_bytes=VMEM_LIMIT),
  )(z_x, z_x, z_x, lo_x, z_x, z_mp, z_mp, z_mp, lo_mp, pvec, mu_l, wup_h, wup_l, aup)


def _out_kernel(x_ref, oa_ref, or_ref, wa_ref, wr_ref, ge_ref, be_ref, gp_ref, bp_ref, o_ref, wb_ref):
  @pl.when(pl.program_id(0) == 0)
  def _():
    wb_ref[0] = wa_ref[...].astype(BF16)
    wb_ref[1] = wr_ref[...].astype(BF16)

  h = _ln_rows(x_ref[...], ge_ref[...], be_ref[...])
  y = (jnp.dot(oa_ref[...], wb_ref[0], preferred_element_type=F32)
       + jnp.dot(or_ref[...], wb_ref[1], preferred_element_type=F32))
  o_ref[...] = _ln_rows(DEEPNORM_ALPHA * h + y, gp_ref[...], bp_ref[...])


def _out_proj(x2d, oa, orw, w, ge, be, gp, bp, tm):
  m, d = x2d.shape
  assert w.shape == (A_WIDTH + R_WIDTH, d) and A_WIDTH == R_WIDTH
  vec = pl.BlockSpec((1, d), lambda i: (0, 0))
  return pl.pallas_call(
      _out_kernel,
      name="out_proj",
      grid=(m // tm,),
      in_specs=[
          pl.BlockSpec((tm, d), lambda i: (i, 0)),
          pl.BlockSpec((tm, A_WIDTH), lambda i: (i, 0)),
          pl.BlockSpec((tm, R_WIDTH), lambda i: (i, 0)),
          pl.BlockSpec((A_WIDTH, d), lambda i: (0, 0), pipeline_mode=pl.Buffered(1)),
          pl.BlockSpec((R_WIDTH, d), lambda i: (1, 0), pipeline_mode=pl.Buffered(1)),
          vec, vec, vec, vec,
      ],
      out_specs=pl.BlockSpec((tm, d), lambda i: (i, 0)),
      out_shape=jax.ShapeDtypeStruct((m, d), F32),
      scratch_shapes=[pltpu.VMEM((2, A_WIDTH, d), BF16)],
      compiler_params=pltpu.CompilerParams(
          dimension_semantics=("arbitrary",),
          vmem_limit_bytes=VMEM_LIMIT),
  )(x2d, oa, orw, w, w, ge, be, gp, bp)


def kernel(x, meta_tokens, ln_emb_g, ln_emb_b, rel_bias, w_in, w_out, lambda_q1, lambda_k1, lambda_q2,
           lambda_k2, subln_g, rw_mu, rw_w0, rw_w_up, rw_a0, rw_a_up, rw_k_k, rw_k_a, rw_r_k, rw_gn_g,
           rw_gn_b, ln_post_g, ln_post_b):
  b, s, d = x.shape
  assert w_in.shape[0] == DEPTH == 1 and w_in.shape[1:] == (d, 4 * A_WIDTH + 4 * R_WIDTH + DECAY_LORA + ICLR_LORA)
  assert s % IN_TM == 0 and (b * s) % OUT_TM == 0 and meta_tokens.shape == (N_META, d)
  l = 0
  wi = w_in[l]
  c_lo = 4 * A_WIDTH + 3 * R_WIDTH
  c_gr = c_lo + DECAY_LORA + ICLR_LORA
  lora_pad = LORA_PAD - DECAY_LORA - ICLR_LORA
  ge, be = ln_emb_g.reshape(1, d), ln_emb_b.reshape(1, d)
  wt_all, wt_gr, wt_lora, zm, zt_tail = _cast_project_meta(wi.T, meta_tokens, ge, be, c_lo)
  z_m = jnp.concatenate([zm[:, :A_WIDTH] * Q_SCALE, zm[:, A_WIDTH:2 * A_WIDTH], zm[:, 3 * A_WIDTH:c_lo],
                         zm[:, c_gr:], zt_tail[:, :N_META].T], axis=1).astype(BF16)
  vt_m = zm[:, 2 * A_WIDTH:3 * A_WIDTH].T.astype(BF16).reshape(1, A_WIDTH, N_META)
  lo_m = jnp.pad(zm[:, c_lo:c_gr], ((0, 0), (0, lora_pad)))

  x2d = x.reshape(b * s, d)
  z_x, vt_x, lo_x = _ln_matmul(x2d, ge, be, wt_all, wt_gr, wt_lora)
  z_x, lo_x = z_x.reshape(b, s, -1), lo_x.reshape(b, s, LORA_PAD)

  bias_d, bias_s, bias_m = _bias_tiles(rel_bias)
  lam_p = jnp.stack([lambda_q1[l], lambda_k1[l], lambda_q2[l], lambda_k2[l]], axis=0)
  o_attn = _attention(z_x, vt_x, z_m, vt_m, bias_d, bias_s, bias_m, lam_p, subln_g[l].reshape(1, A_V_DIM))

  mu = rw_mu[l]
  zeros = jnp.zeros((R_WIDTH,), F32)
  pvec = jnp.stack([mu[:R_WIDTH], mu[R_WIDTH:2 * R_WIDTH], mu[2 * R_WIDTH:3 * R_WIDTH], rw_w0[l], rw_a0[l],
                    rw_k_k[l], rw_k_a[l], rw_r_k[l].reshape(R_WIDTH), rw_gn_g[l], rw_gn_b[l]]
                   + [zeros] * 6, axis=0)
  mu_l = jnp.pad(mu[3 * R_WIDTH:], (0, lora_pad)).reshape(1, LORA_PAD)
  wup = jnp.pad(rw_w_up[l], ((0, LORA_PAD - DECAY_LORA), (0, 0)))
  wup_h = wup.astype(BF16)
  wup_l = (wup - wup_h.astype(F32)).astype(BF16)
  aup = jnp.pad(rw_a_up[l], ((DECAY_LORA, lora_pad), (0, 0))).astype(BF16)
  front = ((RW_TB - N_META, 0), (0, 0))
  o_rwkv = _rwkv(z_x, lo_x, jnp.pad(z_m, front), jnp.pad(lo_m, front), pvec, mu_l, wup_h, wup_l, aup)

  out = _out_proj(x2d, o_attn.reshape(b * s, A_WIDTH), o_rwkv.reshape(b * s, R_WIDTH), w_out[l], ge, be,
                  ln_post_g[l].reshape(1, d), ln_post_b[l].reshape(1, d), OUT_TM)
  return out.reshape(b, s, d)
```

```python
import functools
import math

import numpy as np
import jax
import jax.numpy as jnp
from jax import lax
from jax.experimental import pallas as pl
from jax.experimental.pallas import tpu as pltpu

N_META = 16
A_HEADS = 8
A_V_DIM = 128
A_QK_DIM = 64
A_WIDTH = A_HEADS * A_V_DIM
R_HEAD = 64
R_WIDTH = 1024
R_PAIR = 2 * R_HEAD
R_PAIRS = R_WIDTH // R_PAIR
A_QK_W = 2 * A_QK_DIM
DECAY_LORA = 96
ICLR_LORA = 96
LORA_PAD = 256
N_BUCKETS = 32
MAX_DISTANCE = 128
LN_EPS = 1e-5
SUBLN_EPS = 1e-5
GN_EPS = 64e-5
DEPTH = 1
DEEPNORM_ALPHA = (2 * DEPTH) ** 0.25
LAM_INIT = 0.8 - 0.6 * math.exp(-0.3 * 0)
NEG = -1e30

ATT_T = 256
ATT_G = 8
ONES_ROWS = 16
IN_TM = 1024
IN_TN = 1024
OUT_TM = 512
Z_Q, Z_K, Z_GA, Z_RR, Z_RK, Z_RV, Z_GR = (i * 1024 for i in range(7))
LOG2E = math.log2(math.e)
Q_SCALE = A_QK_DIM ** -0.5 * LOG2E
RW_TB = 128
RW_C = 64
RW_P = 8
META_LANES = 128
VMEM_LIMIT = 56 * 1024 * 1024

F32 = jnp.float32
BF16 = jnp.bfloat16


def _ln_rows(x, g, b):
  mu = jnp.mean(x, axis=-1, keepdims=True)
  xc = x - mu
  var = jnp.mean(xc * xc, axis=-1, keepdims=True)
  return xc * lax.rsqrt(var + LN_EPS) * g + b


def _ln_mm_kernel(x_ref, g_ref, b_ref, wm_ref, wgr_ref, wvt_ref, wl_ref, om_ref, ovt_ref, ol_ref, hn_ref):
  j = pl.program_id(1)
  n_main = pl.num_programs(1) - 1

  @pl.when(j == 0)
  def _():
    hn_ref[...] = _ln_rows(x_ref[...], g_ref[...], b_ref[...]).astype(BF16)

  @pl.when(j < n_main - 1)
  def _():
    scale = jnp.where(j == Z_Q // IN_TN, Q_SCALE, 1.0)
    z = _dot_nt(hn_ref[...], wm_ref[...])
    om_ref[...] = (z * scale).astype(om_ref.dtype)

  @pl.when(j == n_main - 1)
  def _():
    om_ref[...] = _dot_nt(hn_ref[...], wgr_ref[...]).astype(om_ref.dtype)
    ol_ref[...] = _dot_nt(hn_ref[...], wl_ref[...])

  @pl.when(j == n_main)
  def _():
    zt = _dot_nt(wvt_ref[...], hn_ref[...])
    tv = ovt_ref.shape[2]
    for c in range(ovt_ref.shape[0]):
      ovt_ref[c] = zt[:, c * tv:(c + 1) * tv].astype(ovt_ref.dtype)


def _cast_meta_kernel(n_a, n_lo, xm_ref, g_ref, b_ref, wa_ref, wb_ref, oa_ref, ogr_ref, olo_ref, za_ref, zb_ref,
                      hn_ref):
  j = pl.program_id(0)
  tn = wa_ref.shape[0]
  tb = wb_ref.shape[0]

  @pl.when(j == 0)
  def _():
    hn_ref[...] = jnp.zeros_like(hn_ref)
    hn_ref[0:N_META] = _ln_rows(xm_ref[...], g_ref[...], b_ref[...]).astype(BF16)

  def meta_rows(w):
    return _dot_nt(w, hn_ref[...])

  @pl.when(j < n_a - 1)
  def _():
    w = wa_ref[...].astype(BF16)
    oa_ref[...] = w
    za_ref[...] = meta_rows(w).T[:N_META]

  @pl.when(j == n_a - 1)
  def _():
    w = wa_ref[...].astype(BF16)
    olo_ref[0:n_lo] = w[:n_lo]
    olo_ref[n_lo:] = jnp.zeros((olo_ref.shape[0] - n_lo, w.shape[1]), BF16)
    ogr_ref[0:tn - n_lo] = w[n_lo:]
    za_ref[...] = meta_rows(w).T[:N_META]

  @pl.when(j >= n_a)
  def _():
    w = wb_ref[...].astype(BF16)
    ogr_ref[pl.ds(pl.multiple_of(tn - n_lo + (j - n_a) * tb, tb), tb)] = w
    zb_ref[...] = meta_rows(w)


def _cast_project_meta(wt, xm, g, b, c_lo):
  n, d = wt.shape
  tn = IN_TN
  n_a = n // tn
  rem = n - n_a * tn
  tb = math.gcd(rem, tn)
  n_b = rem // tb
  n_lo = n - tn - c_lo
  assert c_lo == (n_a - 1) * tn and n_lo == rem and 0 < n_lo <= LORA_PAD and tb % 16 == 0 and n_lo % 16 == 0
  vec = pl.BlockSpec((1, d), lambda j: (0, 0))
  whole = lambda j: (0, 0)

  return pl.pallas_call(
      functools.partial(_cast_meta_kernel, n_a, n_lo),
      name="cast_meta",
      grid=(n_a + n_b,),
      in_specs=[
          pl.BlockSpec((N_META, d), whole),
          vec, vec,
          pl.BlockSpec((tn, d), lambda j: (jnp.minimum(j, n_a - 1), 0)),
          pl.BlockSpec((tb, d), lambda j: (n_a * tn // tb + jnp.clip(j - n_a, 0, n_b - 1), 0)),
      ],
      out_specs=[
          pl.BlockSpec((tn, d), lambda j: (jnp.minimum(j, n_a - 2), 0)),
          pl.BlockSpec((tn, d), whole),
          pl.BlockSpec((LORA_PAD, d), whole),
          pl.BlockSpec((N_META, tn), lambda j: (0, jnp.minimum(j, n_a - 1))),
          pl.BlockSpec((tb, META_LANES), lambda j: (jnp.clip(j - n_a, 0, n_b - 1), 0)),
      ],
      out_shape=[
          jax.ShapeDtypeStruct((c_lo, d), BF16),
          jax.ShapeDtypeStruct((tn, d), BF16),
          jax.ShapeDtypeStruct((LORA_PAD, d), BF16),
          jax.ShapeDtypeStruct((N_META, n_a * tn), F32),
          jax.ShapeDtypeStruct((rem, META_LANES), F32),
      ],
      scratch_shapes=[pltpu.VMEM((META_LANES, d), BF16)],
      compiler_params=pltpu.CompilerParams(
          dimension_semantics=("arbitrary",),
          vmem_limit_bytes=VMEM_LIMIT),
  )(xm, g, b, wt, wt)


def _ln_matmul(x2d, g, b, wt_all, wt_gr, wt_lora):
  m, d = x2d.shape
  tm = IN_TM
  tn = IN_TN
  nj = Z_GR // tn + 1
  n = nj * tn
  last = nj - 1
  tv = ATT_T
  v_tile = 2 * A_WIDTH // tn
  once = dict(pipeline_mode=pl.Buffered(1))

  def w_map(i, j):
    jj = jnp.minimum(j, last - 1)
    return (jnp.where(jj >= v_tile, jj + 1, jj), 0)

  return pl.pallas_call(
      _ln_mm_kernel,
      name="ln_inproj",
      grid=(m // tm, nj + 1),
      in_specs=[
          pl.BlockSpec((tm, d), lambda i, j: (i, 0)),
          pl.BlockSpec((1, d), lambda i, j: (0, 0)),
          pl.BlockSpec((1, d), lambda i, j: (0, 0)),
          pl.BlockSpec((tn, d), w_map),
          pl.BlockSpec((tn, d), lambda i, j: (0, 0), **once),
          pl.BlockSpec((A_WIDTH, d), lambda i, j: (v_tile, 0), **once),
          pl.BlockSpec((LORA_PAD, d), lambda i, j: (0, 0), **once),
      ],
      out_specs=[
          pl.BlockSpec((tm, tn), lambda i, j: (i, jnp.minimum(j, last))),
          pl.BlockSpec((tm // tv, A_WIDTH, tv), lambda i, j: (i, 0, 0)),
          pl.BlockSpec((tm, LORA_PAD), lambda i, j: (i, 0)),
      ],
      out_shape=[
          jax.ShapeDtypeStruct((m, n), BF16),
          jax.ShapeDtypeStruct((m // tv, A_WIDTH, tv), BF16),
          jax.ShapeDtypeStruct((m, LORA_PAD), F32),
      ],
      scratch_shapes=[pltpu.VMEM((tm, d), BF16)],
      compiler_params=pltpu.CompilerParams(
          dimension_semantics=("parallel", "arbitrary"),
          vmem_limit_bytes=VMEM_LIMIT),
  )(x2d, g, b, wt_all, wt_gr, wt_all, wt_lora)


def _bucket_thresholds():
  n = np.arange(0, 4 * MAX_DISTANCE, dtype=np.int64)
  max_exact = N_BUCKETS // 2
  nf = np.maximum(n, 1).astype(np.float32)
  large = max_exact + (np.log(nf / np.float32(max_exact)) / np.float32(math.log(MAX_DISTANCE / max_exact))
                       * np.float32(N_BUCKETS - max_exact)).astype(np.int32)
  large = np.minimum(large, N_BUCKETS - 1)
  bucket = np.where(n < max_exact, n, large)
  assert np.all(np.diff(bucket) >= 0) and bucket[-1] == N_BUCKETS - 1
  return [int(np.argmax(bucket >= b)) for b in range(N_BUCKETS)]


_THR = _bucket_thresholds()


def _bias_kernel(rb_ref, diag_ref, sub_ref, meta_ref):
  h = pl.program_id(0)
  far = rb_ref[N_BUCKETS - 1, h]

  def bias_of(n):
    out = jnp.full(n.shape, (rb_ref[0, h] - far) * LOG2E, F32)
    for b in range(1, N_BUCKETS):
      out = jnp.where(n >= _THR[b], (rb_ref[b, h] - far) * LOG2E, out)
    return out

  t = ATT_T
  kj = lax.broadcasted_iota(jnp.int32, (t, t), 0)
  qi = lax.broadcasted_iota(jnp.int32, (t, t), 1)
  d = qi - kj
  diag_ref[...] = jnp.where(d >= 0, bias_of(d), NEG)
  sub_ref[...] = bias_of(d + t)
  km = lax.broadcasted_iota(jnp.int32, (N_META, t), 0)
  qm = lax.broadcasted_iota(jnp.int32, (N_META, t), 1)
  meta_ref[...] = bias_of(qm - km + N_META)


def _bias_tiles(rel_bias):
  t = ATT_T
  return pl.pallas_call(
      _bias_kernel,
      name="bias_tiles",
      grid=(A_HEADS,),
      in_specs=[pl.BlockSpec(memory_space=pltpu.SMEM)],
      out_specs=[
          pl.BlockSpec((None, t, t), lambda h: (h, 0, 0)),
          pl.BlockSpec((None, t, t), lambda h: (h, 0, 0)),
          pl.BlockSpec((None, N_META, t), lambda h: (h, 0, 0)),
      ],
      out_shape=[
          jax.ShapeDtypeStruct((A_HEADS, t, t), F32),
          jax.ShapeDtypeStruct((A_HEADS, t, t), F32),
          jax.ShapeDtypeStruct((A_HEADS, N_META, t), F32),
      ],
  )(rel_bias)


def _dot_nt(a, b):
  return lax.dot_general(a, b, (((1,), (1,)), ((), ())), preferred_element_type=F32)


def _attn_kernel(q_ref, kx_ref, vt_ref, km_ref, vmt_ref, ga_ref, bd_ref, bs_ref, bm_ref,
                 lam_ref, sg_ref, o_ref, m_ref, alpha_ref, acc_ref, pt_ref, ptm_ref):
  t = ATT_T
  g = ATT_G
  nc = 2 * g
  dv = A_V_DIM
  qi = pl.program_id(2)
  nq = pl.num_programs(2) - 1

  def v_tile(j):
    ones = jnp.ones((ONES_ROWS, t), BF16)
    return [jnp.concatenate([vt_ref[j, hh * dv:(hh + 1) * dv, :], ones], axis=0) for hh in range(g)]

  def k_tile(j):
    off = pl.multiple_of(j * t, t)
    return [kx_ref[pl.ds(off, t), hh * A_QK_W:(hh + 1) * A_QK_W] for hh in range(g)]

  def meta_v():
    ones_m = jnp.ones((ONES_ROWS, N_META), BF16)
    return [jnp.concatenate([vmt_ref[0, hh * dv:(hh + 1) * dv, :], ones_m], axis=0) for hh in range(g)]

  def queries():
    lane = lax.broadcasted_iota(jnp.int32, (t, A_QK_W), 1)
    qs = []
    for hh in range(g):
      q = q_ref[:, hh * A_QK_W:(hh + 1) * A_QK_W]
      zero = jnp.zeros_like(q)
      qs += [jnp.where(lane < A_QK_DIM, q, zero), jnp.where(lane >= A_QK_DIM, q, zero)]
    return qs

  def softmax_stage(c, m_prev, s_list):
    m_new = m_prev
    for s in s_list:
      m_new = jnp.maximum(m_new, jnp.max(s, axis=0, keepdims=True))
    m_ref[c] = m_new
    alpha_ref[c] = jnp.exp2(m_prev - m_new)
    return [jnp.exp2(s - m_new).astype(BF16) for s in s_list]

  def pending(segments, c):
    pv = None
    for vts, p_ref in segments:
      d = jnp.dot(vts[c // 2], p_ref[c], preferred_element_type=F32)
      pv = d if pv is None else pv + d
    return pv

  def finish_previous(segments):
    lp = lam_ref[...]
    lam = (jnp.exp(jnp.sum(lp[0:1] * lp[1:2], axis=1, keepdims=True))
           - jnp.exp(jnp.sum(lp[2:3] * lp[3:4], axis=1, keepdims=True)) + LAM_INIT)
    for hh in range(g):
      a0, a1 = (alpha_ref[c] * acc_ref[c] + pending(segments, c) for c in (2 * hh, 2 * hh + 1))
      ot = a0[:dv] / a0[dv:dv + 1] - lam * (a1[:dv] / a1[dv:dv + 1])
      ot = ot * lax.rsqrt(jnp.mean(ot * ot, axis=0, keepdims=True) + SUBLN_EPS)
      o = ot.T * (sg_ref[...] * (1.0 - LAM_INIT))
      gate = ga_ref[:, hh * dv:(hh + 1) * dv].astype(F32)
      o_ref[:, hh * dv:(hh + 1) * dv] = (o * (gate / (1.0 + jnp.exp(-gate)))).astype(o_ref.dtype)

  def start_tile(qs, previous):
    if previous is not None:
      finish_previous(previous)
    ks = k_tile(qi)
    st, sm = [], []
    for c in range(nc):
      hh = c // 2
      st.append(_dot_nt(ks[hh], qs[c]) + bd_ref[hh])
      sm.append(_dot_nt(km_ref[:, hh * A_QK_W:(hh + 1) * A_QK_W], qs[c]) + jnp.where(qi == 0, bm_ref[hh], 0.0))
    acc_ref[...] = jnp.zeros(acc_ref.shape, F32)
    m_start = jnp.full((1, t), NEG, F32)
    for c in range(nc):
      pt_ref[c], ptm_ref[c] = softmax_stage(c, m_start, [st[c], sm[c]])

  def step(qs, j_cur, biases, segments):
    ks = k_tile(j_cur)
    pv, st = [], []
    for c in range(nc):
      pv.append(pending(segments, c))
      s = _dot_nt(ks[c // 2], qs[c])
      st.append(s if biases is None else s + biases[c // 2])
    for c in range(nc):
      acc_ref[c] = alpha_ref[c] * acc_ref[c] + pv[c]
    for c in range(nc):
      pt_ref[c], = softmax_stage(c, m_ref[c], [st[c]])

  def below_diagonal(qs):
    step(qs, qi - 1, [bs_ref[hh] for hh in range(g)], [(v_tile(qi), pt_ref), (meta_v(), ptm_ref)])

  @pl.when(qi == 0)
  def _():
    start_tile(queries(), None)

  @pl.when(qi == 1)
  def _():
    qs = queries()
    start_tile(qs, [(v_tile(0), pt_ref), (meta_v(), ptm_ref)])
    below_diagonal(qs)

  @pl.when(jnp.logical_and(qi >= 2, qi < nq))
  def _():
    qs = queries()
    start_tile(qs, [(v_tile(jnp.maximum(qi - 3, 0)), pt_ref)])
    below_diagonal(qs)
    n_far = qi - 1

    def prev_of(j):
      return jnp.where(j == 0, qi - 1, j - 1)

    def far_body(i, carry):
      step(qs, 2 * i, None, [(v_tile(prev_of(2 * i)), pt_ref)])
      step(qs, 2 * i + 1, None, [(v_tile(2 * i), pt_ref)])
      return carry

    lax.fori_loop(0, n_far // 2, far_body, 0)

    @pl.when(n_far % 2 == 1)
    def _():
      step(qs, n_far - 1, None, [(v_tile(prev_of(n_far - 1)), pt_ref)])

  @pl.when(qi == nq)
  def _():
    finish_previous([(v_tile(nq - 3), pt_ref)])


def _attention(z_x, vt_x, z_m, vt_m, bias_d, bias_s, bias_m, lam_p, subln_g):
  b, s, _ = z_x.shape
  t = ATT_T
  g = ATT_G
  nq = s // t
  assert nq >= 3
  w = A_QK_W * g
  hb = A_HEADS // g
  kb, gb = Z_K // w, Z_GA // w

  def cur(qi):
    return jnp.minimum(qi, nq - 1)

  def prev(qi):
    return jnp.maximum(qi - 1, 0)

  return pl.pallas_call(
      _attn_kernel,
      name="diff_attn",
      grid=(b, hb, nq + 1),
      in_specs=[
          pl.BlockSpec((None, t, w), lambda bi, hi, qi: (bi, cur(qi), hi)),
          pl.BlockSpec((None, s, w), lambda bi, hi, qi: (bi, 0, kb + hi)),
          pl.BlockSpec((None, s // t, w, t), lambda bi, hi, qi: (bi, 0, hi, 0)),
          pl.BlockSpec((N_META, w), lambda bi, hi, qi: (0, kb + hi)),
          pl.BlockSpec((1, w, N_META), lambda bi, hi, qi: (0, hi, 0)),
          pl.BlockSpec((None, t, w), lambda bi, hi, qi: (bi, prev(qi), gb + hi)),
          pl.BlockSpec((g, t, t), lambda bi, hi, qi: (hi, 0, 0)),
          pl.BlockSpec((g, t, t), lambda bi, hi, qi: (hi, 0, 0)),
          pl.BlockSpec((g, N_META, t), lambda bi, hi, qi: (hi, 0, 0)),
          pl.BlockSpec((4, A_QK_DIM), lambda bi, hi, qi: (0, 0)),
          pl.BlockSpec((1, A_V_DIM), lambda bi, hi, qi: (0, 0)),
      ],
      out_specs=pl.BlockSpec((None, t, w), lambda bi, hi, qi: (bi, prev(qi), hi)),
      out_shape=jax.ShapeDtypeStruct((b, s, A_WIDTH), BF16),
      scratch_shapes=[
          pltpu.VMEM((2 * g, 1, t), F32),
          pltpu.VMEM((2 * g, 1, t), F32),
          pltpu.VMEM((2 * g, A_V_DIM + ONES_ROWS, t), F32),
          pltpu.VMEM((2 * g, t, t), BF16),
          pltpu.VMEM((2 * g, N_META, t), BF16),
      ],
      compiler_params=pltpu.CompilerParams(
          dimension_semantics=("parallel", "parallel", "arbitrary"),
          vmem_limit_bytes=VMEM_LIMIT),
  )(z_x, z_x, vt_x.reshape(b, s // t, A_WIDTH, t), z_m, vt_m, z_x, bias_d, bias_s, bias_m, lam_p, subln_g)


def _seg_sum(x):
  lane = lax.broadcasted_iota(jnp.int32, x.shape, 1)
  first = lane < R_HEAD
  lo = jnp.sum(jnp.where(first, x, 0.0), axis=1, keepdims=True)
  hi = jnp.sum(jnp.where(first, 0.0, x), axis=1, keepdims=True)
  return jnp.where(first, lo, hi)


def _split_bf16(x):
  hi = x.astype(BF16)
  return hi, x - hi.astype(F32)


def _dot_tn(a, b):
  return lax.dot_general(a, b, (((0,), (0,)), ((), ())), preferred_element_type=F32)


def _bdot(a, b):
  return lax.dot_general(a, b, (((2,), (1,)), ((0,), (0,))), preferred_element_type=F32)


def _bdot_nt(a, b):
  return lax.dot_general(a, b, (((2,), (2,)), ((0,), (0,))), preferred_element_type=F32)


def _rwkv_kernel(*refs):
  *block_refs, sm_ref, pm_ref, plm_ref = refs
  s_ref, prev_ref, prevl_ref = block_refs[-3:]
  bi, hp, ti = pl.program_id(0), pl.program_id(1), pl.program_id(2)

  @pl.when(jnp.logical_or(bi == 0, ti > 0))
  def _():
    _rwkv_block(*block_refs)

  @pl.when(jnp.logical_and(bi == 0, ti == 0))
  def _():
    sm_ref[hp] = s_ref[...]
    pm_ref[hp] = prev_ref[...]
    plm_ref[hp] = prevl_ref[...]

  @pl.when(jnp.logical_and(bi > 0, ti == 0))
  def _():
    s_ref[...] = sm_ref[hp]
    prev_ref[...] = pm_ref[hp]
    prevl_ref[...] = plm_ref[hp]


def _rwkv_block(rx_ref, kx_ref, vx_ref, lx_ref, gr_ref, rm_ref, kmt_ref, vmt_ref, lm_ref,
                pv_ref, mul_ref, wuph_ref, wupl_ref, aup_ref, o_ref, s_ref, prev_ref, prevl_ref):
  tb, c = RW_TB, RW_C
  nh = R_HEAD
  c2 = 2 * c
  ti = pl.program_id(2)
  is_meta = ti == 0

  @pl.when(is_meta)
  def _():
    s_ref[...] = jnp.zeros_like(s_ref)
    prev_ref[...] = jnp.zeros_like(prev_ref)
    prevl_ref[...] = jnp.zeros_like(prevl_ref)

  row = lax.broadcasted_iota(jnp.int32, (tb, R_PAIR), 0)
  rowl = lax.broadcasted_iota(jnp.int32, (tb, LORA_PAD), 0)

  def shifted(z, prev, mu, rows):
    z_prev = jnp.where(rows == 0, prev, pltpu.roll(z, 1, 0))
    return z + (z_prev - z) * mu

  z_l = jnp.where(is_meta, lm_ref[...], lx_ref[...])
  lo = shifted(z_l, prevl_ref[...], mul_ref[...], rowl)
  prevl_ref[...] = z_l[tb - 1:tb]
  th_h, th_l = _split_bf16(jnp.tanh(lo))
  w_lora = (jnp.dot(th_h, wuph_ref[...], preferred_element_type=F32)
            + jnp.dot(th_h, wupl_ref[...], preferred_element_type=F32)
            + jnp.dot(th_l.astype(BF16), wuph_ref[...], preferred_element_type=F32))
  a_lora = jnp.dot(lo.astype(BF16), aup_ref[...], preferred_element_type=F32)

  ii = lax.broadcasted_iota(jnp.int32, (tb, tb), 0)
  jj = lax.broadcasted_iota(jnp.int32, (tb, tb), 1)
  shift = int(math.log2(c))
  same = lax.shift_right_logical(ii, shift) == lax.shift_right_logical(jj, shift)
  cum_op = jnp.where(same, jnp.where(jj <= ii, 1.0, 0.0), 0.0).astype(BF16)

  ci = lax.broadcasted_iota(jnp.int32, (c2, c2), 0)
  cj = lax.broadcasted_iota(jnp.int32, (c2, c2), 1)
  diag = ci == cj
  strict2 = jnp.concatenate([cj < ci, cj < ci], axis=1)
  incl2 = jnp.concatenate([cj <= ci, cj <= ci], axis=1)
  first = lax.broadcasted_iota(jnp.int32, (c, R_PAIR), 1) < nh

  def stack(x):
    return jnp.concatenate([jnp.where(first, x, 0.0), jnp.where(first, 0.0, x)], axis=0)

  ncc = tb // c
  chains = {name: [] for name in ("at", "rt", "bt", "kt", "bh", "kh", "vv", "gd")}
  post = []
  for p in range(RW_P):
    ls = slice(p * R_PAIR, (p + 1) * R_PAIR)
    pv = pv_ref[:, ls]
    mu_r, mu_k, mu_v = pv[0:1], pv[1:2], pv[2:3]
    w0, a0, k_k, k_a, r_k, gn_g, gn_b = pv[3:4], pv[4:5], pv[5:6], pv[6:7], pv[7:8], pv[8:9], pv[9:10]

    z_r = jnp.where(is_meta, rm_ref[:, ls], rx_ref[:, ls]).astype(F32)
    z_k = jnp.where(is_meta, kmt_ref[:, ls], kx_ref[:, ls]).astype(F32)
    z_v = jnp.where(is_meta, vmt_ref[:, ls], vx_ref[:, ls]).astype(F32)
    r = shifted(z_r, prev_ref[0:1, ls], mu_r, row)
    k = shifted(z_k, prev_ref[1:2, ls], mu_k, row)
    v = shifted(z_v, prev_ref[2:3, ls], mu_v, row)
    prev_ref[0:1, ls] = z_r[tb - 1:tb]
    prev_ref[1:2, ls] = z_k[tb - 1:tb]
    prev_ref[2:3, ls] = z_v[tb - 1:tb]

    u = -(w0 + w_lora[:, ls])
    softplus = jnp.maximum(u, 0.0) + jnp.log(1.0 + jnp.exp(-jnp.abs(u)))
    logw = -jnp.exp(-softplus - 0.5) * LOG2E
    a = 1.0 / (1.0 + jnp.exp(-(a0 + a_lora[:, ls])))
    kk = k * k_k
    kk = kk / jnp.maximum(jnp.sqrt(_seg_sum(kk * kk)), 1e-12)
    k_mod = k * (1.0 + (a - 1.0) * k_a)
    bonus = _seg_sum(r * k_mod * r_k) * v

    lw_h, lw_r = _split_bf16(logw)
    lw_m, lw_l = _split_bf16(lw_r)
    cum3 = jnp.dot(cum_op, jnp.concatenate([lw_h, lw_m, lw_l.astype(BF16)], axis=1),
                   preferred_element_type=F32)
    cum = cum3[:, :R_PAIR] + cum3[:, R_PAIR:2 * R_PAIR] + cum3[:, 2 * R_PAIR:]
    tot = jnp.concatenate([jnp.broadcast_to(cum[cc * c + c - 1:cc * c + c], (c, R_PAIR)) for cc in range(ncc)],
                          axis=0)
    p_inv = jnp.exp2(-cum)
    a_t = -kk * jnp.exp2(cum - logw)
    kka = kk * a
    b_t = kka * p_inv
    k_t = k_mod * p_inv
    r_t = r * jnp.exp2(cum)
    p_end = jnp.exp2(tot - cum)
    b_h = kka * p_end
    k_h = k_mod * p_end
    g_diag = jnp.exp2(tot)

    for cc in range(ncc):
      rs = slice(cc * c, (cc + 1) * c)
      for name, val in (("at", a_t), ("rt", r_t), ("bt", b_t), ("kt", k_t), ("bh", b_h), ("kh", k_h),
                        ("vv", v)):
        chains[name].append(stack(val[rs]))
      chains["gd"].append(g_diag[cc * c:cc * c + 1])
    post.append((bonus, gn_g, gn_b))

  nb = RW_P * ncc
  at, rt, bt, kt, bh, kh, vv = (jnp.stack(chains[name]) for name in ("at", "rt", "bt", "kt", "bh", "kh", "vv"))
  at_b, vv_b, bh_b = at.astype(BF16), vv.astype(BF16), bh.astype(BF16)
  bk = jnp.concatenate([bt, kt], axis=1).astype(BF16)
  top = jnp.where(strict2, _bdot_nt(at_b, bk), 0.0)
  lblk = jnp.where(incl2, _bdot_nt(rt.astype(BF16), bk), 0.0)
  nm, mak = top[:, :, :c2], top[:, :, c2:]
  tinv = jnp.where(diag, 1.0, nm)
  npow = nm.astype(BF16)
  for _ in range(5):
    npow = _bdot(npow, npow).astype(BF16)
    tinv = tinv + _bdot(tinv.astype(BF16), npow)
  x1 = _bdot(mak.astype(BF16), vv_b)
  wu_b = _bdot(tinv.astype(BF16), jnp.concatenate([at_b, x1.astype(BF16)], axis=2)).astype(BF16)
  rhs = jnp.concatenate([wu_b, jnp.concatenate([jnp.zeros_like(vv_b), vv_b], axis=2)], axis=1)
  qy = _bdot(lblk.astype(BF16), rhs)
  q_h = (rt + qy[:, :, :c2]).astype(BF16)
  y0 = qy[:, :, c2:]
  uv = jnp.concatenate([wu_b[:, :, c2:], vv_b], axis=1)
  bkh = jnp.concatenate([bh_b, kh.astype(BF16)], axis=1)
  g_m = [(jnp.where(diag, chains["gd"][n], 0.0) + _dot_tn(wu_b[n, :, :c2], bh_b[n])).astype(BF16)
         for n in range(nb)]
  h_m = [_dot_tn(uv[n], bkh[n]) for n in range(nb)]

  states = [s_ref[p] for p in range(RW_P)]
  y_rows = [[] for _ in range(RW_P)]
  for cc in range(ncc):
    for p in range(RW_P):
      n = p * ncc + cc
      s_old_b = states[p].astype(BF16)
      y2 = _dot_nt(q_h[n], s_old_b) + y0[n]
      states[p] = jnp.dot(s_old_b, g_m[n], preferred_element_type=F32) + h_m[n]
      y_rows[p].append(y2[:c] + y2[c:])

  for p in range(RW_P):
    ls = slice(p * R_PAIR, (p + 1) * R_PAIR)
    s_ref[p] = states[p]
    bonus, gn_g, gn_b = post[p]
    y = jnp.concatenate(y_rows[p], axis=0)
    mean = _seg_sum(y) * (1.0 / nh)
    yc = y - mean
    var = _seg_sum(yc * yc) * (1.0 / nh)
    yn = yc * lax.rsqrt(var + GN_EPS) * gn_g + gn_b
    g = gr_ref[:, ls].astype(F32)
    o_ref[:, ls] = ((yn + bonus) * (g / (1.0 + jnp.exp(-g)))).astype(o_ref.dtype)


def _rwkv(z_x, lo_x, z_mp, lo_mp, pvec, mu_l, wup_h, wup_l, aup):
  b, s, _ = z_x.shape
  tb = RW_TB
  nt = s // tb + 1
  pw = R_PAIR * RW_P
  n_hp = R_PAIRS // RW_P

  def xmap(col):
    return lambda bi, hp, ti: (bi, jnp.maximum(ti - 1, 0), col // pw + hp)

  def mmap(col):
    return lambda bi, hp, ti: (0, col // pw + hp)

  return pl.pallas_call(
      _rwkv_kernel,
      name="rwkv7",
      grid=(b, n_hp, nt),
      in_specs=[
          pl.BlockSpec((None, tb, pw), xmap(Z_RR)),
          pl.BlockSpec((None, tb, pw), xmap(Z_RK)),
          pl.BlockSpec((None, tb, pw), xmap(Z_RV)),
          pl.BlockSpec((None, tb, LORA_PAD), lambda bi, hp, ti: (bi, jnp.maximum(ti - 1, 0), 0)),
          pl.BlockSpec((None, tb, pw), xmap(Z_GR)),
          pl.BlockSpec((tb, pw), mmap(Z_RR)),
          pl.BlockSpec((tb, pw), mmap(Z_RK)),
          pl.BlockSpec((tb, pw), mmap(Z_RV)),
          pl.BlockSpec((tb, LORA_PAD), lambda bi, hp, ti: (0, 0)),
          pl.BlockSpec((16, pw), lambda bi, hp, ti: (0, hp)),
          pl.BlockSpec((1, LORA_PAD), lambda bi, hp, ti: (0, 0)),
          pl.BlockSpec((LORA_PAD, pw), lambda bi, hp, ti: (0, hp)),
          pl.BlockSpec((LORA_PAD, pw), lambda bi, hp, ti: (0, hp)),
          pl.BlockSpec((LORA_PAD, pw), lambda bi, hp, ti: (0, hp)),
      ],
      out_specs=pl.BlockSpec((None, tb, pw), lambda bi, hp, ti: (bi, jnp.maximum(ti - 1, 0), hp)),
      out_shape=jax.ShapeDtypeStruct((b, s, R_WIDTH), BF16),
      scratch_shapes=[
          pltpu.VMEM((RW_P, 2 * R_HEAD, 2 * R_HEAD), F32),
          pltpu.VMEM((8, pw), F32),
          pltpu.VMEM((1, LORA_PAD), F32),
          pltpu.VMEM((n_hp, RW_P, 2 * R_HEAD, 2 * R_HEAD), F32),
          pltpu.VMEM((n_hp, 8, pw), F32),
          pltpu.VMEM((n_hp, 1, LORA_PAD), F32),
      ],
      compiler_params=pltpu.CompilerParams(
          dimension_semantics=("arbitrary", "arbitrary", "arbitrary"),
          vmem_limit_bytes=VMEM_LIMIT),
  )(z_x, z_x, z_x, lo_x, z_x, z_mp, z_mp, z_mp, lo_mp, pvec, mu_l, wup_h, wup_l, aup)


def _out_kernel(x_ref, oa_ref, or_ref, wa_ref, wr_ref, ge_ref, be_ref, gp_ref, bp_ref, o_ref, wb_ref):
  @pl.when(pl.program_id(0) == 0)
  def _():
    wb_ref[0] = wa_ref[...].astype(BF16)
    wb_ref[1] = wr_ref[...].astype(BF16)

  h = _ln_rows(x_ref[...], ge_ref[...], be_ref[...])
  y = (jnp.dot(oa_ref[...], wb_ref[0], preferred_element_type=F32)
       + jnp.dot(or_ref[...], wb_ref[1], preferred_element_type=F32))
  o_ref[...] = _ln_rows(DEEPNORM_ALPHA * h + y, gp_ref[...], bp_ref[...])


def _out_proj(x2d, oa, orw, w, ge, be, gp, bp, tm):
  m, d = x2d.shape
  assert w.shape == (A_WIDTH + R_WIDTH, d) and A_WIDTH == R_WIDTH
  vec = pl.BlockSpec((1, d), lambda i: (0, 0))
  return pl.pallas_call(
      _out_kernel,
      name="out_proj",
      grid=(m // tm,),
      in_specs=[
          pl.BlockSpec((tm, d), lambda i: (i, 0)),
          pl.BlockSpec((tm, A_WIDTH), lambda i: (i, 0)),
          pl.BlockSpec((tm, R_WIDTH), lambda i: (i, 0)),
          pl.BlockSpec((A_WIDTH, d), lambda i: (0, 0), pipeline_mode=pl.Buffered(1)),
          pl.BlockSpec((R_WIDTH, d), lambda i: (1, 0), pipeline_mode=pl.Buffered(1)),
          vec, vec, vec, vec,
      ],
      out_specs=pl.BlockSpec((tm, d), lambda i: (i, 0)),
      out_shape=jax.ShapeDtypeStruct((m, d), F32),
      scratch_shapes=[pltpu.VMEM((2, A_WIDTH, d), BF16)],
      compiler_params=pltpu.CompilerParams(
          dimension_semantics=("arbitrary",),
          vmem_limit_bytes=VMEM_LIMIT),
  )(x2d, oa, orw, w, w, ge, be, gp, bp)


def kernel(x, meta_tokens, ln_emb_g, ln_emb_b, rel_bias, w_in, w_out, lambda_q1, lambda_k1, lambda_q2,
           lambda_k2, subln_g, rw_mu, rw_w0, rw_w_up, rw_a0, rw_a_up, rw_k_k, rw_k_a, rw_r_k, rw_gn_g,
           rw_gn_b, ln_post_g, ln_post_b):
  b, s, d = x.shape
  assert w_in.shape[0] == DEPTH == 1 and w_in.shape[1:] == (d, 4 * A_WIDTH + 4 * R_WIDTH + DECAY_LORA + ICLR_LORA)
  assert s % IN_TM == 0 and (b * s) % OUT_TM == 0 and meta_tokens.shape == (N_META, d)
  l = 0
  wi = w_in[l]
  c_lo = 4 * A_WIDTH + 3 * R_WIDTH
  c_gr = c_lo + DECAY_LORA + ICLR_LORA
  lora_pad = LORA_PAD - DECAY_LORA - ICLR_LORA
  ge, be = ln_emb_g.reshape(1, d), ln_emb_b.reshape(1, d)
  wt_all, wt_gr, wt_lora, zm, zt_tail = _cast_project_meta(wi.T, meta_tokens, ge, be, c_lo)
  z_m = jnp.concatenate([zm[:, :A_WIDTH] * Q_SCALE, zm[:, A_WIDTH:2 * A_WIDTH], zm[:, 3 * A_WIDTH:c_lo],
                         zm[:, c_gr:], zt_tail[:, :N_META].T], axis=1).astype(BF16)
  vt_m = zm[:, 2 * A_WIDTH:3 * A_WIDTH].T.astype(BF16).reshape(1, A_WIDTH, N_META)
  lo_m = jnp.pad(zm[:, c_lo:c_gr], ((0, 0), (0, lora_pad)))

  x2d = x.reshape(b * s, d)
  z_x, vt_x, lo_x = _ln_matmul(x2d, ge, be, wt_all, wt_gr, wt_lora)
  z_x, lo_x = z_x.reshape(b, s, -1), lo_x.reshape(b, s, LORA_PAD)

  bias_d, bias_s, bias_m = _bias_tiles(rel_bias)
  lam_p = jnp.stack([lambda_q1[l], lambda_k1[l], lambda_q2[l], lambda_k2[l]], axis=0)
  o_attn = _attention(z_x, vt_x, z_m, vt_m, bias_d, bias_s, bias_m, lam_p, subln_g[l].reshape(1, A_V_DIM))

  mu = rw_mu[l]
  zeros = jnp.zeros((R_WIDTH,), F32)
  pvec = jnp.stack([mu[:R_WIDTH], mu[R_WIDTH:2 * R_WIDTH], mu[2 * R_WIDTH:3 * R_WIDTH], rw_w0[l], rw_a0[l],
                    rw_k_k[l], rw_k_a[l], rw_r_k[l].reshape(R_WIDTH), rw_gn_g[l], rw_gn_b[l]]
                   + [zeros] * 6, axis=0)
  mu_l = jnp.pad(mu[3 * R_WIDTH:], (0, lora_pad)).reshape(1, LORA_PAD)
  wup = jnp.pad(rw_w_up[l], ((0, LORA_PAD - DECAY_LORA), (0, 0)))
  wup_h = wup.astype(BF16)
  wup_l = (wup - wup_h.astype(F32)).astype(BF16)
  aup = jnp.pad(rw_a_up[l], ((DECAY_LORA, lora_pad), (0, 0))).astype(BF16)
  front = ((RW_TB - N_META, 0), (0, 0))
  o_rwkv = _rwkv(z_x, lo_x, jnp.pad(z_m, front), jnp.pad(lo_m, front), pvec, mu_l, wup_h, wup_l, aup)

  out = _out_proj(x2d, o_attn.reshape(b * s, A_WIDTH), o_rwkv.reshape(b * s, R_WIDTH), w_out[l], ge, be,
                  ln_post_g[l].reshape(1, d), ln_post_b[l].reshape(1, d), OUT_TM)
  return out.reshape(b, s, d)
```

```python
import functools
import math

import numpy as np
import jax
import jax.numpy as jnp
from jax import lax
from jax.experimental import pallas as pl
from jax.experimental.pallas import tpu as pltpu

N_META = 16
A_HEADS = 8
A_V_DIM = 128
A_QK_DIM = 64
A_WIDTH = A_HEADS * A_V_DIM
R_HEAD = 64
R_WIDTH = 1024
R_PAIR = 2 * R_HEAD
R_PAIRS = R_WIDTH // R_PAIR
A_QK_W = 2 * A_QK_DIM
DECAY_LORA = 96
ICLR_LORA = 96
LORA_PAD = 256
N_BUCKETS = 32
MAX_DISTANCE = 128
LN_EPS = 1e-5
SUBLN_EPS = 1e-5
GN_EPS = 64e-5
DEPTH = 1
DEEPNORM_ALPHA = (2 * DEPTH) ** 0.25
LAM_INIT = 0.8 - 0.6 * math.exp(-0.3 * 0)
NEG = -1e30

ATT_T = 256
ATT_G = 8
ONES_ROWS = 16
IN_TM = 1024
IN_TN = 1024
OUT_TM = 512
Z_Q, Z_K, Z_GA, Z_RR, Z_RK, Z_RV, Z_GR = (i * 1024 for i in range(7))
LOG2E = math.log2(math.e)
Q_SCALE = A_QK_DIM ** -0.5 * LOG2E
RW_TB = 128
RW_C = 64
RW_P = 8
META_LANES = 128
VMEM_LIMIT = 56 * 1024 * 1024

F32 = jnp.float32
BF16 = jnp.bfloat16


def _ln_rows(x, g, b):
  mu = jnp.mean(x, axis=-1, keepdims=True)
  xc = x - mu
  var = jnp.mean(xc * xc, axis=-1, keepdims=True)
  return xc * lax.rsqrt(var + LN_EPS) * g + b


def _ln_mm_kernel(x_ref, g_ref, b_ref, wm_ref, wl_ref, om_ref, ovt_ref, ol_ref, hn_ref):
  j = pl.program_id(1)
  n_main = pl.num_programs(1) - 1

  @pl.when(j == 0)
  def _():
    hn_ref[...] = _ln_rows(x_ref[...], g_ref[...], b_ref[...]).astype(BF16)

  @pl.when(j < n_main)
  def _():
    scale = jnp.where(j == Z_Q // IN_TN, Q_SCALE, 1.0)
    z = _dot_nt(hn_ref[...], wm_ref[...])
    om_ref[...] = (z * scale).astype(om_ref.dtype)

  @pl.when(j == n_main - 1)
  def _():
    ol_ref[...] = _dot_nt(hn_ref[...], wl_ref[...])

  @pl.when(j == n_main)
  def _():
    zt = _dot_nt(wm_ref[...], hn_ref[...])
    tv = ovt_ref.shape[2]
    for c in range(ovt_ref.shape[0]):
      ovt_ref[c] = zt[:, c * tv:(c + 1) * tv].astype(ovt_ref.dtype)


def _cast_meta_kernel(n_a, n_lo, rb_ref, xm_ref, g_ref, b_ref, wa_ref, wb_ref, oa_ref, olo_ref, za_ref,
                      zb_ref, bd_ref, bs_ref, bm_ref, hn_ref):
  j = pl.program_id(0)
  tn = wa_ref.shape[0]
  tb = wb_ref.shape[0]

  @pl.when(j == 0)
  def _():
    hn_ref[...] = jnp.zeros_like(hn_ref)
    hn_ref[0:N_META] = _ln_rows(xm_ref[...], g_ref[...], b_ref[...]).astype(BF16)

  @pl.when(j < A_HEADS)
  def _():
    _bias_head(rb_ref, j, bd_ref, bs_ref, bm_ref)

  def meta_rows(w):
    return _dot_nt(w, hn_ref[...])

  @pl.when(j < n_a - 1)
  def _():
    w = wa_ref[...].astype(BF16)
    oa_ref[...] = w
    za_ref[...] = meta_rows(w).T[:N_META]

  @pl.when(j == n_a - 1)
  def _():
    w = wa_ref[...].astype(BF16)
    olo_ref[0:n_lo] = w[:n_lo]
    olo_ref[n_lo:] = jnp.zeros((olo_ref.shape[0] - n_lo, w.shape[1]), BF16)
    oa_ref[0:tn - n_lo] = w[n_lo:]
    za_ref[...] = meta_rows(w).T[:N_META]

  @pl.when(j >= n_a)
  def _():
    w = wb_ref[...].astype(BF16)
    oa_ref[pl.ds(pl.multiple_of(tn - n_lo + (j - n_a) * tb, tb), tb)] = w
    zb_ref[...] = meta_rows(w)


def _cast_project_meta(wt, xm, g, b, c_lo, rel_bias):
  n, d = wt.shape
  tn = IN_TN
  n_a = n // tn
  rem = n - n_a * tn
  tb = math.gcd(rem, tn)
  n_b = rem // tb
  n_lo = n - tn - c_lo
  assert c_lo == (n_a - 1) * tn and n_lo == rem and 0 < n_lo <= LORA_PAD and tb % 16 == 0 and n_lo % 16 == 0
  assert A_HEADS <= n_a + n_b
  vec = pl.BlockSpec((1, d), lambda j: (0, 0))
  whole = lambda j: (0, 0)
  t = ATT_T
  head = lambda j: (jnp.minimum(j, A_HEADS - 1), 0, 0)

  return pl.pallas_call(
      functools.partial(_cast_meta_kernel, n_a, n_lo),
      name="cast_meta",
      grid=(n_a + n_b,),
      in_specs=[
          pl.BlockSpec(memory_space=pltpu.SMEM),
          pl.BlockSpec((N_META, d), whole),
          vec, vec,
          pl.BlockSpec((tn, d), lambda j: (jnp.minimum(j, n_a - 1), 0)),
          pl.BlockSpec((tb, d), lambda j: (n_a * tn // tb + jnp.clip(j - n_a, 0, n_b - 1), 0)),
      ],
      out_specs=[
          pl.BlockSpec((tn, d), lambda j: (jnp.minimum(j, n_a - 1), 0)),
          pl.BlockSpec((LORA_PAD, d), whole),
          pl.BlockSpec((N_META, tn), lambda j: (0, jnp.minimum(j, n_a - 1))),
          pl.BlockSpec((tb, META_LANES), lambda j: (jnp.clip(j - n_a, 0, n_b - 1), 0)),
          pl.BlockSpec((None, t, t), head),
          pl.BlockSpec((None, t, t), head),
          pl.BlockSpec((None, N_META, t), head),
      ],
      out_shape=[
          jax.ShapeDtypeStruct((n_a * tn, d), BF16),
          jax.ShapeDtypeStruct((LORA_PAD, d), BF16),
          jax.ShapeDtypeStruct((N_META, n_a * tn), F32),
          jax.ShapeDtypeStruct((rem, META_LANES), F32),
          jax.ShapeDtypeStruct((A_HEADS, t, t), F32),
          jax.ShapeDtypeStruct((A_HEADS, t, t), F32),
          jax.ShapeDtypeStruct((A_HEADS, N_META, t), F32),
      ],
      scratch_shapes=[pltpu.VMEM((META_LANES, d), BF16)],
      compiler_params=pltpu.CompilerParams(
          dimension_semantics=("arbitrary",),
          vmem_limit_bytes=VMEM_LIMIT),
  )(rel_bias, xm, g, b, wt, wt)


def _ln_matmul(x2d, g, b, wt_all, wt_lora):
  m, d = x2d.shape
  tm = IN_TM
  tn = IN_TN
  nj = Z_GR // tn + 1
  n = nj * tn
  last = nj - 1
  tv = ATT_T
  assert A_WIDTH == tn
  v_tile = 2 * A_WIDTH // tn

  def w_map(i, j):
    return (jnp.where(j == nj, v_tile, jnp.where(j >= v_tile, j + 1, j)), 0)

  return pl.pallas_call(
      _ln_mm_kernel,
      name="ln_inproj",
      grid=(m // tm, nj + 1),
      in_specs=[
          pl.BlockSpec((tm, d), lambda i, j: (i, 0)),
          pl.BlockSpec((1, d), lambda i, j: (0, 0)),
          pl.BlockSpec((1, d), lambda i, j: (0, 0)),
          pl.BlockSpec((tn, d), w_map),
          pl.BlockSpec((LORA_PAD, d), lambda i, j: (0, 0), pipeline_mode=pl.Buffered(1)),
      ],
      out_specs=[
          pl.BlockSpec((tm, tn), lambda i, j: (i, jnp.minimum(j, last))),
          pl.BlockSpec((tm // tv, A_WIDTH, tv), lambda i, j: (i, 0, 0)),
          pl.BlockSpec((tm, LORA_PAD), lambda i, j: (i, 0)),
      ],
      out_shape=[
          jax.ShapeDtypeStruct((m, n), BF16),
          jax.ShapeDtypeStruct((m // tv, A_WIDTH, tv), BF16),
          jax.ShapeDtypeStruct((m, LORA_PAD), F32),
      ],
      scratch_shapes=[pltpu.VMEM((tm, d), BF16)],
      compiler_params=pltpu.CompilerParams(
          dimension_semantics=("parallel", "arbitrary"),
          vmem_limit_bytes=VMEM_LIMIT),
  )(x2d, g, b, wt_all, wt_lora)


def _bucket_thresholds():
  n = np.arange(0, 4 * MAX_DISTANCE, dtype=np.int64)
  max_exact = N_BUCKETS // 2
  nf = np.maximum(n, 1).astype(np.float32)
  large = max_exact + (np.log(nf / np.float32(max_exact)) / np.float32(math.log(MAX_DISTANCE / max_exact))
                       * np.float32(N_BUCKETS - max_exact)).astype(np.int32)
  large = np.minimum(large, N_BUCKETS - 1)
  bucket = np.where(n < max_exact, n, large)
  assert np.all(np.diff(bucket) >= 0) and bucket[-1] == N_BUCKETS - 1
  return [int(np.argmax(bucket >= b)) for b in range(N_BUCKETS)]


_THR = _bucket_thresholds()


def _bias_head(rb_ref, h, diag_ref, sub_ref, meta_ref):
  far = rb_ref[N_BUCKETS - 1, h]

  def bias_of(n):
    out = jnp.full(n.shape, (rb_ref[0, h] - far) * LOG2E, F32)
    for b in range(1, N_BUCKETS):
      out = jnp.where(n >= _THR[b], (rb_ref[b, h] - far) * LOG2E, out)
    return out

  t = ATT_T
  kj = lax.broadcasted_iota(jnp.int32, (t, t), 0)
  qi = lax.broadcasted_iota(jnp.int32, (t, t), 1)
  d = qi - kj
  diag_ref[...] = jnp.where(d >= 0, bias_of(d), NEG)
  sub_ref[...] = bias_of(d + t)
  km = lax.broadcasted_iota(jnp.int32, (N_META, t), 0)
  qm = lax.broadcasted_iota(jnp.int32, (N_META, t), 1)
  meta_ref[...] = bias_of(qm - km + N_META)


def _dot_nt(a, b):
  return lax.dot_general(a, b, (((1,), (1,)), ((), ())), preferred_element_type=F32)


def _attn_kernel(q_ref, kx_hbm, vt_hbm, km_ref, vmt_ref, ga_ref, bd_ref, bs_ref, bm_ref,
                 lam_ref, sg_ref, o_ref, m_ref, alpha_ref, acc_ref, pt_ref, ptm_ref, kx_ref, vt_ref, kv_sem):
  t = ATT_T
  g = ATT_G
  nc = 2 * g
  dv = A_V_DIM
  qi = pl.program_id(2)
  nq = pl.num_programs(2) - 1

  hb = pl.num_programs(1)
  grp = pl.program_id(0) * hb + pl.program_id(1)
  n_grp = pl.num_programs(0) * hb
  w = kx_ref.shape[1]

  def kv_copies(gi, j):
    bi, hi = gi // hb, gi % hb
    rows = pl.ds(pl.multiple_of(j * t, t), t)
    return (pltpu.make_async_copy(kx_hbm.at[bi, rows, pl.ds(pl.multiple_of(Z_K + hi * w, w), w)],
                                  kx_ref.at[rows], kv_sem.at[0]),
            pltpu.make_async_copy(vt_hbm.at[bi, j, pl.ds(pl.multiple_of(hi * w, w), w)], vt_ref.at[j],
                                  kv_sem.at[1]))

  def start(gi, j):
    for cp in kv_copies(gi, j):
      cp.start()

  @pl.when(jnp.logical_and(grp == 0, qi == 0))
  def _():
    start(grp, 0)

  @pl.when(qi < nq)
  def _():
    for cp in kv_copies(grp, qi):
      cp.wait()

  @pl.when(qi + 1 < nq)
  def _():
    start(grp, qi + 1)

  @pl.when(jnp.logical_and(qi == nq, grp + 1 < n_grp))
  def _():
    start(grp + 1, 0)

  def v_tile(j):
    ones = jnp.ones((ONES_ROWS, t), BF16)
    return [jnp.concatenate([vt_ref[j, hh * dv:(hh + 1) * dv, :], ones], axis=0) for hh in range(g)]

  def k_tile(j):
    off = pl.multiple_of(j * t, t)
    return [kx_ref[pl.ds(off, t), hh * A_QK_W:(hh + 1) * A_QK_W] for hh in range(g)]

  def meta_v():
    ones_m = jnp.ones((ONES_ROWS, N_META), BF16)
    return [jnp.concatenate([vmt_ref[0, hh * dv:(hh + 1) * dv, :], ones_m], axis=0) for hh in range(g)]

  def queries():
    lane = lax.broadcasted_iota(jnp.int32, (t, A_QK_W), 1)
    qs = []
    for hh in range(g):
      q = q_ref[:, hh * A_QK_W:(hh + 1) * A_QK_W]
      zero = jnp.zeros_like(q)
      qs += [jnp.where(lane < A_QK_DIM, q, zero), jnp.where(lane >= A_QK_DIM, q, zero)]
    return qs

  def softmax_stage(c, m_prev, s_list):
    m_new = m_prev
    for s in s_list:
      m_new = jnp.maximum(m_new, jnp.max(s, axis=0, keepdims=True))
    m_ref[c] = m_new
    alpha_ref[c] = jnp.exp2(m_prev - m_new)
    return [jnp.exp2(s - m_new).astype(BF16) for s in s_list]

  def pending(segments, c):
    pv = None
    for vts, p_ref in segments:
      d = jnp.dot(vts[c // 2], p_ref[c], preferred_element_type=F32)
      pv = d if pv is None else pv + d
    return pv

  def finish_previous(segments):
    lp = lam_ref[...]
    lam = (jnp.exp(jnp.sum(lp[0:1] * lp[1:2], axis=1, keepdims=True))
           - jnp.exp(jnp.sum(lp[2:3] * lp[3:4], axis=1, keepdims=True)) + LAM_INIT)
    for hh in range(g):
      a0, a1 = (alpha_ref[c] * acc_ref[c] + pending(segments, c) for c in (2 * hh, 2 * hh + 1))
      ot = a0[:dv] / a0[dv:dv + 1] - lam * (a1[:dv] / a1[dv:dv + 1])
      ot = ot * lax.rsqrt(jnp.mean(ot * ot, axis=0, keepdims=True) + SUBLN_EPS)
      o = ot.T * (sg_ref[...] * (1.0 - LAM_INIT))
      gate = ga_ref[:, hh * dv:(hh + 1) * dv].astype(F32)
      o_ref[:, hh * dv:(hh + 1) * dv] = (o * (gate / (1.0 + jnp.exp(-gate)))).astype(o_ref.dtype)

  def start_tile(qs, previous):
    if previous is not None:
      finish_previous(previous)
    ks = k_tile(qi)
    st, sm = [], []
    for c in range(nc):
      hh = c // 2
      st.append(_dot_nt(ks[hh], qs[c]) + bd_ref[hh])
      sm.append(_dot_nt(km_ref[:, hh * A_QK_W:(hh + 1) * A_QK_W], qs[c]) + jnp.where(qi == 0, bm_ref[hh], 0.0))
    acc_ref[...] = jnp.zeros(acc_ref.shape, F32)
    m_start = jnp.full((1, t), NEG, F32)
    for c in range(nc):
      pt_ref[c], ptm_ref[c] = softmax_stage(c, m_start, [st[c], sm[c]])

  def step(qs, j_cur, biases, segments):
    ks = k_tile(j_cur)
    pv, st = [], []
    for c in range(nc):
      pv.append(pending(segments, c))
      s = _dot_nt(ks[c // 2], qs[c])
      st.append(s if biases is None else s + biases[c // 2])
    for c in range(nc):
      acc_ref[c] = alpha_ref[c] * acc_ref[c] + pv[c]
    for c in range(nc):
      pt_ref[c], = softmax_stage(c, m_ref[c], [st[c]])

  def below_diagonal(qs):
    step(qs, qi - 1, [bs_ref[hh] for hh in range(g)], [(v_tile(qi), pt_ref), (meta_v(), ptm_ref)])

  @pl.when(qi == 0)
  def _():
    start_tile(queries(), None)

  @pl.when(qi == 1)
  def _():
    qs = queries()
    start_tile(qs, [(v_tile(0), pt_ref), (meta_v(), ptm_ref)])
    below_diagonal(qs)

  @pl.when(jnp.logical_and(qi >= 2, qi < nq))
  def _():
    qs = queries()
    start_tile(qs, [(v_tile(jnp.maximum(qi - 3, 0)), pt_ref)])
    below_diagonal(qs)
    n_far = qi - 1

    def prev_of(j):
      return jnp.where(j == 0, qi - 1, j - 1)

    def far_body(i, carry):
      step(qs, 2 * i, None, [(v_tile(prev_of(2 * i)), pt_ref)])
      step(qs, 2 * i + 1, None, [(v_tile(2 * i), pt_ref)])
      return carry

    lax.fori_loop(0, n_far // 2, far_body, 0)

    @pl.when(n_far % 2 == 1)
    def _():
      step(qs, n_far - 1, None, [(v_tile(prev_of(n_far - 1)), pt_ref)])

  @pl.when(qi == nq)
  def _():
    finish_previous([(v_tile(nq - 3), pt_ref)])


def _attention(z_x, vt_x, z_m, vt_m, bias_d, bias_s, bias_m, lam_p, subln_g):
  b, s, _ = z_x.shape
  t = ATT_T
  g = ATT_G
  nq = s // t
  assert nq >= 4
  w = A_QK_W * g
  hb = A_HEADS // g
  kb, gb = Z_K // w, Z_GA // w

  def cur(qi):
    return jnp.minimum(qi, nq - 1)

  def prev(qi):
    return jnp.maximum(qi - 1, 0)

  return pl.pallas_call(
      _attn_kernel,
      name="diff_attn",
      grid=(b, hb, nq + 1),
      in_specs=[
          pl.BlockSpec((None, t, w), lambda bi, hi, qi: (bi, cur(qi), hi)),
          pl.BlockSpec(memory_space=pl.ANY),
          pl.BlockSpec(memory_space=pl.ANY),
          pl.BlockSpec((N_META, w), lambda bi, hi, qi: (0, kb + hi)),
          pl.BlockSpec((1, w, N_META), lambda bi, hi, qi: (0, hi, 0)),
          pl.BlockSpec((None, t, w), lambda bi, hi, qi: (bi, prev(qi), gb + hi)),
          pl.BlockSpec((g, t, t), lambda bi, hi, qi: (hi, 0, 0)),
          pl.BlockSpec((g, t, t), lambda bi, hi, qi: (hi, 0, 0)),
          pl.BlockSpec((g, N_META, t), lambda bi, hi, qi: (hi, 0, 0)),
          pl.BlockSpec((4, A_QK_DIM), lambda bi, hi, qi: (0, 0)),
          pl.BlockSpec((1, A_V_DIM), lambda bi, hi, qi: (0, 0)),
      ],
      out_specs=pl.BlockSpec((None, t, w), lambda bi, hi, qi: (bi, prev(qi), hi)),
      out_shape=jax.ShapeDtypeStruct((b, s, A_WIDTH), BF16),
      scratch_shapes=[
          pltpu.VMEM((2 * g, 1, t), F32),
          pltpu.VMEM((2 * g, 1, t), F32),
          pltpu.VMEM((2 * g, A_V_DIM + ONES_ROWS, t), F32),
          pltpu.VMEM((2 * g, t, t), BF16),
          pltpu.VMEM((2 * g, N_META, t), BF16),
          pltpu.VMEM((s, w), BF16),
          pltpu.VMEM((nq, w, t), BF16),
          pltpu.SemaphoreType.DMA((2,)),
      ],
      compiler_params=pltpu.CompilerParams(
          dimension_semantics=("arbitrary", "arbitrary", "arbitrary"),
          vmem_limit_bytes=VMEM_LIMIT),
  )(z_x, z_x, vt_x.reshape(b, s // t, A_WIDTH, t), z_m, vt_m, z_x, bias_d, bias_s, bias_m, lam_p, subln_g)


def _seg_sum(x):
  lane = lax.broadcasted_iota(jnp.int32, x.shape, 1)
  first = lane < R_HEAD
  lo = jnp.sum(jnp.where(first, x, 0.0), axis=1, keepdims=True)
  hi = jnp.sum(jnp.where(first, 0.0, x), axis=1, keepdims=True)
  return jnp.where(first, lo, hi)


def _split_bf16(x):
  hi = x.astype(BF16)
  return hi, x - hi.astype(F32)


def _dot_tn(a, b):
  return lax.dot_general(a, b, (((0,), (0,)), ((), ())), preferred_element_type=F32)


def _bdot(a, b):
  return lax.dot_general(a, b, (((2,), (1,)), ((0,), (0,))), preferred_element_type=F32)


def _bdot_nt(a, b):
  return lax.dot_general(a, b, (((2,), (2,)), ((0,), (0,))), preferred_element_type=F32)


def _rwkv_kernel(*refs):
  *block_refs, sm_ref, pm_ref, plm_ref = refs
  s_ref, prev_ref, prevl_ref = block_refs[-3:]
  bi, hp, ti = pl.program_id(0), pl.program_id(1), pl.program_id(2)

  @pl.when(jnp.logical_or(bi == 0, ti > 0))
  def _():
    _rwkv_block(*block_refs)

  @pl.when(jnp.logical_and(bi == 0, ti == 0))
  def _():
    sm_ref[hp] = s_ref[...]
    pm_ref[hp] = prev_ref[...]
    plm_ref[hp] = prevl_ref[...]

  @pl.when(jnp.logical_and(bi > 0, ti == 0))
  def _():
    s_ref[...] = sm_ref[hp]
    prev_ref[...] = pm_ref[hp]
    prevl_ref[...] = plm_ref[hp]


def _rwkv_block(rx_ref, kx_ref, vx_ref, lx_ref, gr_ref, rm_ref, kmt_ref, vmt_ref, lm_ref,
                pv_ref, mul_ref, wuph_ref, wupl_ref, aup_ref, o_ref, s_ref, prev_ref, prevl_ref):
  tb, c = RW_TB, RW_C
  nh = R_HEAD
  c2 = 2 * c
  ti = pl.program_id(2)
  is_meta = ti == 0

  @pl.when(is_meta)
  def _():
    s_ref[...] = jnp.zeros_like(s_ref)
    prev_ref[...] = jnp.zeros_like(prev_ref)
    prevl_ref[...] = jnp.zeros_like(prevl_ref)

  row = lax.broadcasted_iota(jnp.int32, (tb, R_PAIR), 0)
  rowl = lax.broadcasted_iota(jnp.int32, (tb, LORA_PAD), 0)

  def shifted(z, prev, mu, rows):
    z_prev = jnp.where(rows == 0, prev, pltpu.roll(z, 1, 0))
    return z + (z_prev - z) * mu

  z_l = jnp.where(is_meta, lm_ref[...], lx_ref[...])
  lo = shifted(z_l, prevl_ref[...], mul_ref[...], rowl)
  prevl_ref[...] = z_l[tb - 1:tb]
  th_h, th_l = _split_bf16(jnp.tanh(lo))
  w_lora = (jnp.dot(th_h, wuph_ref[...], preferred_element_type=F32)
            + jnp.dot(th_h, wupl_ref[...], preferred_element_type=F32)
            + jnp.dot(th_l.astype(BF16), wuph_ref[...], preferred_element_type=F32))
  a_lora = jnp.dot(lo.astype(BF16), aup_ref[...], preferred_element_type=F32)

  ii = lax.broadcasted_iota(jnp.int32, (tb, tb), 0)
  jj = lax.broadcasted_iota(jnp.int32, (tb, tb), 1)
  shift = int(math.log2(c))
  same = lax.shift_right_logical(ii, shift) == lax.shift_right_logical(jj, shift)
  cum_op = jnp.where(same, jnp.where(jj <= ii, 1.0, 0.0), 0.0).astype(BF16)

  ci = lax.broadcasted_iota(jnp.int32, (c2, c2), 0)
  cj = lax.broadcasted_iota(jnp.int32, (c2, c2), 1)
  diag = ci == cj
  strict2 = jnp.concatenate([cj < ci, cj < ci], axis=1)
  incl2 = jnp.concatenate([cj <= ci, cj <= ci], axis=1)
  first = lax.broadcasted_iota(jnp.int32, (c, R_PAIR), 1) < nh

  def stack(x):
    return jnp.concatenate([jnp.where(first, x, 0.0), jnp.where(first, 0.0, x)], axis=0)

  ncc = tb // c
  chains = {name: [] for name in ("at", "rt", "bt", "kt", "bh", "kh", "vv", "gd")}
  post = []
  for p in range(RW_P):
    ls = slice(p * R_PAIR, (p + 1) * R_PAIR)
    pv = pv_ref[:, ls]
    mu_r, mu_k, mu_v = pv[0:1], pv[1:2], pv[2:3]
    w0, a0, k_k, k_a, r_k, gn_g, gn_b = pv[3:4], pv[4:5], pv[5:6], pv[6:7], pv[7:8], pv[8:9], pv[9:10]

    z_r = jnp.where(is_meta, rm_ref[:, ls], rx_ref[:, ls]).astype(F32)
    z_k = jnp.where(is_meta, kmt_ref[:, ls], kx_ref[:, ls]).astype(F32)
    z_v = jnp.where(is_meta, vmt_ref[:, ls], vx_ref[:, ls]).astype(F32)
    r = shifted(z_r, prev_ref[0:1, ls], mu_r, row)
    k = shifted(z_k, prev_ref[1:2, ls], mu_k, row)
    v = shifted(z_v, prev_ref[2:3, ls], mu_v, row)
    prev_ref[0:1, ls] = z_r[tb - 1:tb]
    prev_ref[1:2, ls] = z_k[tb - 1:tb]
    prev_ref[2:3, ls] = z_v[tb - 1:tb]

    u = -(w0 + w_lora[:, ls])
    softplus = jnp.maximum(u, 0.0) + jnp.log(1.0 + jnp.exp(-jnp.abs(u)))
    logw = -jnp.exp(-softplus - 0.5) * LOG2E
    a = 1.0 / (1.0 + jnp.exp(-(a0 + a_lora[:, ls])))
    kk = k * k_k
    kk = kk / jnp.maximum(jnp.sqrt(_seg_sum(kk * kk)), 1e-12)
    k_mod = k * (1.0 + (a - 1.0) * k_a)
    bonus = _seg_sum(r * k_mod * r_k) * v

    lw_h, lw_r = _split_bf16(logw)
    lw_m, lw_l = _split_bf16(lw_r)
    cum3 = jnp.dot(cum_op, jnp.concatenate([lw_h, lw_m, lw_l.astype(BF16)], axis=1),
                   preferred_element_type=F32)
    cum = cum3[:, :R_PAIR] + cum3[:, R_PAIR:2 * R_PAIR] + cum3[:, 2 * R_PAIR:]
    tot = jnp.concatenate([jnp.broadcast_to(cum[cc * c + c - 1:cc * c + c], (c, R_PAIR)) for cc in range(ncc)],
                          axis=0)
    p_inv = jnp.exp2(-cum)
    a_t = -kk * jnp.exp2(cum - logw)
    kka = kk * a
    b_t = kka * p_inv
    k_t = k_mod * p_inv
    r_t = r * jnp.exp2(cum)
    p_end = jnp.exp2(tot - cum)
    b_h = kka * p_end
    k_h = k_mod * p_end
    g_diag = jnp.exp2(tot)

    for cc in range(ncc):
      rs = slice(cc * c, (cc + 1) * c)
      for name, val in (("at", a_t), ("rt", r_t), ("bt", b_t), ("kt", k_t), ("bh", b_h), ("kh", k_h),
                        ("vv", v)):
        chains[name].append(stack(val[rs]))
      chains["gd"].append(g_diag[cc * c:cc * c + 1])
    post.append((bonus, gn_g, gn_b))

  nb = RW_P * ncc
  at, rt, bt, kt, bh, kh, vv = (jnp.stack(chains[name]) for name in ("at", "rt", "bt", "kt", "bh", "kh", "vv"))
  at_b, vv_b, bh_b = at.astype(BF16), vv.astype(BF16), bh.astype(BF16)
  bk = jnp.concatenate([bt, kt], axis=1).astype(BF16)
  top = jnp.where(strict2, _bdot_nt(at_b, bk), 0.0)
  lblk = jnp.where(incl2, _bdot_nt(rt.astype(BF16), bk), 0.0)
  nm, mak = top[:, :, :c2], top[:, :, c2:]
  tinv = jnp.where(diag, 1.0, nm)
  npow = nm.astype(BF16)
  for _ in range(5):
    npow = _bdot(npow, npow).astype(BF16)
    tinv = tinv + _bdot(tinv.astype(BF16), npow)
  x1 = _bdot(mak.astype(BF16), vv_b)
  wu_b = _bdot(tinv.astype(BF16), jnp.concatenate([at_b, x1.astype(BF16)], axis=2)).astype(BF16)
  rhs = jnp.concatenate([wu_b, jnp.concatenate([jnp.zeros_like(vv_b), vv_b], axis=2)], axis=1)
  qy = _bdot(lblk.astype(BF16), rhs)
  q_h = (rt + qy[:, :, :c2]).astype(BF16)
  y0 = qy[:, :, c2:]
  uv = jnp.concatenate([wu_b[:, :, c2:], vv_b], axis=1)
  bkh = jnp.concatenate([bh_b, kh.astype(BF16)], axis=1)
  g_m = [(jnp.where(diag, chains["gd"][n], 0.0) + _dot_tn(wu_b[n, :, :c2], bh_b[n])).astype(BF16)
         for n in range(nb)]
  h_m = [_dot_tn(uv[n], bkh[n]) for n in range(nb)]

  states = [s_ref[p] for p in range(RW_P)]
  y_rows = [[] for _ in range(RW_P)]
  for cc in range(ncc):
    for p in range(RW_P):
      n = p * ncc + cc
      s_old_b = states[p].astype(BF16)
      y2 = _dot_nt(q_h[n], s_old_b) + y0[n]
      states[p] = jnp.dot(s_old_b, g_m[n], preferred_element_type=F32) + h_m[n]
      y_rows[p].append(y2[:c] + y2[c:])

  for p in range(RW_P):
    ls = slice(p * R_PAIR, (p + 1) * R_PAIR)
    s_ref[p] = states[p]
    bonus, gn_g, gn_b = post[p]
    y = jnp.concatenate(y_rows[p], axis=0)
    mean = _seg_sum(y) * (1.0 / nh)
    yc = y - mean
    var = _seg_sum(yc * yc) * (1.0 / nh)
    yn = yc * lax.rsqrt(var + GN_EPS) * gn_g + gn_b
    g = gr_ref[:, ls].astype(F32)
    o_ref[:, ls] = ((yn + bonus) * (g / (1.0 + jnp.exp(-g)))).astype(o_ref.dtype)


def _rwkv(z_x, lo_x, z_mp, lo_mp, pvec, mu_l, wup_h, wup_l, aup):
  b, s, _ = z_x.shape
  tb = RW_TB
  nt = s // tb + 1
  pw = R_PAIR * RW_P
  n_hp = R_PAIRS // RW_P

  def xmap(col):
    return lambda bi, hp, ti: (bi, jnp.maximum(ti - 1, 0), col // pw + hp)

  def mmap(col):
    return lambda bi, hp, ti: (0, col // pw + hp)

  return pl.pallas_call(
      _rwkv_kernel,
      name="rwkv7",
      grid=(b, n_hp, nt),
      in_specs=[
          pl.BlockSpec((None, tb, pw), xmap(Z_RR)),
          pl.BlockSpec((None, tb, pw), xmap(Z_RK)),
          pl.BlockSpec((None, tb, pw), xmap(Z_RV)),
          pl.BlockSpec((None, tb, LORA_PAD), lambda bi, hp, ti: (bi, jnp.maximum(ti - 1, 0), 0)),
          pl.BlockSpec((None, tb, pw), xmap(Z_GR)),
          pl.BlockSpec((tb, pw), mmap(Z_RR)),
          pl.BlockSpec((tb, pw), mmap(Z_RK)),
          pl.BlockSpec((tb, pw), mmap(Z_RV)),
          pl.BlockSpec((tb, LORA_PAD), lambda bi, hp, ti: (0, 0)),
          pl.BlockSpec((16, pw), lambda bi, hp, ti: (0, hp)),
          pl.BlockSpec((1, LORA_PAD), lambda bi, hp, ti: (0, 0)),
          pl.BlockSpec((LORA_PAD, pw), lambda bi, hp, ti: (0, hp)),
          pl.BlockSpec((LORA_PAD, pw), lambda bi, hp, ti: (0, hp)),
          pl.BlockSpec((LORA_PAD, pw), lambda bi, hp, ti: (0, hp)),
      ],
      out_specs=pl.BlockSpec((None, tb, pw), lambda bi, hp, ti: (bi, jnp.maximum(ti - 1, 0), hp)),
      out_shape=jax.ShapeDtypeStruct((b, s, R_WIDTH), BF16),
      scratch_shapes=[
          pltpu.VMEM((RW_P, 2 * R_HEAD, 2 * R_HEAD), F32),
          pltpu.VMEM((8, pw), F32),
          pltpu.VMEM((1, LORA_PAD), F32),
          pltpu.VMEM((n_hp, RW_P, 2 * R_HEAD, 2 * R_HEAD), F32),
          pltpu.VMEM((n_hp, 8, pw), F32),
          pltpu.VMEM((n_hp, 1, LORA_PAD), F32),
      ],
      compiler_params=pltpu.CompilerParams(
          dimension_semantics=("arbitrary", "arbitrary", "arbitrary"),
          vmem_limit_bytes=VMEM_LIMIT),
  )(z_x, z_x, z_x, lo_x, z_x, z_mp, z_mp, z_mp, lo_mp, pvec, mu_l, wup_h, wup_l, aup)


def _out_kernel(x_ref, oa_ref, or_ref, wa_ref, wr_ref, ge_ref, be_ref, gp_ref, bp_ref, o_ref, wb_ref):
  @pl.when(pl.program_id(0) == 0)
  def _():
    wb_ref[0] = wa_ref[...].astype(BF16)
    wb_ref[1] = wr_ref[...].astype(BF16)

  h = _ln_rows(x_ref[...], ge_ref[...], be_ref[...])
  y = (jnp.dot(oa_ref[...], wb_ref[0], preferred_element_type=F32)
       + jnp.dot(or_ref[...], wb_ref[1], preferred_element_type=F32))
  o_ref[...] = _ln_rows(DEEPNORM_ALPHA * h + y, gp_ref[...], bp_ref[...])


def _out_proj(x2d, oa, orw, w, ge, be, gp, bp, tm):
  m, d = x2d.shape
  assert w.shape == (A_WIDTH + R_WIDTH, d) and A_WIDTH == R_WIDTH
  vec = pl.BlockSpec((1, d), lambda i: (0, 0))
  return pl.pallas_call(
      _out_kernel,
      name="out_proj",
      grid=(m // tm,),
      in_specs=[
          pl.BlockSpec((tm, d), lambda i: (i, 0)),
          pl.BlockSpec((tm, A_WIDTH), lambda i: (i, 0)),
          pl.BlockSpec((tm, R_WIDTH), lambda i: (i, 0)),
          pl.BlockSpec((A_WIDTH, d), lambda i: (0, 0), pipeline_mode=pl.Buffered(1)),
          pl.BlockSpec((R_WIDTH, d), lambda i: (1, 0), pipeline_mode=pl.Buffered(1)),
          vec, vec, vec, vec,
      ],
      out_specs=pl.BlockSpec((tm, d), lambda i: (i, 0)),
      out_shape=jax.ShapeDtypeStruct((m, d), F32),
      scratch_shapes=[pltpu.VMEM((2, A_WIDTH, d), BF16)],
      compiler_params=pltpu.CompilerParams(
          dimension_semantics=("arbitrary",),
          vmem_limit_bytes=VMEM_LIMIT),
  )(x2d, oa, orw, w, w, ge, be, gp, bp)


def kernel(x, meta_tokens, ln_emb_g, ln_emb_b, rel_bias, w_in, w_out, lambda_q1, lambda_k1, lambda_q2,
           lambda_k2, subln_g, rw_mu, rw_w0, rw_w_up, rw_a0, rw_a_up, rw_k_k, rw_k_a, rw_r_k, rw_gn_g,
           rw_gn_b, ln_post_g, ln_post_b):
  b, s, d = x.shape
  assert w_in.shape[0] == DEPTH == 1 and w_in.shape[1:] == (d, 4 * A_WIDTH + 4 * R_WIDTH + DECAY_LORA + ICLR_LORA)
  assert s % IN_TM == 0 and (b * s) % OUT_TM == 0 and meta_tokens.shape == (N_META, d)
  l = 0
  wi = w_in[l]
  c_lo = 4 * A_WIDTH + 3 * R_WIDTH
  c_gr = c_lo + DECAY_LORA + ICLR_LORA
  lora_pad = LORA_PAD - DECAY_LORA - ICLR_LORA
  ge, be = ln_emb_g.reshape(1, d), ln_emb_b.reshape(1, d)
  wt_all, wt_lora, zm, zt_tail, bias_d, bias_s, bias_m = _cast_project_meta(
      wi.T, meta_tokens, ge, be, c_lo, rel_bias)
  z_m = jnp.concatenate([zm[:, :A_WIDTH] * Q_SCALE, zm[:, A_WIDTH:2 * A_WIDTH], zm[:, 3 * A_WIDTH:c_lo],
                         zm[:, c_gr:], zt_tail[:, :N_META].T], axis=1).astype(BF16)
  vt_m = zm[:, 2 * A_WIDTH:3 * A_WIDTH].T.astype(BF16).reshape(1, A_WIDTH, N_META)
  lo_m = jnp.pad(zm[:, c_lo:c_gr], ((0, 0), (0, lora_pad)))

  x2d = x.reshape(b * s, d)
  z_x, vt_x, lo_x = _ln_matmul(x2d, ge, be, wt_all, wt_lora)
  z_x, lo_x = z_x.reshape(b, s, -1), lo_x.reshape(b, s, LORA_PAD)

  lam_p = jnp.stack([lambda_q1[l], lambda_k1[l], lambda_q2[l], lambda_k2[l]], axis=0)
  o_attn = _attention(z_x, vt_x, z_m, vt_m, bias_d, bias_s, bias_m, lam_p, subln_g[l].reshape(1, A_V_DIM))

  mu = rw_mu[l]
  zeros = jnp.zeros((R_WIDTH,), F32)
  pvec = jnp.stack([mu[:R_WIDTH], mu[R_WIDTH:2 * R_WIDTH], mu[2 * R_WIDTH:3 * R_WIDTH], rw_w0[l], rw_a0[l],
                    rw_k_k[l], rw_k_a[l], rw_r_k[l].reshape(R_WIDTH), rw_gn_g[l], rw_gn_b[l]]
                   + [zeros] * 6, axis=0)
  mu_l = jnp.pad(mu[3 * R_WIDTH:], (0, lora_pad)).reshape(1, LORA_PAD)
  wup = jnp.pad(rw_w_up[l], ((0, LORA_PAD - DECAY_LORA), (0, 0)))
  wup_h = wup.astype(BF16)
  wup_l = (wup - wup_h.astype(F32)).astype(BF16)
  aup = jnp.pad(rw_a_up[l], ((DECAY_LORA, lora_pad), (0, 0))).astype(BF16)
  front = ((RW_TB - N_META, 0), (0, 0))
  o_rwkv = _rwkv(z_x, lo_x, jnp.pad(z_m, front), jnp.pad(lo_m, front), pvec, mu_l, wup_h, wup_l, aup)

  out = _out_proj(x2d, o_attn.reshape(b * s, A_WIDTH), o_rwkv.reshape(b * s, R_WIDTH), w_out[l], ge, be,
                  ln_post_g[l].reshape(1, d), ln_post_b[l].reshape(1, d), OUT_TM)
  return out.reshape(b, s, d)
```
